```python
import math
import jax, jax.numpy as jnp
from jax import lax
import numpy as np

D_MODEL = 2048
BATCH = 4
SEQ = 8192
DEPTH = 2
DEC_BATCH = 1
DEC_SEQ = 8192
PAST_LEN = 128

GRID_W = 64
CONV_WIDTH = 512
SHORT_CONV = 3
SSD_HEADS = 16
SSD_HEAD_DIM = 64
SSD_WIDTH = SSD_HEADS * SSD_HEAD_DIM
SSD_GROUPS = 2
SSD_HPG = SSD_HEADS // SSD_GROUPS
SSD_STATE = 64
SSD_CONV = 3
SSD_CHUNK = 128
NA_HEADS = 8
NA_HEAD_DIM = 64
NA_WIDTH = NA_HEADS * NA_HEAD_DIM
WIN_H = 8
WIN_W = 16
D_MIX = CONV_WIDTH + SSD_WIDTH + NA_WIDTH
SSD_XBC = SSD_WIDTH + 2 * SSD_GROUPS * SSD_STATE
PROJ_WIDTHS = (CONV_WIDTH, CONV_WIDTH, CONV_WIDTH, SSD_WIDTH, SSD_XBC, 2 * SSD_HEADS,
               NA_WIDTH, NA_WIDTH, NA_WIDTH)
D_IN_PROJ = sum(PROJ_WIDTHS)
D_FF = 5504
N_EXPERTS = 8
TOP_K = 2
D_FF_EXPERT = 7168
MOE_BLOCK = 1024
N_DENSE = (DEPTH + 1) // 2
N_MOE = DEPTH // 2
ALPHA = (2 * DEPTH) ** 0.25
BETA = (8 * DEPTH) ** -0.25
LN_EPS = 1e-5
RMS_EPS = 1e-6

kernel_name = 'hybrid_bidir_conv_ssd_natten_encoder'


def layer_norm(x, g, b):
    xf = x.astype(jnp.float32)
    mu = xf.mean(-1, keepdims=True)
    var = jnp.square(xf - mu).mean(-1, keepdims=True)
    return ((xf - mu) * lax.rsqrt(var + LN_EPS) * g.astype(jnp.float32) + b.astype(jnp.float32)).astype(x.dtype)


def rms_norm(x, g):
    xf = x.astype(jnp.float32)
    y = xf * lax.rsqrt(jnp.mean(xf * xf, -1, keepdims=True) + RMS_EPS)
    return (y * g.astype(jnp.float32)).astype(x.dtype)


def centred_dwconv(x, w):
    k = w.shape[0]
    pad = k // 2
    t = x.shape[1]
    xp = jnp.pad(x, ((0, 0), (pad, pad), (0, 0)))
    out = xp[:, 0:t] * w[0]
    for i in range(1, k):
        out = out + xp[:, i:i + t] * w[i]
    return out


def ssd_chunked(x, dt, a, bm, cm):
    bsz, t = x.shape[:2]
    nc = t // SSD_CHUNK
    x = x.reshape(bsz, nc, SSD_CHUNK, *x.shape[2:])
    dt = dt.reshape(bsz, nc, SSD_CHUNK, *dt.shape[2:])
    bm = bm.reshape(bsz, nc, SSD_CHUNK, *bm.shape[2:])
    cm = cm.reshape(bsz, nc, SSD_CHUNK, *cm.shape[2:])
    acum = jnp.cumsum(dt * a, axis=2)
    acum_t = jnp.moveaxis(acum, 2, -1)
    seg = acum_t[..., :, None] - acum_t[..., None, :]
    lower = np.tril(np.ones((SSD_CHUNK, SSD_CHUNK), dtype=bool))
    decay = jnp.exp(jnp.where(lower, seg, -jnp.inf))
    cb = jnp.einsum('bclgn,bcsgn->bcgls', cm, bm)
    wts = cb[:, :, :, None] * decay * jnp.moveaxis(dt, 2, -1)[..., None, :]
    y_diag = jnp.einsum('bcgels,bcsgep->bclgep', wts, x)
    decay_end = jnp.exp(acum[:, :, -1:] - acum)
    states = jnp.einsum('bclgn,bclge,bclgep->bcgepn', bm, decay_end * dt, x)
    chunk_decay = jnp.exp(acum[:, :, -1])

    def step(h, inp):
        dec, st = inp
        return dec[..., None, None] * h + st, h

    h0 = jnp.zeros_like(states[:, 0])
    _, h_in = lax.scan(step, h0, (jnp.moveaxis(chunk_decay, 1, 0), jnp.moveaxis(states, 1, 0)))
    h_in = jnp.moveaxis(h_in, 0, 1)
    y_off = jnp.einsum('bclgn,bcgepn,bclge->bclgep', cm, h_in, jnp.exp(acum))
    return (y_diag + y_off).reshape(bsz, t, *x.shape[3:])


def ssd_mixer(z, xbc, dt_raw, conv_w, conv_b, dt_bias, a_log, d_skip, norm_g):
    bsz, t, _ = z.shape
    f32 = jnp.float32
    xbc = jax.nn.silu(centred_dwconv(xbc, conv_w) + conv_b)
    xs, bm, cm = jnp.split(xbc, [SSD_WIDTH, SSD_WIDTH + SSD_GROUPS * SSD_STATE], axis=-1)
    xs = xs.astype(f32).reshape(bsz, t, SSD_GROUPS, SSD_HPG, SSD_HEAD_DIM)
    bm = bm.astype(f32).reshape(bsz, t, SSD_GROUPS, SSD_STATE)
    cm = cm.astype(f32).reshape(bsz, t, SSD_GROUPS, SSD_STATE)
    dt = jax.nn.softplus(dt_raw.astype(f32).reshape(bsz, t, 2, SSD_HEADS) + dt_bias.astype(f32))
    dt = dt.reshape(bsz, t, 2, SSD_GROUPS, SSD_HPG)
    a = -jnp.exp(a_log.astype(f32)).reshape(2, SSD_GROUPS, SSD_HPG)
    y_fwd = ssd_chunked(xs, dt[:, :, 0], a[0], bm, cm)
    y_bwd = jnp.flip(ssd_chunked(jnp.flip(xs, 1), jnp.flip(dt[:, :, 1], 1), a[1],
                                 jnp.flip(bm, 1), jnp.flip(cm, 1)), 1)
    y = y_fwd + y_bwd + d_skip.astype(f32).reshape(SSD_GROUPS, SSD_HPG)[..., None] * xs
    y = y * jax.nn.silu(z.astype(f32)).reshape(bsz, t, SSD_GROUPS, SSD_HPG, SSD_HEAD_DIM)
    y = y.reshape(bsz, t, SSD_GROUPS, SSD_HPG * SSD_HEAD_DIM)
    y = y * lax.rsqrt(jnp.mean(y * y, -1, keepdims=True) + RMS_EPS)
    return (y.reshape(bsz, t, SSD_WIDTH) * norm_g.astype(f32)).astype(z.dtype)


def neighbourhood_attention(q, k, v, rel_bias):
    bsz, t, _ = q.shape
    rows = t // GRID_W
    wh = min(WIN_H, rows)
    shp = (bsz, rows, GRID_W, NA_HEADS, NA_HEAD_DIM)
    q, k, v = q.reshape(shp), k.reshape(shp), v.reshape(shp)
    r = np.arange(rows)
    rs = np.clip(r - wh // 2, 0, rows - wh)
    row_idx = rs[:, None] + np.arange(wh)[None, :]
    di = row_idx - r[:, None]
    qc = np.arange(GRID_W)
    cs = np.clip(qc - WIN_W // 2, 0, GRID_W - WIN_W)
    kc = np.arange(GRID_W)
    col_valid = (kc[None, :] >= cs[:, None]) & (kc[None, :] < cs[:, None] + WIN_W)
    dj = np.clip(kc[None, :] - qc[:, None] + WIN_W - 1, 0, 2 * WIN_W - 2)
    k_rows = k[:, row_idx]
    v_rows = v[:, row_idx]
    s = jnp.einsum('brchd,brikhd->brhcik', q, k_rows,
                   preferred_element_type=jnp.float32) * (NA_HEAD_DIM ** -0.5)
    bias = rel_bias.astype(jnp.float32)[:, di + WIN_H - 1]
    bias = bias[..., dj].transpose(1, 0, 3, 2, 4)
    s = jnp.where(col_valid[:, None, :], s + bias, -jnp.inf)
    p = jax.nn.softmax(s.reshape(*s.shape[:4], wh * GRID_W), axis=-1).reshape(s.shape)
    o = jnp.einsum('brhcik,brikhd->brchd', p.astype(v.dtype), v_rows)
    return o.reshape(bsz, t, NA_WIDTH)


def token_mixer(x, w_in, conv_w, conv_norm_g, ssd_conv_w, ssd_conv_b, ssd_dt_bias, ssd_a_log,
                ssd_d, ssd_norm_g, na_rel_bias, na_norm_g, w_out):
    proj = x @ w_in
    splits = [int(s) for s in np.cumsum(PROJ_WIDTHS)[:-1]]
    cb, cc, ch, z, xbc, dt_raw, q, k, v = jnp.split(proj, splits, axis=-1)
    y_conv = rms_norm(cb * centred_dwconv(cc * ch, conv_w), conv_norm_g)
    y_ssd = ssd_mixer(z, xbc, dt_raw, ssd_conv_w, ssd_conv_b, ssd_dt_bias, ssd_a_log, ssd_d, ssd_norm_g)
    y_na = rms_norm(neighbourhood_attention(q, k, v, na_rel_bias), na_norm_g)
    return jnp.concatenate([y_conv, y_ssd, y_na], axis=-1) @ w_out


def swiglu(x, w_gate, w_up, w_down):
    return (jax.nn.silu(x @ w_gate) * (x @ w_up)) @ w_down


def moe_swiglu(x, router_w, w_gate, w_up, w_down):
    bsz, t, d = x.shape
    n = bsz * t
    nk = n * TOP_K
    xf = x.reshape(n, d)
    logits = (xf @ router_w).astype(jnp.float32)
    top_logits, top_idx = lax.top_k(logits, TOP_K)
    gates = jax.nn.softmax(top_logits, axis=-1)
    flat_e = top_idx.reshape(-1).astype(jnp.int32)
    flat_tok = jnp.repeat(jnp.arange(n, dtype=jnp.int32), TOP_K)
    flat_gate = gates.reshape(-1)
    order = jnp.argsort(flat_e)
    se, stok, sg = flat_e[order], flat_tok[order], flat_gate[order]
    counts = jnp.zeros((N_EXPERTS,), jnp.int32).at[flat_e].add(1)
    padded = ((counts + MOE_BLOCK - 1) // MOE_BLOCK) * MOE_BLOCK
    pend = jnp.cumsum(padded)
    pstart = pend - padded
    sstart = jnp.cumsum(counts) - counts
    dest = pstart[se] + jnp.arange(nk, dtype=jnp.int32) - sstart[se]
    n_blocks = -(-nk // MOE_BLOCK) + N_EXPERTS
    p_len = n_blocks * MOE_BLOCK
    pad_tok = jnp.full((p_len,), n, jnp.int32).at[dest].set(stok)
    pad_gate = jnp.zeros((p_len,), jnp.float32).at[dest].set(sg)
    block_e = jnp.minimum(jnp.searchsorted(pend, jnp.arange(n_blocks, dtype=jnp.int32) * MOE_BLOCK,
                                           side='right'), N_EXPERTS - 1)
    x_pad = jnp.concatenate([xf, jnp.zeros((1, d), x.dtype)], axis=0)

    def run_block(args):
        tok, e = args
        xb = x_pad[tok]
        return swiglu(xb, w_gate[e], w_up[e], w_down[e])

    out = lax.map(run_block, (pad_tok.reshape(n_blocks, MOE_BLOCK), block_e)).reshape(p_len, d)
    out = out * pad_gate[:, None].astype(x.dtype)
    y = jnp.zeros((n + 1, d), x.dtype).at[pad_tok].add(out)[:n]
    return y.reshape(bsz, t, d)


def trunk(x, w_in, conv_w, conv_norm_g, ssd_conv_w, ssd_conv_b, ssd_dt_bias, ssd_a_log, ssd_d,
          ssd_norm_g, na_rel_bias, na_norm_g, w_out, ln_mix_g, ln_mix_b, ln_ffn_g, ln_ffn_b,
          ffn_w_gate, ffn_w_up, ffn_w_down, router_w, moe_w_gate, moe_w_up, moe_w_down):
    for l in range(DEPTH):
        mix = token_mixer(x, w_in[l], conv_w[l], conv_norm_g[l], ssd_conv_w[l], ssd_conv_b[l],
                          ssd_dt_bias[l], ssd_a_log[l], ssd_d[l], ssd_norm_g[l], na_rel_bias[l],
                          na_norm_g[l], w_out[l])
        x = layer_norm(ALPHA * x + mix, ln_mix_g[l], ln_mix_b[l])
        if l % 2 == 0:
            f = swiglu(x, ffn_w_gate[l // 2], ffn_w_up[l // 2], ffn_w_down[l // 2])
        else:
            f = moe_swiglu(x, router_w[l // 2], moe_w_gate[l // 2], moe_w_up[l // 2], moe_w_down[l // 2])
        x = layer_norm(ALPHA * x + f, ln_ffn_g[l], ln_ffn_b[l])
    return x


def setup_inputs(seed: int = 0) -> dict:
    key = jax.random.key(seed)
    ks = iter(jax.random.split(key, 32))
    nrm = lambda shape, scale: jax.random.normal(next(ks), shape, jnp.float32) * scale
    gain = lambda shape: 1.0 + nrm(shape, 0.02)
    dt0 = jnp.exp(jax.random.uniform(next(ks), (DEPTH, 2, SSD_HEADS), jnp.float32,
                                     math.log(1e-3), math.log(1e-1)))
    return {
        'x_prompt': nrm((BATCH, SEQ, D_MODEL), 1.0),
        'x_sample': nrm((DEC_BATCH, DEC_SEQ, D_MODEL), 1.0),
        'w_in': nrm((DEPTH, D_MODEL, D_IN_PROJ), D_MODEL ** -0.5),
        'conv_w': nrm((DEPTH, SHORT_CONV, CONV_WIDTH), SHORT_CONV ** -0.5),
        'conv_norm_g': gain((DEPTH, CONV_WIDTH)),
        'ssd_conv_w': nrm((DEPTH, SSD_CONV, SSD_XBC), SSD_CONV ** -0.5),
        'ssd_conv_b': nrm((DEPTH, SSD_XBC), 0.01),
        'ssd_dt_bias': dt0 + jnp.log(-jnp.expm1(-dt0)),
        'ssd_a_log': jnp.log(jax.random.uniform(next(ks), (DEPTH, 2, SSD_HEADS), jnp.float32, 1.0, 16.0)),
        'ssd_d': gain((DEPTH, SSD_HEADS)),
        'ssd_norm_g': gain((DEPTH, SSD_WIDTH)),
        'na_rel_bias': nrm((DEPTH, NA_HEADS, 2 * WIN_H - 1, 2 * WIN_W - 1), 0.1),
        'na_norm_g': gain((DEPTH, NA_WIDTH)),
        'w_out': nrm((DEPTH, D_MIX, D_MODEL), BETA * D_MIX ** -0.5),
        'ln_mix_g': gain((DEPTH, D_MODEL)),
        'ln_mix_b': nrm((DEPTH, D_MODEL), 0.01),
        'ln_ffn_g': gain((DEPTH, D_MODEL)),
        'ln_ffn_b': nrm((DEPTH, D_MODEL), 0.01),
        'ffn_w_gate': nrm((N_DENSE, D_MODEL, D_FF), D_MODEL ** -0.5),
        'ffn_w_up': nrm((N_DENSE, D_MODEL, D_FF), D_MODEL ** -0.5),
        'ffn_w_down': nrm((N_DENSE, D_FF, D_MODEL), BETA * D_FF ** -0.5),
        'router_w': nrm((N_MOE, D_MODEL, N_EXPERTS), D_MODEL ** -0.5),
        'moe_w_gate': nrm((N_MOE, N_EXPERTS, D_MODEL, D_FF_EXPERT), D_MODEL ** -0.5),
        'moe_w_up': nrm((N_MOE, N_EXPERTS, D_MODEL, D_FF_EXPERT), D_MODEL ** -0.5),
        'moe_w_down': nrm((N_MOE, N_EXPERTS, D_FF_EXPERT, D_MODEL), BETA * D_FF_EXPERT ** -0.5),
    }


def reference(x_prompt, x_sample, w_in, conv_w, conv_norm_g, ssd_conv_w, ssd_conv_b, ssd_dt_bias,
              ssd_a_log, ssd_d, ssd_norm_g, na_rel_bias, na_norm_g, w_out, ln_mix_g, ln_mix_b,
              ln_ffn_g, ln_ffn_b, ffn_w_gate, ffn_w_up, ffn_w_down, router_w, moe_w_gate,
              moe_w_up, moe_w_down):
    y_prompt = trunk(x_prompt, w_in, conv_w, conv_norm_g, ssd_conv_w, ssd_conv_b, ssd_dt_bias,
                     ssd_a_log, ssd_d, ssd_norm_g, na_rel_bias, na_norm_g, w_out, ln_mix_g, ln_mix_b,
                     ln_ffn_g, ln_ffn_b, ffn_w_gate, ffn_w_up, ffn_w_down, router_w, moe_w_gate,
                     moe_w_up, moe_w_down)
    y_sample = trunk(x_sample, w_in, conv_w, conv_norm_g, ssd_conv_w, ssd_conv_b, ssd_dt_bias,
                     ssd_a_log, ssd_d, ssd_norm_g, na_rel_bias, na_norm_g, w_out, ln_mix_g, ln_mix_b,
                     ln_ffn_g, ln_ffn_b, ffn_w_gate, ffn_w_up, ffn_w_down, router_w, moe_w_gate,
                     moe_w_up, moe_w_down)
    return (y_prompt, y_sample)
```

```python
import functools

import numpy as np
import jax
import jax.numpy as jnp
from jax import lax
from jax.experimental import pallas as pl
from jax.experimental.pallas import tpu as pltpu

F32 = jnp.float32
BF16 = jnp.bfloat16

D_MODEL = 2048
DEPTH = 2
GRID_W = 64
CONV_WIDTH = 512
SSD_HEADS = 16
SSD_HEAD_DIM = 64
SSD_WIDTH = 1024
SSD_GROUPS = 2
SSD_STATE = 64
SSD_CHUNK = 128
NA_HEADS = 8
NA_HEAD_DIM = 64
NA_WIDTH = 512
WIN_H = 8
WIN_W = 16
D_FF = 5504
N_EXPERTS = 8
TOP_K = 2
D_FF_EXPERT = 7168
ALPHA = (2 * DEPTH) ** 0.25
LN_EPS = 1e-5
RMS_EPS = 1e-6

LANES = 128
SUBLANES = 8
NEG_BIG = -1e30

A_Z, A_XS, A_CB, A_CC, A_CH, A_BC, A_DT, A_WIDTH = 0, 1024, 2048, 2560, 3072, 3584, 3840, 4096
FF_TILE = 512
D_FF_PAD = 5632
MOE_TM = 1024
VMEM_LIMIT = 56 * 1024 * 1024


def _cparams(sem, vmem=VMEM_LIMIT):
    return pltpu.CompilerParams(dimension_semantics=sem, vmem_limit_bytes=vmem)


def _sigmoid(x):
    return 1.0 / (1.0 + jnp.exp(-x))


def _split3(x):
    x1 = x.astype(BF16)
    r1 = x - x1.astype(F32)
    x2 = r1.astype(BF16)
    r2 = r1 - x2.astype(F32)
    return x1, x2, r2.astype(BF16)


def _dot(a, b):
    return jnp.dot(a, b, preferred_element_type=F32)


def _dot_exact_lhs(x, m):
    x1, x2, x3 = _split3(x)
    return (_dot(x3, m) + _dot(x2, m)) + _dot(x1, m)


def _dot_exact_rhs(m, x):
    x1, x2, x3 = _split3(x)
    return (_dot(m, x3) + _dot(m, x2)) + _dot(m, x1)


def _layer_norm(v, g, b):
    mu = jnp.mean(v, axis=-1, keepdims=True)
    c = v - mu
    var = jnp.mean(c * c, axis=-1, keepdims=True)
    return c * lax.rsqrt(var + LN_EPS) * g + b


def _mm_kernel(x_ref, w_ref, o_ref, xb_ref):
    @pl.when(pl.program_id(1) == 0)
    def _():
        xb_ref[...] = x_ref[...].astype(BF16)

    o_ref[...] = _dot(xb_ref[...], w_ref[...]).astype(o_ref.dtype)


def _matmul(x, w, out_dtype, tm, tn, name):
    n, k = x.shape
    nout = w.shape[1]
    return pl.pallas_call(
        _mm_kernel,
        grid=(n // tm, nout // tn),
        in_specs=[pl.BlockSpec((tm, k), lambda i, j: (i, 0)),
                  pl.BlockSpec((k, tn), lambda i, j: (0, j))],
        out_specs=pl.BlockSpec((tm, tn), lambda i, j: (i, j)),
        out_shape=jax.ShapeDtypeStruct((n, nout), out_dtype),
        scratch_shapes=[pltpu.VMEM((tm, k), BF16)],
        compiler_params=_cparams(("parallel", "arbitrary")),
        name=name,
    )(x, w)


def _shift_rows(x, prev_row, next_row, first, last):
    n = x.shape[0]
    row = lax.broadcasted_iota(jnp.int32, (n, 1), 0)
    prev_row = jnp.where(first, 0.0, prev_row)
    next_row = jnp.where(last, 0.0, next_row)
    x_prev = jnp.where(row == 0, prev_row, pltpu.roll(x, 1, 0))
    x_next = jnp.where(row == n - 1, next_row, pltpu.roll(x, n - 1, 0))
    return x_prev, x_next


def _convmix_kernel(cb_ref, cc_ref, ch_ref, ccp_ref, chp_ref, ccn_ref, chn_ref, w_ref, g_ref, o_ref, *, tiles_per_seq):
    i = pl.program_id(0)
    first = (i % tiles_per_seq) == 0
    last = (i % tiles_per_seq) == tiles_per_seq - 1
    u = cc_ref[...] * ch_ref[...]
    up = ccp_ref[SUBLANES - 1:SUBLANES, :] * chp_ref[SUBLANES - 1:SUBLANES, :]
    un = ccn_ref[0:1, :] * chn_ref[0:1, :]
    u_prev, u_next = _shift_rows(u, up, un, first, last)
    conv = u_prev * w_ref[0:1, :] + u * w_ref[1:2, :] + u_next * w_ref[2:3, :]
    y = cb_ref[...] * conv
    y = y * lax.rsqrt(jnp.mean(y * y, axis=-1, keepdims=True) + RMS_EPS)
    o_ref[...] = (y * g_ref[...]).astype(o_ref.dtype)


def _halo_specs(tt, width, col0, n_rows):
    cb = col0 // width
    r = tt // SUBLANES
    last_blk = n_rows // SUBLANES - 1
    prev = pl.BlockSpec((SUBLANES, width), lambda i: (jnp.maximum(i * r - 1, 0), cb))
    nxt = pl.BlockSpec((SUBLANES, width), lambda i: (jnp.minimum((i + 1) * r, last_blk), cb))
    return prev, nxt


def _conv_mix(proj_a, conv_w, norm_g, seq, tt=512):
    n = proj_a.shape[0]
    w = CONV_WIDTH
    main = lambda c0: pl.BlockSpec((tt, w), lambda i: (i, c0 // w))
    ccp, ccn = _halo_specs(tt, w, A_CC, n)
    chp, chn = _halo_specs(tt, w, A_CH, n)
    full = lambda shape: pl.BlockSpec(shape, lambda i: (0, 0))
    return pl.pallas_call(
        functools.partial(_convmix_kernel, tiles_per_seq=seq // tt),
        grid=(n // tt,),
        in_specs=[main(A_CB), main(A_CC), main(A_CH), ccp, chp, ccn, chn, full((3, w)), full((1, w))],
        out_specs=pl.BlockSpec((tt, w), lambda i: (i, 0)),
        out_shape=jax.ShapeDtypeStruct((n, w), BF16),
        compiler_params=_cparams(("parallel",)),
        name="conv_mix",
    )(proj_a, proj_a, proj_a, proj_a, proj_a, proj_a, proj_a, conv_w, norm_g.reshape(1, w))


def _ssd_kernel(*refs, direction, final, nc):
    if final:
        (xs_ref, xsp_ref, xsn_ref, bc_ref, bcp_ref, bcn_ref, dt_ref, z_ref, yf_ref,
         cwx_ref, cbx_ref, cwb_ref, cbb_ref, dtb_ref, alog_ref, e_ref, hm_ref, dsk_ref, ng_ref,
         o_ref, h_ref, y_ref) = refs
    else:
        (xs_ref, xsp_ref, xsn_ref, bc_ref, bcp_ref, bcn_ref, dt_ref,
         cwx_ref, cbx_ref, cwb_ref, cbb_ref, dtb_ref, alog_ref, e_ref, hm_ref,
         o_ref, h_ref) = refs
        y_ref = o_ref
    L = SSD_CHUNK
    c = pl.program_id(1)
    chunk = c if direction == 0 else nc - 1 - c
    first = chunk == 0
    last = chunk == nc - 1

    @pl.when(c == 0)
    def _():
        h_ref[...] = jnp.zeros_like(h_ref)

    def conv_silu(x_ref, p_ref, n_ref, w_ref, b_ref):
        x = x_ref[...]
        x_prev, x_next = _shift_rows(x, p_ref[SUBLANES - 1:SUBLANES, :], n_ref[0:1, :], first, last)
        v = x_prev * w_ref[0:1, :] + x * w_ref[1:2, :] + x_next * w_ref[2:3, :] + b_ref[...]
        return v * _sigmoid(v)

    xs = conv_silu(xs_ref, xsp_ref, xsn_ref, cwx_ref, cbx_ref)
    bc = conv_silu(bc_ref, bcp_ref, bcn_ref, cwb_ref, cbb_ref)
    bm = bc[:, :LANES]
    cm = bc[:, LANES:]

    dtr = dt_ref[...] + dtb_ref[...]
    dt = jnp.maximum(dtr, 0.0) + jnp.log1p(jnp.exp(-jnp.abs(dtr)))
    dta = dt * (-jnp.exp(alog_ref[...]))

    li = lax.broadcasted_iota(jnp.int32, (L, L), 0)
    si = lax.broadcasted_iota(jnp.int32, (L, L), 1)
    valid = (si <= li) if direction == 0 else (si >= li)
    tri = jnp.where(valid, 1.0, 0.0).astype(BF16)
    acum = _dot_exact_rhs(tri, dta)
    acum_t = acum.T
    total = acum[L - 1:L, :] if direction == 0 else acum[0:1, :]
    dec_end = jnp.exp(total - acum)
    ea = jnp.exp(acum)

    e_mat = e_ref[...]
    dtx = _dot_exact_lhs(dt, e_mat)
    eax = _dot_exact_lhs(ea, e_mat)
    dex = _dot_exact_lhs(dec_end * dt, e_mat)

    xdt = xs * dtx
    xw = (xs * dex).astype(BF16)

    lane = lax.broadcasted_iota(jnp.int32, (1, LANES), 1)
    cm_b = cm.astype(BF16)
    bm_b = bm.astype(BF16)
    nt = (((1,), (1,)), ((), ()))

    h_prev = h_ref[...]
    y_off = _dot(cm_b, h_prev.astype(BF16)) * eax

    heads_per_group = SSD_HEADS // SSD_GROUPS
    for g in range(SSD_GROUPS):
        cg = jnp.where((lane // SSD_STATE) == g, cm, 0.0).astype(BF16)
        cb = lax.dot_general(cg, bm_b, nt, preferred_element_type=F32)
        for pair in range(heads_per_group // 2):
            acc = None
            for half in range(2):
                h = g * heads_per_group + pair * 2 + half
                hl = direction * SSD_HEADS + h
                seg = acum[:, hl:hl + 1] - acum_t[hl:hl + 1, :]
                w = (jnp.where(valid, jnp.exp(seg), 0.0) * cb).astype(BF16)
                c0 = (h // 2) * LANES
                rhs = jnp.where((lane // SSD_HEAD_DIM) == half, xdt[:, c0:c0 + LANES], 0.0).astype(BF16)
                part = _dot(w, rhs)
                acc = part if acc is None else acc + part
            c0 = (g * heads_per_group // 2 + pair) * LANES
            y_pair = acc + y_off[:, c0:c0 + LANES]
            if final:
                y_pair = yf_ref[:, c0:c0 + LANES] + y_pair + dsk_ref[:, c0:c0 + LANES] * xs[:, c0:c0 + LANES]
                zz = z_ref[:, c0:c0 + LANES]
                y_pair = y_pair * (zz * _sigmoid(zz))
            y_ref[:, c0:c0 + LANES] = y_pair

    cd = eax[L - 1:L, :] if direction == 0 else eax[0:1, :]
    s_new = _dot(bm.T.astype(BF16), xw)
    h_ref[...] = (cd * h_prev + s_new) * hm_ref[...]

    if final:
        gw = SSD_WIDTH // SSD_GROUPS
        for g in range(SSD_GROUPS):
            y = y_ref[:, g * gw:(g + 1) * gw]
            y = y * lax.rsqrt(jnp.mean(y * y, axis=-1, keepdims=True) + RMS_EPS)
            o_ref[:, g * gw:(g + 1) * gw] = (y * ng_ref[:, g * gw:(g + 1) * gw]).astype(o_ref.dtype)


def _ssd_pass(proj_a, z_and_yf, params, seq, direction):
    n = proj_a.shape[0]
    L = SSD_CHUNK
    nc = seq // L
    nseq = n // seq
    final = direction == 1
    cw, cb_, dtb, alog, e_mat, hmask, dsk, ng = params
    bcw = 2 * SSD_GROUPS * SSD_STATE

    def row_blk(b, c):
        ch = c if direction == 0 else nc - 1 - c
        return b * nc + ch

    def main(width, col0):
        return pl.BlockSpec((L, width), lambda b, c: (row_blk(b, c), col0 // width))

    r = L // SUBLANES
    last_blk = n // SUBLANES - 1

    def halo(width, col0):
        prev = pl.BlockSpec((SUBLANES, width), lambda b, c: (jnp.maximum(row_blk(b, c) * r - 1, 0), col0 // width))
        nxt = pl.BlockSpec((SUBLANES, width), lambda b, c: (jnp.minimum((row_blk(b, c) + 1) * r, last_blk), col0 // width))
        return prev, nxt

    xsp, xsn = halo(SSD_WIDTH, A_XS)
    bcp, bcn = halo(bcw, A_BC)
    full = lambda a: pl.BlockSpec(a.shape, lambda b, c: (0,) * a.ndim)

    in_specs = [main(SSD_WIDTH, A_XS), xsp, xsn, main(bcw, A_BC), bcp, bcn, main(LANES, A_DT)]
    args = [proj_a] * 7
    if final:
        in_specs += [main(SSD_WIDTH, A_Z), pl.BlockSpec((L, SSD_WIDTH), lambda b, c: (row_blk(b, c), 0))]
        args += [proj_a, z_and_yf]
    consts = [cw[:, :SSD_WIDTH], cb_[:, :SSD_WIDTH], cw[:, SSD_WIDTH:], cb_[:, SSD_WIDTH:], dtb, alog, e_mat, hmask]
    if final:
        consts += [dsk, ng]
    in_specs += [full(a) for a in consts]
    args += consts
    return pl.pallas_call(
        functools.partial(_ssd_kernel, direction=direction, final=final, nc=nc),
        grid=(nseq, nc),
        in_specs=in_specs,
        out_specs=pl.BlockSpec((L, SSD_WIDTH), lambda b, c: (row_blk(b, c), 0)),
        out_shape=jax.ShapeDtypeStruct((n, SSD_WIDTH), BF16 if final else F32),
        scratch_shapes=[pltpu.VMEM((LANES, SSD_WIDTH), F32)] + ([pltpu.VMEM((L, SSD_WIDTH), F32)] if final else []),
        compiler_params=_cparams(("parallel", "arbitrary")),
        name="ssd_bwd" if final else "ssd_fwd",
    )(*args)


def _ssd_consts(ssd_conv_w, ssd_conv_b, ssd_dt_bias, ssd_a_log, ssd_d, ssd_norm_g):
    pad = LANES - 2 * SSD_HEADS
    dtb = jnp.pad(ssd_dt_bias.reshape(1, -1), ((0, 0), (0, pad)))
    alog = jnp.pad(ssd_a_log.reshape(1, -1), ((0, 0), (0, pad)))
    col_head = np.arange(SSD_WIDTH) // SSD_HEAD_DIM
    e_mats = [jnp.asarray((np.arange(LANES)[:, None] == d * SSD_HEADS + col_head[None, :]), BF16) for d in range(2)]
    row_group = np.arange(LANES) // SSD_STATE
    col_group = col_head // (SSD_HEADS // SSD_GROUPS)
    hmask = jnp.asarray(row_group[:, None] == col_group[None, :], F32)
    dsk = jnp.repeat(ssd_d, SSD_HEAD_DIM).reshape(1, SSD_WIDTH)
    ng = ssd_norm_g.reshape(1, SSD_WIDTH)
    cb_ = ssd_conv_b.reshape(1, -1)
    return [(ssd_conv_w, cb_, dtb, alog, e_mats[d], hmask, dsk, ng) for d in range(2)]


def _na_kernel(q_ref, k_ref, v_ref, bias_ref, o_ref, *, rows_per_step, rows):
    rb = pl.program_id(2)
    lane = lax.broadcasted_iota(jnp.int32, (1, LANES), 1)
    nt = (((1,), (1,)), ((), ()))
    nkeys = WIN_H * GRID_W

    def body(rr, carry):
        r = rb * rows_per_step + rr
        rs = jnp.clip(r - WIN_H // 2, 0, rows - WIN_H)
        cls = r - rs
        q0 = pl.multiple_of(rr * GRID_W, GRID_W)
        k0 = pl.multiple_of(rs * GRID_W, GRID_W)
        q = q_ref[pl.ds(q0, GRID_W), :]
        kk = k_ref[pl.ds(k0, nkeys), :]
        vv = v_ref[pl.ds(k0, nkeys), :]
        outs = []
        for hh in range(2):
            qm = jnp.where((lane // NA_HEAD_DIM) == hh, q, jnp.zeros_like(q))
            s = lax.dot_general(qm, kk, nt, preferred_element_type=F32) * (NA_HEAD_DIM ** -0.5)
            s = s + bias_ref[cls, hh]
            m = jnp.max(s, axis=-1, keepdims=True)
            p = jnp.exp(s - m)
            l = jnp.sum(p, axis=-1, keepdims=True)
            outs.append(_dot(p.astype(BF16), vv) / l)
        o = jnp.where((lane // NA_HEAD_DIM) == 0, outs[0], outs[1])
        o_ref[pl.ds(q0, GRID_W), :] = o.astype(o_ref.dtype)
        return carry

    lax.fori_loop(0, rows_per_step, body, 0)


def _na_bias_table(rel_bias):
    qc = np.arange(GRID_W)
    cs = np.clip(qc - WIN_W // 2, 0, GRID_W - WIN_W)
    kc = np.arange(GRID_W)
    col_valid = (kc[None, :] >= cs[:, None]) & (kc[None, :] < cs[:, None] + WIN_W)
    dj = np.clip(kc[None, :] - qc[:, None] + WIN_W - 1, 0, 2 * WIN_W - 2)
    cls = np.arange(WIN_H)
    di = np.arange(WIN_H)[None, :] - cls[:, None] + WIN_H - 1
    b = rel_bias.astype(F32)[:, di]
    b = b[..., dj]
    b = jnp.where(col_valid[None, None, None], b, NEG_BIG)
    b = b.transpose(1, 0, 3, 2, 4)
    return b.reshape(WIN_H, NA_HEADS, GRID_W, WIN_H * GRID_W)


def _neighbourhood_attention(proj_b, bias_tab, seq, rows_per_step=8):
    n = proj_b.shape[0]
    nseq = n // seq
    rows = seq // GRID_W
    tq = rows_per_step * GRID_W
    pairs = NA_HEADS // 2
    qblocks = seq // tq
    return pl.pallas_call(
        functools.partial(_na_kernel, rows_per_step=rows_per_step, rows=rows),
        grid=(nseq, pairs, qblocks),
        in_specs=[pl.BlockSpec((tq, LANES), lambda b, p, r: (b * qblocks + r, p)),
                  pl.BlockSpec((seq, LANES), lambda b, p, r: (b, pairs + p)),
                  pl.BlockSpec((seq, LANES), lambda b, p, r: (b, 2 * pairs + p)),
                  pl.BlockSpec((WIN_H, 2, GRID_W, WIN_H * GRID_W), lambda b, p, r: (0, p, 0, 0))],
        out_specs=pl.BlockSpec((tq, LANES), lambda b, p, r: (b * qblocks + r, p)),
        out_shape=jax.ShapeDtypeStruct((n, NA_WIDTH), F32),
        compiler_params=_cparams(("parallel", "arbitrary", "arbitrary")),
        name="nbr_attn",
    )(proj_b, proj_b, proj_b, bias_tab)


def _rms_rows(y, g):
    return y * lax.rsqrt(jnp.mean(y * y, axis=-1, keepdims=True) + RMS_EPS) * g


def _outproj_kernel(*refs, router):
    if router:
        yc_ref, ys_ref, yn_ref, x_ref, w_ref, nag_ref, g_ref, b_ref, rw_ref, o_ref, idx_ref, gate_ref = refs
    else:
        yc_ref, ys_ref, yn_ref, x_ref, w_ref, nag_ref, g_ref, b_ref, o_ref = refs
    c1 = CONV_WIDTH
    c2 = CONV_WIDTH + SSD_WIDTH
    yn = _rms_rows(yn_ref[...], nag_ref[...]).astype(BF16)
    mix = _dot(yc_ref[...], w_ref[0:c1, :]) + _dot(ys_ref[...], w_ref[c1:c2, :]) + _dot(yn, w_ref[c2:, :])
    out = _layer_norm(ALPHA * x_ref[...] + mix, g_ref[...], b_ref[...])
    o_ref[...] = out
    if router:
        o1, o2, o3 = _split3(out)
        r1, r2, r3 = rw_ref[0], rw_ref[1], rw_ref[2]
        logits = (((_dot(o3, r1) + _dot(o2, r2)) + _dot(o1, r3)) + (_dot(o2, r1) + _dot(o1, r2))) + _dot(o1, r1)
        tm = logits.shape[0]
        lane = lax.broadcasted_iota(jnp.int32, (tm, LANES), 1)
        lg = jnp.where(lane < N_EXPERTS, logits, -jnp.inf)
        m1 = jnp.max(lg, axis=-1, keepdims=True)
        i1 = jnp.min(jnp.where(lg == m1, lane, LANES), axis=-1, keepdims=True)
        lg2 = jnp.where(lane == i1, -jnp.inf, lg)
        m2 = jnp.max(lg2, axis=-1, keepdims=True)
        i2 = jnp.min(jnp.where(lg2 == m2, lane, LANES), axis=-1, keepdims=True)
        e2 = jnp.exp(m2 - m1)
        den = 1.0 + e2
        idx_ref[...] = jnp.where(lane == 0, i1, jnp.where(lane == 1, i2, 0))
        gate_ref[...] = jnp.where(lane == 0, 1.0 / den, jnp.where(lane == 1, e2 / den, 0.0))


def _out_proj_ln(yc, ys, yn, x, w_out, na_g, ln_g, ln_b, router_w3=None, tm=512):
    n = x.shape[0]
    router = router_w3 is not None
    row = lambda w: pl.BlockSpec((tm, w), lambda i: (i, 0))
    full = lambda a: pl.BlockSpec(a.shape, lambda i: (0,) * a.ndim)
    consts = [w_out, na_g.reshape(1, -1), ln_g.reshape(1, -1), ln_b.reshape(1, -1)]
    if router:
        consts.append(router_w3)
    out_shape = [jax.ShapeDtypeStruct((n, D_MODEL), F32)]
    out_specs = [row(D_MODEL)]
    if router:
        out_shape += [jax.ShapeDtypeStruct((n, LANES), jnp.int32), jax.ShapeDtypeStruct((n, LANES), F32)]
        out_specs += [row(LANES), row(LANES)]
    res = pl.pallas_call(
        functools.partial(_outproj_kernel, router=router),
        grid=(n // tm,),
        in_specs=[row(CONV_WIDTH), row(SSD_WIDTH), row(NA_WIDTH), row(D_MODEL)] + [full(a) for a in consts],
        out_specs=out_specs,
        out_shape=out_shape,
        compiler_params=_cparams(("parallel",)),
        name="out_proj_router" if router else "out_proj",
    )(yc, ys, yn, x, *consts)
    return res if router else res[0]


def _ffn_kernel(be_ref, bv_ref, x_ref, wg_ref, wu_ref, wd_ref, o_ref, xb_ref):
    i = pl.program_id(0)
    j = pl.program_id(1)
    valid = bv_ref[i] != 0

    @pl.when(j == 0)
    def _():
        xb_ref[...] = x_ref[...].astype(BF16)
        o_ref[...] = jnp.zeros_like(o_ref)

    @pl.when(valid)
    def _():
        xb = xb_ref[...]
        g = _dot(xb, wg_ref[0])
        u = _dot(xb, wu_ref[0])
        h = ((g * _sigmoid(g)) * u).astype(BF16)
        o_ref[...] += _dot(h, wd_ref[0])


def _ffn_blocks(x, block_e, block_valid, w_gate, w_up, w_down, tm, tf=FF_TILE):
    n, d = x.shape
    f = w_gate.shape[2]
    nf = f // tf
    col = lambda i, j, be, bv: (be[i], 0, jnp.where(bv[i] != 0, j, nf - 1))
    rowj = lambda i, j, be, bv: (be[i], jnp.where(bv[i] != 0, j, nf - 1), 0)
    grid_spec = pltpu.PrefetchScalarGridSpec(
        num_scalar_prefetch=2,
        grid=(n // tm, nf),
        in_specs=[pl.BlockSpec((tm, d), lambda i, j, be, bv: (i, 0)),
                  pl.BlockSpec((1, d, tf), col), pl.BlockSpec((1, d, tf), col), pl.BlockSpec((1, tf, d), rowj)],
        out_specs=pl.BlockSpec((tm, d), lambda i, j, be, bv: (i, 0)),
        scratch_shapes=[pltpu.VMEM((tm, d), BF16)],
    )
    return pl.pallas_call(
        _ffn_kernel,
        grid_spec=grid_spec,
        out_shape=jax.ShapeDtypeStruct((n, d), F32),
        compiler_params=_cparams(("parallel", "arbitrary")),
        name="swiglu_blocks",
    )(block_e, block_valid, x, w_gate, w_up, w_down)


def _add_ln_kernel(x_ref, f_ref, g_ref, b_ref, o_ref):
    o_ref[...] = _layer_norm(ALPHA * x_ref[...] + f_ref[...], g_ref[...], b_ref[...])


def _add_ln(x, f, g, b, tm=512):
    n, d = x.shape
    row = pl.BlockSpec((tm, d), lambda i: (i, 0))
    vec = pl.BlockSpec((1, d), lambda i: (0, 0))
    return pl.pallas_call(
        _add_ln_kernel, grid=(n // tm,), in_specs=[row, row, vec, vec], out_specs=row,
        out_shape=jax.ShapeDtypeStruct((n, d), F32), compiler_params=_cparams(("parallel",)), name="add_ln",
    )(x, f, g.reshape(1, d), b.reshape(1, d))


def _combine_ln_kernel(x_ref, o0_ref, o1_ref, gate_ref, g_ref, b_ref, o_ref):
    f = o0_ref[...] * gate_ref[:, 0:1] + o1_ref[...] * gate_ref[:, 1:2]
    o_ref[...] = _layer_norm(ALPHA * x_ref[...] + f, g_ref[...], b_ref[...])


def _combine_ln(x, o0, o1, gates, g, b, tm=512):
    n, d = x.shape
    row = pl.BlockSpec((tm, d), lambda i: (i, 0))
    vec = pl.BlockSpec((1, d), lambda i: (0, 0))
    return pl.pallas_call(
        _combine_ln_kernel, grid=(n // tm,),
        in_specs=[row, row, row, pl.BlockSpec((tm, LANES), lambda i: (i, 0)), vec, vec], out_specs=row,
        out_shape=jax.ShapeDtypeStruct((n, d), F32), compiler_params=_cparams(("parallel",)), name="combine_ln",
    )(x, o0, o1, gates, g.reshape(1, d), b.reshape(1, d))


def _gather_kernel(idx_ref, src_ref, o_ref, sem, *, tr):
    def row_copy(r):
        return pltpu.make_async_copy(src_ref.at[pl.ds(idx_ref[0, 0, r], 1), :], o_ref.at[pl.ds(r, 1), :], sem)

    def issue(r, carry):
        row_copy(r).start()
        return carry

    lax.fori_loop(0, tr, issue, 0)
    pltpu.make_async_copy(src_ref.at[pl.ds(0, tr), :], o_ref, sem).wait()


def _gather_rows(src, idx, tr=256):
    m = idx.shape[0]
    d = src.shape[1]
    idx3 = idx.reshape(m // tr, 1, tr)
    return pl.pallas_call(
        functools.partial(_gather_kernel, tr=tr),
        grid=(m // tr,),
        in_specs=[pl.BlockSpec((1, 1, tr), lambda i: (i, 0, 0), memory_space=pltpu.SMEM),
                  pl.BlockSpec(memory_space=pl.ANY)],
        out_specs=pl.BlockSpec((tr, d), lambda i: (i, 0)),
        out_shape=jax.ShapeDtypeStruct((m, d), src.dtype),
        scratch_shapes=[pltpu.SemaphoreType.DMA(())],
        compiler_params=_cparams(("arbitrary",)),
        name="gather_rows",
    )(idx3, src)


def _routing_tables(idx, n, tm):
    flat_e = idx[:, :TOP_K].reshape(-1)
    onehot = (flat_e[:, None] == jnp.arange(N_EXPERTS, dtype=jnp.int32)[None, :]).astype(jnp.int32)
    incl = jnp.cumsum(onehot, axis=0)
    rank = jnp.sum((incl - onehot) * onehot, axis=1)
    counts = incl[-1]
    padded = ((counts + tm - 1) // tm) * tm
    pend = jnp.cumsum(padded)
    pstart = pend - padded
    dest = pstart[flat_e] + rank
    n_blocks = -(-(n * TOP_K) // tm) + N_EXPERTS
    p_len = n_blocks * tm
    flat_tok = jnp.arange(n * TOP_K, dtype=jnp.int32) // TOP_K
    pad_tok = jnp.zeros((p_len,), jnp.int32).at[dest].set(flat_tok)
    starts = jnp.arange(n_blocks, dtype=jnp.int32) * tm
    block_valid = (starts < pend[-1]).astype(jnp.int32)
    block_e = jnp.minimum(jnp.searchsorted(pend, starts, side='right'), N_EXPERTS - 1).astype(jnp.int32)
    last_e = block_e[jnp.maximum(jnp.sum(block_valid) - 1, 0)]
    block_e = jnp.where(block_valid != 0, block_e, last_e)
    dest2 = dest.reshape(n, TOP_K)
    return pad_tok, block_e, block_valid, dest2[:, 0], dest2[:, 1]


def _prep_w_in(w):
    cb, cc, ch = w[:, 0:512], w[:, 512:1024], w[:, 1024:1536]
    z = w[:, 1536:2560]
    xs, bc = w[:, 2560:3584], w[:, 3584:3840]
    dt = w[:, 3840:3872]
    qkv = w[:, 3872:5408]
    pad = jnp.zeros((D_MODEL, A_WIDTH - (A_DT + 2 * SSD_HEADS)), w.dtype)
    wa = jnp.concatenate([z, xs, cb, cc, ch, bc, dt, pad], axis=1)
    return wa.astype(BF16), qkv.astype(BF16)


def _trunk(x, seq, p):
    n = x.shape[0]
    for l in range(DEPTH):
        wa, wb = _prep_w_in(p['w_in'][l])
        proj_a = _matmul(x, wa, F32, 1024, 512, "in_proj_a")
        proj_b = _matmul(x, wb, BF16, 1024, 512, "in_proj_b")
        y_conv = _conv_mix(proj_a, p['conv_w'][l], p['conv_norm_g'][l], seq)
        cf, cbw = _ssd_consts(p['ssd_conv_w'][l], p['ssd_conv_b'][l], p['ssd_dt_bias'][l], p['ssd_a_log'][l],
                              p['ssd_d'][l], p['ssd_norm_g'][l])
        y_fwd = _ssd_pass(proj_a, None, cf, seq, 0)
        y_ssd = _ssd_pass(proj_a, y_fwd, cbw, seq, 1)
        y_na = _neighbourhood_attention(proj_b, _na_bias_table(p['na_rel_bias'][l]), seq)
        w_out = p['w_out'][l].astype(BF16)
        if l % 2 == 0:
            x = _out_proj_ln(y_conv, y_ssd, y_na, x, w_out, p['na_norm_g'][l], p['ln_mix_g'][l], p['ln_mix_b'][l])
            padf = ((0, 0), (0, D_FF_PAD - D_FF))
            wg = jnp.pad(p['ffn_w_gate'][l // 2], padf).astype(BF16)[None]
            wu = jnp.pad(p['ffn_w_up'][l // 2], padf).astype(BF16)[None]
            wd = jnp.pad(p['ffn_w_down'][l // 2], ((0, D_FF_PAD - D_FF), (0, 0))).astype(BF16)[None]
            nb = n // MOE_TM
            f = _ffn_blocks(x, jnp.zeros((nb,), jnp.int32), jnp.ones((nb,), jnp.int32), wg, wu, wd, MOE_TM)
            x = _add_ln(x, f, p['ln_ffn_g'][l], p['ln_ffn_b'][l])
        else:
            rw = jnp.pad(p['router_w'][l // 2], ((0, 0), (0, LANES - N_EXPERTS)))
            r1 = rw.astype(BF16)
            r2 = (rw - r1.astype(F32)).astype(BF16)
            r3 = (rw - r1.astype(F32) - r2.astype(F32)).astype(BF16)
            x, idx, gates = _out_proj_ln(y_conv, y_ssd, y_na, x, w_out, p['na_norm_g'][l], p['ln_mix_g'][l],
                                         p['ln_mix_b'][l], jnp.stack([r1, r2, r3]))
            pad_tok, block_e, block_valid, d0, d1 = _routing_tables(idx, n, MOE_TM)
            xs = _gather_rows(x, pad_tok)
            outs = _ffn_blocks(xs, block_e, block_valid, p['moe_w_gate'][l // 2].astype(BF16),
                               p['moe_w_up'][l // 2].astype(BF16), p['moe_w_down'][l // 2].astype(BF16), MOE_TM)
            o0 = _gather_rows(outs, d0)
            o1 = _gather_rows(outs, d1)
            x = _combine_ln(x, o0, o1, gates, p['ln_ffn_g'][l], p['ln_ffn_b'][l])
    return x


def kernel(x_prompt, x_sample, w_in, conv_w, conv_norm_g, ssd_conv_w, ssd_conv_b, ssd_dt_bias, ssd_a_log, ssd_d,
           ssd_norm_g, na_rel_bias, na_norm_g, w_out, ln_mix_g, ln_mix_b, ln_ffn_g, ln_ffn_b, ffn_w_gate, ffn_w_up,
           ffn_w_down, router_w, moe_w_gate, moe_w_up, moe_w_down):
    p = dict(w_in=w_in, conv_w=conv_w, conv_norm_g=conv_norm_g, ssd_conv_w=ssd_conv_w, ssd_conv_b=ssd_conv_b,
             ssd_dt_bias=ssd_dt_bias, ssd_a_log=ssd_a_log, ssd_d=ssd_d, ssd_norm_g=ssd_norm_g,
             na_rel_bias=na_rel_bias, na_norm_g=na_norm_g, w_out=w_out, ln_mix_g=ln_mix_g, ln_mix_b=ln_mix_b,
             ln_ffn_g=ln_ffn_g, ln_ffn_b=ln_ffn_b, ffn_w_gate=ffn_w_gate, ffn_w_up=ffn_w_up, ffn_w_down=ffn_w_down,
             router_w=router_w, moe_w_gate=moe_w_gate, moe_w_up=moe_w_up, moe_w_down=moe_w_down)
    bp, seq, d = x_prompt.shape
    bs, seq_s, _ = x_sample.shape
    assert seq == seq_s, "both request groups must share one sequence length"
    x = jnp.concatenate([x_prompt.reshape(bp * seq, d), x_sample.reshape(bs * seq, d)], axis=0)
    y = _trunk(x, seq, p)
    return y[:bp * seq].reshape(bp, seq, d), y[bp * seq:].reshape(bs, seq, d)
```

```python
import functools

import numpy as np
import jax
import jax.numpy as jnp
from jax import lax
from jax.experimental import pallas as pl
from jax.experimental.pallas import tpu as pltpu

F32 = jnp.float32
BF16 = jnp.bfloat16

D_MODEL = 2048
DEPTH = 2
GRID_W = 64
CONV_WIDTH = 512
SSD_HEADS = 16
SSD_HEAD_DIM = 64
SSD_WIDTH = 1024
SSD_GROUPS = 2
SSD_STATE = 64
SSD_CHUNK = 128
NA_HEADS = 8
NA_HEAD_DIM = 64
NA_WIDTH = 512
WIN_H = 8
WIN_W = 16
D_FF = 5504
N_EXPERTS = 8
TOP_K = 2
D_FF_EXPERT = 7168
ALPHA = (2 * DEPTH) ** 0.25
LN_EPS = 1e-5
RMS_EPS = 1e-6

LANES = 128
SUBLANES = 8
NEG_BIG = -1e30

A_Z, A_XS, A_CB, A_CC, A_CH, A_BC, A_DT, A_WIDTH = 0, 1024, 2048, 2560, 3072, 3584, 3840, 4096
FF_TILE = 512
D_FF_PAD = 5632
MOE_TM = 1024
VMEM_LIMIT = 56 * 1024 * 1024


def _cparams(sem, vmem=VMEM_LIMIT):
    return pltpu.CompilerParams(dimension_semantics=sem, vmem_limit_bytes=vmem)


def _sigmoid(x):
    return 1.0 / (1.0 + jnp.exp(-x))


def _split3(x):
    x1 = x.astype(BF16)
    r1 = x - x1.astype(F32)
    x2 = r1.astype(BF16)
    r2 = r1 - x2.astype(F32)
    return x1, x2, r2.astype(BF16)


def _dot(a, b):
    return jnp.dot(a, b, preferred_element_type=F32)


def _dot_exact_lhs(x, m):
    x1, x2, x3 = _split3(x)
    return (_dot(x3, m) + _dot(x2, m)) + _dot(x1, m)


def _dot_exact_rhs(m, x):
    x1, x2, x3 = _split3(x)
    return (_dot(m, x3) + _dot(m, x2)) + _dot(m, x1)


def _layer_norm(v, g, b):
    mu = jnp.mean(v, axis=-1, keepdims=True)
    c = v - mu
    var = jnp.mean(c * c, axis=-1, keepdims=True)
    return c * lax.rsqrt(var + LN_EPS) * g + b


def _row_parts_specs(parts, tm, width, grid_rank):
    if len(parts) == 1:
        return [pl.BlockSpec((tm, width), (lambda i: (i, 0)) if grid_rank == 1 else (lambda i, j: (i, 0)))]
    first_tiles = parts[0].shape[0] // tm
    if grid_rank == 1:
        return [pl.BlockSpec((tm, width), lambda i: (jnp.minimum(i, first_tiles - 1), 0)),
                pl.BlockSpec((tm, width), lambda i: (jnp.maximum(i - first_tiles, 0), 0))]
    return [pl.BlockSpec((tm, width), lambda i, j: (jnp.minimum(i, first_tiles - 1), 0)),
            pl.BlockSpec((tm, width), lambda i, j: (jnp.maximum(i - first_tiles, 0), 0))]


def _read_row_parts(x_refs, first_tiles):
    x = x_refs[0][...]
    if len(x_refs) == 2:
        x = jnp.where(pl.program_id(0) < first_tiles, x, x_refs[1][...])
    return x


def _inproj_kernel(*refs, n_parts, first_tiles, a_blocks):
    x_refs = refs[:n_parts]
    w_ref, oa_ref, ob_ref, xb_ref = refs[n_parts:]
    j = pl.program_id(1)

    @pl.when(j == 0)
    def _():
        xb_ref[...] = _read_row_parts(x_refs, first_tiles).astype(BF16)

    res = _dot(xb_ref[...], w_ref[...])

    @pl.when(j < a_blocks)
    def _():
        oa_ref[...] = res

    @pl.when(j >= a_blocks)
    def _():
        ob_ref[...] = res.astype(ob_ref.dtype)


def _in_proj(x_parts, w, tm=1024, tn=512):
    n = sum(a.shape[0] for a in x_parts)
    k = x_parts[0].shape[1]
    a_blocks = A_WIDTH // tn
    b_width = w.shape[1] - A_WIDTH
    first_tiles = x_parts[0].shape[0] // tm
    return pl.pallas_call(
        functools.partial(_inproj_kernel, n_parts=len(x_parts), first_tiles=first_tiles, a_blocks=a_blocks),
        grid=(n // tm, w.shape[1] // tn),
        in_specs=_row_parts_specs(x_parts, tm, k, 2) + [pl.BlockSpec((k, tn), lambda i, j: (0, j))],
        out_specs=[pl.BlockSpec((tm, tn), lambda i, j: (i, jnp.minimum(j, a_blocks - 1))),
                   pl.BlockSpec((tm, tn), lambda i, j: (i, jnp.maximum(j - a_blocks, 0)))],
        out_shape=[jax.ShapeDtypeStruct((n, A_WIDTH), F32), jax.ShapeDtypeStruct((n, b_width), BF16)],
        scratch_shapes=[pltpu.VMEM((tm, k), BF16)],
        compiler_params=_cparams(("parallel", "arbitrary")),
        name="in_proj",
    )(*x_parts, w)


def _shift_rows(x, prev_row, next_row, first, last):
    n = x.shape[0]
    row = lax.broadcasted_iota(jnp.int32, (n, 1), 0)
    prev_row = jnp.where(first, 0.0, prev_row)
    next_row = jnp.where(last, 0.0, next_row)
    x_prev = jnp.where(row == 0, prev_row, pltpu.roll(x, 1, 0))
    x_next = jnp.where(row == n - 1, next_row, pltpu.roll(x, n - 1, 0))
    return x_prev, x_next


def _convmix_kernel(cb_ref, cc_ref, ch_ref, ccp_ref, chp_ref, ccn_ref, chn_ref, w_ref, g_ref, o_ref, *, tiles_per_seq):
    i = pl.program_id(0)
    first = (i % tiles_per_seq) == 0
    last = (i % tiles_per_seq) == tiles_per_seq - 1
    u = cc_ref[...] * ch_ref[...]
    up = ccp_ref[SUBLANES - 1:SUBLANES, :] * chp_ref[SUBLANES - 1:SUBLANES, :]
    un = ccn_ref[0:1, :] * chn_ref[0:1, :]
    u_prev, u_next = _shift_rows(u, up, un, first, last)
    conv = u_prev * w_ref[0:1, :] + u * w_ref[1:2, :] + u_next * w_ref[2:3, :]
    y = cb_ref[...] * conv
    y = y * lax.rsqrt(jnp.mean(y * y, axis=-1, keepdims=True) + RMS_EPS)
    o_ref[...] = (y * g_ref[...]).astype(o_ref.dtype)


def _halo_specs(tt, width, col0, n_rows):
    cb = col0 // width
    r = tt // SUBLANES
    last_blk = n_rows // SUBLANES - 1
    prev = pl.BlockSpec((SUBLANES, width), lambda i: (jnp.maximum(i * r - 1, 0), cb))
    nxt = pl.BlockSpec((SUBLANES, width), lambda i: (jnp.minimum((i + 1) * r, last_blk), cb))
    return prev, nxt


def _conv_mix(proj_a, conv_w, norm_g, seq, tt=512):
    n = proj_a.shape[0]
    w = CONV_WIDTH
    main = lambda c0: pl.BlockSpec((tt, w), lambda i: (i, c0 // w))
    ccp, ccn = _halo_specs(tt, w, A_CC, n)
    chp, chn = _halo_specs(tt, w, A_CH, n)
    full = lambda shape: pl.BlockSpec(shape, lambda i: (0, 0))
    return pl.pallas_call(
        functools.partial(_convmix_kernel, tiles_per_seq=seq // tt),
        grid=(n // tt,),
        in_specs=[main(A_CB), main(A_CC), main(A_CH), ccp, chp, ccn, chn, full((3, w)), full((1, w))],
        out_specs=pl.BlockSpec((tt, w), lambda i: (i, 0)),
        out_shape=jax.ShapeDtypeStruct((n, w), BF16),
        compiler_params=_cparams(("parallel",)),
        name="conv_mix",
    )(proj_a, proj_a, proj_a, proj_a, proj_a, proj_a, proj_a, conv_w, norm_g.reshape(1, w))


def _ssd_kernel(*refs, direction, final, nc):
    if final:
        (xs_ref, xsp_ref, xsn_ref, bc_ref, bcp_ref, bcn_ref, dt_ref, z_ref, yf_ref,
         cwx_ref, cbx_ref, cwb_ref, cbb_ref, dtb_ref, alog_ref, e_ref, hm_ref, dsk_ref, ng_ref,
         o_ref, h_ref, y_ref) = refs
    else:
        (xs_ref, xsp_ref, xsn_ref, bc_ref, bcp_ref, bcn_ref, dt_ref,
         cwx_ref, cbx_ref, cwb_ref, cbb_ref, dtb_ref, alog_ref, e_ref, hm_ref,
         o_ref, h_ref) = refs
        y_ref = o_ref
    L = SSD_CHUNK
    c = pl.program_id(1)
    chunk = c if direction == 0 else nc - 1 - c
    first = chunk == 0
    last = chunk == nc - 1

    @pl.when(c == 0)
    def _():
        h_ref[...] = jnp.zeros_like(h_ref)

    def conv_silu(x_ref, p_ref, n_ref, w_ref, b_ref):
        x = x_ref[...]
        x_prev, x_next = _shift_rows(x, p_ref[SUBLANES - 1:SUBLANES, :], n_ref[0:1, :], first, last)
        v = x_prev * w_ref[0:1, :] + x * w_ref[1:2, :] + x_next * w_ref[2:3, :] + b_ref[...]
        return v * _sigmoid(v)

    xs = conv_silu(xs_ref, xsp_ref, xsn_ref, cwx_ref, cbx_ref)
    bc = conv_silu(bc_ref, bcp_ref, bcn_ref, cwb_ref, cbb_ref)
    bm = bc[:, :LANES]
    cm = bc[:, LANES:]

    dtr = dt_ref[...] + dtb_ref[...]
    dt = jnp.maximum(dtr, 0.0) + jnp.log1p(jnp.exp(-jnp.abs(dtr)))
    dta = dt * (-jnp.exp(alog_ref[...]))

    li = lax.broadcasted_iota(jnp.int32, (L, L), 0)
    si = lax.broadcasted_iota(jnp.int32, (L, L), 1)
    valid = (si <= li) if direction == 0 else (si >= li)
    tri = jnp.where(valid, 1.0, 0.0).astype(BF16)
    acum = _dot_exact_rhs(tri, dta)
    acum_t = acum.T
    total = acum[L - 1:L, :] if direction == 0 else acum[0:1, :]
    dec_end = jnp.exp(total - acum)
    ea = jnp.exp(acum)

    e_mat = e_ref[...]
    dtx = _dot_exact_lhs(dt, e_mat)
    eax = _dot_exact_lhs(ea, e_mat)
    dex = _dot_exact_lhs(dec_end * dt, e_mat)

    xdt = xs * dtx
    xw = (xs * dex).astype(BF16)

    lane = lax.broadcasted_iota(jnp.int32, (1, LANES), 1)
    cm_b = cm.astype(BF16)
    bm_b = bm.astype(BF16)
    nt = (((1,), (1,)), ((), ()))

    h_prev = h_ref[...]
    y_off = _dot(cm_b, h_prev.astype(BF16)) * eax

    heads_per_group = SSD_HEADS // SSD_GROUPS
    for g in range(SSD_GROUPS):
        cg = jnp.where((lane // SSD_STATE) == g, cm, 0.0).astype(BF16)
        cb = lax.dot_general(cg, bm_b, nt, preferred_element_type=F32)
        for pair in range(heads_per_group // 2):
            acc = None
            for half in range(2):
                h = g * heads_per_group + pair * 2 + half
                hl = direction * SSD_HEADS + h
                seg = acum[:, hl:hl + 1] - acum_t[hl:hl + 1, :]
                w = (jnp.where(valid, jnp.exp(seg), 0.0) * cb).astype(BF16)
                c0 = (h // 2) * LANES
                rhs = jnp.where((lane // SSD_HEAD_DIM) == half, xdt[:, c0:c0 + LANES], 0.0).astype(BF16)
                part = _dot(w, rhs)
                acc = part if acc is None else acc + part
            c0 = (g * heads_per_group // 2 + pair) * LANES
            y_pair = acc + y_off[:, c0:c0 + LANES]
            if final:
                y_pair = yf_ref[:, c0:c0 + LANES] + y_pair + dsk_ref[:, c0:c0 + LANES] * xs[:, c0:c0 + LANES]
                zz = z_ref[:, c0:c0 + LANES]
                y_pair = y_pair * (zz * _sigmoid(zz))
            y_ref[:, c0:c0 + LANES] = y_pair

    cd = eax[L - 1:L, :] if direction == 0 else eax[0:1, :]
    s_new = _dot(bm.T.astype(BF16), xw)
    h_ref[...] = (cd * h_prev + s_new) * hm_ref[...]

    if final:
        gw = SSD_WIDTH // SSD_GROUPS
        for g in range(SSD_GROUPS):
            y = y_ref[:, g * gw:(g + 1) * gw]
            y = y * lax.rsqrt(jnp.mean(y * y, axis=-1, keepdims=True) + RMS_EPS)
            o_ref[:, g * gw:(g + 1) * gw] = (y * ng_ref[:, g * gw:(g + 1) * gw]).astype(o_ref.dtype)


def _ssd_pass(proj_a, z_and_yf, params, seq, direction):
    n = proj_a.shape[0]
    L = SSD_CHUNK
    nc = seq // L
    nseq = n // seq
    final = direction == 1
    cw, cb_, dtb, alog, e_mat, hmask, dsk, ng = params
    bcw = 2 * SSD_GROUPS * SSD_STATE

    def row_blk(b, c):
        ch = c if direction == 0 else nc - 1 - c
        return b * nc + ch

    def main(width, col0):
        return pl.BlockSpec((L, width), lambda b, c: (row_blk(b, c), col0 // width))

    r = L // SUBLANES
    last_blk = n // SUBLANES - 1

    def halo(width, col0):
        prev = pl.BlockSpec((SUBLANES, width), lambda b, c: (jnp.maximum(row_blk(b, c) * r - 1, 0), col0 // width))
        nxt = pl.BlockSpec((SUBLANES, width), lambda b, c: (jnp.minimum((row_blk(b, c) + 1) * r, last_blk), col0 // width))
        return prev, nxt

    xsp, xsn = halo(SSD_WIDTH, A_XS)
    bcp, bcn = halo(bcw, A_BC)
    full = lambda a: pl.BlockSpec(a.shape, lambda b, c: (0,) * a.ndim)

    in_specs = [main(SSD_WIDTH, A_XS), xsp, xsn, main(bcw, A_BC), bcp, bcn, main(LANES, A_DT)]
    args = [proj_a] * 7
    if final:
        in_specs += [main(SSD_WIDTH, A_Z), pl.BlockSpec((L, SSD_WIDTH), lambda b, c: (row_blk(b, c), 0))]
        args += [proj_a, z_and_yf]
    consts = [cw[:, :SSD_WIDTH], cb_[:, :SSD_WIDTH], cw[:, SSD_WIDTH:], cb_[:, SSD_WIDTH:], dtb, alog, e_mat, hmask]
    if final:
        consts += [dsk, ng]
    in_specs += [full(a) for a in consts]
    args += consts
    return pl.pallas_call(
        functools.partial(_ssd_kernel, direction=direction, final=final, nc=nc),
        grid=(nseq, nc),
        in_specs=in_specs,
        out_specs=pl.BlockSpec((L, SSD_WIDTH), lambda b, c: (row_blk(b, c), 0)),
        out_shape=jax.ShapeDtypeStruct((n, SSD_WIDTH), BF16 if final else F32),
        scratch_shapes=[pltpu.VMEM((LANES, SSD_WIDTH), F32)] + ([pltpu.VMEM((L, SSD_WIDTH), F32)] if final else []),
        compiler_params=_cparams(("parallel", "arbitrary")),
        name="ssd_bwd" if final else "ssd_fwd",
    )(*args)


def _ssd_consts(ssd_conv_w, ssd_conv_b, ssd_dt_bias, ssd_a_log, ssd_d, ssd_norm_g):
    pad = LANES - 2 * SSD_HEADS
    dtb = jnp.pad(ssd_dt_bias.reshape(1, -1), ((0, 0), (0, pad)))
    alog = jnp.pad(ssd_a_log.reshape(1, -1), ((0, 0), (0, pad)))
    col_head = np.arange(SSD_WIDTH) // SSD_HEAD_DIM
    e_mats = [jnp.asarray((np.arange(LANES)[:, None] == d * SSD_HEADS + col_head[None, :]), BF16) for d in range(2)]
    row_group = np.arange(LANES) // SSD_STATE
    col_group = col_head // (SSD_HEADS // SSD_GROUPS)
    hmask = jnp.asarray(row_group[:, None] == col_group[None, :], F32)
    dsk = jnp.repeat(ssd_d, SSD_HEAD_DIM).reshape(1, SSD_WIDTH)
    ng = ssd_norm_g.reshape(1, SSD_WIDTH)
    cb_ = ssd_conv_b.reshape(1, -1)
    return [(ssd_conv_w, cb_, dtb, alog, e_mats[d], hmask, dsk, ng) for d in range(2)]


NA_QROWS = 4
NA_KROWS = NA_QROWS + WIN_H


def _na_kernel(kb_ref, cls_ref, q_ref, k_ref, v_ref, bias_ref, o_ref, *, blocks_per_step):
    step = pl.program_id(2)
    lane = lax.broadcasted_iota(jnp.int32, (1, LANES), 1)
    nt = (((1,), (1,)), ((), ()))
    nq = NA_QROWS * GRID_W
    nk = NA_KROWS * GRID_W
    for j in range(blocks_per_step):
        blk = step * blocks_per_step + j
        k0 = pl.multiple_of(kb_ref[blk] * GRID_W, GRID_W)
        cls = cls_ref[blk]
        q = q_ref[j * nq:(j + 1) * nq, :] * (NA_HEAD_DIM ** -0.5)
        kk = k_ref[pl.ds(k0, nk), :]
        vv = v_ref[pl.ds(k0, nk), :]
        outs = []
        for hh in range(2):
            qm = jnp.where((lane // NA_HEAD_DIM) == hh, q, jnp.zeros_like(q))
            s = lax.dot_general(qm, kk, nt, preferred_element_type=F32) + bias_ref[cls, hh]
            m = jnp.max(s, axis=-1, keepdims=True)
            p = jnp.exp(s - m)
            l = jnp.sum(p, axis=-1, keepdims=True)
            outs.append(_dot(p.astype(BF16), vv) / l)
        o = jnp.where((lane // NA_HEAD_DIM) == 0, outs[0], outs[1])
        o_ref[j * nq:(j + 1) * nq, :] = o.astype(o_ref.dtype)


def _na_tables(rel_bias, rows):
    qc = np.arange(GRID_W)
    cs = np.clip(qc - WIN_W // 2, 0, GRID_W - WIN_W)
    kc = np.arange(GRID_W)
    col_valid = (kc[None, :] >= cs[:, None]) & (kc[None, :] < cs[:, None] + WIN_W)
    dj = np.clip(kc[None, :] - qc[:, None] + WIN_W - 1, 0, 2 * WIN_W - 2)
    nblk = rows // NA_QROWS
    kb = np.clip(np.arange(nblk) * NA_QROWS - WIN_H // 2, 0, rows - NA_KROWS)
    patterns, cls = [], []
    for b in range(nblk):
        r = b * NA_QROWS + np.arange(NA_QROWS)
        rs = np.clip(r - WIN_H // 2, 0, rows - WIN_H)
        key = (tuple(rs - kb[b]), int(kb[b] - b * NA_QROWS))
        if key not in patterns:
            patterns.append(key)
        cls.append(patterns.index(key))
    tabs = []
    for off, shift in patterns:
        kr = np.arange(NA_KROWS)[None, :]
        i = np.arange(NA_QROWS)[:, None]
        rel = kr - np.asarray(off)[:, None]
        row_valid = (rel >= 0) & (rel < WIN_H)
        di = np.clip(shift + kr - i + WIN_H - 1, 0, 2 * WIN_H - 2)
        b = rel_bias.astype(F32)[:, di]
        b = b[..., dj]
        mask = row_valid[:, :, None, None] & col_valid[None, None]
        b = jnp.where(mask[None], b, NEG_BIG).transpose(0, 1, 3, 2, 4)
        tabs.append(b.reshape(NA_HEADS, NA_QROWS * GRID_W, NA_KROWS * GRID_W))
    return jnp.asarray(kb, jnp.int32), jnp.asarray(cls, jnp.int32), jnp.stack(tabs)


def _neighbourhood_attention(proj_b, rel_bias, seq, blocks_per_step=2):
    n = proj_b.shape[0]
    nseq = n // seq
    rows = seq // GRID_W
    kb, cls, bias_tab = _na_tables(rel_bias, rows)
    tq = blocks_per_step * NA_QROWS * GRID_W
    pairs = NA_HEADS // 2
    steps = seq // tq
    ncls = bias_tab.shape[0]
    grid_spec = pltpu.PrefetchScalarGridSpec(
        num_scalar_prefetch=2,
        grid=(nseq, pairs, steps),
        in_specs=[pl.BlockSpec((tq, LANES), lambda b, p, r, kb_, cls_: (b * steps + r, p)),
                  pl.BlockSpec((seq, LANES), lambda b, p, r, kb_, cls_: (b, pairs + p)),
                  pl.BlockSpec((seq, LANES), lambda b, p, r, kb_, cls_: (b, 2 * pairs + p)),
                  pl.BlockSpec((ncls, 2) + bias_tab.shape[2:], lambda b, p, r, kb_, cls_: (0, p, 0, 0))],
        out_specs=pl.BlockSpec((tq, LANES), lambda b, p, r, kb_, cls_: (b * steps + r, p)),
    )
    return pl.pallas_call(
        functools.partial(_na_kernel, blocks_per_step=blocks_per_step),
        grid_spec=grid_spec,
        out_shape=jax.ShapeDtypeStruct((n, NA_WIDTH), F32),
        compiler_params=_cparams(("parallel", "arbitrary", "arbitrary")),
        name="nbr_attn",
    )(kb, cls, proj_b, proj_b, proj_b, bias_tab)


def _rms_rows(y, g):
    return y * lax.rsqrt(jnp.mean(y * y, axis=-1, keepdims=True) + RMS_EPS) * g


def _outproj_kernel(*refs, router, n_parts, first_tiles):
    yc_ref, ys_ref, yn_ref = refs[:3]
    x_refs = refs[3:3 + n_parts]
    rest = refs[3 + n_parts:]
    if router:
        w_ref, nag_ref, g_ref, b_ref, rw_ref, o_ref, idx_ref, gate_ref = rest
    else:
        w_ref, nag_ref, g_ref, b_ref, o_ref = rest
    c1 = CONV_WIDTH
    c2 = CONV_WIDTH + SSD_WIDTH
    yn = _rms_rows(yn_ref[...], nag_ref[...]).astype(BF16)
    mix = _dot(yc_ref[...], w_ref[0:c1, :]) + _dot(ys_ref[...], w_ref[c1:c2, :]) + _dot(yn, w_ref[c2:, :])
    out = _layer_norm(ALPHA * _read_row_parts(x_refs, first_tiles) + mix, g_ref[...], b_ref[...])
    o_ref[...] = out
    if router:
        o1, o2, _ = _split3(out)
        t1 = _dot(o1, rw_ref[...])
        t2 = _dot(o2, rw_ref[...])
        sh1 = LANES - N_EXPERTS
        sh2 = LANES - 2 * N_EXPERTS
        logits = ((pltpu.roll(t2, sh1, 1) + pltpu.roll(t1, sh2, 1)) + (t2 + pltpu.roll(t1, sh1, 1))) + t1
        tm = logits.shape[0]
        lane = lax.broadcasted_iota(jnp.int32, (tm, LANES), 1)
        lg = jnp.where(lane < N_EXPERTS, logits, -jnp.inf)
        m1 = jnp.max(lg, axis=-1, keepdims=True)
        i1 = jnp.min(jnp.where(lg == m1, lane, LANES), axis=-1, keepdims=True)
        lg2 = jnp.where(lane == i1, -jnp.inf, lg)
        m2 = jnp.max(lg2, axis=-1, keepdims=True)
        i2 = jnp.min(jnp.where(lg2 == m2, lane, LANES), axis=-1, keepdims=True)
        e2 = jnp.exp(m2 - m1)
        den = 1.0 + e2
        idx_ref[...] = jnp.where(lane == 0, i1, jnp.where(lane == 1, i2, 0))
        gate_ref[...] = jnp.where(lane == 0, 1.0 / den, jnp.where(lane == 1, e2 / den, 0.0))


def _out_proj_ln(yc, ys, yn, x_parts, w_out, na_g, ln_g, ln_b, router_w=None, tm=512):
    n = yc.shape[0]
    router = router_w is not None
    row = lambda w: pl.BlockSpec((tm, w), lambda i: (i, 0))
    full = lambda a: pl.BlockSpec(a.shape, lambda i: (0,) * a.ndim)
    consts = [w_out, na_g.reshape(1, -1), ln_g.reshape(1, -1), ln_b.reshape(1, -1)]
    if router:
        consts.append(router_w)
    out_shape = [jax.ShapeDtypeStruct((n, D_MODEL), F32)]
    out_specs = [row(D_MODEL)]
    if router:
        out_shape += [jax.ShapeDtypeStruct((n, LANES), jnp.int32), jax.ShapeDtypeStruct((n, LANES), F32)]
        out_specs += [row(LANES), row(LANES)]
    res = pl.pallas_call(
        functools.partial(_outproj_kernel, router=router, n_parts=len(x_parts),
                          first_tiles=x_parts[0].shape[0] // tm),
        grid=(n // tm,),
        in_specs=[row(CONV_WIDTH), row(SSD_WIDTH), row(NA_WIDTH)] + _row_parts_specs(x_parts, tm, D_MODEL, 1)
        + [full(a) for a in consts],
        out_specs=out_specs,
        out_shape=out_shape,
        compiler_params=_cparams(("parallel",)),
        name="out_proj_router" if router else "out_proj",
    )(yc, ys, yn, *x_parts, *consts)
    return res if router else res[0]


def _ffn_kernel(be_ref, bv_ref, x_ref, wg_ref, wu_ref, wd_ref, o_ref, xb_ref):
    i = pl.program_id(0)
    j = pl.program_id(1)
    valid = bv_ref[i] != 0

    @pl.when(j == 0)
    def _():
        xb_ref[...] = x_ref[...].astype(BF16)
        o_ref[...] = jnp.zeros_like(o_ref)

    @pl.when(valid)
    def _():
        xb = xb_ref[...]
        g = _dot(xb, wg_ref[0])
        u = _dot(xb, wu_ref[0])
        h = ((g * _sigmoid(g)) * u).astype(BF16)
        o_ref[...] += _dot(h, wd_ref[0])


def _ffn_blocks(x, block_e, block_valid, w_gate, w_up, w_down, tm, tf=FF_TILE):
    n, d = x.shape
    f = w_gate.shape[2]
    nf = f // tf
    col = lambda i, j, be, bv: (be[i], 0, jnp.where(bv[i] != 0, j, nf - 1))
    rowj = lambda i, j, be, bv: (be[i], jnp.where(bv[i] != 0, j, nf - 1), 0)
    grid_spec = pltpu.PrefetchScalarGridSpec(
        num_scalar_prefetch=2,
        grid=(n // tm, nf),
        in_specs=[pl.BlockSpec((tm, d), lambda i, j, be, bv: (i, 0)),
                  pl.BlockSpec((1, d, tf), col), pl.BlockSpec((1, d, tf), col), pl.BlockSpec((1, tf, d), rowj)],
        out_specs=pl.BlockSpec((tm, d), lambda i, j, be, bv: (i, 0)),
        scratch_shapes=[pltpu.VMEM((tm, d), BF16)],
    )
    return pl.pallas_call(
        _ffn_kernel,
        grid_spec=grid_spec,
        out_shape=jax.ShapeDtypeStruct((n, d), F32),
        compiler_params=_cparams(("parallel", "arbitrary")),
        name="swiglu_blocks",
    )(block_e, block_valid, x, w_gate, w_up, w_down)


def _add_ln_kernel(x_ref, f_ref, g_ref, b_ref, o_ref):
    o_ref[...] = _layer_norm(ALPHA * x_ref[...] + f_ref[...], g_ref[...], b_ref[...])


def _add_ln(x, f, g, b, tm=512):
    n, d = x.shape
    row = pl.BlockSpec((tm, d), lambda i: (i, 0))
    vec = pl.BlockSpec((1, d), lambda i: (0, 0))
    return pl.pallas_call(
        _add_ln_kernel, grid=(n // tm,), in_specs=[row, row, vec, vec], out_specs=row,
        out_shape=jax.ShapeDtypeStruct((n, d), F32), compiler_params=_cparams(("parallel",)), name="add_ln",
    )(x, f, g.reshape(1, d), b.reshape(1, d))


def _combine_ln_kernel(x_ref, o0_ref, o1_ref, gate_ref, g_ref, b_ref, oa_ref, ob_ref, *, first_tiles):
    f = o0_ref[...] * gate_ref[:, 0:1] + o1_ref[...] * gate_ref[:, 1:2]
    res = _layer_norm(ALPHA * x_ref[...] + f, g_ref[...], b_ref[...])
    i = pl.program_id(0)

    @pl.when(i < first_tiles)
    def _():
        oa_ref[...] = res

    @pl.when(i >= first_tiles)
    def _():
        ob_ref[...] = res


def _combine_ln(x, o0, o1, gates, g, b, n_first, tm=512):
    n, d = x.shape
    first_tiles = n_first // tm
    row = pl.BlockSpec((tm, d), lambda i: (i, 0))
    vec = pl.BlockSpec((1, d), lambda i: (0, 0))
    return pl.pallas_call(
        functools.partial(_combine_ln_kernel, first_tiles=first_tiles), grid=(n // tm,),
        in_specs=[row, row, row, pl.BlockSpec((tm, LANES), lambda i: (i, 0)), vec, vec],
        out_specs=[pl.BlockSpec((tm, d), lambda i: (jnp.minimum(i, first_tiles - 1), 0)),
                   pl.BlockSpec((tm, d), lambda i: (jnp.maximum(i - first_tiles, 0), 0))],
        out_shape=[jax.ShapeDtypeStruct((n_first, d), F32), jax.ShapeDtypeStruct((n - n_first, d), F32)],
        compiler_params=_cparams(("arbitrary",)), name="combine_ln",
    )(x, o0, o1, gates, g.reshape(1, d), b.reshape(1, d))


def _gather_kernel(idx_ref, src_ref, o_ref, sem, *, tr):
    def row_copy(r):
        return pltpu.make_async_copy(src_ref.at[pl.ds(idx_ref[0, 0, r], 1), :], o_ref.at[pl.ds(r, 1), :], sem)

    def issue(r, carry):
        row_copy(r).start()
        return carry

    lax.fori_loop(0, tr, issue, 0)
    pltpu.make_async_copy(src_ref.at[pl.ds(0, tr), :], o_ref, sem).wait()


def _gather_rows(src, idx, tr=256):
    m = idx.shape[0]
    d = src.shape[1]
    idx3 = idx.reshape(m // tr, 1, tr)
    return pl.pallas_call(
        functools.partial(_gather_kernel, tr=tr),
        grid=(m // tr,),
        in_specs=[pl.BlockSpec((1, 1, tr), lambda i: (i, 0, 0), memory_space=pltpu.SMEM),
                  pl.BlockSpec(memory_space=pl.ANY)],
        out_specs=pl.BlockSpec((tr, d), lambda i: (i, 0)),
        out_shape=jax.ShapeDtypeStruct((m, d), src.dtype),
        scratch_shapes=[pltpu.SemaphoreType.DMA(())],
        compiler_params=_cparams(("arbitrary",)),
        name="gather_rows",
    )(idx3, src)


def _routing_tables(idx, n, tm):
    flat_e = idx[:, :TOP_K].reshape(-1)
    onehot = (flat_e[:, None] == jnp.arange(N_EXPERTS, dtype=jnp.int32)[None, :]).astype(jnp.int32)
    incl = jnp.cumsum(onehot, axis=0)
    rank = jnp.sum((incl - onehot) * onehot, axis=1)
    counts = incl[-1]
    padded = ((counts + tm - 1) // tm) * tm
    pend = jnp.cumsum(padded)
    pstart = pend - padded
    dest = pstart[flat_e] + rank
    n_blocks = -(-(n * TOP_K) // tm) + N_EXPERTS
    p_len = n_blocks * tm
    flat_tok = jnp.arange(n * TOP_K, dtype=jnp.int32) // TOP_K
    pad_tok = jnp.zeros((p_len,), jnp.int32).at[dest].set(flat_tok)
    starts = jnp.arange(n_blocks, dtype=jnp.int32) * tm
    block_valid = (starts < pend[-1]).astype(jnp.int32)
    block_e = jnp.minimum(jnp.searchsorted(pend, starts, side='right'), N_EXPERTS - 1).astype(jnp.int32)
    last_e = block_e[jnp.maximum(jnp.sum(block_valid) - 1, 0)]
    block_e = jnp.where(block_valid != 0, block_e, last_e)
    dest2 = dest.reshape(n, TOP_K)
    return pad_tok, block_e, block_valid, dest2[:, 0], dest2[:, 1]


def _prep_w_in(w):
    cb, cc, ch = w[:, 0:512], w[:, 512:1024], w[:, 1024:1536]
    z = w[:, 1536:2560]
    xs, bc = w[:, 2560:3584], w[:, 3584:3840]
    dt = w[:, 3840:3872]
    qkv = w[:, 3872:5408]
    pad = jnp.zeros((D_MODEL, A_WIDTH - (A_DT + 2 * SSD_HEADS)), w.dtype)
    return jnp.concatenate([z, xs, cb, cc, ch, bc, dt, pad, qkv], axis=1).astype(BF16)


def _prep_router_w(rw):
    r1 = rw.astype(BF16)
    r2 = (rw - r1.astype(F32)).astype(BF16)
    r3 = (rw - r1.astype(F32) - r2.astype(F32)).astype(BF16)
    pad = jnp.zeros((rw.shape[0], LANES - 3 * N_EXPERTS), BF16)
    return jnp.concatenate([r1, r2, r3, pad], axis=1)


def _trunk(x_parts, seq, p):
    n = sum(a.shape[0] for a in x_parts)
    n_first = x_parts[0].shape[0]
    for l in range(DEPTH):
        proj_a, proj_b = _in_proj(x_parts, _prep_w_in(p['w_in'][l]))
        y_conv = _conv_mix(proj_a, p['conv_w'][l], p['conv_norm_g'][l], seq)
        cf, cbw = _ssd_consts(p['ssd_conv_w'][l], p['ssd_conv_b'][l], p['ssd_dt_bias'][l], p['ssd_a_log'][l],
                              p['ssd_d'][l], p['ssd_norm_g'][l])
        y_fwd = _ssd_pass(proj_a, None, cf, seq, 0)
        y_ssd = _ssd_pass(proj_a, y_fwd, cbw, seq, 1)
        y_na = _neighbourhood_attention(proj_b, p['na_rel_bias'][l], seq)
        w_out = p['w_out'][l].astype(BF16)
        if l % 2 == 0:
            x = _out_proj_ln(y_conv, y_ssd, y_na, x_parts, w_out, p['na_norm_g'][l], p['ln_mix_g'][l], p['ln_mix_b'][l])
            padf = ((0, 0), (0, D_FF_PAD - D_FF))
            wg = jnp.pad(p['ffn_w_gate'][l // 2], padf).astype(BF16)[None]
            wu = jnp.pad(p['ffn_w_up'][l // 2], padf).astype(BF16)[None]
            wd = jnp.pad(p['ffn_w_down'][l // 2], ((0, D_FF_PAD - D_FF), (0, 0))).astype(BF16)[None]
            nb = n // MOE_TM
            f = _ffn_blocks(x, jnp.zeros((nb,), jnp.int32), jnp.ones((nb,), jnp.int32), wg, wu, wd, MOE_TM)
            x = _add_ln(x, f, p['ln_ffn_g'][l], p['ln_ffn_b'][l])
            outs_split = (x[:n_first], x[n_first:])
        else:
            x, idx, gates = _out_proj_ln(y_conv, y_ssd, y_na, x_parts, w_out, p['na_norm_g'][l], p['ln_mix_g'][l],
                                         p['ln_mix_b'][l], _prep_router_w(p['router_w'][l // 2]))
            pad_tok, block_e, block_valid, d0, d1 = _routing_tables(idx, n, MOE_TM)
            xs = _gather_rows(x, pad_tok)
            outs = _ffn_blocks(xs, block_e, block_valid, p['moe_w_gate'][l // 2].astype(BF16),
                               p['moe_w_up'][l // 2].astype(BF16), p['moe_w_down'][l // 2].astype(BF16), MOE_TM)
            o0 = _gather_rows(outs, d0)
            o1 = _gather_rows(outs, d1)
            outs_split = _combine_ln(x, o0, o1, gates, p['ln_ffn_g'][l], p['ln_ffn_b'][l], n_first)
            if l + 1 < DEPTH:
                x = jnp.concatenate(outs_split, axis=0)
        x_parts = [x]
    return outs_split


def kernel(x_prompt, x_sample, w_in, conv_w, conv_norm_g, ssd_conv_w, ssd_conv_b, ssd_dt_bias, ssd_a_log, ssd_d,
           ssd_norm_g, na_rel_bias, na_norm_g, w_out, ln_mix_g, ln_mix_b, ln_ffn_g, ln_ffn_b, ffn_w_gate, ffn_w_up,
           ffn_w_down, router_w, moe_w_gate, moe_w_up, moe_w_down):
    p = dict(w_in=w_in, conv_w=conv_w, conv_norm_g=conv_norm_g, ssd_conv_w=ssd_conv_w, ssd_conv_b=ssd_conv_b,
             ssd_dt_bias=ssd_dt_bias, ssd_a_log=ssd_a_log, ssd_d=ssd_d, ssd_norm_g=ssd_norm_g,
             na_rel_bias=na_rel_bias, na_norm_g=na_norm_g, w_out=w_out, ln_mix_g=ln_mix_g, ln_mix_b=ln_mix_b,
             ln_ffn_g=ln_ffn_g, ln_ffn_b=ln_ffn_b, ffn_w_gate=ffn_w_gate, ffn_w_up=ffn_w_up, ffn_w_down=ffn_w_down,
             router_w=router_w, moe_w_gate=moe_w_gate, moe_w_up=moe_w_up, moe_w_down=moe_w_down)
    bp, seq, d = x_prompt.shape
    bs, seq_s, _ = x_sample.shape
    assert seq == seq_s, "both request groups must share one sequence length"
    yp, ys = _trunk([x_prompt.reshape(bp * seq, d), x_sample.reshape(bs * seq, d)], seq, p)
    return yp.reshape(bp, seq, d), ys.reshape(bs, seq, d)
```

```python
import functools

import numpy as np
import jax
import jax.numpy as jnp
from jax import lax
from jax.experimental import pallas as pl
from jax.experimental.pallas import tpu as pltpu

F32 = jnp.float32
BF16 = jnp.bfloat16

D_MODEL = 2048
DEPTH = 2
GRID_W = 64
CONV_WIDTH = 512
SSD_HEADS = 16
SSD_HEAD_DIM = 64
SSD_WIDTH = 1024
SSD_GROUPS = 2
SSD_STATE = 64
SSD_CHUNK = 128
NA_HEADS = 8
NA_HEAD_DIM = 64
NA_WIDTH = 512
WIN_H = 8
WIN_W = 16
D_FF = 5504
N_EXPERTS = 8
TOP_K = 2
D_FF_EXPERT = 7168
ALPHA = (2 * DEPTH) ** 0.25
LN_EPS = 1e-5
RMS_EPS = 1e-6

LANES = 128
SUBLANES = 8
NEG_BIG = -1e30

A_Z, A_XS, A_CB, A_CC, A_CH, A_BC, A_DT, A_WIDTH = 0, 1024, 2048, 2560, 3072, 3584, 3840, 4096
FF_TILE = 512
D_FF_PAD = 5632
MOE_TM = 1024
VMEM_LIMIT = 56 * 1024 * 1024


def _cparams(sem, vmem=VMEM_LIMIT):
    return pltpu.CompilerParams(dimension_semantics=sem, vmem_limit_bytes=vmem)


def _sigmoid(x):
    return 1.0 / (1.0 + jnp.exp(-x))


def _split3(x):
    x1 = x.astype(BF16)
    r1 = x - x1.astype(F32)
    x2 = r1.astype(BF16)
    r2 = r1 - x2.astype(F32)
    return x1, x2, r2.astype(BF16)


def _dot(a, b):
    return jnp.dot(a, b, preferred_element_type=F32)


def _dot_exact_lhs(x, m):
    x1, x2, x3 = _split3(x)
    return (_dot(x3, m) + _dot(x2, m)) + _dot(x1, m)


def _dot_exact_rhs(m, x):
    x1, x2, x3 = _split3(x)
    return (_dot(m, x3) + _dot(m, x2)) + _dot(m, x1)


def _layer_norm(v, g, b):
    mu = jnp.mean(v, axis=-1, keepdims=True)
    c = v - mu
    var = jnp.mean(c * c, axis=-1, keepdims=True)
    return c * lax.rsqrt(var + LN_EPS) * g + b


def _row_parts_specs(parts, tm, width, grid_rank):
    if len(parts) == 1:
        return [pl.BlockSpec((tm, width), (lambda i: (i, 0)) if grid_rank == 1 else (lambda i, j: (i, 0)))]
    first_tiles = parts[0].shape[0] // tm
    if grid_rank == 1:
        return [pl.BlockSpec((tm, width), lambda i: (jnp.minimum(i, first_tiles - 1), 0)),
                pl.BlockSpec((tm, width), lambda i: (jnp.maximum(i - first_tiles, 0), 0))]
    return [pl.BlockSpec((tm, width), lambda i, j: (jnp.minimum(i, first_tiles - 1), 0)),
            pl.BlockSpec((tm, width), lambda i, j: (jnp.maximum(i - first_tiles, 0), 0))]


def _read_row_parts(x_refs, first_tiles):
    x = x_refs[0][...]
    if len(x_refs) == 2:
        x = jnp.where(pl.program_id(0) < first_tiles, x, x_refs[1][...])
    return x


def _inproj_kernel(*refs, n_parts, first_tiles, a_blocks):
    x_refs = refs[:n_parts]
    w_ref, oa_ref, ob_ref, xb_ref = refs[n_parts:]
    j = pl.program_id(1)

    @pl.when(j == 0)
    def _():
        xb_ref[...] = _read_row_parts(x_refs, first_tiles).astype(BF16)

    res = _dot(xb_ref[...], w_ref[...])

    @pl.when(j < a_blocks)
    def _():
        oa_ref[...] = res

    @pl.when(j >= a_blocks)
    def _():
        ob_ref[...] = res.astype(ob_ref.dtype)


def _in_proj(x_parts, w, tm=1024, tn=512):
    n = sum(a.shape[0] for a in x_parts)
    k = x_parts[0].shape[1]
    a_blocks = A_WIDTH // tn
    b_width = w.shape[1] - A_WIDTH
    first_tiles = x_parts[0].shape[0] // tm
    return pl.pallas_call(
        functools.partial(_inproj_kernel, n_parts=len(x_parts), first_tiles=first_tiles, a_blocks=a_blocks),
        grid=(n // tm, w.shape[1] // tn),
        in_specs=_row_parts_specs(x_parts, tm, k, 2) + [pl.BlockSpec((k, tn), lambda i, j: (0, j))],
        out_specs=[pl.BlockSpec((tm, tn), lambda i, j: (i, jnp.minimum(j, a_blocks - 1))),
                   pl.BlockSpec((tm, tn), lambda i, j: (i, jnp.maximum(j - a_blocks, 0)))],
        out_shape=[jax.ShapeDtypeStruct((n, A_WIDTH), F32), jax.ShapeDtypeStruct((n, b_width), BF16)],
        scratch_shapes=[pltpu.VMEM((tm, k), BF16)],
        compiler_params=_cparams(("parallel", "arbitrary")),
        name="in_proj",
    )(*x_parts, w)


def _shift_rows(x, prev_row, next_row, first, last):
    n = x.shape[0]
    row = lax.broadcasted_iota(jnp.int32, (n, 1), 0)
    prev_row = jnp.where(first, 0.0, prev_row)
    next_row = jnp.where(last, 0.0, next_row)
    x_prev = jnp.where(row == 0, prev_row, pltpu.roll(x, 1, 0))
    x_next = jnp.where(row == n - 1, next_row, pltpu.roll(x, n - 1, 0))
    return x_prev, x_next


def _convmix_kernel(cb_ref, cc_ref, ch_ref, ccp_ref, chp_ref, ccn_ref, chn_ref, w_ref, g_ref, o_ref, *, tiles_per_seq):
    i = pl.program_id(0)
    first = (i % tiles_per_seq) == 0
    last = (i % tiles_per_seq) == tiles_per_seq - 1
    u = cc_ref[...] * ch_ref[...]
    up = ccp_ref[SUBLANES - 1:SUBLANES, :] * chp_ref[SUBLANES - 1:SUBLANES, :]
    un = ccn_ref[0:1, :] * chn_ref[0:1, :]
    u_prev, u_next = _shift_rows(u, up, un, first, last)
    conv = u_prev * w_ref[0:1, :] + u * w_ref[1:2, :] + u_next * w_ref[2:3, :]
    y = cb_ref[...] * conv
    y = y * lax.rsqrt(jnp.mean(y * y, axis=-1, keepdims=True) + RMS_EPS)
    o_ref[...] = (y * g_ref[...]).astype(o_ref.dtype)


def _halo_specs(tt, width, col0, n_rows):
    cb = col0 // width
    r = tt // SUBLANES
    last_blk = n_rows // SUBLANES - 1
    prev = pl.BlockSpec((SUBLANES, width), lambda i: (jnp.maximum(i * r - 1, 0), cb))
    nxt = pl.BlockSpec((SUBLANES, width), lambda i: (jnp.minimum((i + 1) * r, last_blk), cb))
    return prev, nxt


def _conv_mix(proj_a, conv_w, norm_g, seq, tt=512):
    n = proj_a.shape[0]
    w = CONV_WIDTH
    main = lambda c0: pl.BlockSpec((tt, w), lambda i: (i, c0 // w))
    ccp, ccn = _halo_specs(tt, w, A_CC, n)
    chp, chn = _halo_specs(tt, w, A_CH, n)
    full = lambda shape: pl.BlockSpec(shape, lambda i: (0, 0))
    return pl.pallas_call(
        functools.partial(_convmix_kernel, tiles_per_seq=seq // tt),
        grid=(n // tt,),
        in_specs=[main(A_CB), main(A_CC), main(A_CH), ccp, chp, ccn, chn, full((3, w)), full((1, w))],
        out_specs=pl.BlockSpec((tt, w), lambda i: (i, 0)),
        out_shape=jax.ShapeDtypeStruct((n, w), BF16),
        compiler_params=_cparams(("parallel",)),
        name="conv_mix",
    )(proj_a, proj_a, proj_a, proj_a, proj_a, proj_a, proj_a, conv_w, norm_g.reshape(1, w))


def _ssd_kernel(*refs, direction, final, nc):
    if final:
        (xs_ref, xsp_ref, xsn_ref, bc_ref, bcp_ref, bcn_ref, dt_ref, z_ref, yf_ref,
         cwx_ref, cbx_ref, cwb_ref, cbb_ref, dtb_ref, alog_ref, e_ref, hm_ref, dsk_ref, ng_ref,
         o_ref, h_ref, y_ref) = refs
    else:
        (xs_ref, xsp_ref, xsn_ref, bc_ref, bcp_ref, bcn_ref, dt_ref,
         cwx_ref, cbx_ref, cwb_ref, cbb_ref, dtb_ref, alog_ref, e_ref, hm_ref,
         o_ref, h_ref) = refs
        y_ref = o_ref
    L = SSD_CHUNK
    c = pl.program_id(1)
    chunk = c if direction == 0 else nc - 1 - c
    first = chunk == 0
    last = chunk == nc - 1

    @pl.when(c == 0)
    def _():
        h_ref[...] = jnp.zeros_like(h_ref)

    def conv_silu(x_ref, p_ref, n_ref, w_ref, b_ref):
        x = x_ref[...]
        x_prev, x_next = _shift_rows(x, p_ref[SUBLANES - 1:SUBLANES, :], n_ref[0:1, :], first, last)
        v = x_prev * w_ref[0:1, :] + x * w_ref[1:2, :] + x_next * w_ref[2:3, :] + b_ref[...]
        return v * _sigmoid(v)

    xs = conv_silu(xs_ref, xsp_ref, xsn_ref, cwx_ref, cbx_ref)
    bc = conv_silu(bc_ref, bcp_ref, bcn_ref, cwb_ref, cbb_ref)
    bm = bc[:, :LANES]
    cm = bc[:, LANES:]

    dtr = dt_ref[...] + dtb_ref[...]
    dt = jnp.maximum(dtr, 0.0) + jnp.log1p(jnp.exp(-jnp.abs(dtr)))
    dta = dt * (-jnp.exp(alog_ref[...]))

    li = lax.broadcasted_iota(jnp.int32, (L, L), 0)
    si = lax.broadcasted_iota(jnp.int32, (L, L), 1)
    valid = (si <= li) if direction == 0 else (si >= li)
    tri = jnp.where(valid, 1.0, 0.0).astype(BF16)
    acum = _dot_exact_rhs(tri, dta)
    acum_t = acum.T
    total = acum[L - 1:L, :] if direction == 0 else acum[0:1, :]
    dec_end = jnp.exp(total - acum)
    ea = jnp.exp(acum)

    e_mat = e_ref[...]
    dtx = _dot_exact_lhs(dt, e_mat)
    eax = _dot_exact_lhs(ea, e_mat)
    dex = _dot_exact_lhs(dec_end * dt, e_mat)

    xdt = xs * dtx
    xw = (xs * dex).astype(BF16)

    lane = lax.broadcasted_iota(jnp.int32, (1, LANES), 1)
    cm_b = cm.astype(BF16)
    bm_b = bm.astype(BF16)
    nt = (((1,), (1,)), ((), ()))

    h_prev = h_ref[...]
    y_off = _dot(cm_b, h_prev.astype(BF16)) * eax

    heads_per_group = SSD_HEADS // SSD_GROUPS
    for g in range(SSD_GROUPS):
        cg = jnp.where((lane // SSD_STATE) == g, cm, 0.0).astype(BF16)
        cb = lax.dot_general(cg, bm_b, nt, preferred_element_type=F32)
        for pair in range(heads_per_group // 2):
            acc = None
            for half in range(2):
                h = g * heads_per_group + pair * 2 + half
                hl = direction * SSD_HEADS + h
                seg = acum[:, hl:hl + 1] - acum_t[hl:hl + 1, :]
                w = (jnp.where(valid, jnp.exp(seg), 0.0) * cb).astype(BF16)
                c0 = (h // 2) * LANES
                rhs = jnp.where((lane // SSD_HEAD_DIM) == half, xdt[:, c0:c0 + LANES], 0.0).astype(BF16)
                part = _dot(w, rhs)
                acc = part if acc is None else acc + part
            c0 = (g * heads_per_group // 2 + pair) * LANES
            y_pair = acc + y_off[:, c0:c0 + LANES]
            if final:
                y_pair = yf_ref[:, c0:c0 + LANES] + y_pair + dsk_ref[:, c0:c0 + LANES] * xs[:, c0:c0 + LANES]
                zz = z_ref[:, c0:c0 + LANES]
                y_pair = y_pair * (zz * _sigmoid(zz))
            y_ref[:, c0:c0 + LANES] = y_pair

    cd = eax[L - 1:L, :] if direction == 0 else eax[0:1, :]
    s_new = _dot(bm.T.astype(BF16), xw)
    h_ref[...] = (cd * h_prev + s_new) * hm_ref[...]

    if final:
        gw = SSD_WIDTH // SSD_GROUPS
        for g in range(SSD_GROUPS):
            y = y_ref[:, g * gw:(g + 1) * gw]
            y = y * lax.rsqrt(jnp.mean(y * y, axis=-1, keepdims=True) + RMS_EPS)
            o_ref[:, g * gw:(g + 1) * gw] = (y * ng_ref[:, g * gw:(g + 1) * gw]).astype(o_ref.dtype)


def _ssd_pass(proj_a, z_and_yf, params, seq, direction):
    n = proj_a.shape[0]
    L = SSD_CHUNK
    nc = seq // L
    nseq = n // seq
    final = direction == 1
    cw, cb_, dtb, alog, e_mat, hmask, dsk, ng = params
    bcw = 2 * SSD_GROUPS * SSD_STATE

    def row_blk(b, c):
        ch = c if direction == 0 else nc - 1 - c
        return b * nc + ch

    def main(width, col0):
        return pl.BlockSpec((L, width), lambda b, c: (row_blk(b, c), col0 // width))

    r = L // SUBLANES
    last_blk = n // SUBLANES - 1

    def halo(width, col0):
        prev = pl.BlockSpec((SUBLANES, width), lambda b, c: (jnp.maximum(row_blk(b, c) * r - 1, 0), col0 // width))
        nxt = pl.BlockSpec((SUBLANES, width), lambda b, c: (jnp.minimum((row_blk(b, c) + 1) * r, last_blk), col0 // width))
        return prev, nxt

    xsp, xsn = halo(SSD_WIDTH, A_XS)
    bcp, bcn = halo(bcw, A_BC)
    full = lambda a: pl.BlockSpec(a.shape, lambda b, c: (0,) * a.ndim)

    in_specs = [main(SSD_WIDTH, A_XS), xsp, xsn, main(bcw, A_BC), bcp, bcn, main(LANES, A_DT)]
    args = [proj_a] * 7
    if final:
        in_specs += [main(SSD_WIDTH, A_Z), pl.BlockSpec((L, SSD_WIDTH), lambda b, c: (row_blk(b, c), 0))]
        args += [proj_a, z_and_yf]
    consts = [cw[:, :SSD_WIDTH], cb_[:, :SSD_WIDTH], cw[:, SSD_WIDTH:], cb_[:, SSD_WIDTH:], dtb, alog, e_mat, hmask]
    if final:
        consts += [dsk, ng]
    in_specs += [full(a) for a in consts]
    args += consts
    return pl.pallas_call(
        functools.partial(_ssd_kernel, direction=direction, final=final, nc=nc),
        grid=(nseq, nc),
        in_specs=in_specs,
        out_specs=pl.BlockSpec((L, SSD_WIDTH), lambda b, c: (row_blk(b, c), 0)),
        out_shape=jax.ShapeDtypeStruct((n, SSD_WIDTH), BF16 if final else F32),
        scratch_shapes=[pltpu.VMEM((LANES, SSD_WIDTH), F32)] + ([pltpu.VMEM((L, SSD_WIDTH), F32)] if final else []),
        compiler_params=_cparams(("parallel", "arbitrary")),
        name="ssd_bwd" if final else "ssd_fwd",
    )(*args)


def _ssd_consts(ssd_conv_w, ssd_conv_b, ssd_dt_bias, ssd_a_log, ssd_d, ssd_norm_g):
    pad = LANES - 2 * SSD_HEADS
    dtb = jnp.pad(ssd_dt_bias.reshape(1, -1), ((0, 0), (0, pad)))
    alog = jnp.pad(ssd_a_log.reshape(1, -1), ((0, 0), (0, pad)))
    col_head = np.arange(SSD_WIDTH) // SSD_HEAD_DIM
    e_mats = [jnp.asarray((np.arange(LANES)[:, None] == d * SSD_HEADS + col_head[None, :]), BF16) for d in range(2)]
    row_group = np.arange(LANES) // SSD_STATE
    col_group = col_head // (SSD_HEADS // SSD_GROUPS)
    hmask = jnp.asarray(row_group[:, None] == col_group[None, :], F32)
    dsk = jnp.repeat(ssd_d, SSD_HEAD_DIM).reshape(1, SSD_WIDTH)
    ng = ssd_norm_g.reshape(1, SSD_WIDTH)
    cb_ = ssd_conv_b.reshape(1, -1)
    return [(ssd_conv_w, cb_, dtb, alog, e_mats[d], hmask, dsk, ng) for d in range(2)]


NA_QROWS = 4
NA_KROWS = NA_QROWS + WIN_H


def _na_kernel(kb_ref, cls_ref, q_ref, k_ref, v_ref, bias_ref, o_ref, *, blocks_per_step):
    step = pl.program_id(2)
    lane = lax.broadcasted_iota(jnp.int32, (1, LANES), 1)
    nt = (((1,), (1,)), ((), ()))
    nq = NA_QROWS * GRID_W
    nk = NA_KROWS * GRID_W
    for j in range(blocks_per_step):
        blk = step * blocks_per_step + j
        k0 = pl.multiple_of(kb_ref[blk] * GRID_W, GRID_W)
        cls = cls_ref[blk]
        q = q_ref[j * nq:(j + 1) * nq, :] * (NA_HEAD_DIM ** -0.5)
        kk = k_ref[pl.ds(k0, nk), :]
        vv = v_ref[pl.ds(k0, nk), :]
        outs = []
        for hh in range(2):
            qm = jnp.where((lane // NA_HEAD_DIM) == hh, q, jnp.zeros_like(q))
            s = lax.dot_general(qm, kk, nt, preferred_element_type=F32) + bias_ref[cls, hh]
            m = jnp.max(s, axis=-1, keepdims=True)
            p = jnp.exp(s - m)
            l = jnp.sum(p, axis=-1, keepdims=True)
            outs.append(_dot(p.astype(BF16), vv) / l)
        o = jnp.where((lane // NA_HEAD_DIM) == 0, outs[0], outs[1])
        o_ref[j * nq:(j + 1) * nq, :] = o.astype(o_ref.dtype)


def _na_tables(rel_bias, rows):
    qc = np.arange(GRID_W)
    cs = np.clip(qc - WIN_W // 2, 0, GRID_W - WIN_W)
    kc = np.arange(GRID_W)
    col_valid = (kc[None, :] >= cs[:, None]) & (kc[None, :] < cs[:, None] + WIN_W)
    dj = np.clip(kc[None, :] - qc[:, None] + WIN_W - 1, 0, 2 * WIN_W - 2)
    nblk = rows // NA_QROWS
    kb = np.clip(np.arange(nblk) * NA_QROWS - WIN_H // 2, 0, rows - NA_KROWS)
    patterns, cls = [], []
    for b in range(nblk):
        r = b * NA_QROWS + np.arange(NA_QROWS)
        rs = np.clip(r - WIN_H // 2, 0, rows - WIN_H)
        key = (tuple(rs - kb[b]), int(kb[b] - b * NA_QROWS))
        if key not in patterns:
            patterns.append(key)
        cls.append(patterns.index(key))
    tabs = []
    for off, shift in patterns:
        kr = np.arange(NA_KROWS)[None, :]
        i = np.arange(NA_QROWS)[:, None]
        rel = kr - np.asarray(off)[:, None]
        row_valid = (rel >= 0) & (rel < WIN_H)
        di = np.clip(shift + kr - i + WIN_H - 1, 0, 2 * WIN_H - 2)
        b = rel_bias.astype(F32)[:, di]
        b = b[..., dj]
        mask = row_valid[:, :, None, None] & col_valid[None, None]
        b = jnp.where(mask[None], b, NEG_BIG).transpose(0, 1, 3, 2, 4)
        tabs.append(b.reshape(NA_HEADS, NA_QROWS * GRID_W, NA_KROWS * GRID_W))
    return jnp.asarray(kb, jnp.int32), jnp.asarray(cls, jnp.int32), jnp.stack(tabs)


def _neighbourhood_attention(proj_b, rel_bias, seq, blocks_per_step=2):
    n = proj_b.shape[0]
    nseq = n // seq
    rows = seq // GRID_W
    kb, cls, bias_tab = _na_tables(rel_bias, rows)
    tq = blocks_per_step * NA_QROWS * GRID_W
    pairs = NA_HEADS // 2
    steps = seq // tq
    ncls = bias_tab.shape[0]
    grid_spec = pltpu.PrefetchScalarGridSpec(
        num_scalar_prefetch=2,
        grid=(nseq, pairs, steps),
        in_specs=[pl.BlockSpec((tq, LANES), lambda b, p, r, kb_, cls_: (b * steps + r, p)),
                  pl.BlockSpec((seq, LANES), lambda b, p, r, kb_, cls_: (b, pairs + p)),
                  pl.BlockSpec((seq, LANES), lambda b, p, r, kb_, cls_: (b, 2 * pairs + p)),
                  pl.BlockSpec((ncls, 2) + bias_tab.shape[2:], lambda b, p, r, kb_, cls_: (0, p, 0, 0))],
        out_specs=pl.BlockSpec((tq, LANES), lambda b, p, r, kb_, cls_: (b * steps + r, p)),
    )
    return pl.pallas_call(
        functools.partial(_na_kernel, blocks_per_step=blocks_per_step),
        grid_spec=grid_spec,
        out_shape=jax.ShapeDtypeStruct((n, NA_WIDTH), F32),
        compiler_params=_cparams(("parallel", "arbitrary", "arbitrary")),
        name="nbr_attn",
    )(kb, cls, proj_b, proj_b, proj_b, bias_tab)


def _rms_rows(y, g):
    return y * lax.rsqrt(jnp.mean(y * y, axis=-1, keepdims=True) + RMS_EPS) * g


def _outproj_kernel(*refs, router, n_parts, first_tiles):
    yc_ref, ys_ref, yn_ref = refs[:3]
    x_refs = refs[3:3 + n_parts]
    rest = refs[3 + n_parts:]
    if router:
        w_ref, nag_ref, g_ref, b_ref, rw_ref, o_ref, idx_ref, gate_ref = rest
    else:
        w_ref, nag_ref, g_ref, b_ref, o_ref, ob_ref = rest
    c1 = CONV_WIDTH
    c2 = CONV_WIDTH + SSD_WIDTH
    yn = _rms_rows(yn_ref[...], nag_ref[...]).astype(BF16)
    mix = _dot(yc_ref[...], w_ref[0:c1, :]) + _dot(ys_ref[...], w_ref[c1:c2, :]) + _dot(yn, w_ref[c2:, :])
    out = _layer_norm(ALPHA * _read_row_parts(x_refs, first_tiles) + mix, g_ref[...], b_ref[...])
    o_ref[...] = out
    if not router:
        ob_ref[...] = out.astype(BF16)
    if router:
        o1, o2, _ = _split3(out)
        t1 = _dot(o1, rw_ref[...])
        t2 = _dot(o2, rw_ref[...])
        sh1 = LANES - N_EXPERTS
        sh2 = LANES - 2 * N_EXPERTS
        logits = ((pltpu.roll(t2, sh1, 1) + pltpu.roll(t1, sh2, 1)) + (t2 + pltpu.roll(t1, sh1, 1))) + t1
        tm = logits.shape[0]
        lane = lax.broadcasted_iota(jnp.int32, (tm, LANES), 1)
        lg = jnp.where(lane < N_EXPERTS, logits, -jnp.inf)
        m1 = jnp.max(lg, axis=-1, keepdims=True)
        i1 = jnp.min(jnp.where(lg == m1, lane, LANES), axis=-1, keepdims=True)
        lg2 = jnp.where(lane == i1, -jnp.inf, lg)
        m2 = jnp.max(lg2, axis=-1, keepdims=True)
        i2 = jnp.min(jnp.where(lg2 == m2, lane, LANES), axis=-1, keepdims=True)
        e2 = jnp.exp(m2 - m1)
        den = 1.0 + e2
        idx_ref[...] = jnp.where(lane == 0, i1, jnp.where(lane == 1, i2, 0))
        gate_ref[...] = jnp.where(lane == 0, 1.0 / den, jnp.where(lane == 1, e2 / den, 0.0))


def _out_proj_ln(yc, ys, yn, x_parts, w_out, na_g, ln_g, ln_b, router_w=None, tm=512):
    n = yc.shape[0]
    router = router_w is not None
    row = lambda w: pl.BlockSpec((tm, w), lambda i: (i, 0))
    full = lambda a: pl.BlockSpec(a.shape, lambda i: (0,) * a.ndim)
    consts = [w_out, na_g.reshape(1, -1), ln_g.reshape(1, -1), ln_b.reshape(1, -1)]
    if router:
        consts.append(router_w)
    out_shape = [jax.ShapeDtypeStruct((n, D_MODEL), F32)]
    out_specs = [row(D_MODEL)]
    if router:
        out_shape += [jax.ShapeDtypeStruct((n, LANES), jnp.int32), jax.ShapeDtypeStruct((n, LANES), F32)]
        out_specs += [row(LANES), row(LANES)]
    else:
        out_shape += [jax.ShapeDtypeStruct((n, D_MODEL), BF16)]
        out_specs += [row(D_MODEL)]
    res = pl.pallas_call(
        functools.partial(_outproj_kernel, router=router, n_parts=len(x_parts),
                          first_tiles=x_parts[0].shape[0] // tm),
        grid=(n // tm,),
        in_specs=[row(CONV_WIDTH), row(SSD_WIDTH), row(NA_WIDTH)] + _row_parts_specs(x_parts, tm, D_MODEL, 1)
        + [full(a) for a in consts],
        out_specs=out_specs,
        out_shape=out_shape,
        compiler_params=_cparams(("parallel",)),
        name="out_proj_router" if router else "out_proj",
    )(yc, ys, yn, *x_parts, *consts)
    return res


def _ffn_kernel(be_ref, bv_ref, x_ref, wg_ref, wu_ref, wd_ref, o_ref):
    i = pl.program_id(0)
    j = pl.program_id(1)
    valid = bv_ref[i] != 0

    @pl.when(j == 0)
    def _():
        o_ref[...] = jnp.zeros_like(o_ref)

    @pl.when(valid)
    def _():
        x = x_ref[...]
        g = _dot(x, wg_ref[0].astype(BF16))
        u = _dot(x, wu_ref[0].astype(BF16))
        h = ((g * _sigmoid(g)) * u).astype(BF16)
        o_ref[...] += _dot(h, wd_ref[0].astype(BF16))


def _ffn_blocks(x, block_e, block_valid, w_gate, w_up, w_down, tm, tf=FF_TILE):
    n, d = x.shape
    f = w_gate.shape[2]
    nf = f // tf
    col = lambda i, j, be, bv: (be[i], 0, jnp.where(bv[i] != 0, j, nf - 1))
    rowj = lambda i, j, be, bv: (be[i], jnp.where(bv[i] != 0, j, nf - 1), 0)
    grid_spec = pltpu.PrefetchScalarGridSpec(
        num_scalar_prefetch=2,
        grid=(n // tm, nf),
        in_specs=[pl.BlockSpec((tm, d), lambda i, j, be, bv: (i, 0)),
                  pl.BlockSpec((1, d, tf), col), pl.BlockSpec((1, d, tf), col), pl.BlockSpec((1, tf, d), rowj)],
        out_specs=pl.BlockSpec((tm, d), lambda i, j, be, bv: (i, 0)),
    )
    return pl.pallas_call(
        _ffn_kernel,
        grid_spec=grid_spec,
        out_shape=jax.ShapeDtypeStruct((n, d), F32),
        compiler_params=_cparams(("parallel", "arbitrary")),
        name="swiglu_blocks",
    )(block_e, block_valid, x, w_gate, w_up, w_down)


def _add_ln_kernel(x_ref, f_ref, g_ref, b_ref, o_ref):
    o_ref[...] = _layer_norm(ALPHA * x_ref[...] + f_ref[...], g_ref[...], b_ref[...])


def _add_ln(x, f, g, b, tm=512):
    n, d = x.shape
    row = pl.BlockSpec((tm, d), lambda i: (i, 0))
    vec = pl.BlockSpec((1, d), lambda i: (0, 0))
    return pl.pallas_call(
        _add_ln_kernel, grid=(n // tm,), in_specs=[row, row, vec, vec], out_specs=row,
        out_shape=jax.ShapeDtypeStruct((n, d), F32), compiler_params=_cparams(("parallel",)), name="add_ln",
    )(x, f, g.reshape(1, d), b.reshape(1, d))


def _combine_ln_kernel(i0_ref, i1_ref, src_ref, x_ref, gate_ref, g_ref, b_ref, oa_ref, ob_ref, b0_ref, b1_ref, sem,
                       *, tm, first_tiles):
    _start_row_gather(src_ref, i0_ref, b0_ref, sem.at[0], tm)
    _start_row_gather(src_ref, i1_ref, b1_ref, sem.at[1], tm)
    _wait_row_gather(src_ref, b0_ref, sem.at[0], tm)
    _wait_row_gather(src_ref, b1_ref, sem.at[1], tm)
    f = b0_ref[...] * gate_ref[:, 0:1] + b1_ref[...] * gate_ref[:, 1:2]
    res = _layer_norm(ALPHA * x_ref[...] + f, g_ref[...], b_ref[...])
    i = pl.program_id(0)

    @pl.when(i < first_tiles)
    def _():
        oa_ref[...] = res

    @pl.when(i >= first_tiles)
    def _():
        ob_ref[...] = res


def _combine_ln(x, outs, d0, d1, gates, g, b, n_first, tm=512):
    n, d = x.shape
    first_tiles = n_first // tm
    row = pl.BlockSpec((tm, d), lambda i: (i, 0))
    vec = pl.BlockSpec((1, d), lambda i: (0, 0))
    idx = pl.BlockSpec((1, 1, tm), lambda i: (i, 0, 0), memory_space=pltpu.SMEM)
    return pl.pallas_call(
        functools.partial(_combine_ln_kernel, tm=tm, first_tiles=first_tiles), grid=(n // tm,),
        in_specs=[idx, idx, pl.BlockSpec(memory_space=pl.ANY), row, pl.BlockSpec((tm, LANES), lambda i: (i, 0)), vec, vec],
        out_specs=[pl.BlockSpec((tm, d), lambda i: (jnp.minimum(i, first_tiles - 1), 0)),
                   pl.BlockSpec((tm, d), lambda i: (jnp.maximum(i - first_tiles, 0), 0))],
        out_shape=[jax.ShapeDtypeStruct((n_first, d), F32), jax.ShapeDtypeStruct((n - n_first, d), F32)],
        scratch_shapes=[pltpu.VMEM((tm, d), F32), pltpu.VMEM((tm, d), F32), pltpu.SemaphoreType.DMA((2,))],
        compiler_params=_cparams(("arbitrary",)), name="combine_ln",
    )(d0.reshape(n // tm, 1, tm), d1.reshape(n // tm, 1, tm), outs, x, gates, g.reshape(1, d), b.reshape(1, d))


GATHER_UNROLL = 8


def _start_row_gather(src_ref, idx_ref, buf_ref, sem, tr):
    def issue(r, carry):
        pltpu.make_async_copy(src_ref.at[pl.ds(idx_ref[0, 0, r], 1), :], buf_ref.at[pl.ds(r, 1), :], sem).start()
        return carry

    lax.fori_loop(0, tr, issue, 0, unroll=GATHER_UNROLL)


def _wait_row_gather(src_ref, buf_ref, sem, tr):
    pltpu.make_async_copy(src_ref.at[pl.ds(0, tr), :], buf_ref, sem).wait()


def _gather_kernel(idx_ref, src_ref, o_ref, buf_ref, sem, *, tr):
    _start_row_gather(src_ref, idx_ref, buf_ref, sem, tr)
    _wait_row_gather(src_ref, buf_ref, sem, tr)
    o_ref[...] = buf_ref[...].astype(o_ref.dtype)


def _gather_rows(src, idx, out_dtype, tr=1024):
    m = idx.shape[0]
    d = src.shape[1]
    idx3 = idx.reshape(m // tr, 1, tr)
    return pl.pallas_call(
        functools.partial(_gather_kernel, tr=tr),
        grid=(m // tr,),
        in_specs=[pl.BlockSpec((1, 1, tr), lambda i: (i, 0, 0), memory_space=pltpu.SMEM),
                  pl.BlockSpec(memory_space=pl.ANY)],
        out_specs=pl.BlockSpec((tr, d), lambda i: (i, 0)),
        out_shape=jax.ShapeDtypeStruct((m, d), out_dtype),
        scratch_shapes=[pltpu.VMEM((tr, d), src.dtype), pltpu.SemaphoreType.DMA(())],
        compiler_params=_cparams(("arbitrary",)),
        name="gather_rows",
    )(idx3, src)


def _routing_tables(idx, n, tm):
    flat_e = idx[:, :TOP_K].reshape(-1)
    onehot = (flat_e[:, None] == jnp.arange(N_EXPERTS, dtype=jnp.int32)[None, :]).astype(jnp.int32)
    incl = jnp.cumsum(onehot, axis=0)
    rank = jnp.sum((incl - onehot) * onehot, axis=1)
    counts = incl[-1]
    padded = ((counts + tm - 1) // tm) * tm
    pend = jnp.cumsum(padded)
    pstart = pend - padded
    dest = pstart[flat_e] + rank
    n_blocks = -(-(n * TOP_K) // tm) + N_EXPERTS
    p_len = n_blocks * tm
    flat_tok = jnp.arange(n * TOP_K, dtype=jnp.int32) // TOP_K
    pad_tok = jnp.zeros((p_len,), jnp.int32).at[dest].set(flat_tok)
    starts = jnp.arange(n_blocks, dtype=jnp.int32) * tm
    block_valid = (starts < pend[-1]).astype(jnp.int32)
    block_e = jnp.minimum(jnp.searchsorted(pend, starts, side='right'), N_EXPERTS - 1).astype(jnp.int32)
    last_e = block_e[jnp.maximum(jnp.sum(block_valid) - 1, 0)]
    block_e = jnp.where(block_valid != 0, block_e, last_e)
    dest2 = dest.reshape(n, TOP_K)
    return pad_tok, block_e, block_valid, dest2[:, 0], dest2[:, 1]


def _prep_w_in(w):
    cb, cc, ch = w[:, 0:512], w[:, 512:1024], w[:, 1024:1536]
    z = w[:, 1536:2560]
    xs, bc = w[:, 2560:3584], w[:, 3584:3840]
    dt = w[:, 3840:3872]
    qkv = w[:, 3872:5408]
    pad = jnp.zeros((D_MODEL, A_WIDTH - (A_DT + 2 * SSD_HEADS)), w.dtype)
    return jnp.concatenate([z, xs, cb, cc, ch, bc, dt, pad, qkv], axis=1).astype(BF16)


def _prep_router_w(rw):
    r1 = rw.astype(BF16)
    r2 = (rw - r1.astype(F32)).astype(BF16)
    r3 = (rw - r1.astype(F32) - r2.astype(F32)).astype(BF16)
    pad = jnp.zeros((rw.shape[0], LANES - 3 * N_EXPERTS), BF16)
    return jnp.concatenate([r1, r2, r3, pad], axis=1)


def _trunk(x_parts, seq, p):
    n = sum(a.shape[0] for a in x_parts)
    n_first = x_parts[0].shape[0]
    for l in range(DEPTH):
        proj_a, proj_b = _in_proj(x_parts, _prep_w_in(p['w_in'][l]))
        y_conv = _conv_mix(proj_a, p['conv_w'][l], p['conv_norm_g'][l], seq)
        cf, cbw = _ssd_consts(p['ssd_conv_w'][l], p['ssd_conv_b'][l], p['ssd_dt_bias'][l], p['ssd_a_log'][l],
                              p['ssd_d'][l], p['ssd_norm_g'][l])
        y_fwd = _ssd_pass(proj_a, None, cf, seq, 0)
        y_ssd = _ssd_pass(proj_a, y_fwd, cbw, seq, 1)
        y_na = _neighbourhood_attention(proj_b, p['na_rel_bias'][l], seq)
        w_out = p['w_out'][l].astype(BF16)
        if l % 2 == 0:
            x, x_b16 = _out_proj_ln(y_conv, y_ssd, y_na, x_parts, w_out, p['na_norm_g'][l], p['ln_mix_g'][l],
                                    p['ln_mix_b'][l])
            padf = ((0, 0), (0, D_FF_PAD - D_FF))
            wg = jnp.pad(p['ffn_w_gate'][l // 2], padf).astype(BF16)[None]
            wu = jnp.pad(p['ffn_w_up'][l // 2], padf).astype(BF16)[None]
            wd = jnp.pad(p['ffn_w_down'][l // 2], ((0, D_FF_PAD - D_FF), (0, 0))).astype(BF16)[None]
            nb = n // MOE_TM
            f = _ffn_blocks(x_b16, jnp.zeros((nb,), jnp.int32), jnp.ones((nb,), jnp.int32), wg, wu, wd, MOE_TM)
            x = _add_ln(x, f, p['ln_ffn_g'][l], p['ln_ffn_b'][l])
            outs_split = (x[:n_first], x[n_first:])
        else:
            x, idx, gates = _out_proj_ln(y_conv, y_ssd, y_na, x_parts, w_out, p['na_norm_g'][l], p['ln_mix_g'][l],
                                         p['ln_mix_b'][l], _prep_router_w(p['router_w'][l // 2]))
            pad_tok, block_e, block_valid, d0, d1 = _routing_tables(idx, n, MOE_TM)
            xs = _gather_rows(x, pad_tok, BF16)
            outs = _ffn_blocks(xs, block_e, block_valid, p['moe_w_gate'][l // 2], p['moe_w_up'][l // 2],
                               p['moe_w_down'][l // 2], MOE_TM)
            outs_split = _combine_ln(x, outs, d0, d1, gates, p['ln_ffn_g'][l], p['ln_ffn_b'][l], n_first)
            if l + 1 < DEPTH:
                x = jnp.concatenate(outs_split, axis=0)
        x_parts = [x]
    return outs_split


def kernel(x_prompt, x_sample, w_in, conv_w, conv_norm_g, ssd_conv_w, ssd_conv_b, ssd_dt_bias, ssd_a_log, ssd_d,
           ssd_norm_g, na_rel_bias, na_norm_g, w_out, ln_mix_g, ln_mix_b, ln_ffn_g, ln_ffn_b, ffn_w_gate, ffn_w_up,
           ffn_w_down, router_w, moe_w_gate, moe_w_up, moe_w_down):
    p = dict(w_in=w_in, conv_w=conv_w, conv_norm_g=conv_norm_g, ssd_conv_w=ssd_conv_w, ssd_conv_b=ssd_conv_b,
             ssd_dt_bias=ssd_dt_bias, ssd_a_log=ssd_a_log, ssd_d=ssd_d, ssd_norm_g=ssd_norm_g,
             na_rel_bias=na_rel_bias, na_norm_g=na_norm_g, w_out=w_out, ln_mix_g=ln_mix_g, ln_mix_b=ln_mix_b,
             ln_ffn_g=ln_ffn_g, ln_ffn_b=ln_ffn_b, ffn_w_gate=ffn_w_gate, ffn_w_up=ffn_w_up, ffn_w_down=ffn_w_down,
             router_w=router_w, moe_w_gate=moe_w_gate, moe_w_up=moe_w_up, moe_w_down=moe_w_down)
    bp, seq, d = x_prompt.shape
    bs, seq_s, _ = x_sample.shape
    assert seq == seq_s, "both request groups must share one sequence length"
    yp, ys = _trunk([x_prompt.reshape(bp * seq, d), x_sample.reshape(bs * seq, d)], seq, p)
    return yp.reshape(bp, seq, d), ys.reshape(bs, seq, d)
```

```python
import functools

import numpy as np
import jax
import jax.numpy as jnp
from jax import lax
from jax.experimental import pallas as pl
from jax.experimental.pallas import tpu as pltpu

F32 = jnp.float32
BF16 = jnp.bfloat16

D_MODEL = 2048
DEPTH = 2
GRID_W = 64
CONV_WIDTH = 512
SSD_HEADS = 16
SSD_HEAD_DIM = 64
SSD_WIDTH = 1024
SSD_GROUPS = 2
SSD_STATE = 64
SSD_CHUNK = 128
NA_HEADS = 8
NA_HEAD_DIM = 64
NA_WIDTH = 512
WIN_H = 8
WIN_W = 16
D_FF = 5504
N_EXPERTS = 8
TOP_K = 2
D_FF_EXPERT = 7168
ALPHA = (2 * DEPTH) ** 0.25
LN_EPS = 1e-5
RMS_EPS = 1e-6

LANES = 128
SUBLANES = 8
NEG_BIG = -1e30

A_Z, A_XS, A_CB, A_CC, A_CH, A_BC, A_DT, A_WIDTH = 0, 1024, 2048, 2560, 3072, 3584, 3840, 4096
FF_TILE = 512
D_FF_PAD = 5632
MOE_TM = 1024
VMEM_LIMIT = 56 * 1024 * 1024


def _cparams(sem, vmem=VMEM_LIMIT):
    return pltpu.CompilerParams(dimension_semantics=sem, vmem_limit_bytes=vmem)


def _sigmoid(x):
    return 1.0 / (1.0 + jnp.exp(-x))


def _split3(x):
    x1 = x.astype(BF16)
    r1 = x - x1.astype(F32)
    x2 = r1.astype(BF16)
    r2 = r1 - x2.astype(F32)
    return x1, x2, r2.astype(BF16)


def _dot(a, b):
    return jnp.dot(a, b, preferred_element_type=F32)


def _dot_exact_rhs(m, x):
    x1, x2, x3 = _split3(x)
    return (_dot(m, x3) + _dot(m, x2)) + _dot(m, x1)


def _dot_split2_lhs(x, m):
    x1, x2, _ = _split3(x)
    return _dot(x2, m) + _dot(x1, m)


def _log1p(e):
    u = 1.0 + e
    d = u - 1.0
    return jnp.where(d == 0.0, e, jnp.log(u) * (e / jnp.where(d == 0.0, 1.0, d)))


def _layer_norm(v, g, b):
    mu = jnp.mean(v, axis=-1, keepdims=True)
    c = v - mu
    var = jnp.mean(c * c, axis=-1, keepdims=True)
    return c * lax.rsqrt(var + LN_EPS) * g + b


def _row_parts_specs(parts, tm, width, grid_rank):
    if len(parts) == 1:
        return [pl.BlockSpec((tm, width), (lambda i: (i, 0)) if grid_rank == 1 else (lambda i, j: (i, 0)))]
    first_tiles = parts[0].shape[0] // tm
    if grid_rank == 1:
        return [pl.BlockSpec((tm, width), lambda i: (jnp.minimum(i, first_tiles - 1), 0)),
                pl.BlockSpec((tm, width), lambda i: (jnp.maximum(i - first_tiles, 0), 0))]
    return [pl.BlockSpec((tm, width), lambda i, j: (jnp.minimum(i, first_tiles - 1), 0)),
            pl.BlockSpec((tm, width), lambda i, j: (jnp.maximum(i - first_tiles, 0), 0))]


def _read_row_parts(x_refs, first_tiles):
    x = x_refs[0][...]
    if len(x_refs) == 2:
        x = jnp.where(pl.program_id(0) < first_tiles, x, x_refs[1][...])
    return x


def _inproj_kernel(*refs, n_parts, first_tiles, a_blocks):
    x_refs = refs[:n_parts]
    w_ref, oa_ref, ob_ref, xb_ref = refs[n_parts:]
    j = pl.program_id(1)

    @pl.when(j == 0)
    def _():
        xb_ref[...] = _read_row_parts(x_refs, first_tiles).astype(BF16)

    res = _dot(xb_ref[...], w_ref[...])

    @pl.when(j < a_blocks)
    def _():
        oa_ref[...] = res

    @pl.when(j >= a_blocks)
    def _():
        ob_ref[...] = res.astype(ob_ref.dtype)


def _in_proj(x_parts, w, tm=1024, tn=512):
    n = sum(a.shape[0] for a in x_parts)
    k = x_parts[0].shape[1]
    a_blocks = A_WIDTH // tn
    b_width = w.shape[1] - A_WIDTH
    first_tiles = x_parts[0].shape[0] // tm
    return pl.pallas_call(
        functools.partial(_inproj_kernel, n_parts=len(x_parts), first_tiles=first_tiles, a_blocks=a_blocks),
        grid=(n // tm, w.shape[1] // tn),
        in_specs=_row_parts_specs(x_parts, tm, k, 2) + [pl.BlockSpec((k, tn), lambda i, j: (0, j))],
        out_specs=[pl.BlockSpec((tm, tn), lambda i, j: (i, jnp.minimum(j, a_blocks - 1))),
                   pl.BlockSpec((tm, tn), lambda i, j: (i, jnp.maximum(j - a_blocks, 0)))],
        out_shape=[jax.ShapeDtypeStruct((n, A_WIDTH), F32), jax.ShapeDtypeStruct((n, b_width), BF16)],
        scratch_shapes=[pltpu.VMEM((tm, k), BF16)],
        compiler_params=_cparams(("parallel", "arbitrary")),
        name="in_proj",
    )(*x_parts, w)


def _shift_rows(x, prev_row, next_row, first, last):
    n = x.shape[0]
    row = lax.broadcasted_iota(jnp.int32, (n, 1), 0)
    prev_row = jnp.where(first, 0.0, prev_row)
    next_row = jnp.where(last, 0.0, next_row)
    x_prev = jnp.where(row == 0, prev_row, pltpu.roll(x, 1, 0))
    x_next = jnp.where(row == n - 1, next_row, pltpu.roll(x, n - 1, 0))
    return x_prev, x_next


def _convmix_kernel(cb_ref, cc_ref, ch_ref, ccp_ref, chp_ref, ccn_ref, chn_ref, w_ref, g_ref, o_ref, *, tiles_per_seq):
    i = pl.program_id(0)
    first = (i % tiles_per_seq) == 0
    last = (i % tiles_per_seq) == tiles_per_seq - 1
    u = cc_ref[...] * ch_ref[...]
    up = ccp_ref[SUBLANES - 1:SUBLANES, :] * chp_ref[SUBLANES - 1:SUBLANES, :]
    un = ccn_ref[0:1, :] * chn_ref[0:1, :]
    u_prev, u_next = _shift_rows(u, up, un, first, last)
    conv = u_prev * w_ref[0:1, :] + u * w_ref[1:2, :] + u_next * w_ref[2:3, :]
    y = cb_ref[...] * conv
    y = y * lax.rsqrt(jnp.mean(y * y, axis=-1, keepdims=True) + RMS_EPS)
    o_ref[...] = (y * g_ref[...]).astype(o_ref.dtype)


def _halo_specs(tt, width, col0, n_rows):
    cb = col0 // width
    r = tt // SUBLANES
    last_blk = n_rows // SUBLANES - 1
    prev = pl.BlockSpec((SUBLANES, width), lambda i: (jnp.maximum(i * r - 1, 0), cb))
    nxt = pl.BlockSpec((SUBLANES, width), lambda i: (jnp.minimum((i + 1) * r, last_blk), cb))
    return prev, nxt


def _conv_mix(proj_a, conv_w, norm_g, seq, tt=512):
    n = proj_a.shape[0]
    w = CONV_WIDTH
    main = lambda c0: pl.BlockSpec((tt, w), lambda i: (i, c0 // w))
    ccp, ccn = _halo_specs(tt, w, A_CC, n)
    chp, chn = _halo_specs(tt, w, A_CH, n)
    full = lambda shape: pl.BlockSpec(shape, lambda i: (0, 0))
    return pl.pallas_call(
        functools.partial(_convmix_kernel, tiles_per_seq=seq // tt),
        grid=(n // tt,),
        in_specs=[main(A_CB), main(A_CC), main(A_CH), ccp, chp, ccn, chn, full((3, w)), full((1, w))],
        out_specs=pl.BlockSpec((tt, w), lambda i: (i, 0)),
        out_shape=jax.ShapeDtypeStruct((n, w), BF16),
        compiler_params=_cparams(("parallel",)),
        name="conv_mix",
    )(proj_a, proj_a, proj_a, proj_a, proj_a, proj_a, proj_a, conv_w, norm_g.reshape(1, w))


def _ssd_kernel(*refs, direction, final, nc):
    if final:
        (xs_ref, xsp_ref, xsn_ref, bc_ref, bcp_ref, bcn_ref, dt_ref, z_ref, yf_ref,
         cwx_ref, cbx_ref, cwb_ref, cbb_ref, dtb_ref, alog_ref, e_ref, hm_ref, dsk_ref, ng_ref,
         o_ref, h_ref, y_ref) = refs
    else:
        (xs_ref, xsp_ref, xsn_ref, bc_ref, bcp_ref, bcn_ref, dt_ref,
         cwx_ref, cbx_ref, cwb_ref, cbb_ref, dtb_ref, alog_ref, e_ref, hm_ref,
         o_ref, h_ref) = refs
        y_ref = o_ref
    L = SSD_CHUNK
    c = pl.program_id(1)
    chunk = c if direction == 0 else nc - 1 - c
    first = chunk == 0
    last = chunk == nc - 1

    @pl.when(c == 0)
    def _():
        h_ref[...] = jnp.zeros_like(h_ref)

    def conv_silu(x_ref, p_ref, n_ref, w_ref, b_ref):
        x = x_ref[...]
        x_prev, x_next = _shift_rows(x, p_ref[SUBLANES - 1:SUBLANES, :], n_ref[0:1, :], first, last)
        v = x_prev * w_ref[0:1, :] + x * w_ref[1:2, :] + x_next * w_ref[2:3, :] + b_ref[...]
        return v * _sigmoid(v)

    xs = conv_silu(xs_ref, xsp_ref, xsn_ref, cwx_ref, cbx_ref)
    bc = conv_silu(bc_ref, bcp_ref, bcn_ref, cwb_ref, cbb_ref)
    bm = bc[:, :LANES]
    cm = bc[:, LANES:]

    dtr = dt_ref[...] + dtb_ref[...]
    dt = jnp.maximum(dtr, 0.0) + _log1p(jnp.exp(-jnp.abs(dtr)))
    dta = dt * (-jnp.exp(alog_ref[...]))

    li = lax.broadcasted_iota(jnp.int32, (L, L), 0)
    si = lax.broadcasted_iota(jnp.int32, (L, L), 1)
    valid = (si <= li) if direction == 0 else (si >= li)
    tri = jnp.where(valid, 1.0, 0.0).astype(BF16)
    acum = _dot_exact_rhs(tri, dta)
    acum_t = acum.T
    total = acum[L - 1:L, :] if direction == 0 else acum[0:1, :]
    dec_end = jnp.exp(total - acum)
    ea = jnp.exp(acum)

    e_mat = e_ref[...]
    dtx = _dot_split2_lhs(dt, e_mat)
    eax = _dot_split2_lhs(ea, e_mat)
    dex = _dot_split2_lhs(dec_end * dt, e_mat)

    xdt = xs * dtx
    xw = (xs * dex).astype(BF16)

    lane = lax.broadcasted_iota(jnp.int32, (1, LANES), 1)
    cm_b = cm.astype(BF16)
    bm_b = bm.astype(BF16)
    nt = (((1,), (1,)), ((), ()))

    h_prev = h_ref[...]
    y_off = _dot(cm_b, h_prev.astype(BF16)) * eax

    heads_per_group = SSD_HEADS // SSD_GROUPS
    for g in range(SSD_GROUPS):
        cg = jnp.where((lane // SSD_STATE) == g, cm, 0.0).astype(BF16)
        cb = lax.dot_general(cg, bm_b, nt, preferred_element_type=F32)
        for pair in range(heads_per_group // 2):
            acc = None
            for half in range(2):
                h = g * heads_per_group + pair * 2 + half
                hl = direction * SSD_HEADS + h
                seg = acum[:, hl:hl + 1] - acum_t[hl:hl + 1, :]
                w = (jnp.where(valid, jnp.exp(seg), 0.0) * cb).astype(BF16)
                c0 = (h // 2) * LANES
                rhs = jnp.where((lane // SSD_HEAD_DIM) == half, xdt[:, c0:c0 + LANES], 0.0).astype(BF16)
                part = _dot(w, rhs)
                acc = part if acc is None else acc + part
            c0 = (g * heads_per_group // 2 + pair) * LANES
            y_pair = acc + y_off[:, c0:c0 + LANES]
            if final:
                y_pair = yf_ref[:, c0:c0 + LANES] + y_pair + dsk_ref[:, c0:c0 + LANES] * xs[:, c0:c0 + LANES]
                zz = z_ref[:, c0:c0 + LANES]
                y_pair = y_pair * (zz * _sigmoid(zz))
            y_ref[:, c0:c0 + LANES] = y_pair

    cd = eax[L - 1:L, :] if direction == 0 else eax[0:1, :]
    s_new = _dot(bm.T.astype(BF16), xw)
    h_ref[...] = (cd * h_prev + s_new) * hm_ref[...]

    if final:
        gw = SSD_WIDTH // SSD_GROUPS
        for g in range(SSD_GROUPS):
            y = y_ref[:, g * gw:(g + 1) * gw]
            y = y * lax.rsqrt(jnp.mean(y * y, axis=-1, keepdims=True) + RMS_EPS)
            o_ref[:, g * gw:(g + 1) * gw] = (y * ng_ref[:, g * gw:(g + 1) * gw]).astype(o_ref.dtype)


def _ssd_pass(proj_a, z_and_yf, params, seq, direction):
    n = proj_a.shape[0]
    L = SSD_CHUNK
    nc = seq // L
    nseq = n // seq
    final = direction == 1
    cw, cb_, dtb, alog, e_mat, hmask, dsk, ng = params
    bcw = 2 * SSD_GROUPS * SSD_STATE

    def row_blk(b, c):
        ch = c if direction == 0 else nc - 1 - c
        return b * nc + ch

    def main(width, col0):
        return pl.BlockSpec((L, width), lambda b, c: (row_blk(b, c), col0 // width))

    r = L // SUBLANES
    last_blk = n // SUBLANES - 1

    def halo(width, col0):
        prev = pl.BlockSpec((SUBLANES, width), lambda b, c: (jnp.maximum(row_blk(b, c) * r - 1, 0), col0 // width))
        nxt = pl.BlockSpec((SUBLANES, width), lambda b, c: (jnp.minimum((row_blk(b, c) + 1) * r, last_blk), col0 // width))
        return prev, nxt

    xsp, xsn = halo(SSD_WIDTH, A_XS)
    bcp, bcn = halo(bcw, A_BC)
    full = lambda a: pl.BlockSpec(a.shape, lambda b, c: (0,) * a.ndim)

    in_specs = [main(SSD_WIDTH, A_XS), xsp, xsn, main(bcw, A_BC), bcp, bcn, main(LANES, A_DT)]
    args = [proj_a] * 7
    if final:
        in_specs += [main(SSD_WIDTH, A_Z), pl.BlockSpec((L, SSD_WIDTH), lambda b, c: (row_blk(b, c), 0))]
        args += [proj_a, z_and_yf]
    consts = [cw[:, :SSD_WIDTH], cb_[:, :SSD_WIDTH], cw[:, SSD_WIDTH:], cb_[:, SSD_WIDTH:], dtb, alog, e_mat, hmask]
    if final:
        consts += [dsk, ng]
    in_specs += [full(a) for a in consts]
    args += consts
    return pl.pallas_call(
        functools.partial(_ssd_kernel, direction=direction, final=final, nc=nc),
        grid=(nseq, nc),
        in_specs=in_specs,
        out_specs=pl.BlockSpec((L, SSD_WIDTH), lambda b, c: (row_blk(b, c), 0)),
        out_shape=jax.ShapeDtypeStruct((n, SSD_WIDTH), BF16 if final else F32),
        scratch_shapes=[pltpu.VMEM((LANES, SSD_WIDTH), F32)] + ([pltpu.VMEM((L, SSD_WIDTH), F32)] if final else []),
        compiler_params=_cparams(("parallel", "arbitrary")),
        name="ssd_bwd" if final else "ssd_fwd",
    )(*args)


def _ssd_consts(ssd_conv_w, ssd_conv_b, ssd_dt_bias, ssd_a_log, ssd_d, ssd_norm_g):
    pad = LANES - 2 * SSD_HEADS
    dtb = jnp.pad(ssd_dt_bias.reshape(1, -1), ((0, 0), (0, pad)))
    alog = jnp.pad(ssd_a_log.reshape(1, -1), ((0, 0), (0, pad)))
    col_head = np.arange(SSD_WIDTH) // SSD_HEAD_DIM
    e_mats = [jnp.asarray((np.arange(LANES)[:, None] == d * SSD_HEADS + col_head[None, :]), BF16) for d in range(2)]
    row_group = np.arange(LANES) // SSD_STATE
    col_group = col_head // (SSD_HEADS // SSD_GROUPS)
    hmask = jnp.asarray(row_group[:, None] == col_group[None, :], F32)
    dsk = jnp.repeat(ssd_d, SSD_HEAD_DIM).reshape(1, SSD_WIDTH)
    ng = ssd_norm_g.reshape(1, SSD_WIDTH)
    cb_ = ssd_conv_b.reshape(1, -1)
    return [(ssd_conv_w, cb_, dtb, alog, e_mats[d], hmask, dsk, ng) for d in range(2)]


NA_QROWS = 4
NA_KROWS = NA_QROWS + WIN_H


def _na_kernel(kb_ref, cls_ref, q_ref, k_ref, v_ref, bias_ref, o_ref, *, blocks_per_step):
    step = pl.program_id(2)
    lane = lax.broadcasted_iota(jnp.int32, (1, LANES), 1)
    nt = (((1,), (1,)), ((), ()))
    nq = NA_QROWS * GRID_W
    nk = NA_KROWS * GRID_W
    for j in range(blocks_per_step):
        blk = step * blocks_per_step + j
        k0 = pl.multiple_of(kb_ref[blk] * GRID_W, GRID_W)
        cls = cls_ref[blk]
        q = q_ref[j * nq:(j + 1) * nq, :] * (NA_HEAD_DIM ** -0.5)
        kk = k_ref[pl.ds(k0, nk), :]
        vv = v_ref[pl.ds(k0, nk), :]
        outs = []
        for hh in range(2):
            qm = jnp.where((lane // NA_HEAD_DIM) == hh, q, jnp.zeros_like(q))
            s = lax.dot_general(qm, kk, nt, preferred_element_type=F32) + bias_ref[cls, hh]
            m = jnp.max(s, axis=-1, keepdims=True)
            p = jnp.exp(s - m)
            l = jnp.sum(p, axis=-1, keepdims=True)
            outs.append(_dot(p.astype(BF16), vv) / l)
        o = jnp.where((lane // NA_HEAD_DIM) == 0, outs[0], outs[1])
        o_ref[j * nq:(j + 1) * nq, :] = o.astype(o_ref.dtype)


def _na_tables(rel_bias, rows):
    qc = np.arange(GRID_W)
    cs = np.clip(qc - WIN_W // 2, 0, GRID_W - WIN_W)
    kc = np.arange(GRID_W)
    col_valid = (kc[None, :] >= cs[:, None]) & (kc[None, :] < cs[:, None] + WIN_W)
    dj = np.clip(kc[None, :] - qc[:, None] + WIN_W - 1, 0, 2 * WIN_W - 2)
    nblk = rows // NA_QROWS
    kb = np.clip(np.arange(nblk) * NA_QROWS - WIN_H // 2, 0, rows - NA_KROWS)
    patterns, cls = [], []
    for b in range(nblk):
        r = b * NA_QROWS + np.arange(NA_QROWS)
        rs = np.clip(r - WIN_H // 2, 0, rows - WIN_H)
        key = (tuple(rs - kb[b]), int(kb[b] - b * NA_QROWS))
        if key not in patterns:
            patterns.append(key)
        cls.append(patterns.index(key))
    di_sel, masks = [], []
    for off, shift in patterns:
        kr = np.arange(NA_KROWS)[None, :]
        i = np.arange(NA_QROWS)[:, None]
        rel = kr - np.asarray(off)[:, None]
        row_valid = (rel >= 0) & (rel < WIN_H)
        di = np.clip(shift + kr - i + WIN_H - 1, 0, 2 * WIN_H - 2)
        di_sel.append(np.eye(2 * WIN_H - 1, dtype=np.float32)[di])
        masks.append(row_valid[:, None, :, None] & col_valid[None, :, None, :])
    dj_sel = np.eye(2 * WIN_W - 1, dtype=np.float32)[dj]
    tab = jnp.einsum('pikd,hde,qwe->phiqkw', np.stack(di_sel), rel_bias.astype(F32), dj_sel,
                     precision=lax.Precision.HIGHEST)
    tab = jnp.where(np.stack(masks)[:, None], tab, NEG_BIG)
    tab = tab.reshape(len(patterns), NA_HEADS, NA_QROWS * GRID_W, NA_KROWS * GRID_W)
    return jnp.asarray(kb, jnp.int32), jnp.asarray(cls, jnp.int32), tab


def _neighbourhood_attention(proj_b, rel_bias, seq, blocks_per_step=4):
    n = proj_b.shape[0]
    nseq = n // seq
    rows = seq // GRID_W
    kb, cls, bias_tab = _na_tables(rel_bias, rows)
    tq = blocks_per_step * NA_QROWS * GRID_W
    pairs = NA_HEADS // 2
    steps = seq // tq
    ncls = bias_tab.shape[0]
    grid_spec = pltpu.PrefetchScalarGridSpec(
        num_scalar_prefetch=2,
        grid=(nseq, pairs, steps),
        in_specs=[pl.BlockSpec((tq, LANES), lambda b, p, r, kb_, cls_: (b * steps + r, p)),
                  pl.BlockSpec((seq, LANES), lambda b, p, r, kb_, cls_: (b, pairs + p)),
                  pl.BlockSpec((seq, LANES), lambda b, p, r, kb_, cls_: (b, 2 * pairs + p)),
                  pl.BlockSpec((ncls, 2) + bias_tab.shape[2:], lambda b, p, r, kb_, cls_: (0, p, 0, 0))],
        out_specs=pl.BlockSpec((tq, LANES), lambda b, p, r, kb_, cls_: (b * steps + r, p)),
    )
    return pl.pallas_call(
        functools.partial(_na_kernel, blocks_per_step=blocks_per_step),
        grid_spec=grid_spec,
        out_shape=jax.ShapeDtypeStruct((n, NA_WIDTH), F32),
        compiler_params=_cparams(("parallel", "arbitrary", "arbitrary")),
        name="nbr_attn",
    )(kb, cls, proj_b, proj_b, proj_b, bias_tab)


def _rms_rows(y, g):
    return y * lax.rsqrt(jnp.mean(y * y, axis=-1, keepdims=True) + RMS_EPS) * g


def _outproj_kernel(*refs, router, n_parts, first_tiles):
    yc_ref, ys_ref, yn_ref = refs[:3]
    x_refs = refs[3:3 + n_parts]
    rest = refs[3 + n_parts:]
    if router:
        w_ref, nag_ref, g_ref, b_ref, rw_ref, o_ref, idx_ref, gate_ref = rest
    else:
        w_ref, nag_ref, g_ref, b_ref, o_ref, ob_ref = rest
    c1 = CONV_WIDTH
    c2 = CONV_WIDTH + SSD_WIDTH
    yn = _rms_rows(yn_ref[...], nag_ref[...]).astype(BF16)
    mix = _dot(yc_ref[...], w_ref[0:c1, :]) + _dot(ys_ref[...], w_ref[c1:c2, :]) + _dot(yn, w_ref[c2:, :])
    out = _layer_norm(ALPHA * _read_row_parts(x_refs, first_tiles) + mix, g_ref[...], b_ref[...])
    o_ref[...] = out
    if not router:
        ob_ref[...] = out.astype(BF16)
    if router:
        o1, o2, _ = _split3(out)
        t1 = _dot(o1, rw_ref[...])
        t2 = _dot(o2, rw_ref[...])
        sh1 = LANES - N_EXPERTS
        sh2 = LANES - 2 * N_EXPERTS
        logits = ((pltpu.roll(t2, sh1, 1) + pltpu.roll(t1, sh2, 1)) + (t2 + pltpu.roll(t1, sh1, 1))) + t1
        tm = logits.shape[0]
        lane = lax.broadcasted_iota(jnp.int32, (tm, LANES), 1)
        lg = jnp.where(lane < N_EXPERTS, logits, -jnp.inf)
        m1 = jnp.max(lg, axis=-1, keepdims=True)
        i1 = jnp.min(jnp.where(lg == m1, lane, LANES), axis=-1, keepdims=True)
        lg2 = jnp.where(lane == i1, -jnp.inf, lg)
        m2 = jnp.max(lg2, axis=-1, keepdims=True)
        i2 = jnp.min(jnp.where(lg2 == m2, lane, LANES), axis=-1, keepdims=True)
        e2 = jnp.exp(m2 - m1)
        den = 1.0 + e2
        idx_ref[...] = jnp.where(lane == 0, i1, jnp.where(lane == 1, i2, 0))
        gate_ref[...] = jnp.where(lane == 0, 1.0 / den, jnp.where(lane == 1, e2 / den, 0.0))


def _out_proj_ln(yc, ys, yn, x_parts, w_out, na_g, ln_g, ln_b, router_w=None, tm=512):
    n = yc.shape[0]
    router = router_w is not None
    row = lambda w: pl.BlockSpec((tm, w), lambda i: (i, 0))
    full = lambda a: pl.BlockSpec(a.shape, lambda i: (0,) * a.ndim)
    consts = [w_out, na_g.reshape(1, -1), ln_g.reshape(1, -1), ln_b.reshape(1, -1)]
    if router:
        consts.append(router_w)
    out_shape = [jax.ShapeDtypeStruct((n, D_MODEL), F32)]
    out_specs = [row(D_MODEL)]
    if router:
        out_shape += [jax.ShapeDtypeStruct((n, LANES), jnp.int32), jax.ShapeDtypeStruct((n, LANES), F32)]
        out_specs += [row(LANES), row(LANES)]
    else:
        out_shape += [jax.ShapeDtypeStruct((n, D_MODEL), BF16)]
        out_specs += [row(D_MODEL)]
    res = pl.pallas_call(
        functools.partial(_outproj_kernel, router=router, n_parts=len(x_parts),
                          first_tiles=x_parts[0].shape[0] // tm),
        grid=(n // tm,),
        in_specs=[row(CONV_WIDTH), row(SSD_WIDTH), row(NA_WIDTH)] + _row_parts_specs(x_parts, tm, D_MODEL, 1)
        + [full(a) for a in consts],
        out_specs=out_specs,
        out_shape=out_shape,
        compiler_params=_cparams(("parallel",)),
        name="out_proj_router" if router else "out_proj",
    )(yc, ys, yn, *x_parts, *consts)
    return res


def _ffn_kernel(be_ref, bv_ref, x_ref, wg_ref, wu_ref, wd_ref, o_ref, *, packed):
    i = pl.program_id(0)
    j = pl.program_id(1)
    valid = bv_ref[i] != 0

    def partial_out():
        if packed:
            lo, hi = _unpack_bf16_pairs(x_ref[...])
            c = lo.shape[1]
            g = _dot(lo, wg_ref[0, :c, :].astype(BF16)) + _dot(hi, wg_ref[0, c:, :].astype(BF16))
            u = _dot(lo, wu_ref[0, :c, :].astype(BF16)) + _dot(hi, wu_ref[0, c:, :].astype(BF16))
        else:
            x = x_ref[...]
            g = _dot(x, wg_ref[0].astype(BF16))
            u = _dot(x, wu_ref[0].astype(BF16))
        h = ((g * _sigmoid(g)) * u).astype(BF16)
        return _dot(h, wd_ref[0].astype(BF16))

    @pl.when(jnp.logical_and(valid, j == 0))
    def _():
        o_ref[...] = partial_out()

    @pl.when(jnp.logical_and(valid, j != 0))
    def _():
        o_ref[...] += partial_out()

    @pl.when(jnp.logical_and(jnp.logical_not(valid), j == 0))
    def _():
        o_ref[...] = jnp.zeros_like(o_ref)


def _ffn_blocks(x, block_e, block_valid, w_gate, w_up, w_down, tm, tf=FF_TILE):
    n, xw = x.shape
    packed = x.dtype == jnp.uint32
    d = w_gate.shape[1]
    f = w_gate.shape[2]
    nf = f // tf
    col = lambda i, j, be, bv: (be[i], 0, jnp.where(bv[i] != 0, j, nf - 1))
    rowj = lambda i, j, be, bv: (be[i], jnp.where(bv[i] != 0, j, nf - 1), 0)
    grid_spec = pltpu.PrefetchScalarGridSpec(
        num_scalar_prefetch=2,
        grid=(n // tm, nf),
        in_specs=[pl.BlockSpec((tm, xw), lambda i, j, be, bv: (i, 0)),
                  pl.BlockSpec((1, d, tf), col), pl.BlockSpec((1, d, tf), col), pl.BlockSpec((1, tf, d), rowj)],
        out_specs=pl.BlockSpec((tm, d), lambda i, j, be, bv: (i, 0)),
    )
    return pl.pallas_call(
        functools.partial(_ffn_kernel, packed=packed),
        grid_spec=grid_spec,
        out_shape=jax.ShapeDtypeStruct((n, d), F32),
        compiler_params=_cparams(("parallel", "arbitrary")),
        name="swiglu_blocks",
    )(block_e, block_valid, x, w_gate, w_up, w_down)


def _add_ln_kernel(x_ref, f_ref, g_ref, b_ref, o_ref):
    o_ref[...] = _layer_norm(ALPHA * x_ref[...] + f_ref[...], g_ref[...], b_ref[...])


def _add_ln(x, f, g, b, tm=512):
    n, d = x.shape
    row = pl.BlockSpec((tm, d), lambda i: (i, 0))
    vec = pl.BlockSpec((1, d), lambda i: (0, 0))
    return pl.pallas_call(
        _add_ln_kernel, grid=(n // tm,), in_specs=[row, row, vec, vec], out_specs=row,
        out_shape=jax.ShapeDtypeStruct((n, d), F32), compiler_params=_cparams(("parallel",)), name="add_ln",
    )(x, f, g.reshape(1, d), b.reshape(1, d))


def _combine_ln_kernel(i0_ref, i1_ref, src_ref, x_ref, gate_ref, g_ref, b_ref, oa_ref, ob_ref, b0_ref, b1_ref, sem,
                       *, tm, first_tiles):
    _start_row_gather(src_ref, i0_ref, b0_ref, sem.at[0], tm)
    _start_row_gather(src_ref, i1_ref, b1_ref, sem.at[1], tm)
    _wait_row_gather(src_ref, b0_ref, sem.at[0], tm)
    _wait_row_gather(src_ref, b1_ref, sem.at[1], tm)
    f = b0_ref[...] * gate_ref[:, 0:1] + b1_ref[...] * gate_ref[:, 1:2]
    res = _layer_norm(ALPHA * x_ref[...] + f, g_ref[...], b_ref[...])
    i = pl.program_id(0)

    @pl.when(i < first_tiles)
    def _():
        oa_ref[...] = res

    @pl.when(i >= first_tiles)
    def _():
        ob_ref[...] = res


def _combine_ln(x, outs, d0, d1, gates, g, b, n_first, tm=512):
    n, d = x.shape
    first_tiles = n_first // tm
    row = pl.BlockSpec((tm, d), lambda i: (i, 0))
    vec = pl.BlockSpec((1, d), lambda i: (0, 0))
    idx = pl.BlockSpec((1, 1, tm), lambda i: (i, 0, 0), memory_space=pltpu.SMEM)
    return pl.pallas_call(
        functools.partial(_combine_ln_kernel, tm=tm, first_tiles=first_tiles), grid=(n // tm,),
        in_specs=[idx, idx, pl.BlockSpec(memory_space=pl.ANY), row, pl.BlockSpec((tm, LANES), lambda i: (i, 0)), vec, vec],
        out_specs=[pl.BlockSpec((tm, d), lambda i: (jnp.minimum(i, first_tiles - 1), 0)),
                   pl.BlockSpec((tm, d), lambda i: (jnp.maximum(i - first_tiles, 0), 0))],
        out_shape=[jax.ShapeDtypeStruct((n_first, d), F32), jax.ShapeDtypeStruct((n - n_first, d), F32)],
        scratch_shapes=[pltpu.VMEM((tm, d), F32), pltpu.VMEM((tm, d), F32), pltpu.SemaphoreType.DMA((2,))],
        compiler_params=_cparams(("arbitrary",)), name="combine_ln",
    )(d0.reshape(n // tm, 1, tm), d1.reshape(n // tm, 1, tm), outs, x, gates, g.reshape(1, d), b.reshape(1, d))


GATHER_UNROLL = 8


def _start_row_gather(src_ref, idx_ref, buf_ref, sem, tr):
    def issue(r, carry):
        pltpu.make_async_copy(src_ref.at[pl.ds(idx_ref[0, 0, r], 1), :], buf_ref.at[pl.ds(r, 1), :], sem).start()
        return carry

    lax.fori_loop(0, tr, issue, 0, unroll=GATHER_UNROLL)


def _wait_row_gather(src_ref, buf_ref, sem, tr):
    pltpu.make_async_copy(src_ref.at[pl.ds(0, tr), :], buf_ref, sem).wait()


HI16 = 0xFFFF0000


def _pack_bf16_pairs(x):
    c = x.shape[1] // 2
    lo = lax.bitcast_convert_type(x[:, :c].astype(BF16).astype(F32), jnp.uint32)
    hi = lax.bitcast_convert_type(x[:, c:].astype(BF16).astype(F32), jnp.uint32)
    return (hi & jnp.uint32(HI16)) | (lo >> 16)


def _unpack_bf16_pairs(u):
    lo = lax.bitcast_convert_type(u << 16, F32).astype(BF16)
    hi = lax.bitcast_convert_type(u & jnp.uint32(HI16), F32).astype(BF16)
    return lo, hi


def _dispatch_kernel(d0_ref, d1_ref, x_ref, init_ref, xs_ref, pk_ref, sem, *, tm):
    del init_ref
    pk_ref[...] = _pack_bf16_pairs(x_ref[...])

    def issue(r, carry):
        row = pk_ref.at[pl.ds(r, 1), :]
        pltpu.make_async_copy(row, xs_ref.at[pl.ds(d0_ref[0, 0, r], 1), :], sem.at[0]).start()
        pltpu.make_async_copy(row, xs_ref.at[pl.ds(d1_ref[0, 0, r], 1), :], sem.at[1]).start()
        return carry

    lax.fori_loop(0, tm, issue, 0, unroll=GATHER_UNROLL)
    for k in range(TOP_K):
        pltpu.make_async_copy(pk_ref, xs_ref.at[pl.ds(0, tm), :], sem.at[k]).wait()


def _dispatch_rows(x, d0, d1, p_len, tm=512):
    n, d = x.shape
    idx = pl.BlockSpec((1, 1, tm), lambda i: (i, 0, 0), memory_space=pltpu.SMEM)
    return pl.pallas_call(
        functools.partial(_dispatch_kernel, tm=tm),
        grid=(n // tm,),
        in_specs=[idx, idx, pl.BlockSpec((tm, d), lambda i: (i, 0)), pl.BlockSpec(memory_space=pl.ANY)],
        out_specs=pl.BlockSpec(memory_space=pl.ANY),
        out_shape=jax.ShapeDtypeStruct((p_len, d // 2), jnp.uint32),
        scratch_shapes=[pltpu.VMEM((tm, d // 2), jnp.uint32), pltpu.SemaphoreType.DMA((TOP_K,))],
        input_output_aliases={3: 0},
        compiler_params=_cparams(("arbitrary",)),
        name="dispatch_rows",
    )(d0.reshape(n // tm, 1, tm), d1.reshape(n // tm, 1, tm), x, jnp.zeros((p_len, d // 2), jnp.uint32))


def _routing_tables(idx, n, tm):
    flat_e = idx[:, :TOP_K].reshape(-1)
    onehot = (flat_e[:, None] == jnp.arange(N_EXPERTS, dtype=jnp.int32)[None, :]).astype(jnp.int32)
    incl = jnp.cumsum(onehot, axis=0)
    rank = jnp.sum((incl - onehot) * onehot, axis=1)
    counts = incl[-1]
    padded = ((counts + tm - 1) // tm) * tm
    pend = jnp.cumsum(padded)
    pstart = pend - padded
    dest = pstart[flat_e] + rank
    n_blocks = -(-(n * TOP_K) // tm) + N_EXPERTS
    starts = jnp.arange(n_blocks, dtype=jnp.int32) * tm
    block_valid = (starts < pend[-1]).astype(jnp.int32)
    block_e = jnp.minimum(jnp.searchsorted(pend, starts, side='right'), N_EXPERTS - 1).astype(jnp.int32)
    last_e = block_e[jnp.maximum(jnp.sum(block_valid) - 1, 0)]
    block_e = jnp.where(block_valid != 0, block_e, last_e)
    dest2 = dest.reshape(n, TOP_K)
    return n_blocks * tm, block_e, block_valid, dest2[:, 0], dest2[:, 1]


def _prep_w_in(w):
    cb, cc, ch = w[:, 0:512], w[:, 512:1024], w[:, 1024:1536]
    z = w[:, 1536:2560]
    xs, bc = w[:, 2560:3584], w[:, 3584:3840]
    dt = w[:, 3840:3872]
    qkv = w[:, 3872:5408]
    pad = jnp.zeros((D_MODEL, A_WIDTH - (A_DT + 2 * SSD_HEADS)), w.dtype)
    return jnp.concatenate([z, xs, cb, cc, ch, bc, dt, pad, qkv], axis=1).astype(BF16)


def _prep_router_w(rw):
    r1 = rw.astype(BF16)
    r2 = (rw - r1.astype(F32)).astype(BF16)
    r3 = (rw - r1.astype(F32) - r2.astype(F32)).astype(BF16)
    pad = jnp.zeros((rw.shape[0], LANES - 3 * N_EXPERTS), BF16)
    return jnp.concatenate([r1, r2, r3, pad], axis=1)


def _trunk(x_parts, seq, p):
    n = sum(a.shape[0] for a in x_parts)
    n_first = x_parts[0].shape[0]
    for l in range(DEPTH):
        proj_a, proj_b = _in_proj(x_parts, _prep_w_in(p['w_in'][l]))
        y_conv = _conv_mix(proj_a, p['conv_w'][l], p['conv_norm_g'][l], seq)
        cf, cbw = _ssd_consts(p['ssd_conv_w'][l], p['ssd_conv_b'][l], p['ssd_dt_bias'][l], p['ssd_a_log'][l],
                              p['ssd_d'][l], p['ssd_norm_g'][l])
        y_fwd = _ssd_pass(proj_a, None, cf, seq, 0)
        y_ssd = _ssd_pass(proj_a, y_fwd, cbw, seq, 1)
        y_na = _neighbourhood_attention(proj_b, p['na_rel_bias'][l], seq)
        w_out = p['w_out'][l].astype(BF16)
        if l % 2 == 0:
            x, x_b16 = _out_proj_ln(y_conv, y_ssd, y_na, x_parts, w_out, p['na_norm_g'][l], p['ln_mix_g'][l],
                                    p['ln_mix_b'][l])
            padf = ((0, 0), (0, D_FF_PAD - D_FF))
            wg = jnp.pad(p['ffn_w_gate'][l // 2], padf).astype(BF16)[None]
            wu = jnp.pad(p['ffn_w_up'][l // 2], padf).astype(BF16)[None]
            wd = jnp.pad(p['ffn_w_down'][l // 2], ((0, D_FF_PAD - D_FF), (0, 0))).astype(BF16)[None]
            nb = n // MOE_TM
            f = _ffn_blocks(x_b16, jnp.zeros((nb,), jnp.int32), jnp.ones((nb,), jnp.int32), wg, wu, wd, MOE_TM)
            x = _add_ln(x, f, p['ln_ffn_g'][l], p['ln_ffn_b'][l])
            outs_split = (x[:n_first], x[n_first:])
        else:
            x, idx, gates = _out_proj_ln(y_conv, y_ssd, y_na, x_parts, w_out, p['na_norm_g'][l], p['ln_mix_g'][l],
                                         p['ln_mix_b'][l], _prep_router_w(p['router_w'][l // 2]))
            p_len, block_e, block_valid, d0, d1 = _routing_tables(idx, n, MOE_TM)
            xs = _dispatch_rows(x, d0, d1, p_len)
            outs = _ffn_blocks(xs, block_e, block_valid, p['moe_w_gate'][l // 2], p['moe_w_up'][l // 2],
                               p['moe_w_down'][l // 2], MOE_TM)
            outs_split = _combine_ln(x, outs, d0, d1, gates, p['ln_ffn_g'][l], p['ln_ffn_b'][l], n_first)
            if l + 1 < DEPTH:
                x = jnp.concatenate(outs_split, axis=0)
        x_parts = [x]
    return outs_split


def kernel(x_prompt, x_sample, w_in, conv_w, conv_norm_g, ssd_conv_w, ssd_conv_b, ssd_dt_bias, ssd_a_log, ssd_d,
           ssd_norm_g, na_rel_bias, na_norm_g, w_out, ln_mix_g, ln_mix_b, ln_ffn_g, ln_ffn_b, ffn_w_gate, ffn_w_up,
           ffn_w_down, router_w, moe_w_gate, moe_w_up, moe_w_down):
    p = dict(w_in=w_in, conv_w=conv_w, conv_norm_g=conv_norm_g, ssd_conv_w=ssd_conv_w, ssd_conv_b=ssd_conv_b,
             ssd_dt_bias=ssd_dt_bias, ssd_a_log=ssd_a_log, ssd_d=ssd_d, ssd_norm_g=ssd_norm_g,
             na_rel_bias=na_rel_bias, na_norm_g=na_norm_g, w_out=w_out, ln_mix_g=ln_mix_g, ln_mix_b=ln_mix_b,
             ln_ffn_g=ln_ffn_g, ln_ffn_b=ln_ffn_b, ffn_w_gate=ffn_w_gate, ffn_w_up=ffn_w_up, ffn_w_down=ffn_w_down,
             router_w=router_w, moe_w_gate=moe_w_gate, moe_w_up=moe_w_up, moe_w_down=moe_w_down)
    bp, seq, d = x_prompt.shape
    bs, seq_s, _ = x_sample.shape
    assert seq == seq_s, "both request groups must share one sequence length"
    yp, ys = _trunk([x_prompt.reshape(bp * seq, d), x_sample.reshape(bs * seq, d)], seq, p)
    return yp.reshape(bp, seq, d), ys.reshape(bs, seq, d)
```

```python
import functools

import numpy as np
import jax
import jax.numpy as jnp
from jax import lax
from jax.experimental import pallas as pl
from jax.experimental.pallas import tpu as pltpu

F32 = jnp.float32
BF16 = jnp.bfloat16

D_MODEL = 2048
DEPTH = 2
GRID_W = 64
CONV_WIDTH = 512
SSD_HEADS = 16
SSD_HEAD_DIM = 64
SSD_WIDTH = 1024
SSD_GROUPS = 2
SSD_STATE = 64
SSD_CHUNK = 128
NA_HEADS = 8
NA_HEAD_DIM = 64
NA_WIDTH = 512
WIN_H = 8
WIN_W = 16
D_FF = 5504
N_EXPERTS = 8
TOP_K = 2
D_FF_EXPERT = 7168
ALPHA = (2 * DEPTH) ** 0.25
LN_EPS = 1e-5
RMS_EPS = 1e-6

LANES = 128
SUBLANES = 8
NEG_BIG = -1e30

A_Z, A_XS, A_CB, A_CC, A_CH, A_BC, A_DT, A_WIDTH = 0, 1024, 2048, 2560, 3072, 3584, 3840, 4096
FF_TILE = 512
D_FF_PAD = 5632
MOE_TM = 1024
VMEM_LIMIT = 56 * 1024 * 1024


def _cparams(sem, vmem=VMEM_LIMIT):
    return pltpu.CompilerParams(dimension_semantics=sem, vmem_limit_bytes=vmem)


def _sigmoid(x):
    return 1.0 / (1.0 + jnp.exp(-x))


def _split3(x):
    x1 = x.astype(BF16)
    r1 = x - x1.astype(F32)
    x2 = r1.astype(BF16)
    r2 = r1 - x2.astype(F32)
    return x1, x2, r2.astype(BF16)


def _dot(a, b):
    return jnp.dot(a, b, preferred_element_type=F32)


def _dot_exact_rhs(m, x):
    x1, x2, x3 = _split3(x)
    return (_dot(m, x3) + _dot(m, x2)) + _dot(m, x1)


def _dot_split2_lhs(x, m):
    x1, x2, _ = _split3(x)
    return _dot(x2, m) + _dot(x1, m)


def _log1p(e):
    u = 1.0 + e
    d = u - 1.0
    return jnp.where(d == 0.0, e, jnp.log(u) * (e / jnp.where(d == 0.0, 1.0, d)))


def _layer_norm(v, g, b):
    mu = jnp.mean(v, axis=-1, keepdims=True)
    c = v - mu
    var = jnp.mean(c * c, axis=-1, keepdims=True)
    return c * lax.rsqrt(var + LN_EPS) * g + b


def _row_parts_specs(parts, tm, width, grid_rank):
    if len(parts) == 1:
        return [pl.BlockSpec((tm, width), (lambda i: (i, 0)) if grid_rank == 1 else (lambda i, j: (i, 0)))]
    first_tiles = parts[0].shape[0] // tm
    if grid_rank == 1:
        return [pl.BlockSpec((tm, width), lambda i: (jnp.minimum(i, first_tiles - 1), 0)),
                pl.BlockSpec((tm, width), lambda i: (jnp.maximum(i - first_tiles, 0), 0))]
    return [pl.BlockSpec((tm, width), lambda i, j: (jnp.minimum(i, first_tiles - 1), 0)),
            pl.BlockSpec((tm, width), lambda i, j: (jnp.maximum(i - first_tiles, 0), 0))]


def _read_row_parts(x_refs, first_tiles, rows=slice(None)):
    x = x_refs[0][rows, :]
    if len(x_refs) == 2:
        x = jnp.where(pl.program_id(0) < first_tiles, x, x_refs[1][rows, :])
    return x


def _inproj_kernel(*refs, n_parts, first_tiles, a_blocks):
    x_refs = refs[:n_parts]
    w_ref, oa_ref, ob_ref, xb_ref = refs[n_parts:]
    j = pl.program_id(1)

    @pl.when(j == 0)
    def _():
        xb_ref[...] = _read_row_parts(x_refs, first_tiles).astype(BF16)

    res = _dot(xb_ref[...], w_ref[...])

    @pl.when(j < a_blocks)
    def _():
        oa_ref[...] = res

    @pl.when(j >= a_blocks)
    def _():
        ob_ref[...] = res.astype(ob_ref.dtype)


def _in_proj(x_parts, w, tm=1024, tn=512):
    n = sum(a.shape[0] for a in x_parts)
    k = x_parts[0].shape[1]
    a_blocks = A_WIDTH // tn
    b_width = w.shape[1] - A_WIDTH
    first_tiles = x_parts[0].shape[0] // tm
    return pl.pallas_call(
        functools.partial(_inproj_kernel, n_parts=len(x_parts), first_tiles=first_tiles, a_blocks=a_blocks),
        grid=(n // tm, w.shape[1] // tn),
        in_specs=_row_parts_specs(x_parts, tm, k, 2) + [pl.BlockSpec((k, tn), lambda i, j: (0, j))],
        out_specs=[pl.BlockSpec((tm, tn), lambda i, j: (i, jnp.minimum(j, a_blocks - 1))),
                   pl.BlockSpec((tm, tn), lambda i, j: (i, jnp.maximum(j - a_blocks, 0)))],
        out_shape=[jax.ShapeDtypeStruct((n, A_WIDTH), F32), jax.ShapeDtypeStruct((n, b_width), BF16)],
        scratch_shapes=[pltpu.VMEM((tm, k), BF16)],
        compiler_params=_cparams(("parallel", "arbitrary")),
        name="in_proj",
    )(*x_parts, w)


def _shift_rows(x, prev_row, next_row, first, last):
    n = x.shape[0]
    row = lax.broadcasted_iota(jnp.int32, (n, 1), 0)
    prev_row = jnp.where(first, 0.0, prev_row)
    next_row = jnp.where(last, 0.0, next_row)
    x_prev = jnp.where(row == 0, prev_row, pltpu.roll(x, 1, 0))
    x_next = jnp.where(row == n - 1, next_row, pltpu.roll(x, n - 1, 0))
    return x_prev, x_next


def _convmix_kernel(cb_ref, cc_ref, ch_ref, ccp_ref, chp_ref, ccn_ref, chn_ref, w_ref, g_ref, o_ref, *, tiles_per_seq):
    i = pl.program_id(0)
    first = (i % tiles_per_seq) == 0
    last = (i % tiles_per_seq) == tiles_per_seq - 1
    u = cc_ref[...] * ch_ref[...]
    up = ccp_ref[SUBLANES - 1:SUBLANES, :] * chp_ref[SUBLANES - 1:SUBLANES, :]
    un = ccn_ref[0:1, :] * chn_ref[0:1, :]
    u_prev, u_next = _shift_rows(u, up, un, first, last)
    conv = u_prev * w_ref[0:1, :] + u * w_ref[1:2, :] + u_next * w_ref[2:3, :]
    y = cb_ref[...] * conv
    y = y * lax.rsqrt(jnp.mean(y * y, axis=-1, keepdims=True) + RMS_EPS)
    o_ref[...] = (y * g_ref[...]).astype(o_ref.dtype)


def _halo_specs(tt, width, col0, n_rows):
    cb = col0 // width
    r = tt // SUBLANES
    last_blk = n_rows // SUBLANES - 1
    prev = pl.BlockSpec((SUBLANES, width), lambda i: (jnp.maximum(i * r - 1, 0), cb))
    nxt = pl.BlockSpec((SUBLANES, width), lambda i: (jnp.minimum((i + 1) * r, last_blk), cb))
    return prev, nxt


def _conv_mix(proj_a, conv_w, norm_g, seq, tt=512):
    n = proj_a.shape[0]
    w = CONV_WIDTH
    main = lambda c0: pl.BlockSpec((tt, w), lambda i: (i, c0 // w))
    ccp, ccn = _halo_specs(tt, w, A_CC, n)
    chp, chn = _halo_specs(tt, w, A_CH, n)
    full = lambda shape: pl.BlockSpec(shape, lambda i: (0, 0))
    return pl.pallas_call(
        functools.partial(_convmix_kernel, tiles_per_seq=seq // tt),
        grid=(n // tt,),
        in_specs=[main(A_CB), main(A_CC), main(A_CH), ccp, chp, ccn, chn, full((3, w)), full((1, w))],
        out_specs=pl.BlockSpec((tt, w), lambda i: (i, 0)),
        out_shape=jax.ShapeDtypeStruct((n, w), BF16),
        compiler_params=_cparams(("parallel",)),
        name="conv_mix",
    )(proj_a, proj_a, proj_a, proj_a, proj_a, proj_a, proj_a, conv_w, norm_g.reshape(1, w))


SSD_CHUNKS_PER_STEP = 2


def _ssd_kernel(*refs, direction, final, tiles):
    if final:
        (xs_ref, xsp_ref, xsn_ref, bc_ref, bcp_ref, bcn_ref, dt_ref, z_ref, yf_ref,
         cwx_ref, cbx_ref, cwb_ref, cbb_ref, dtb_ref, alog_ref, e_ref, hm_ref, dsk_ref, ng_ref,
         o_ref, h_ref, xs_s, bc_s, y_ref) = refs
    else:
        (xs_ref, xsp_ref, xsn_ref, bc_ref, bcp_ref, bcn_ref, dt_ref,
         cwx_ref, cbx_ref, cwb_ref, cbb_ref, dtb_ref, alog_ref, e_ref, hm_ref,
         o_ref, h_ref, xs_s, bc_s) = refs
        y_ref = o_ref
    L = SSD_CHUNK
    c = pl.program_id(1)
    tile = c if direction == 0 else tiles - 1 - c
    first = tile == 0
    last = tile == tiles - 1

    @pl.when(c == 0)
    def _():
        h_ref[...] = jnp.zeros_like(h_ref)

    def conv_silu(x_ref, p_ref, n_ref, w_ref, b_ref):
        x = x_ref[...]
        x_prev, x_next = _shift_rows(x, p_ref[SUBLANES - 1:SUBLANES, :], n_ref[0:1, :], first, last)
        v = x_prev * w_ref[0:1, :] + x * w_ref[1:2, :] + x_next * w_ref[2:3, :] + b_ref[...]
        return v * _sigmoid(v)

    xs_s[...] = conv_silu(xs_ref, xsp_ref, xsn_ref, cwx_ref, cbx_ref)
    bc_s[...] = conv_silu(bc_ref, bcp_ref, bcn_ref, cwb_ref, cbb_ref)

    li = lax.broadcasted_iota(jnp.int32, (L, L), 0)
    si = lax.broadcasted_iota(jnp.int32, (L, L), 1)
    valid = (si <= li) if direction == 0 else (si >= li)
    tri = jnp.where(valid, 1.0, 0.0).astype(BF16)
    lane = lax.broadcasted_iota(jnp.int32, (1, LANES), 1)
    nt = (((1,), (1,)), ((), ()))
    e_mat = e_ref[...]
    neg_a = -jnp.exp(alog_ref[...])
    heads_per_group = SSD_HEADS // SSD_GROUPS
    chunks = xs_ref.shape[0] // L
    order = range(chunks) if direction == 0 else range(chunks - 1, -1, -1)

    for k in order:
        rows = slice(k * L, (k + 1) * L)
        xs = xs_s[rows, :]
        bm = bc_s[rows, :LANES]
        cm = bc_s[rows, LANES:]

        dtr = dt_ref[rows, :] + dtb_ref[...]
        dt = jnp.maximum(dtr, 0.0) + _log1p(jnp.exp(-jnp.abs(dtr)))
        acum = _dot_exact_rhs(tri, dt * neg_a)
        acum_t = acum.T
        total = acum[L - 1:L, :] if direction == 0 else acum[0:1, :]
        dec_end = jnp.exp(total - acum)
        ea = jnp.exp(acum)

        dtx = _dot_split2_lhs(dt, e_mat)
        eax = _dot_split2_lhs(ea, e_mat)
        dex = _dot_split2_lhs(dec_end * dt, e_mat)

        xdt = xs * dtx
        xw = (xs * dex).astype(BF16)
        cm_b = cm.astype(BF16)
        bm_b = bm.astype(BF16)

        h_prev = h_ref[...]
        y_off = _dot(cm_b, h_prev.astype(BF16)) * eax

        for g in range(SSD_GROUPS):
            cg = jnp.where((lane // SSD_STATE) == g, cm, 0.0).astype(BF16)
            cb = lax.dot_general(cg, bm_b, nt, preferred_element_type=F32)
            for pair in range(heads_per_group // 2):
                acc = None
                for half in range(2):
                    h = g * heads_per_group + pair * 2 + half
                    hl = direction * SSD_HEADS + h
                    seg = acum[:, hl:hl + 1] - acum_t[hl:hl + 1, :]
                    w = (jnp.where(valid, jnp.exp(seg), 0.0) * cb).astype(BF16)
                    c0 = (h // 2) * LANES
                    rhs = jnp.where((lane // SSD_HEAD_DIM) == half, xdt[:, c0:c0 + LANES], 0.0).astype(BF16)
                    part = _dot(w, rhs)
                    acc = part if acc is None else acc + part
                c0 = (g * heads_per_group // 2 + pair) * LANES
                y_pair = acc + y_off[:, c0:c0 + LANES]
                if final:
                    y_pair = yf_ref[rows, c0:c0 + LANES] + y_pair + dsk_ref[:, c0:c0 + LANES] * xs[:, c0:c0 + LANES]
                    zz = z_ref[rows, c0:c0 + LANES]
                    y_pair = y_pair * (zz * _sigmoid(zz))
                y_ref[rows, c0:c0 + LANES] = y_pair

        cd = eax[L - 1:L, :] if direction == 0 else eax[0:1, :]
        s_new = _dot(bm.T.astype(BF16), xw)
        h_ref[...] = (cd * h_prev + s_new) * hm_ref[...]

    if final:
        gw = SSD_WIDTH // SSD_GROUPS
        for g in range(SSD_GROUPS):
            y = y_ref[:, g * gw:(g + 1) * gw]
            y = y * lax.rsqrt(jnp.mean(y * y, axis=-1, keepdims=True) + RMS_EPS)
            o_ref[:, g * gw:(g + 1) * gw] = (y * ng_ref[:, g * gw:(g + 1) * gw]).astype(o_ref.dtype)


def _ssd_pass(proj_a, z_and_yf, params, seq, direction):
    n = proj_a.shape[0]
    T = SSD_CHUNK * SSD_CHUNKS_PER_STEP
    tiles = seq // T
    nseq = n // seq
    final = direction == 1
    cw, cb_, dtb, alog, e_mat, hmask, dsk, ng = params
    bcw = 2 * SSD_GROUPS * SSD_STATE

    def row_blk(b, c):
        t = c if direction == 0 else tiles - 1 - c
        return b * tiles + t

    def main(width, col0):
        return pl.BlockSpec((T, width), lambda b, c: (row_blk(b, c), col0 // width))

    r = T // SUBLANES
    last_blk = n // SUBLANES - 1

    def halo(width, col0):
        prev = pl.BlockSpec((SUBLANES, width), lambda b, c: (jnp.maximum(row_blk(b, c) * r - 1, 0), col0 // width))
        nxt = pl.BlockSpec((SUBLANES, width), lambda b, c: (jnp.minimum((row_blk(b, c) + 1) * r, last_blk), col0 // width))
        return prev, nxt

    xsp, xsn = halo(SSD_WIDTH, A_XS)
    bcp, bcn = halo(bcw, A_BC)
    full = lambda a: pl.BlockSpec(a.shape, lambda b, c: (0,) * a.ndim)

    in_specs = [main(SSD_WIDTH, A_XS), xsp, xsn, main(bcw, A_BC), bcp, bcn, main(LANES, A_DT)]
    args = [proj_a] * 7
    if final:
        in_specs += [main(SSD_WIDTH, A_Z), pl.BlockSpec((T, SSD_WIDTH), lambda b, c: (row_blk(b, c), 0))]
        args += [proj_a, z_and_yf]
    consts = [cw[:, :SSD_WIDTH], cb_[:, :SSD_WIDTH], cw[:, SSD_WIDTH:], cb_[:, SSD_WIDTH:], dtb, alog, e_mat, hmask]
    if final:
        consts += [dsk, ng]
    in_specs += [full(a) for a in consts]
    args += consts
    scratch = [pltpu.VMEM((LANES, SSD_WIDTH), F32), pltpu.VMEM((T, SSD_WIDTH), F32), pltpu.VMEM((T, bcw), F32)]
    if final:
        scratch.append(pltpu.VMEM((T, SSD_WIDTH), F32))
    return pl.pallas_call(
        functools.partial(_ssd_kernel, direction=direction, final=final, tiles=tiles),
        grid=(nseq, tiles),
        in_specs=in_specs,
        out_specs=pl.BlockSpec((T, SSD_WIDTH), lambda b, c: (row_blk(b, c), 0)),
        out_shape=jax.ShapeDtypeStruct((n, SSD_WIDTH), BF16 if final else F32),
        scratch_shapes=scratch,
        compiler_params=_cparams(("parallel", "arbitrary")),
        name="ssd_bwd" if final else "ssd_fwd",
    )(*args)


def _ssd_consts(ssd_conv_w, ssd_conv_b, ssd_dt_bias, ssd_a_log, ssd_d, ssd_norm_g):
    pad = LANES - 2 * SSD_HEADS
    dtb = jnp.pad(ssd_dt_bias.reshape(1, -1), ((0, 0), (0, pad)))
    alog = jnp.pad(ssd_a_log.reshape(1, -1), ((0, 0), (0, pad)))
    col_head = np.arange(SSD_WIDTH) // SSD_HEAD_DIM
    e_mats = [jnp.asarray((np.arange(LANES)[:, None] == d * SSD_HEADS + col_head[None, :]), BF16) for d in range(2)]
    row_group = np.arange(LANES) // SSD_STATE
    col_group = col_head // (SSD_HEADS // SSD_GROUPS)
    hmask = jnp.asarray(row_group[:, None] == col_group[None, :], F32)
    dsk = jnp.repeat(ssd_d, SSD_HEAD_DIM).reshape(1, SSD_WIDTH)
    ng = ssd_norm_g.reshape(1, SSD_WIDTH)
    cb_ = ssd_conv_b.reshape(1, -1)
    return [(ssd_conv_w, cb_, dtb, alog, e_mats[d], hmask, dsk, ng) for d in range(2)]


NA_QROWS = 4
NA_KROWS = NA_QROWS + WIN_H


def _na_kernel(kb_ref, cls_ref, q_ref, k_ref, v_ref, bias_ref, o_ref, *, blocks_per_step):
    step = pl.program_id(2)
    lane = lax.broadcasted_iota(jnp.int32, (1, LANES), 1)
    nt = (((1,), (1,)), ((), ()))
    nq = NA_QROWS * GRID_W
    nk = NA_KROWS * GRID_W
    for j in range(blocks_per_step):
        blk = step * blocks_per_step + j
        k0 = pl.multiple_of(kb_ref[blk] * GRID_W, GRID_W)
        cls = cls_ref[blk]
        q = q_ref[j * nq:(j + 1) * nq, :] * (NA_HEAD_DIM ** -0.5)
        kk = k_ref[pl.ds(k0, nk), :]
        vv = v_ref[pl.ds(k0, nk), :]
        outs = []
        for hh in range(2):
            qm = jnp.where((lane // NA_HEAD_DIM) == hh, q, jnp.zeros_like(q))
            s = lax.dot_general(qm, kk, nt, preferred_element_type=F32) + bias_ref[cls, hh]
            m = jnp.max(s, axis=-1, keepdims=True)
            p = jnp.exp(s - m)
            l = jnp.sum(p, axis=-1, keepdims=True)
            outs.append(_dot(p.astype(BF16), vv) / l)
        o = jnp.where((lane // NA_HEAD_DIM) == 0, outs[0], outs[1])
        o_ref[j * nq:(j + 1) * nq, :] = o.astype(o_ref.dtype)


def _na_tables(rel_bias, rows):
    qc = np.arange(GRID_W)
    cs = np.clip(qc - WIN_W // 2, 0, GRID_W - WIN_W)
    kc = np.arange(GRID_W)
    col_valid = (kc[None, :] >= cs[:, None]) & (kc[None, :] < cs[:, None] + WIN_W)
    dj = np.clip(kc[None, :] - qc[:, None] + WIN_W - 1, 0, 2 * WIN_W - 2)
    nblk = rows // NA_QROWS
    kb = np.clip(np.arange(nblk) * NA_QROWS - WIN_H // 2, 0, rows - NA_KROWS)
    patterns, cls = [], []
    for b in range(nblk):
        r = b * NA_QROWS + np.arange(NA_QROWS)
        rs = np.clip(r - WIN_H // 2, 0, rows - WIN_H)
        key = (tuple(rs - kb[b]), int(kb[b] - b * NA_QROWS))
        if key not in patterns:
            patterns.append(key)
        cls.append(patterns.index(key))
    di_sel, masks = [], []
    for off, shift in patterns:
        kr = np.arange(NA_KROWS)[None, :]
        i = np.arange(NA_QROWS)[:, None]
        rel = kr - np.asarray(off)[:, None]
        row_valid = (rel >= 0) & (rel < WIN_H)
        di = np.clip(shift + kr - i + WIN_H - 1, 0, 2 * WIN_H - 2)
        di_sel.append(np.eye(2 * WIN_H - 1, dtype=np.float32)[di])
        masks.append(row_valid[:, None, :, None] & col_valid[None, :, None, :])
    dj_sel = np.eye(2 * WIN_W - 1, dtype=np.float32)[dj]
    tab = jnp.einsum('pikd,hde,qwe->phiqkw', np.stack(di_sel), rel_bias.astype(F32), dj_sel,
                     precision=lax.Precision.HIGHEST)
    tab = jnp.where(np.stack(masks)[:, None], tab, NEG_BIG)
    tab = tab.reshape(len(patterns), NA_HEADS, NA_QROWS * GRID_W, NA_KROWS * GRID_W)
    return jnp.asarray(kb, jnp.int32), jnp.asarray(cls, jnp.int32), tab


def _neighbourhood_attention(proj_b, rel_bias, seq, blocks_per_step=4):
    n = proj_b.shape[0]
    nseq = n // seq
    rows = seq // GRID_W
    kb, cls, bias_tab = _na_tables(rel_bias, rows)
    tq = blocks_per_step * NA_QROWS * GRID_W
    pairs = NA_HEADS // 2
    steps = seq // tq
    ncls = bias_tab.shape[0]
    grid_spec = pltpu.PrefetchScalarGridSpec(
        num_scalar_prefetch=2,
        grid=(nseq, pairs, steps),
        in_specs=[pl.BlockSpec((tq, LANES), lambda b, p, r, kb_, cls_: (b * steps + r, p)),
                  pl.BlockSpec((seq, LANES), lambda b, p, r, kb_, cls_: (b, pairs + p)),
                  pl.BlockSpec((seq, LANES), lambda b, p, r, kb_, cls_: (b, 2 * pairs + p)),
                  pl.BlockSpec((ncls, 2) + bias_tab.shape[2:], lambda b, p, r, kb_, cls_: (0, p, 0, 0))],
        out_specs=pl.BlockSpec((tq, LANES), lambda b, p, r, kb_, cls_: (b * steps + r, p)),
    )
    return pl.pallas_call(
        functools.partial(_na_kernel, blocks_per_step=blocks_per_step),
        grid_spec=grid_spec,
        out_shape=jax.ShapeDtypeStruct((n, NA_WIDTH), F32),
        compiler_params=_cparams(("parallel", "arbitrary", "arbitrary")),
        name="nbr_attn",
    )(kb, cls, proj_b, proj_b, proj_b, bias_tab)


def _rms_rows(y, g):
    return y * lax.rsqrt(jnp.mean(y * y, axis=-1, keepdims=True) + RMS_EPS) * g


OUTPROJ_ROW_SPLITS = 2


def _outproj_kernel(*refs, router, n_parts, first_tiles):
    yc_ref, ys_ref, yn_ref = refs[:3]
    x_refs = refs[3:3 + n_parts]
    rest = refs[3 + n_parts:]
    if router:
        w_ref, nag_ref, g_ref, b_ref, rw_ref, o_ref, idx_ref, gate_ref = rest
    else:
        w_ref, nag_ref, g_ref, b_ref, o_ref = rest
    c1 = CONV_WIDTH
    c2 = CONV_WIDTH + SSD_WIDTH
    rows = o_ref.shape[0] // OUTPROJ_ROW_SPLITS
    for part in range(OUTPROJ_ROW_SPLITS):
        sl = slice(part * rows, (part + 1) * rows)
        yn = _rms_rows(yn_ref[sl, :], nag_ref[...]).astype(BF16)
        mix = _dot(yc_ref[sl, :], w_ref[0:c1, :]) + _dot(ys_ref[sl, :], w_ref[c1:c2, :]) + _dot(yn, w_ref[c2:, :])
        out = _layer_norm(ALPHA * _read_row_parts(x_refs, first_tiles, sl) + mix, g_ref[...], b_ref[...])
        o_ref[sl, :] = out
        if not router:
            continue
        o1, o2, _ = _split3(out)
        t1 = _dot(o1, rw_ref[...])
        t2 = _dot(o2, rw_ref[...])
        sh1 = LANES - N_EXPERTS
        sh2 = LANES - 2 * N_EXPERTS
        logits = ((pltpu.roll(t2, sh1, 1) + pltpu.roll(t1, sh2, 1)) + (t2 + pltpu.roll(t1, sh1, 1))) + t1
        lane = lax.broadcasted_iota(jnp.int32, (rows, LANES), 1)
        lg = jnp.where(lane < N_EXPERTS, logits, -jnp.inf)
        m1 = jnp.max(lg, axis=-1, keepdims=True)
        i1 = jnp.min(jnp.where(lg == m1, lane, LANES), axis=-1, keepdims=True)
        lg2 = jnp.where(lane == i1, -jnp.inf, lg)
        m2 = jnp.max(lg2, axis=-1, keepdims=True)
        i2 = jnp.min(jnp.where(lg2 == m2, lane, LANES), axis=-1, keepdims=True)
        e2 = jnp.exp(m2 - m1)
        den = 1.0 + e2
        idx_ref[sl, :] = jnp.where(lane == 0, i1, jnp.where(lane == 1, i2, 0))
        gate_ref[sl, :] = jnp.where(lane == 0, 1.0 / den, jnp.where(lane == 1, e2 / den, 0.0))


def _out_proj_ln(yc, ys, yn, x_parts, w_out, na_g, ln_g, ln_b, router_w=None, tm=512):
    n = yc.shape[0]
    router = router_w is not None
    row = lambda w: pl.BlockSpec((tm, w), lambda i: (i, 0))
    full = lambda a: pl.BlockSpec(a.shape, lambda i: (0,) * a.ndim)
    consts = [w_out, na_g.reshape(1, -1), ln_g.reshape(1, -1), ln_b.reshape(1, -1)]
    if router:
        consts.append(router_w)
    out_shape = [jax.ShapeDtypeStruct((n, D_MODEL), F32)]
    out_specs = [row(D_MODEL)]
    if router:
        out_shape += [jax.ShapeDtypeStruct((n, LANES), jnp.int32), jax.ShapeDtypeStruct((n, LANES), F32)]
        out_specs += [row(LANES), row(LANES)]
    res = pl.pallas_call(
        functools.partial(_outproj_kernel, router=router, n_parts=len(x_parts),
                          first_tiles=x_parts[0].shape[0] // tm),
        grid=(n // tm,),
        in_specs=[row(CONV_WIDTH), row(SSD_WIDTH), row(NA_WIDTH)] + _row_parts_specs(x_parts, tm, D_MODEL, 1)
        + [full(a) for a in consts],
        out_specs=out_specs,
        out_shape=out_shape,
        compiler_params=_cparams(("parallel",)),
        name="out_proj_router" if router else "out_proj",
    )(yc, ys, yn, *x_parts, *consts)
    return res


def _ffn_kernel(be_ref, bv_ref, x_ref, wg_ref, wu_ref, wd_ref, *rest, packed, ln):
    if ln:
        g_ref, b_ref, o_ref = rest
    else:
        (o_ref,) = rest
    i = pl.program_id(0)
    j = pl.program_id(1)

    @pl.when(j == 0)
    def _():
        o_ref[...] = jnp.zeros_like(o_ref)

    @pl.when(bv_ref[i] != 0)
    def _():
        if packed:
            lo, hi = _unpack_bf16_pairs(x_ref[...])
            c = lo.shape[1]
            g = _dot(lo, wg_ref[0, :c, :].astype(BF16)) + _dot(hi, wg_ref[0, c:, :].astype(BF16))
            u = _dot(lo, wu_ref[0, :c, :].astype(BF16)) + _dot(hi, wu_ref[0, c:, :].astype(BF16))
        else:
            x = x_ref[...].astype(BF16)
            g = _dot(x, wg_ref[0].astype(BF16))
            u = _dot(x, wu_ref[0].astype(BF16))
        h = ((g * _sigmoid(g)) * u).astype(BF16)
        o_ref[...] += _dot(h, wd_ref[0].astype(BF16))

    if ln:
        @pl.when(j == pl.num_programs(1) - 1)
        def _():
            o_ref[...] = _layer_norm(ALPHA * x_ref[...] + o_ref[...], g_ref[...], b_ref[...])


def _ffn_blocks(x, block_e, block_valid, w_gate, w_up, w_down, tm, tf=FF_TILE, residual_ln=None):
    n, xw = x.shape
    packed = x.dtype == jnp.uint32
    d = w_gate.shape[1]
    f = w_gate.shape[2]
    nf = f // tf
    col = lambda i, j, be, bv: (be[i], 0, jnp.where(bv[i] != 0, j, nf - 1))
    rowj = lambda i, j, be, bv: (be[i], jnp.where(bv[i] != 0, j, nf - 1), 0)
    in_specs = [pl.BlockSpec((tm, xw), lambda i, j, be, bv: (i, 0)),
                pl.BlockSpec((1, d, tf), col), pl.BlockSpec((1, d, tf), col), pl.BlockSpec((1, tf, d), rowj)]
    args = [x, w_gate, w_up, w_down]
    if residual_ln is not None:
        gain, bias = residual_ln
        assert not packed
        vec = pl.BlockSpec((1, d), lambda i, j, be, bv: (0, 0))
        in_specs += [vec, vec]
        args += [gain.reshape(1, d), bias.reshape(1, d)]
    grid_spec = pltpu.PrefetchScalarGridSpec(
        num_scalar_prefetch=2,
        grid=(n // tm, nf),
        in_specs=in_specs,
        out_specs=pl.BlockSpec((tm, d), lambda i, j, be, bv: (i, 0)),
    )
    return pl.pallas_call(
        functools.partial(_ffn_kernel, packed=packed, ln=residual_ln is not None),
        grid_spec=grid_spec,
        out_shape=jax.ShapeDtypeStruct((n, d), F32),
        compiler_params=_cparams(("parallel", "arbitrary")),
        name="swiglu_blocks",
    )(block_e, block_valid, *args)


def _combine_ln_kernel(i0_ref, i1_ref, src_ref, x_ref, gate_ref, g_ref, b_ref, oa_ref, ob_ref, b0_ref, b1_ref, sem,
                       *, tm, first_tiles):
    _start_row_gather(src_ref, i0_ref, b0_ref, sem.at[0], tm)
    _start_row_gather(src_ref, i1_ref, b1_ref, sem.at[1], tm)
    _wait_row_gather(src_ref, b0_ref, sem.at[0], tm)
    _wait_row_gather(src_ref, b1_ref, sem.at[1], tm)
    f = b0_ref[...] * gate_ref[:, 0:1] + b1_ref[...] * gate_ref[:, 1:2]
    res = _layer_norm(ALPHA * x_ref[...] + f, g_ref[...], b_ref[...])
    i = pl.program_id(0)

    @pl.when(i < first_tiles)
    def _():
        oa_ref[...] = res

    @pl.when(i >= first_tiles)
    def _():
        ob_ref[...] = res


def _combine_ln(x, outs, d0, d1, gates, g, b, n_first, tm=512):
    n, d = x.shape
    first_tiles = n_first // tm
    row = pl.BlockSpec((tm, d), lambda i: (i, 0))
    vec = pl.BlockSpec((1, d), lambda i: (0, 0))
    idx = pl.BlockSpec((1, 1, tm), lambda i: (i, 0, 0), memory_space=pltpu.SMEM)
    return pl.pallas_call(
        functools.partial(_combine_ln_kernel, tm=tm, first_tiles=first_tiles), grid=(n // tm,),
        in_specs=[idx, idx, pl.BlockSpec(memory_space=pl.ANY), row, pl.BlockSpec((tm, LANES), lambda i: (i, 0)), vec, vec],
        out_specs=[pl.BlockSpec((tm, d), lambda i: (jnp.minimum(i, first_tiles - 1), 0)),
                   pl.BlockSpec((tm, d), lambda i: (jnp.maximum(i - first_tiles, 0), 0))],
        out_shape=[jax.ShapeDtypeStruct((n_first, d), F32), jax.ShapeDtypeStruct((n - n_first, d), F32)],
        scratch_shapes=[pltpu.VMEM((tm, d), F32), pltpu.VMEM((tm, d), F32), pltpu.SemaphoreType.DMA((2,))],
        compiler_params=_cparams(("arbitrary",)), name="combine_ln",
    )(d0.reshape(n // tm, 1, tm), d1.reshape(n // tm, 1, tm), outs, x, gates, g.reshape(1, d), b.reshape(1, d))


GATHER_UNROLL = 8


def _start_row_gather(src_ref, idx_ref, buf_ref, sem, tr):
    def issue(r, carry):
        pltpu.make_async_copy(src_ref.at[pl.ds(idx_ref[0, 0, r], 1), :], buf_ref.at[pl.ds(r, 1), :], sem).start()
        return carry

    lax.fori_loop(0, tr, issue, 0, unroll=GATHER_UNROLL)


def _wait_row_gather(src_ref, buf_ref, sem, tr):
    pltpu.make_async_copy(src_ref.at[pl.ds(0, tr), :], buf_ref, sem).wait()


HI16 = 0xFFFF0000


def _pack_bf16_pairs(x):
    c = x.shape[1] // 2
    lo = lax.bitcast_convert_type(x[:, :c].astype(BF16).astype(F32), jnp.uint32)
    hi = lax.bitcast_convert_type(x[:, c:].astype(BF16).astype(F32), jnp.uint32)
    return (hi & jnp.uint32(HI16)) | (lo >> 16)


def _unpack_bf16_pairs(u):
    lo = lax.bitcast_convert_type(u << 16, F32).astype(BF16)
    hi = lax.bitcast_convert_type(u & jnp.uint32(HI16), F32).astype(BF16)
    return lo, hi


def _dispatch_kernel(d0_ref, d1_ref, x_ref, init_ref, xs_ref, pk_ref, sem, *, tm):
    del init_ref
    pk_ref[...] = _pack_bf16_pairs(x_ref[...])

    def issue(r, carry):
        row = pk_ref.at[pl.ds(r, 1), :]
        pltpu.make_async_copy(row, xs_ref.at[pl.ds(d0_ref[0, 0, r], 1), :], sem.at[0]).start()
        pltpu.make_async_copy(row, xs_ref.at[pl.ds(d1_ref[0, 0, r], 1), :], sem.at[1]).start()
        return carry

    lax.fori_loop(0, tm, issue, 0, unroll=GATHER_UNROLL)
    for k in range(TOP_K):
        pltpu.make_async_copy(pk_ref, xs_ref.at[pl.ds(0, tm), :], sem.at[k]).wait()


def _dispatch_rows(x, d0, d1, p_len, tm=512):
    n, d = x.shape
    idx = pl.BlockSpec((1, 1, tm), lambda i: (i, 0, 0), memory_space=pltpu.SMEM)
    return pl.pallas_call(
        functools.partial(_dispatch_kernel, tm=tm),
        grid=(n // tm,),
        in_specs=[idx, idx, pl.BlockSpec((tm, d), lambda i: (i, 0)), pl.BlockSpec(memory_space=pl.ANY)],
        out_specs=pl.BlockSpec(memory_space=pl.ANY),
        out_shape=jax.ShapeDtypeStruct((p_len, d // 2), jnp.uint32),
        scratch_shapes=[pltpu.VMEM((tm, d // 2), jnp.uint32), pltpu.SemaphoreType.DMA((TOP_K,))],
        input_output_aliases={3: 0},
        compiler_params=_cparams(("arbitrary",)),
        name="dispatch_rows",
    )(d0.reshape(n // tm, 1, tm), d1.reshape(n // tm, 1, tm), x, jnp.zeros((p_len, d // 2), jnp.uint32))


def _routing_tables(idx, n, tm):
    flat_e = idx[:, :TOP_K].reshape(-1)
    onehot = (flat_e[:, None] == jnp.arange(N_EXPERTS, dtype=jnp.int32)[None, :]).astype(jnp.int32)
    incl = jnp.cumsum(onehot, axis=0)
    rank = jnp.sum((incl - onehot) * onehot, axis=1)
    counts = incl[-1]
    padded = ((counts + tm - 1) // tm) * tm
    pend = jnp.cumsum(padded)
    pstart = pend - padded
    dest = pstart[flat_e] + rank
    n_blocks = -(-(n * TOP_K) // tm) + N_EXPERTS
    starts = jnp.arange(n_blocks, dtype=jnp.int32) * tm
    block_valid = (starts < pend[-1]).astype(jnp.int32)
    block_e = jnp.minimum(jnp.searchsorted(pend, starts, side='right'), N_EXPERTS - 1).astype(jnp.int32)
    last_e = block_e[jnp.maximum(jnp.sum(block_valid) - 1, 0)]
    block_e = jnp.where(block_valid != 0, block_e, last_e)
    dest2 = dest.reshape(n, TOP_K)
    return n_blocks * tm, block_e, block_valid, dest2[:, 0], dest2[:, 1]


def _prep_w_in(w):
    cb, cc, ch = w[:, 0:512], w[:, 512:1024], w[:, 1024:1536]
    z = w[:, 1536:2560]
    xs, bc = w[:, 2560:3584], w[:, 3584:3840]
    dt = w[:, 3840:3872]
    qkv = w[:, 3872:5408]
    pad = jnp.zeros((D_MODEL, A_WIDTH - (A_DT + 2 * SSD_HEADS)), w.dtype)
    return jnp.concatenate([z, xs, cb, cc, ch, bc, dt, pad, qkv], axis=1).astype(BF16)


def _prep_router_w(rw):
    r1 = rw.astype(BF16)
    r2 = (rw - r1.astype(F32)).astype(BF16)
    r3 = (rw - r1.astype(F32) - r2.astype(F32)).astype(BF16)
    pad = jnp.zeros((rw.shape[0], LANES - 3 * N_EXPERTS), BF16)
    return jnp.concatenate([r1, r2, r3, pad], axis=1)


def _trunk(x_parts, seq, p):
    n = sum(a.shape[0] for a in x_parts)
    n_first = x_parts[0].shape[0]
    for l in range(DEPTH):
        proj_a, proj_b = _in_proj(x_parts, _prep_w_in(p['w_in'][l]))
        y_conv = _conv_mix(proj_a, p['conv_w'][l], p['conv_norm_g'][l], seq)
        cf, cbw = _ssd_consts(p['ssd_conv_w'][l], p['ssd_conv_b'][l], p['ssd_dt_bias'][l], p['ssd_a_log'][l],
                              p['ssd_d'][l], p['ssd_norm_g'][l])
        y_fwd = _ssd_pass(proj_a, None, cf, seq, 0)
        y_ssd = _ssd_pass(proj_a, y_fwd, cbw, seq, 1)
        y_na = _neighbourhood_attention(proj_b, p['na_rel_bias'][l], seq)
        w_out = p['w_out'][l].astype(BF16)
        if l % 2 == 0:
            x = _out_proj_ln(y_conv, y_ssd, y_na, x_parts, w_out, p['na_norm_g'][l], p['ln_mix_g'][l],
                             p['ln_mix_b'][l])[0]
            padf = ((0, 0), (0, D_FF_PAD - D_FF))
            wg = jnp.pad(p['ffn_w_gate'][l // 2], padf).astype(BF16)[None]
            wu = jnp.pad(p['ffn_w_up'][l // 2], padf).astype(BF16)[None]
            wd = jnp.pad(p['ffn_w_down'][l // 2], ((0, D_FF_PAD - D_FF), (0, 0))).astype(BF16)[None]
            nb = n // MOE_TM
            x = _ffn_blocks(x, jnp.zeros((nb,), jnp.int32), jnp.ones((nb,), jnp.int32), wg, wu, wd, MOE_TM,
                            residual_ln=(p['ln_ffn_g'][l], p['ln_ffn_b'][l]))
            outs_split = (x[:n_first], x[n_first:])
        else:
            x, idx, gates = _out_proj_ln(y_conv, y_ssd, y_na, x_parts, w_out, p['na_norm_g'][l], p['ln_mix_g'][l],
                                         p['ln_mix_b'][l], _prep_router_w(p['router_w'][l // 2]))
            p_len, block_e, block_valid, d0, d1 = _routing_tables(idx, n, MOE_TM)
            xs = _dispatch_rows(x, d0, d1, p_len)
            outs = _ffn_blocks(xs, block_e, block_valid, p['moe_w_gate'][l // 2], p['moe_w_up'][l // 2],
                               p['moe_w_down'][l // 2], MOE_TM)
            outs_split = _combine_ln(x, outs, d0, d1, gates, p['ln_ffn_g'][l], p['ln_ffn_b'][l], n_first)
            if l + 1 < DEPTH:
                x = jnp.concatenate(outs_split, axis=0)
        x_parts = [x]
    return outs_split


def kernel(x_prompt, x_sample, w_in, conv_w, conv_norm_g, ssd_conv_w, ssd_conv_b, ssd_dt_bias, ssd_a_log, ssd_d,
           ssd_norm_g, na_rel_bias, na_norm_g, w_out, ln_mix_g, ln_mix_b, ln_ffn_g, ln_ffn_b, ffn_w_gate, ffn_w_up,
           ffn_w_down, router_w, moe_w_gate, moe_w_up, moe_w_down):
    p = dict(w_in=w_in, conv_w=conv_w, conv_norm_g=conv_norm_g, ssd_conv_w=ssd_conv_w, ssd_conv_b=ssd_conv_b,
             ssd_dt_bias=ssd_dt_bias, ssd_a_log=ssd_a_log, ssd_d=ssd_d, ssd_norm_g=ssd_norm_g,
             na_rel_bias=na_rel_bias, na_norm_g=na_norm_g, w_out=w_out, ln_mix_g=ln_mix_g, ln_mix_b=ln_mix_b,
             ln_ffn_g=ln_ffn_g, ln_ffn_b=ln_ffn_b, ffn_w_gate=ffn_w_gate, ffn_w_up=ffn_w_up, ffn_w_down=ffn_w_down,
             router_w=router_w, moe_w_gate=moe_w_gate, moe_w_up=moe_w_up, moe_w_down=moe_w_down)
    bp, seq, d = x_prompt.shape
    bs, seq_s, _ = x_sample.shape
    assert seq == seq_s, "both request groups must share one sequence length"
    yp, ys = _trunk([x_prompt.reshape(bp * seq, d), x_sample.reshape(bs * seq, d)], seq, p)
    return yp.reshape(bp, seq, d), ys.reshape(bs, seq, d)
```

```python
import functools

import numpy as np
import jax
import jax.numpy as jnp
from jax import lax
from jax.experimental import pallas as pl
from jax.experimental.pallas import tpu as pltpu

F32 = jnp.float32
BF16 = jnp.bfloat16

D_MODEL = 2048
DEPTH = 2
GRID_W = 64
CONV_WIDTH = 512
SSD_HEADS = 16
SSD_HEAD_DIM = 64
SSD_WIDTH = 1024
SSD_GROUPS = 2
SSD_STATE = 64
SSD_CHUNK = 128
NA_HEADS = 8
NA_HEAD_DIM = 64
NA_WIDTH = 512
WIN_H = 8
WIN_W = 16
D_FF = 5504
N_EXPERTS = 8
TOP_K = 2
D_FF_EXPERT = 7168
ALPHA = (2 * DEPTH) ** 0.25
LN_EPS = 1e-5
RMS_EPS = 1e-6

LANES = 128
SUBLANES = 8
NEG_BIG = -1e30

A_Z, A_XS, A_CB, A_CC, A_CH, A_BC, A_DT, A_WIDTH = 0, 1024, 2048, 2560, 3072, 3584, 3840, 4096
FF_TILE = 512
D_FF_PAD = 5632
MOE_TM = 1024
VMEM_LIMIT = 56 * 1024 * 1024


def _cparams(sem, vmem=VMEM_LIMIT):
    return pltpu.CompilerParams(dimension_semantics=sem, vmem_limit_bytes=vmem)


def _sigmoid(x):
    return 1.0 / (1.0 + jnp.exp(-x))


def _split3(x):
    x1 = x.astype(BF16)
    r1 = x - x1.astype(F32)
    x2 = r1.astype(BF16)
    r2 = r1 - x2.astype(F32)
    return x1, x2, r2.astype(BF16)


def _dot(a, b):
    return jnp.dot(a, b, preferred_element_type=F32)


def _dot_exact_rhs(m, x):
    x1, x2, x3 = _split3(x)
    return (_dot(m, x3) + _dot(m, x2)) + _dot(m, x1)


def _dot_split2_lhs(x, m):
    x1, x2, _ = _split3(x)
    return _dot(x2, m) + _dot(x1, m)


def _log1p(e):
    u = 1.0 + e
    d = u - 1.0
    return jnp.where(d == 0.0, e, jnp.log(u) * (e / jnp.where(d == 0.0, 1.0, d)))


def _layer_norm(v, g, b):
    mu = jnp.mean(v, axis=-1, keepdims=True)
    c = v - mu
    var = jnp.mean(c * c, axis=-1, keepdims=True)
    return c * lax.rsqrt(var + LN_EPS) * g + b


def _row_parts_specs(parts, tm, width, grid_rank):
    if len(parts) == 1:
        return [pl.BlockSpec((tm, width), (lambda i: (i, 0)) if grid_rank == 1 else (lambda i, j: (i, 0)))]
    first_tiles = parts[0].shape[0] // tm
    if grid_rank == 1:
        return [pl.BlockSpec((tm, width), lambda i: (jnp.minimum(i, first_tiles - 1), 0)),
                pl.BlockSpec((tm, width), lambda i: (jnp.maximum(i - first_tiles, 0), 0))]
    return [pl.BlockSpec((tm, width), lambda i, j: (jnp.minimum(i, first_tiles - 1), 0)),
            pl.BlockSpec((tm, width), lambda i, j: (jnp.maximum(i - first_tiles, 0), 0))]


def _read_row_parts(x_refs, first_tiles, rows=slice(None)):
    x = x_refs[0][rows, :]
    if len(x_refs) == 2:
        x = jnp.where(pl.program_id(0) < first_tiles, x, x_refs[1][rows, :])
    return x


def _inproj_kernel(*refs, n_parts, first_tiles, a_blocks):
    x_refs = refs[:n_parts]
    w_ref, oa_ref, ob_ref, xb_ref = refs[n_parts:]
    j = pl.program_id(1)

    @pl.when(j == 0)
    def _():
        xb_ref[...] = _read_row_parts(x_refs, first_tiles).astype(BF16)

    res = _dot(xb_ref[...], w_ref[...])

    @pl.when(j < a_blocks)
    def _():
        oa_ref[...] = res

    @pl.when(j >= a_blocks)
    def _():
        ob_ref[...] = res.astype(ob_ref.dtype)


def _in_proj(x_parts, w, tm=1024, tn=512):
    n = sum(a.shape[0] for a in x_parts)
    k = x_parts[0].shape[1]
    a_blocks = A_WIDTH // tn
    b_width = w.shape[1] - A_WIDTH
    first_tiles = x_parts[0].shape[0] // tm
    return pl.pallas_call(
        functools.partial(_inproj_kernel, n_parts=len(x_parts), first_tiles=first_tiles, a_blocks=a_blocks),
        grid=(n // tm, w.shape[1] // tn),
        in_specs=_row_parts_specs(x_parts, tm, k, 2) + [pl.BlockSpec((k, tn), lambda i, j: (0, j))],
        out_specs=[pl.BlockSpec((tm, tn), lambda i, j: (i, jnp.minimum(j, a_blocks - 1))),
                   pl.BlockSpec((tm, tn), lambda i, j: (i, jnp.maximum(j - a_blocks, 0)))],
        out_shape=[jax.ShapeDtypeStruct((n, A_WIDTH), F32), jax.ShapeDtypeStruct((n, b_width), BF16)],
        scratch_shapes=[pltpu.VMEM((tm, k), BF16)],
        compiler_params=_cparams(("parallel", "arbitrary")),
        name="in_proj",
    )(*x_parts, w)


def _shift_rows(x, prev_row, next_row, first, last):
    n = x.shape[0]
    row = lax.broadcasted_iota(jnp.int32, (n, 1), 0)
    prev_row = jnp.where(first, 0.0, prev_row)
    next_row = jnp.where(last, 0.0, next_row)
    x_prev = jnp.where(row == 0, prev_row, pltpu.roll(x, 1, 0))
    x_next = jnp.where(row == n - 1, next_row, pltpu.roll(x, n - 1, 0))
    return x_prev, x_next


def _convmix_kernel(cb_ref, cc_ref, ch_ref, ccp_ref, chp_ref, ccn_ref, chn_ref, w_ref, g_ref, o_ref, *, tiles_per_seq):
    i = pl.program_id(0)
    first = (i % tiles_per_seq) == 0
    last = (i % tiles_per_seq) == tiles_per_seq - 1
    u = cc_ref[...] * ch_ref[...]
    up = ccp_ref[SUBLANES - 1:SUBLANES, :] * chp_ref[SUBLANES - 1:SUBLANES, :]
    un = ccn_ref[0:1, :] * chn_ref[0:1, :]
    u_prev, u_next = _shift_rows(u, up, un, first, last)
    conv = u_prev * w_ref[0:1, :] + u * w_ref[1:2, :] + u_next * w_ref[2:3, :]
    y = cb_ref[...] * conv
    y = y * lax.rsqrt(jnp.mean(y * y, axis=-1, keepdims=True) + RMS_EPS)
    o_ref[...] = (y * g_ref[...]).astype(o_ref.dtype)


def _halo_specs(tt, width, col0, n_rows):
    cb = col0 // width
    r = tt // SUBLANES
    last_blk = n_rows // SUBLANES - 1
    prev = pl.BlockSpec((SUBLANES, width), lambda i: (jnp.maximum(i * r - 1, 0), cb))
    nxt = pl.BlockSpec((SUBLANES, width), lambda i: (jnp.minimum((i + 1) * r, last_blk), cb))
    return prev, nxt


def _conv_mix(proj_a, conv_w, norm_g, seq, tt=512):
    n = proj_a.shape[0]
    w = CONV_WIDTH
    main = lambda c0: pl.BlockSpec((tt, w), lambda i: (i, c0 // w))
    ccp, ccn = _halo_specs(tt, w, A_CC, n)
    chp, chn = _halo_specs(tt, w, A_CH, n)
    full = lambda shape: pl.BlockSpec(shape, lambda i: (0, 0))
    return pl.pallas_call(
        functools.partial(_convmix_kernel, tiles_per_seq=seq // tt),
        grid=(n // tt,),
        in_specs=[main(A_CB), main(A_CC), main(A_CH), ccp, chp, ccn, chn, full((3, w)), full((1, w))],
        out_specs=pl.BlockSpec((tt, w), lambda i: (i, 0)),
        out_shape=jax.ShapeDtypeStruct((n, w), BF16),
        compiler_params=_cparams(("parallel",)),
        name="conv_mix",
    )(proj_a, proj_a, proj_a, proj_a, proj_a, proj_a, proj_a, conv_w, norm_g.reshape(1, w))


SSD_CHUNKS_PER_STEP = 2


def _ssd_kernel(*refs, direction, final, tiles):
    if final:
        (xs_s, bc_s, dt_ref, z_ref, yf_ref, dtb_ref, alog_ref, e_ref, hm_ref, dsk_ref, ng_ref,
         o_ref, h_ref, y_ref) = refs
    else:
        (xs_ref, xsp_ref, xsn_ref, bc_ref, bcp_ref, bcn_ref, dt_ref,
         cwx_ref, cbx_ref, cwb_ref, cbb_ref, dtb_ref, alog_ref, e_ref, hm_ref,
         o_ref, xs_s, bc_s, h_ref) = refs
        y_ref = o_ref
    L = SSD_CHUNK
    c = pl.program_id(1)
    tile = c if direction == 0 else tiles - 1 - c
    first = tile == 0
    last = tile == tiles - 1

    @pl.when(c == 0)
    def _():
        h_ref[...] = jnp.zeros_like(h_ref)

    def conv_silu(x_ref, p_ref, n_ref, w_ref, b_ref):
        x = x_ref[...]
        x_prev, x_next = _shift_rows(x, p_ref[SUBLANES - 1:SUBLANES, :], n_ref[0:1, :], first, last)
        v = x_prev * w_ref[0:1, :] + x * w_ref[1:2, :] + x_next * w_ref[2:3, :] + b_ref[...]
        return v * _sigmoid(v)

    if not final:
        xs_s[...] = conv_silu(xs_ref, xsp_ref, xsn_ref, cwx_ref, cbx_ref)
        bc_s[...] = conv_silu(bc_ref, bcp_ref, bcn_ref, cwb_ref, cbb_ref)

    li = lax.broadcasted_iota(jnp.int32, (L, L), 0)
    si = lax.broadcasted_iota(jnp.int32, (L, L), 1)
    valid = (si <= li) if direction == 0 else (si >= li)
    tri = jnp.where(valid, 1.0, 0.0).astype(BF16)
    lane = lax.broadcasted_iota(jnp.int32, (1, LANES), 1)
    nt = (((1,), (1,)), ((), ()))
    e_mat = e_ref[...]
    neg_a = -jnp.exp(alog_ref[...])
    heads_per_group = SSD_HEADS // SSD_GROUPS
    chunks = xs_s.shape[0] // L
    order = range(chunks) if direction == 0 else range(chunks - 1, -1, -1)

    for k in order:
        rows = slice(k * L, (k + 1) * L)
        xs = xs_s[rows, :]
        bm = bc_s[rows, :LANES]
        cm = bc_s[rows, LANES:]

        dtr = dt_ref[rows, :] + dtb_ref[...]
        dt = jnp.maximum(dtr, 0.0) + _log1p(jnp.exp(-jnp.abs(dtr)))
        acum = _dot_exact_rhs(tri, dt * neg_a)
        acum_t = acum.T
        total = acum[L - 1:L, :] if direction == 0 else acum[0:1, :]
        dec_end = jnp.exp(total - acum)
        ea = jnp.exp(acum)

        dtx = _dot_split2_lhs(dt, e_mat)
        eax = _dot_split2_lhs(ea, e_mat)
        dex = _dot_split2_lhs(dec_end * dt, e_mat)

        xdt = xs * dtx
        xw = (xs * dex).astype(BF16)
        cm_b = cm.astype(BF16)
        bm_b = bm.astype(BF16)

        h_prev = h_ref[...]
        y_off = _dot(cm_b, h_prev.astype(BF16)) * eax

        for g in range(SSD_GROUPS):
            cg = jnp.where((lane // SSD_STATE) == g, cm, 0.0).astype(BF16)
            cb = lax.dot_general(cg, bm_b, nt, preferred_element_type=F32)
            for pair in range(heads_per_group // 2):
                acc = None
                for half in range(2):
                    h = g * heads_per_group + pair * 2 + half
                    hl = direction * SSD_HEADS + h
                    seg = acum[:, hl:hl + 1] - acum_t[hl:hl + 1, :]
                    w = (jnp.where(valid, jnp.exp(seg), 0.0) * cb).astype(BF16)
                    c0 = (h // 2) * LANES
                    rhs = jnp.where((lane // SSD_HEAD_DIM) == half, xdt[:, c0:c0 + LANES], 0.0).astype(BF16)
                    part = _dot(w, rhs)
                    acc = part if acc is None else acc + part
                c0 = (g * heads_per_group // 2 + pair) * LANES
                y_pair = acc + y_off[:, c0:c0 + LANES]
                if final:
                    y_pair = yf_ref[rows, c0:c0 + LANES] + y_pair + dsk_ref[:, c0:c0 + LANES] * xs[:, c0:c0 + LANES]
                    zz = z_ref[rows, c0:c0 + LANES]
                    y_pair = y_pair * (zz * _sigmoid(zz))
                y_ref[rows, c0:c0 + LANES] = y_pair

        cd = eax[L - 1:L, :] if direction == 0 else eax[0:1, :]
        s_new = _dot(bm.T.astype(BF16), xw)
        h_ref[...] = (cd * h_prev + s_new) * hm_ref[...]

    if final:
        gw = SSD_WIDTH // SSD_GROUPS
        for g in range(SSD_GROUPS):
            y = y_ref[:, g * gw:(g + 1) * gw]
            y = y * lax.rsqrt(jnp.mean(y * y, axis=-1, keepdims=True) + RMS_EPS)
            o_ref[:, g * gw:(g + 1) * gw] = (y * ng_ref[:, g * gw:(g + 1) * gw]).astype(o_ref.dtype)


def _ssd_pass(proj_a, fwd_outs, params, seq, direction):
    n = proj_a.shape[0]
    T = SSD_CHUNK * SSD_CHUNKS_PER_STEP
    tiles = seq // T
    nseq = n // seq
    final = direction == 1
    cw, cb_, dtb, alog, e_mat, hmask, dsk, ng = params
    bcw = 2 * SSD_GROUPS * SSD_STATE

    def row_blk(b, c):
        t = c if direction == 0 else tiles - 1 - c
        return b * tiles + t

    def main(width, col0):
        return pl.BlockSpec((T, width), lambda b, c: (row_blk(b, c), col0 // width))

    r = T // SUBLANES
    last_blk = n // SUBLANES - 1

    def halo(width, col0):
        prev = pl.BlockSpec((SUBLANES, width), lambda b, c: (jnp.maximum(row_blk(b, c) * r - 1, 0), col0 // width))
        nxt = pl.BlockSpec((SUBLANES, width), lambda b, c: (jnp.minimum((row_blk(b, c) + 1) * r, last_blk), col0 // width))
        return prev, nxt

    xsp, xsn = halo(SSD_WIDTH, A_XS)
    bcp, bcn = halo(bcw, A_BC)
    full = lambda a: pl.BlockSpec(a.shape, lambda b, c: (0,) * a.ndim)

    rows = lambda width: pl.BlockSpec((T, width), lambda b, c: (row_blk(b, c), 0))
    if final:
        y_fwd, xs_act, bc_act = fwd_outs
        in_specs = [rows(SSD_WIDTH), rows(bcw), main(LANES, A_DT), main(SSD_WIDTH, A_Z), rows(SSD_WIDTH)]
        args = [xs_act, bc_act, proj_a, proj_a, y_fwd]
        consts = [dtb, alog, e_mat, hmask, dsk, ng]
        out_specs = rows(SSD_WIDTH)
        out_shape = jax.ShapeDtypeStruct((n, SSD_WIDTH), BF16)
    else:
        in_specs = [main(SSD_WIDTH, A_XS), xsp, xsn, main(bcw, A_BC), bcp, bcn, main(LANES, A_DT)]
        args = [proj_a] * 7
        consts = [cw[:, :SSD_WIDTH], cb_[:, :SSD_WIDTH], cw[:, SSD_WIDTH:], cb_[:, SSD_WIDTH:], dtb, alog, e_mat, hmask]
        out_specs = [rows(SSD_WIDTH), rows(SSD_WIDTH), rows(bcw)]
        out_shape = [jax.ShapeDtypeStruct((n, SSD_WIDTH), F32), jax.ShapeDtypeStruct((n, SSD_WIDTH), F32),
                     jax.ShapeDtypeStruct((n, bcw), F32)]
    in_specs += [full(a) for a in consts]
    args += consts
    scratch = [pltpu.VMEM((LANES, SSD_WIDTH), F32)]
    if final:
        scratch.append(pltpu.VMEM((T, SSD_WIDTH), F32))
    return pl.pallas_call(
        functools.partial(_ssd_kernel, direction=direction, final=final, tiles=tiles),
        grid=(nseq, tiles),
        in_specs=in_specs,
        out_specs=out_specs,
        out_shape=out_shape,
        scratch_shapes=scratch,
        compiler_params=_cparams(("parallel", "arbitrary")),
        name="ssd_bwd" if final else "ssd_fwd",
    )(*args)


def _ssd_consts(ssd_conv_w, ssd_conv_b, ssd_dt_bias, ssd_a_log, ssd_d, ssd_norm_g):
    pad = LANES - 2 * SSD_HEADS
    dtb = jnp.pad(ssd_dt_bias.reshape(1, -1), ((0, 0), (0, pad)))
    alog = jnp.pad(ssd_a_log.reshape(1, -1), ((0, 0), (0, pad)))
    col_head = np.arange(SSD_WIDTH) // SSD_HEAD_DIM
    e_mats = [jnp.asarray((np.arange(LANES)[:, None] == d * SSD_HEADS + col_head[None, :]), BF16) for d in range(2)]
    row_group = np.arange(LANES) // SSD_STATE
    col_group = col_head // (SSD_HEADS // SSD_GROUPS)
    hmask = jnp.asarray(row_group[:, None] == col_group[None, :], F32)
    dsk = jnp.repeat(ssd_d, SSD_HEAD_DIM).reshape(1, SSD_WIDTH)
    ng = ssd_norm_g.reshape(1, SSD_WIDTH)
    cb_ = ssd_conv_b.reshape(1, -1)
    return [(ssd_conv_w, cb_, dtb, alog, e_mats[d], hmask, dsk, ng) for d in range(2)]


NA_QROWS = 4
NA_KROWS = NA_QROWS + WIN_H


def _na_kernel(kb_ref, cls_ref, q_ref, k_ref, v_ref, bias_ref, o_ref, *, blocks_per_step):
    step = pl.program_id(2)
    lane = lax.broadcasted_iota(jnp.int32, (1, LANES), 1)
    nt = (((1,), (1,)), ((), ()))
    nq = NA_QROWS * GRID_W
    nk = NA_KROWS * GRID_W
    for j in range(blocks_per_step):
        blk = step * blocks_per_step + j
        k0 = pl.multiple_of(kb_ref[blk] * GRID_W, GRID_W)
        cls = cls_ref[blk]
        q = q_ref[j * nq:(j + 1) * nq, :] * (NA_HEAD_DIM ** -0.5)
        kk = k_ref[pl.ds(k0, nk), :]
        vv = v_ref[pl.ds(k0, nk), :]
        outs = []
        for hh in range(2):
            qm = jnp.where((lane // NA_HEAD_DIM) == hh, q, jnp.zeros_like(q))
            s = lax.dot_general(qm, kk, nt, preferred_element_type=F32) + bias_ref[cls, hh]
            m = jnp.max(s, axis=-1, keepdims=True)
            p = jnp.exp(s - m)
            l = jnp.sum(p, axis=-1, keepdims=True)
            outs.append(_dot(p.astype(BF16), vv) / l)
        o = jnp.where((lane // NA_HEAD_DIM) == 0, outs[0], outs[1])
        o_ref[j * nq:(j + 1) * nq, :] = o.astype(o_ref.dtype)


def _na_tables(rel_bias, rows):
    qc = np.arange(GRID_W)
    cs = np.clip(qc - WIN_W // 2, 0, GRID_W - WIN_W)
    kc = np.arange(GRID_W)
    col_valid = (kc[None, :] >= cs[:, None]) & (kc[None, :] < cs[:, None] + WIN_W)
    dj = np.clip(kc[None, :] - qc[:, None] + WIN_W - 1, 0, 2 * WIN_W - 2)
    nblk = rows // NA_QROWS
    kb = np.clip(np.arange(nblk) * NA_QROWS - WIN_H // 2, 0, rows - NA_KROWS)
    patterns, cls = [], []
    for b in range(nblk):
        r = b * NA_QROWS + np.arange(NA_QROWS)
        rs = np.clip(r - WIN_H // 2, 0, rows - WIN_H)
        key = (tuple(rs - kb[b]), int(kb[b] - b * NA_QROWS))
        if key not in patterns:
            patterns.append(key)
        cls.append(patterns.index(key))
    nh, nw = 2 * WIN_H - 1, 2 * WIN_W - 1
    bias_ext = jnp.pad(rel_bias.astype(F32), ((0, 0), (0, 1), (0, 1)), constant_values=NEG_BIG)
    di_sel = []
    for off, shift in patterns:
        kr = np.arange(NA_KROWS)[None, :]
        i = np.arange(NA_QROWS)[:, None]
        rel = kr - np.asarray(off)[:, None]
        row_valid = (rel >= 0) & (rel < WIN_H)
        di = np.where(row_valid, np.clip(shift + kr - i + WIN_H - 1, 0, nh - 1), nh)
        di_sel.append(np.eye(nh + 1, dtype=np.float32)[di])
    dj_sel = np.eye(nw + 1, dtype=np.float32)[np.where(col_valid, dj, nw)]
    tab = jnp.einsum('pikd,hde,qwe->phiqkw', np.stack(di_sel), bias_ext, dj_sel, precision=lax.Precision.HIGHEST)
    tab = tab.reshape(len(patterns), NA_HEADS, NA_QROWS * GRID_W, NA_KROWS * GRID_W)
    return jnp.asarray(kb, jnp.int32), jnp.asarray(cls, jnp.int32), tab


def _neighbourhood_attention(proj_b, rel_bias, seq, blocks_per_step=4):
    n = proj_b.shape[0]
    nseq = n // seq
    rows = seq // GRID_W
    kb, cls, bias_tab = _na_tables(rel_bias, rows)
    tq = blocks_per_step * NA_QROWS * GRID_W
    pairs = NA_HEADS // 2
    steps = seq // tq
    ncls = bias_tab.shape[0]
    grid_spec = pltpu.PrefetchScalarGridSpec(
        num_scalar_prefetch=2,
        grid=(nseq, pairs, steps),
        in_specs=[pl.BlockSpec((tq, LANES), lambda b, p, r, kb_, cls_: (b * steps + r, p)),
                  pl.BlockSpec((seq, LANES), lambda b, p, r, kb_, cls_: (b, pairs + p)),
                  pl.BlockSpec((seq, LANES), lambda b, p, r, kb_, cls_: (b, 2 * pairs + p)),
                  pl.BlockSpec((ncls, 2) + bias_tab.shape[2:], lambda b, p, r, kb_, cls_: (0, p, 0, 0))],
        out_specs=pl.BlockSpec((tq, LANES), lambda b, p, r, kb_, cls_: (b * steps + r, p)),
    )
    return pl.pallas_call(
        functools.partial(_na_kernel, blocks_per_step=blocks_per_step),
        grid_spec=grid_spec,
        out_shape=jax.ShapeDtypeStruct((n, NA_WIDTH), F32),
        compiler_params=_cparams(("parallel", "arbitrary", "arbitrary")),
        name="nbr_attn",
    )(kb, cls, proj_b, proj_b, proj_b, bias_tab)


def _rms_rows(y, g):
    return y * lax.rsqrt(jnp.mean(y * y, axis=-1, keepdims=True) + RMS_EPS) * g


OUTPROJ_ROW_SPLITS = 2


def _outproj_kernel(*refs, router, n_parts, first_tiles):
    yc_ref, ys_ref, yn_ref = refs[:3]
    x_refs = refs[3:3 + n_parts]
    rest = refs[3 + n_parts:]
    if router:
        w_ref, nag_ref, g_ref, b_ref, rw_ref, o_ref, idx_ref, gate_ref = rest
    else:
        w_ref, nag_ref, g_ref, b_ref, o_ref = rest
    c1 = CONV_WIDTH
    c2 = CONV_WIDTH + SSD_WIDTH
    rows = o_ref.shape[0] // OUTPROJ_ROW_SPLITS
    for part in range(OUTPROJ_ROW_SPLITS):
        sl = slice(part * rows, (part + 1) * rows)
        yn = _rms_rows(yn_ref[sl, :], nag_ref[...]).astype(BF16)
        mix = _dot(yc_ref[sl, :], w_ref[0:c1, :]) + _dot(ys_ref[sl, :], w_ref[c1:c2, :]) + _dot(yn, w_ref[c2:, :])
        out = _layer_norm(ALPHA * _read_row_parts(x_refs, first_tiles, sl) + mix, g_ref[...], b_ref[...])
        o_ref[sl, :] = out
        if not router:
            continue
        o1, o2, _ = _split3(out)
        t1 = _dot(o1, rw_ref[...])
        t2 = _dot(o2, rw_ref[...])
        sh1 = LANES - N_EXPERTS
        sh2 = LANES - 2 * N_EXPERTS
        logits = ((pltpu.roll(t2, sh1, 1) + pltpu.roll(t1, sh2, 1)) + (t2 + pltpu.roll(t1, sh1, 1))) + t1
        lane = lax.broadcasted_iota(jnp.int32, (rows, LANES), 1)
        lg = jnp.where(lane < N_EXPERTS, logits, -jnp.inf)
        m1 = jnp.max(lg, axis=-1, keepdims=True)
        i1 = jnp.min(jnp.where(lg == m1, lane, LANES), axis=-1, keepdims=True)
        lg2 = jnp.where(lane == i1, -jnp.inf, lg)
        m2 = jnp.max(lg2, axis=-1, keepdims=True)
        i2 = jnp.min(jnp.where(lg2 == m2, lane, LANES), axis=-1, keepdims=True)
        e2 = jnp.exp(m2 - m1)
        den = 1.0 + e2
        idx_ref[sl, :] = jnp.where(lane == 0, i1, jnp.where(lane == 1, i2, 0))
        gate_ref[sl, :] = jnp.where(lane == 0, 1.0 / den, jnp.where(lane == 1, e2 / den, 0.0))


def _out_proj_ln(yc, ys, yn, x_parts, w_out, na_g, ln_g, ln_b, router_w=None, tm=512):
    n = yc.shape[0]
    router = router_w is not None
    row = lambda w: pl.BlockSpec((tm, w), lambda i: (i, 0))
    full = lambda a: pl.BlockSpec(a.shape, lambda i: (0,) * a.ndim)
    consts = [w_out, na_g.reshape(1, -1), ln_g.reshape(1, -1), ln_b.reshape(1, -1)]
    if router:
        consts.append(router_w)
    out_shape = [jax.ShapeDtypeStruct((n, D_MODEL), F32)]
    out_specs = [row(D_MODEL)]
    if router:
        out_shape += [jax.ShapeDtypeStruct((n, LANES), jnp.int32), jax.ShapeDtypeStruct((n, LANES), F32)]
        out_specs += [row(LANES), row(LANES)]
    res = pl.pallas_call(
        functools.partial(_outproj_kernel, router=router, n_parts=len(x_parts),
                          first_tiles=x_parts[0].shape[0] // tm),
        grid=(n // tm,),
        in_specs=[row(CONV_WIDTH), row(SSD_WIDTH), row(NA_WIDTH)] + _row_parts_specs(x_parts, tm, D_MODEL, 1)
        + [full(a) for a in consts],
        out_specs=out_specs,
        out_shape=out_shape,
        compiler_params=_cparams(("parallel",)),
        name="out_proj_router" if router else "out_proj",
    )(yc, ys, yn, *x_parts, *consts)
    return res


FFN_ROW_GROUP = 256


def _ffn_kernel(be_ref, bv_ref, x_ref, wg_ref, wu_ref, wd_ref, *rest, packed, ln, ragged):
    if ln:
        g_ref, b_ref, o_ref = rest
    else:
        (o_ref,) = rest
    i = pl.program_id(0)
    j = pl.program_id(1)

    @pl.when(j == 0)
    def _():
        o_ref[...] = jnp.zeros_like(o_ref)

    def accumulate(rows):
        if packed:
            lo, hi = _unpack_bf16_pairs(x_ref[rows, :])
            c = lo.shape[1]
            g = _dot(lo, wg_ref[0, :c, :].astype(BF16)) + _dot(hi, wg_ref[0, c:, :].astype(BF16))
            u = _dot(lo, wu_ref[0, :c, :].astype(BF16)) + _dot(hi, wu_ref[0, c:, :].astype(BF16))
        else:
            x = x_ref[rows, :].astype(BF16)
            g = _dot(x, wg_ref[0].astype(BF16))
            u = _dot(x, wu_ref[0].astype(BF16))
        h = ((g * _sigmoid(g)) * u).astype(BF16)
        o_ref[rows, :] += _dot(h, wd_ref[0].astype(BF16))

    tm = o_ref.shape[0]
    filled = bv_ref[i]

    @pl.when(filled == tm)
    def _():
        accumulate(slice(None))

    if ragged:
        for s in range(tm // FFN_ROW_GROUP):
            @pl.when(jnp.logical_and(filled < tm, filled > s * FFN_ROW_GROUP))
            def _():
                accumulate(slice(s * FFN_ROW_GROUP, (s + 1) * FFN_ROW_GROUP))

    if ln:
        @pl.when(j == pl.num_programs(1) - 1)
        def _():
            o_ref[...] = _layer_norm(ALPHA * x_ref[...] + o_ref[...], g_ref[...], b_ref[...])


def _ffn_blocks(x, block_e, block_fill, w_gate, w_up, w_down, tm, tf=FF_TILE, residual_ln=None, ragged=False):
    n, xw = x.shape
    packed = x.dtype == jnp.uint32
    d = w_gate.shape[1]
    f = w_gate.shape[2]
    nf = f // tf
    col = lambda i, j, be, bv: (be[i], 0, jnp.where(bv[i] != 0, j, nf - 1))
    rowj = lambda i, j, be, bv: (be[i], jnp.where(bv[i] != 0, j, nf - 1), 0)
    in_specs = [pl.BlockSpec((tm, xw), lambda i, j, be, bv: (i, 0)),
                pl.BlockSpec((1, d, tf), col), pl.BlockSpec((1, d, tf), col), pl.BlockSpec((1, tf, d), rowj)]
    args = [x, w_gate, w_up, w_down]
    if residual_ln is not None:
        gain, bias = residual_ln
        assert not packed
        vec = pl.BlockSpec((1, d), lambda i, j, be, bv: (0, 0))
        in_specs += [vec, vec]
        args += [gain.reshape(1, d), bias.reshape(1, d)]
    grid_spec = pltpu.PrefetchScalarGridSpec(
        num_scalar_prefetch=2,
        grid=(n // tm, nf),
        in_specs=in_specs,
        out_specs=pl.BlockSpec((tm, d), lambda i, j, be, bv: (i, 0)),
    )
    return pl.pallas_call(
        functools.partial(_ffn_kernel, packed=packed, ln=residual_ln is not None, ragged=ragged),
        grid_spec=grid_spec,
        out_shape=jax.ShapeDtypeStruct((n, d), F32),
        compiler_params=_cparams(("parallel", "arbitrary")),
        name="swiglu_blocks",
    )(block_e, block_fill, *args)


def _combine_ln_kernel(i0_ref, i1_ref, src_ref, x_ref, gate_ref, g_ref, b_ref, oa_ref, ob_ref, b0_ref, b1_ref, sem,
                       *, tm, first_tiles):
    _start_row_gather(src_ref, i0_ref, b0_ref, sem.at[0], tm)
    _start_row_gather(src_ref, i1_ref, b1_ref, sem.at[1], tm)
    _wait_row_gather(src_ref, b0_ref, sem.at[0], tm)
    _wait_row_gather(src_ref, b1_ref, sem.at[1], tm)
    f = b0_ref[...] * gate_ref[:, 0:1] + b1_ref[...] * gate_ref[:, 1:2]
    res = _layer_norm(ALPHA * x_ref[...] + f, g_ref[...], b_ref[...])
    i = pl.program_id(0)

    @pl.when(i < first_tiles)
    def _():
        oa_ref[...] = res

    @pl.when(i >= first_tiles)
    def _():
        ob_ref[...] = res


def _combine_ln(x, outs, d0, d1, gates, g, b, n_first, tm=512):
    n, d = x.shape
    first_tiles = n_first // tm
    row = pl.BlockSpec((tm, d), lambda i: (i, 0))
    vec = pl.BlockSpec((1, d), lambda i: (0, 0))
    idx = pl.BlockSpec((1, 1, tm), lambda i: (i, 0, 0), memory_space=pltpu.SMEM)
    return pl.pallas_call(
        functools.partial(_combine_ln_kernel, tm=tm, first_tiles=first_tiles), grid=(n // tm,),
        in_specs=[idx, idx, pl.BlockSpec(memory_space=pl.ANY), row, pl.BlockSpec((tm, LANES), lambda i: (i, 0)), vec, vec],
        out_specs=[pl.BlockSpec((tm, d), lambda i: (jnp.minimum(i, first_tiles - 1), 0)),
                   pl.BlockSpec((tm, d), lambda i: (jnp.maximum(i - first_tiles, 0), 0))],
        out_shape=[jax.ShapeDtypeStruct((n_first, d), F32), jax.ShapeDtypeStruct((n - n_first, d), F32)],
        scratch_shapes=[pltpu.VMEM((tm, d), F32), pltpu.VMEM((tm, d), F32), pltpu.SemaphoreType.DMA((2,))],
        compiler_params=_cparams(("arbitrary",)), name="combine_ln",
    )(d0.reshape(n // tm, 1, tm), d1.reshape(n // tm, 1, tm), outs, x, gates, g.reshape(1, d), b.reshape(1, d))


GATHER_UNROLL = 8


def _start_row_gather(src_ref, idx_ref, buf_ref, sem, tr):
    def issue(r, carry):
        pltpu.make_async_copy(src_ref.at[pl.ds(idx_ref[0, 0, r], 1), :], buf_ref.at[pl.ds(r, 1), :], sem).start()
        return carry

    lax.fori_loop(0, tr, issue, 0, unroll=GATHER_UNROLL)


def _wait_row_gather(src_ref, buf_ref, sem, tr):
    pltpu.make_async_copy(src_ref.at[pl.ds(0, tr), :], buf_ref, sem).wait()


HI16 = 0xFFFF0000


def _pack_bf16_pairs(x):
    c = x.shape[1] // 2
    lo = lax.bitcast_convert_type(x[:, :c].astype(BF16).astype(F32), jnp.uint32)
    hi = lax.bitcast_convert_type(x[:, c:].astype(BF16).astype(F32), jnp.uint32)
    return (hi & jnp.uint32(HI16)) | (lo >> 16)


def _unpack_bf16_pairs(u):
    lo = lax.bitcast_convert_type(u << 16, F32).astype(BF16)
    hi = lax.bitcast_convert_type(u & jnp.uint32(HI16), F32).astype(BF16)
    return lo, hi


def _dispatch_kernel(d0_ref, d1_ref, x_ref, init_ref, xs_ref, pk_ref, sem, *, tm):
    del init_ref
    pk_ref[...] = _pack_bf16_pairs(x_ref[...])

    def issue(r, carry):
        row = pk_ref.at[pl.ds(r, 1), :]
        pltpu.make_async_copy(row, xs_ref.at[pl.ds(d0_ref[0, 0, r], 1), :], sem.at[0]).start()
        pltpu.make_async_copy(row, xs_ref.at[pl.ds(d1_ref[0, 0, r], 1), :], sem.at[1]).start()
        return carry

    lax.fori_loop(0, tm, issue, 0, unroll=GATHER_UNROLL)
    for k in range(TOP_K):
        pltpu.make_async_copy(pk_ref, xs_ref.at[pl.ds(0, tm), :], sem.at[k]).wait()


def _dispatch_rows(x, d0, d1, p_len, tm=512):
    n, d = x.shape
    idx = pl.BlockSpec((1, 1, tm), lambda i: (i, 0, 0), memory_space=pltpu.SMEM)
    return pl.pallas_call(
        functools.partial(_dispatch_kernel, tm=tm),
        grid=(n // tm,),
        in_specs=[idx, idx, pl.BlockSpec((tm, d), lambda i: (i, 0)), pl.BlockSpec(memory_space=pl.ANY)],
        out_specs=pl.BlockSpec(memory_space=pl.ANY),
        out_shape=jax.ShapeDtypeStruct((p_len, d // 2), jnp.uint32),
        scratch_shapes=[pltpu.VMEM((tm, d // 2), jnp.uint32), pltpu.SemaphoreType.DMA((TOP_K,))],
        input_output_aliases={3: 0},
        compiler_params=_cparams(("arbitrary",)),
        name="dispatch_rows",
    )(d0.reshape(n // tm, 1, tm), d1.reshape(n // tm, 1, tm), x, jnp.zeros((p_len, d // 2), jnp.uint32))


def _routing_tables(idx, n, tm):
    flat_e = idx[:, :TOP_K].reshape(-1)
    onehot = (flat_e[:, None] == jnp.arange(N_EXPERTS, dtype=jnp.int32)[None, :]).astype(jnp.int32)
    incl = jnp.cumsum(onehot, axis=0)
    rank = jnp.sum((incl - onehot) * onehot, axis=1)
    counts = incl[-1]
    padded = ((counts + tm - 1) // tm) * tm
    pend = jnp.cumsum(padded)
    pstart = pend - padded
    dest = pstart[flat_e] + rank
    n_blocks = -(-(n * TOP_K) // tm) + N_EXPERTS
    starts = jnp.arange(n_blocks, dtype=jnp.int32) * tm
    block_e = jnp.minimum(jnp.searchsorted(pend, starts, side='right'), N_EXPERTS - 1).astype(jnp.int32)
    block_fill = jnp.clip((pstart + counts)[block_e] - starts, 0, tm).astype(jnp.int32)
    block_fill = jnp.where(starts < pend[-1], block_fill, 0)
    last_e = block_e[jnp.maximum(jnp.sum((block_fill > 0).astype(jnp.int32)) - 1, 0)]
    block_e = jnp.where(block_fill > 0, block_e, last_e)
    dest2 = dest.reshape(n, TOP_K)
    return n_blocks * tm, block_e, block_fill, dest2[:, 0], dest2[:, 1]


def _prep_w_in(w):
    cb, cc, ch = w[:, 0:512], w[:, 512:1024], w[:, 1024:1536]
    z = w[:, 1536:2560]
    xs, bc = w[:, 2560:3584], w[:, 3584:3840]
    dt = w[:, 3840:3872]
    qkv = w[:, 3872:5408]
    pad = jnp.zeros((D_MODEL, A_WIDTH - (A_DT + 2 * SSD_HEADS)), w.dtype)
    return jnp.concatenate([z, xs, cb, cc, ch, bc, dt, pad, qkv], axis=1).astype(BF16)


def _prep_router_w(rw):
    r1 = rw.astype(BF16)
    r2 = (rw - r1.astype(F32)).astype(BF16)
    r3 = (rw - r1.astype(F32) - r2.astype(F32)).astype(BF16)
    pad = jnp.zeros((rw.shape[0], LANES - 3 * N_EXPERTS), BF16)
    return jnp.concatenate([r1, r2, r3, pad], axis=1)


def _trunk(x_parts, seq, p):
    n = sum(a.shape[0] for a in x_parts)
    n_first = x_parts[0].shape[0]
    for l in range(DEPTH):
        proj_a, proj_b = _in_proj(x_parts, _prep_w_in(p['w_in'][l]))
        y_conv = _conv_mix(proj_a, p['conv_w'][l], p['conv_norm_g'][l], seq)
        cf, cbw = _ssd_consts(p['ssd_conv_w'][l], p['ssd_conv_b'][l], p['ssd_dt_bias'][l], p['ssd_a_log'][l],
                              p['ssd_d'][l], p['ssd_norm_g'][l])
        y_ssd = _ssd_pass(proj_a, _ssd_pass(proj_a, None, cf, seq, 0), cbw, seq, 1)
        y_na = _neighbourhood_attention(proj_b, p['na_rel_bias'][l], seq)
        w_out = p['w_out'][l].astype(BF16)
        if l % 2 == 0:
            x = _out_proj_ln(y_conv, y_ssd, y_na, x_parts, w_out, p['na_norm_g'][l], p['ln_mix_g'][l],
                             p['ln_mix_b'][l])[0]
            padf = ((0, 0), (0, D_FF_PAD - D_FF))
            wg = jnp.pad(p['ffn_w_gate'][l // 2], padf).astype(BF16)[None]
            wu = jnp.pad(p['ffn_w_up'][l // 2], padf).astype(BF16)[None]
            wd = jnp.pad(p['ffn_w_down'][l // 2], ((0, D_FF_PAD - D_FF), (0, 0))).astype(BF16)[None]
            nb = n // MOE_TM
            x = _ffn_blocks(x, jnp.zeros((nb,), jnp.int32), jnp.full((nb,), MOE_TM, jnp.int32), wg, wu, wd, MOE_TM,
                            residual_ln=(p['ln_ffn_g'][l], p['ln_ffn_b'][l]))
            outs_split = (x[:n_first], x[n_first:])
        else:
            x, idx, gates = _out_proj_ln(y_conv, y_ssd, y_na, x_parts, w_out, p['na_norm_g'][l], p['ln_mix_g'][l],
                                         p['ln_mix_b'][l], _prep_router_w(p['router_w'][l // 2]))
            p_len, block_e, block_fill, d0, d1 = _routing_tables(idx, n, MOE_TM)
            xs = _dispatch_rows(x, d0, d1, p_len)
            outs = _ffn_blocks(xs, block_e, block_fill, p['moe_w_gate'][l // 2], p['moe_w_up'][l // 2],
                               p['moe_w_down'][l // 2], MOE_TM, ragged=True)
            outs_split = _combine_ln(x, outs, d0, d1, gates, p['ln_ffn_g'][l], p['ln_ffn_b'][l], n_first)
            if l + 1 < DEPTH:
                x = jnp.concatenate(outs_split, axis=0)
        x_parts = [x]
    return outs_split


def kernel(x_prompt, x_sample, w_in, conv_w, conv_norm_g, ssd_conv_w, ssd_conv_b, ssd_dt_bias, ssd_a_log, ssd_d,
           ssd_norm_g, na_rel_bias, na_norm_g, w_out, ln_mix_g, ln_mix_b, ln_ffn_g, ln_ffn_b, ffn_w_gate, ffn_w_up,
           ffn_w_down, router_w, moe_w_gate, moe_w_up, moe_w_down):
    p = dict(w_in=w_in, conv_w=conv_w, conv_norm_g=conv_norm_g, ssd_conv_w=ssd_conv_w, ssd_conv_b=ssd_conv_b,
             ssd_dt_bias=ssd_dt_bias, ssd_a_log=ssd_a_log, ssd_d=ssd_d, ssd_norm_g=ssd_norm_g,
             na_rel_bias=na_rel_bias, na_norm_g=na_norm_g, w_out=w_out, ln_mix_g=ln_mix_g, ln_mix_b=ln_mix_b,
             ln_ffn_g=ln_ffn_g, ln_ffn_b=ln_ffn_b, ffn_w_gate=ffn_w_gate, ffn_w_up=ffn_w_up, ffn_w_down=ffn_w_down,
             router_w=router_w, moe_w_gate=moe_w_gate, moe_w_up=moe_w_up, moe_w_down=moe_w_down)
    bp, seq, d = x_prompt.shape
    bs, seq_s, _ = x_sample.shape
    assert seq == seq_s, "both request groups must share one sequence length"
    yp, ys = _trunk([x_prompt.reshape(bp * seq, d), x_sample.reshape(bs * seq, d)], seq, p)
    return yp.reshape(bp, seq, d), ys.reshape(bs, seq, d)
```

```python
import functools

import numpy as np
import jax
import jax.numpy as jnp
from jax import lax
from jax.experimental import pallas as pl
from jax.experimental.pallas import tpu as pltpu

F32 = jnp.float32
BF16 = jnp.bfloat16

D_MODEL = 2048
DEPTH = 2
GRID_W = 64
CONV_WIDTH = 512
SSD_HEADS = 16
SSD_HEAD_DIM = 64
SSD_WIDTH = 1024
SSD_GROUPS = 2
SSD_STATE = 64
SSD_CHUNK = 128
NA_HEADS = 8
NA_HEAD_DIM = 64
NA_WIDTH = 512
WIN_H = 8
WIN_W = 16
D_FF = 5504
N_EXPERTS = 8
TOP_K = 2
D_FF_EXPERT = 7168
ALPHA = (2 * DEPTH) ** 0.25
LN_EPS = 1e-5
RMS_EPS = 1e-6

LANES = 128
SUBLANES = 8
NEG_BIG = -1e30

A_Z, A_XS, A_CB, A_CC, A_CH, A_BC, A_DT, A_WIDTH = 0, 1024, 2048, 2560, 3072, 3584, 3840, 4096
FF_TILE = 512
D_FF_PAD = 5632
MOE_TM = 1024
VMEM_LIMIT = 56 * 1024 * 1024


def _cparams(sem, vmem=VMEM_LIMIT):
    return pltpu.CompilerParams(dimension_semantics=sem, vmem_limit_bytes=vmem)


def _sigmoid(x):
    return 1.0 / (1.0 + jnp.exp(-x))


def _split3(x):
    x1 = x.astype(BF16)
    r1 = x - x1.astype(F32)
    x2 = r1.astype(BF16)
    r2 = r1 - x2.astype(F32)
    return x1, x2, r2.astype(BF16)


def _dot(a, b):
    return jnp.dot(a, b, preferred_element_type=F32)


def _dot_exact_rhs(m, x):
    x1, x2, x3 = _split3(x)
    return (_dot(m, x3) + _dot(m, x2)) + _dot(m, x1)


def _dot_split2_lhs(x, m):
    x1, x2, _ = _split3(x)
    return _dot(x2, m) + _dot(x1, m)


def _log1p(e):
    u = 1.0 + e
    d = u - 1.0
    return jnp.where(d == 0.0, e, jnp.log(u) * (e / jnp.where(d == 0.0, 1.0, d)))


def _layer_norm(v, g, b):
    mu = jnp.mean(v, axis=-1, keepdims=True)
    c = v - mu
    var = jnp.mean(c * c, axis=-1, keepdims=True)
    return c * lax.rsqrt(var + LN_EPS) * g + b


def _row_parts_specs(parts, tm, width, grid_rank):
    if len(parts) == 1:
        return [pl.BlockSpec((tm, width), (lambda i: (i, 0)) if grid_rank == 1 else (lambda i, j: (i, 0)))]
    first_tiles = parts[0].shape[0] // tm
    if grid_rank == 1:
        return [pl.BlockSpec((tm, width), lambda i: (jnp.minimum(i, first_tiles - 1), 0)),
                pl.BlockSpec((tm, width), lambda i: (jnp.maximum(i - first_tiles, 0), 0))]
    return [pl.BlockSpec((tm, width), lambda i, j: (jnp.minimum(i, first_tiles - 1), 0)),
            pl.BlockSpec((tm, width), lambda i, j: (jnp.maximum(i - first_tiles, 0), 0))]


def _read_row_parts(x_refs, first_tiles, rows=slice(None)):
    x = x_refs[0][rows, :]
    if len(x_refs) == 2:
        x = jnp.where(pl.program_id(0) < first_tiles, x, x_refs[1][rows, :])
    return x


def _inproj_kernel(*refs, n_parts, first_tiles, a_blocks):
    x_refs = refs[:n_parts]
    w_ref, oa_ref, ob_ref, xb_ref = refs[n_parts:]
    j = pl.program_id(1)

    @pl.when(j == 0)
    def _():
        xb_ref[...] = _read_row_parts(x_refs, first_tiles).astype(BF16)

    res = _dot(xb_ref[...], w_ref[...])

    @pl.when(j < a_blocks)
    def _():
        oa_ref[...] = res

    @pl.when(j >= a_blocks)
    def _():
        ob_ref[...] = res.astype(ob_ref.dtype)


def _in_proj(x_parts, w, tm=1024, tn=512):
    n = sum(a.shape[0] for a in x_parts)
    k = x_parts[0].shape[1]
    a_blocks = A_WIDTH // tn
    b_width = w.shape[1] - A_WIDTH
    first_tiles = x_parts[0].shape[0] // tm
    return pl.pallas_call(
        functools.partial(_inproj_kernel, n_parts=len(x_parts), first_tiles=first_tiles, a_blocks=a_blocks),
        grid=(n // tm, w.shape[1] // tn),
        in_specs=_row_parts_specs(x_parts, tm, k, 2) + [pl.BlockSpec((k, tn), lambda i, j: (0, j))],
        out_specs=[pl.BlockSpec((tm, tn), lambda i, j: (i, jnp.minimum(j, a_blocks - 1))),
                   pl.BlockSpec((tm, tn), lambda i, j: (i, jnp.maximum(j - a_blocks, 0)))],
        out_shape=[jax.ShapeDtypeStruct((n, A_WIDTH), F32), jax.ShapeDtypeStruct((n, b_width), BF16)],
        scratch_shapes=[pltpu.VMEM((tm, k), BF16)],
        compiler_params=_cparams(("parallel", "arbitrary")),
        name="in_proj",
    )(*x_parts, w)


def _shift_rows(x, prev_row, next_row, first, last):
    n = x.shape[0]
    row = lax.broadcasted_iota(jnp.int32, (n, 1), 0)
    prev_row = jnp.where(first, 0.0, prev_row)
    next_row = jnp.where(last, 0.0, next_row)
    x_prev = jnp.where(row == 0, prev_row, pltpu.roll(x, 1, 0))
    x_next = jnp.where(row == n - 1, next_row, pltpu.roll(x, n - 1, 0))
    return x_prev, x_next


def _convmix_kernel(cb_ref, cc_ref, ch_ref, ccp_ref, chp_ref, ccn_ref, chn_ref, w_ref, g_ref, o_ref, *, tiles_per_seq):
    i = pl.program_id(0)
    first = (i % tiles_per_seq) == 0
    last = (i % tiles_per_seq) == tiles_per_seq - 1
    u = cc_ref[...] * ch_ref[...]
    up = ccp_ref[SUBLANES - 1:SUBLANES, :] * chp_ref[SUBLANES - 1:SUBLANES, :]
    un = ccn_ref[0:1, :] * chn_ref[0:1, :]
    u_prev, u_next = _shift_rows(u, up, un, first, last)
    conv = u_prev * w_ref[0:1, :] + u * w_ref[1:2, :] + u_next * w_ref[2:3, :]
    y = cb_ref[...] * conv
    y = y * lax.rsqrt(jnp.mean(y * y, axis=-1, keepdims=True) + RMS_EPS)
    o_ref[...] = (y * g_ref[...]).astype(o_ref.dtype)


def _halo_specs(tt, width, col0, n_rows):
    cb = col0 // width
    r = tt // SUBLANES
    last_blk = n_rows // SUBLANES - 1
    prev = pl.BlockSpec((SUBLANES, width), lambda i: (jnp.maximum(i * r - 1, 0), cb))
    nxt = pl.BlockSpec((SUBLANES, width), lambda i: (jnp.minimum((i + 1) * r, last_blk), cb))
    return prev, nxt


def _conv_mix(proj_a, conv_w, norm_g, seq, tt=1024):
    n = proj_a.shape[0]
    w = CONV_WIDTH
    main = lambda c0: pl.BlockSpec((tt, w), lambda i: (i, c0 // w))
    ccp, ccn = _halo_specs(tt, w, A_CC, n)
    chp, chn = _halo_specs(tt, w, A_CH, n)
    full = lambda shape: pl.BlockSpec(shape, lambda i: (0, 0))
    return pl.pallas_call(
        functools.partial(_convmix_kernel, tiles_per_seq=seq // tt),
        grid=(n // tt,),
        in_specs=[main(A_CB), main(A_CC), main(A_CH), ccp, chp, ccn, chn, full((3, w)), full((1, w))],
        out_specs=pl.BlockSpec((tt, w), lambda i: (i, 0)),
        out_shape=jax.ShapeDtypeStruct((n, w), BF16),
        compiler_params=_cparams(("parallel",)),
        name="conv_mix",
    )(proj_a, proj_a, proj_a, proj_a, proj_a, proj_a, proj_a, conv_w, norm_g.reshape(1, w))


SSD_CHUNKS_PER_STEP = 4


def _ssd_kernel(*refs, direction, final, tiles):
    if final:
        (xs_s, bc_s, dt_ref, z_ref, yf_ref, dtb_ref, alog_ref, e_ref, hm_ref, dsk_ref, ng_ref,
         o_ref, h_ref, y_ref) = refs
    else:
        (xs_ref, xsp_ref, xsn_ref, bc_ref, bcp_ref, bcn_ref, dt_ref,
         cwx_ref, cbx_ref, cwb_ref, cbb_ref, dtb_ref, alog_ref, e_ref, hm_ref,
         o_ref, xs_s, bc_s, h_ref) = refs
        y_ref = o_ref
    L = SSD_CHUNK
    c = pl.program_id(1)
    tile = c if direction == 0 else tiles - 1 - c
    first = tile == 0
    last = tile == tiles - 1

    @pl.when(c == 0)
    def _():
        h_ref[...] = jnp.zeros_like(h_ref)

    def conv_silu(x_ref, p_ref, n_ref, w_ref, b_ref):
        x = x_ref[...]
        x_prev, x_next = _shift_rows(x, p_ref[SUBLANES - 1:SUBLANES, :], n_ref[0:1, :], first, last)
        v = x_prev * w_ref[0:1, :] + x * w_ref[1:2, :] + x_next * w_ref[2:3, :] + b_ref[...]
        return v * _sigmoid(v)

    if not final:
        xs_s[...] = conv_silu(xs_ref, xsp_ref, xsn_ref, cwx_ref, cbx_ref)
        bc_s[...] = conv_silu(bc_ref, bcp_ref, bcn_ref, cwb_ref, cbb_ref)

    li = lax.broadcasted_iota(jnp.int32, (L, L), 0)
    si = lax.broadcasted_iota(jnp.int32, (L, L), 1)
    valid = (si <= li) if direction == 0 else (si >= li)
    tri = jnp.where(valid, 1.0, 0.0).astype(BF16)
    lane = lax.broadcasted_iota(jnp.int32, (1, LANES), 1)
    nt = (((1,), (1,)), ((), ()))
    e_mat = e_ref[...]
    neg_a = -jnp.exp(alog_ref[...])
    heads_per_group = SSD_HEADS // SSD_GROUPS
    chunks = xs_s.shape[0] // L
    order = range(chunks) if direction == 0 else range(chunks - 1, -1, -1)

    for k in order:
        rows = slice(k * L, (k + 1) * L)
        xs = xs_s[rows, :]
        bm = bc_s[rows, :LANES]
        cm = bc_s[rows, LANES:]

        dtr = dt_ref[rows, :] + dtb_ref[...]
        dt = jnp.maximum(dtr, 0.0) + _log1p(jnp.exp(-jnp.abs(dtr)))
        acum = _dot_exact_rhs(tri, dt * neg_a)
        acum_t = acum.T
        total = acum[L - 1:L, :] if direction == 0 else acum[0:1, :]
        dec_end = jnp.exp(total - acum)
        ea = jnp.exp(acum)

        dtx = _dot_split2_lhs(dt, e_mat)
        eax = _dot_split2_lhs(ea, e_mat)
        dex = _dot_split2_lhs(dec_end * dt, e_mat)

        xdt = xs * dtx
        xw = (xs * dex).astype(BF16)
        cm_b = cm.astype(BF16)
        bm_b = bm.astype(BF16)

        h_prev = h_ref[...]
        y_off = _dot(cm_b, h_prev.astype(BF16)) * eax

        for g in range(SSD_GROUPS):
            cg = jnp.where((lane // SSD_STATE) == g, cm, 0.0).astype(BF16)
            cb = lax.dot_general(cg, bm_b, nt, preferred_element_type=F32)
            for pair in range(heads_per_group // 2):
                acc = None
                for half in range(2):
                    h = g * heads_per_group + pair * 2 + half
                    hl = direction * SSD_HEADS + h
                    seg = acum[:, hl:hl + 1] - acum_t[hl:hl + 1, :]
                    w = (jnp.where(valid, jnp.exp(seg), 0.0) * cb).astype(BF16)
                    c0 = (h // 2) * LANES
                    rhs = jnp.where((lane // SSD_HEAD_DIM) == half, xdt[:, c0:c0 + LANES], 0.0).astype(BF16)
                    part = _dot(w, rhs)
                    acc = part if acc is None else acc + part
                c0 = (g * heads_per_group // 2 + pair) * LANES
                y_pair = acc + y_off[:, c0:c0 + LANES]
                if final:
                    y_pair = yf_ref[rows, c0:c0 + LANES] + y_pair + dsk_ref[:, c0:c0 + LANES] * xs[:, c0:c0 + LANES]
                    zz = z_ref[rows, c0:c0 + LANES]
                    y_pair = y_pair * (zz * _sigmoid(zz))
                y_ref[rows, c0:c0 + LANES] = y_pair

        cd = eax[L - 1:L, :] if direction == 0 else eax[0:1, :]
        s_new = _dot(bm.T.astype(BF16), xw)
        h_ref[...] = (cd * h_prev + s_new) * hm_ref[...]

    if final:
        gw = SSD_WIDTH // SSD_GROUPS
        for g in range(SSD_GROUPS):
            y = y_ref[:, g * gw:(g + 1) * gw]
            y = y * lax.rsqrt(jnp.mean(y * y, axis=-1, keepdims=True) + RMS_EPS)
            o_ref[:, g * gw:(g + 1) * gw] = (y * ng_ref[:, g * gw:(g + 1) * gw]).astype(o_ref.dtype)


def _ssd_pass(proj_a, fwd_outs, params, seq, direction):
    n = proj_a.shape[0]
    T = SSD_CHUNK * SSD_CHUNKS_PER_STEP
    tiles = seq // T
    nseq = n // seq
    final = direction == 1
    cw, cb_, dtb, alog, e_mat, hmask, dsk, ng = params
    bcw = 2 * SSD_GROUPS * SSD_STATE

    def row_blk(b, c):
        t = c if direction == 0 else tiles - 1 - c
        return b * tiles + t

    def main(width, col0):
        return pl.BlockSpec((T, width), lambda b, c: (row_blk(b, c), col0 // width))

    r = T // SUBLANES
    last_blk = n // SUBLANES - 1

    def halo(width, col0):
        prev = pl.BlockSpec((SUBLANES, width), lambda b, c: (jnp.maximum(row_blk(b, c) * r - 1, 0), col0 // width))
        nxt = pl.BlockSpec((SUBLANES, width), lambda b, c: (jnp.minimum((row_blk(b, c) + 1) * r, last_blk), col0 // width))
        return prev, nxt

    xsp, xsn = halo(SSD_WIDTH, A_XS)
    bcp, bcn = halo(bcw, A_BC)
    full = lambda a: pl.BlockSpec(a.shape, lambda b, c: (0,) * a.ndim)

    rows = lambda width: pl.BlockSpec((T, width), lambda b, c: (row_blk(b, c), 0))
    if final:
        y_fwd, xs_act, bc_act = fwd_outs
        in_specs = [rows(SSD_WIDTH), rows(bcw), main(LANES, A_DT), main(SSD_WIDTH, A_Z), rows(SSD_WIDTH)]
        args = [xs_act, bc_act, proj_a, proj_a, y_fwd]
        consts = [dtb, alog, e_mat, hmask, dsk, ng]
        out_specs = rows(SSD_WIDTH)
        out_shape = jax.ShapeDtypeStruct((n, SSD_WIDTH), BF16)
    else:
        in_specs = [main(SSD_WIDTH, A_XS), xsp, xsn, main(bcw, A_BC), bcp, bcn, main(LANES, A_DT)]
        args = [proj_a] * 7
        consts = [cw[:, :SSD_WIDTH], cb_[:, :SSD_WIDTH], cw[:, SSD_WIDTH:], cb_[:, SSD_WIDTH:], dtb, alog, e_mat, hmask]
        out_specs = [rows(SSD_WIDTH), rows(SSD_WIDTH), rows(bcw)]
        out_shape = [jax.ShapeDtypeStruct((n, SSD_WIDTH), F32), jax.ShapeDtypeStruct((n, SSD_WIDTH), F32),
                     jax.ShapeDtypeStruct((n, bcw), F32)]
    in_specs += [full(a) for a in consts]
    args += consts
    scratch = [pltpu.VMEM((LANES, SSD_WIDTH), F32)]
    if final:
        scratch.append(pltpu.VMEM((T, SSD_WIDTH), F32))
    return pl.pallas_call(
        functools.partial(_ssd_kernel, direction=direction, final=final, tiles=tiles),
        grid=(nseq, tiles),
        in_specs=in_specs,
        out_specs=out_specs,
        out_shape=out_shape,
        scratch_shapes=scratch,
        compiler_params=_cparams(("parallel", "arbitrary")),
        name="ssd_bwd" if final else "ssd_fwd",
    )(*args)


def _ssd_consts(ssd_conv_w, ssd_conv_b, ssd_dt_bias, ssd_a_log, ssd_d, ssd_norm_g):
    pad = LANES - 2 * SSD_HEADS
    dtb = jnp.pad(ssd_dt_bias.reshape(1, -1), ((0, 0), (0, pad)))
    alog = jnp.pad(ssd_a_log.reshape(1, -1), ((0, 0), (0, pad)))
    col_head = np.arange(SSD_WIDTH) // SSD_HEAD_DIM
    e_mats = [jnp.asarray((np.arange(LANES)[:, None] == d * SSD_HEADS + col_head[None, :]), BF16) for d in range(2)]
    row_group = np.arange(LANES) // SSD_STATE
    col_group = col_head // (SSD_HEADS // SSD_GROUPS)
    hmask = jnp.asarray(row_group[:, None] == col_group[None, :], F32)
    dsk = jnp.repeat(ssd_d, SSD_HEAD_DIM).reshape(1, SSD_WIDTH)
    ng = ssd_norm_g.reshape(1, SSD_WIDTH)
    cb_ = ssd_conv_b.reshape(1, -1)
    return [(ssd_conv_w, cb_, dtb, alog, e_mats[d], hmask, dsk, ng) for d in range(2)]


NA_QROWS = 4
NA_KROWS = NA_QROWS + WIN_H


def _na_kernel(kb_ref, cls_ref, q_ref, k_ref, v_ref, bias_ref, o_ref, *, blocks_per_step):
    step = pl.program_id(2)
    lane = lax.broadcasted_iota(jnp.int32, (1, LANES), 1)
    nt = (((1,), (1,)), ((), ()))
    nq = NA_QROWS * GRID_W
    nk = NA_KROWS * GRID_W
    for j in range(blocks_per_step):
        blk = step * blocks_per_step + j
        k0 = pl.multiple_of(kb_ref[blk] * GRID_W, GRID_W)
        cls = cls_ref[blk]
        q = q_ref[j * nq:(j + 1) * nq, :] * (NA_HEAD_DIM ** -0.5)
        kk = k_ref[pl.ds(k0, nk), :]
        vv = v_ref[pl.ds(k0, nk), :]
        outs = []
        for hh in range(2):
            qm = jnp.where((lane // NA_HEAD_DIM) == hh, q, jnp.zeros_like(q))
            s = lax.dot_general(qm, kk, nt, preferred_element_type=F32) + bias_ref[cls, hh]
            m = jnp.max(s, axis=-1, keepdims=True)
            p = jnp.exp(s - m)
            l = jnp.sum(p, axis=-1, keepdims=True)
            outs.append(_dot(p.astype(BF16), vv) / l)
        o = jnp.where((lane // NA_HEAD_DIM) == 0, outs[0], outs[1])
        o_ref[j * nq:(j + 1) * nq, :] = o.astype(o_ref.dtype)


def _na_tables(rel_bias, rows):
    qc = np.arange(GRID_W)
    cs = np.clip(qc - WIN_W // 2, 0, GRID_W - WIN_W)
    kc = np.arange(GRID_W)
    col_valid = (kc[None, :] >= cs[:, None]) & (kc[None, :] < cs[:, None] + WIN_W)
    dj = np.clip(kc[None, :] - qc[:, None] + WIN_W - 1, 0, 2 * WIN_W - 2)
    nblk = rows // NA_QROWS
    kb = np.clip(np.arange(nblk) * NA_QROWS - WIN_H // 2, 0, rows - NA_KROWS)
    patterns, cls = [], []
    for b in range(nblk):
        r = b * NA_QROWS + np.arange(NA_QROWS)
        rs = np.clip(r - WIN_H // 2, 0, rows - WIN_H)
        key = (tuple(rs - kb[b]), int(kb[b] - b * NA_QROWS))
        if key not in patterns:
            patterns.append(key)
        cls.append(patterns.index(key))
    nh, nw = 2 * WIN_H - 1, 2 * WIN_W - 1
    bias_ext = jnp.pad(rel_bias.astype(F32), ((0, 0), (0, 1), (0, 1)), constant_values=NEG_BIG)
    di_sel = []
    for off, shift in patterns:
        kr = np.arange(NA_KROWS)[None, :]
        i = np.arange(NA_QROWS)[:, None]
        rel = kr - np.asarray(off)[:, None]
        row_valid = (rel >= 0) & (rel < WIN_H)
        di = np.where(row_valid, np.clip(shift + kr - i + WIN_H - 1, 0, nh - 1), nh)
        di_sel.append(np.eye(nh + 1, dtype=np.float32)[di])
    dj_sel = np.eye(nw + 1, dtype=np.float32)[np.where(col_valid, dj, nw)]
    tab = jnp.einsum('pikd,hde,qwe->phiqkw', np.stack(di_sel), bias_ext, dj_sel, precision=lax.Precision.HIGHEST)
    tab = tab.reshape(len(patterns), NA_HEADS, NA_QROWS * GRID_W, NA_KROWS * GRID_W)
    return jnp.asarray(kb, jnp.int32), jnp.asarray(cls, jnp.int32), tab


def _neighbourhood_attention(proj_b, rel_bias, seq, blocks_per_step=8):
    n = proj_b.shape[0]
    nseq = n // seq
    rows = seq // GRID_W
    kb, cls, bias_tab = _na_tables(rel_bias, rows)
    tq = blocks_per_step * NA_QROWS * GRID_W
    pairs = NA_HEADS // 2
    steps = seq // tq
    ncls = bias_tab.shape[0]
    grid_spec = pltpu.PrefetchScalarGridSpec(
        num_scalar_prefetch=2,
        grid=(nseq, pairs, steps),
        in_specs=[pl.BlockSpec((tq, LANES), lambda b, p, r, kb_, cls_: (b * steps + r, p)),
                  pl.BlockSpec((seq, LANES), lambda b, p, r, kb_, cls_: (b, pairs + p)),
                  pl.BlockSpec((seq, LANES), lambda b, p, r, kb_, cls_: (b, 2 * pairs + p)),
                  pl.BlockSpec((ncls, 2) + bias_tab.shape[2:], lambda b, p, r, kb_, cls_: (0, p, 0, 0))],
        out_specs=pl.BlockSpec((tq, LANES), lambda b, p, r, kb_, cls_: (b * steps + r, p)),
    )
    return pl.pallas_call(
        functools.partial(_na_kernel, blocks_per_step=blocks_per_step),
        grid_spec=grid_spec,
        out_shape=jax.ShapeDtypeStruct((n, NA_WIDTH), F32),
        compiler_params=_cparams(("parallel", "arbitrary", "arbitrary")),
        name="nbr_attn",
    )(kb, cls, proj_b, proj_b, proj_b, bias_tab)


def _rms_rows(y, g):
    return y * lax.rsqrt(jnp.mean(y * y, axis=-1, keepdims=True) + RMS_EPS) * g


OUTPROJ_ROW_SPLITS = 2


def _outproj_kernel(*refs, router, n_parts, first_tiles):
    yc_ref, ys_ref, yn_ref = refs[:3]
    x_refs = refs[3:3 + n_parts]
    rest = refs[3 + n_parts:]
    if router:
        w_ref, nag_ref, g_ref, b_ref, rw_ref, o_ref, idx_ref, gate_ref = rest
    else:
        w_ref, nag_ref, g_ref, b_ref, o_ref = rest
    c1 = CONV_WIDTH
    c2 = CONV_WIDTH + SSD_WIDTH
    rows = o_ref.shape[0] // OUTPROJ_ROW_SPLITS
    for part in range(OUTPROJ_ROW_SPLITS):
        sl = slice(part * rows, (part + 1) * rows)
        yn = _rms_rows(yn_ref[sl, :], nag_ref[...]).astype(BF16)
        mix = _dot(yc_ref[sl, :], w_ref[0:c1, :]) + _dot(ys_ref[sl, :], w_ref[c1:c2, :]) + _dot(yn, w_ref[c2:, :])
        out = _layer_norm(ALPHA * _read_row_parts(x_refs, first_tiles, sl) + mix, g_ref[...], b_ref[...])
        o_ref[sl, :] = out
        if not router:
            continue
        o1, o2, _ = _split3(out)
        t1 = _dot(o1, rw_ref[...])
        t2 = _dot(o2, rw_ref[...])
        sh1 = LANES - N_EXPERTS
        sh2 = LANES - 2 * N_EXPERTS
        logits = ((pltpu.roll(t2, sh1, 1) + pltpu.roll(t1, sh2, 1)) + (t2 + pltpu.roll(t1, sh1, 1))) + t1
        lane = lax.broadcasted_iota(jnp.int32, (rows, LANES), 1)
        lg = jnp.where(lane < N_EXPERTS, logits, -jnp.inf)
        m1 = jnp.max(lg, axis=-1, keepdims=True)
        i1 = jnp.min(jnp.where(lg == m1, lane, LANES), axis=-1, keepdims=True)
        lg2 = jnp.where(lane == i1, -jnp.inf, lg)
        m2 = jnp.max(lg2, axis=-1, keepdims=True)
        i2 = jnp.min(jnp.where(lg2 == m2, lane, LANES), axis=-1, keepdims=True)
        e2 = jnp.exp(m2 - m1)
        den = 1.0 + e2
        idx_ref[sl, :] = jnp.where(lane == 0, i1, jnp.where(lane == 1, i2, 0))
        gate_ref[sl, :] = jnp.where(lane == 0, 1.0 / den, jnp.where(lane == 1, e2 / den, 0.0))


def _out_proj_ln(yc, ys, yn, x_parts, w_out, na_g, ln_g, ln_b, router_w=None, tm=512):
    n = yc.shape[0]
    router = router_w is not None
    row = lambda w: pl.BlockSpec((tm, w), lambda i: (i, 0))
    full = lambda a: pl.BlockSpec(a.shape, lambda i: (0,) * a.ndim)
    consts = [w_out, na_g.reshape(1, -1), ln_g.reshape(1, -1), ln_b.reshape(1, -1)]
    if router:
        consts.append(router_w)
    out_shape = [jax.ShapeDtypeStruct((n, D_MODEL), F32)]
    out_specs = [row(D_MODEL)]
    if router:
        out_shape += [jax.ShapeDtypeStruct((n, LANES), jnp.int32), jax.ShapeDtypeStruct((n, LANES), F32)]
        out_specs += [row(LANES), row(LANES)]
    res = pl.pallas_call(
        functools.partial(_outproj_kernel, router=router, n_parts=len(x_parts),
                          first_tiles=x_parts[0].shape[0] // tm),
        grid=(n // tm,),
        in_specs=[row(CONV_WIDTH), row(SSD_WIDTH), row(NA_WIDTH)] + _row_parts_specs(x_parts, tm, D_MODEL, 1)
        + [full(a) for a in consts],
        out_specs=out_specs,
        out_shape=out_shape,
        compiler_params=_cparams(("parallel",)),
        name="out_proj_router" if router else "out_proj",
    )(yc, ys, yn, *x_parts, *consts)
    return res


FFN_ROW_GROUP = 256


def _ffn_kernel(be_ref, bv_ref, x_ref, wg_ref, wu_ref, wd_ref, *rest, packed, ln, ragged):
    if ln:
        g_ref, b_ref, o_ref = rest
    else:
        (o_ref,) = rest
    i = pl.program_id(0)
    j = pl.program_id(1)

    @pl.when(j == 0)
    def _():
        o_ref[...] = jnp.zeros_like(o_ref)

    def accumulate(rows):
        if packed:
            lo, hi = _unpack_bf16_pairs(x_ref[rows, :])
            c = lo.shape[1]
            g = _dot(lo, wg_ref[0, :c, :].astype(BF16)) + _dot(hi, wg_ref[0, c:, :].astype(BF16))
            u = _dot(lo, wu_ref[0, :c, :].astype(BF16)) + _dot(hi, wu_ref[0, c:, :].astype(BF16))
        else:
            x = x_ref[rows, :].astype(BF16)
            g = _dot(x, wg_ref[0].astype(BF16))
            u = _dot(x, wu_ref[0].astype(BF16))
        h = ((g * _sigmoid(g)) * u).astype(BF16)
        o_ref[rows, :] += _dot(h, wd_ref[0].astype(BF16))

    tm = o_ref.shape[0]
    filled = bv_ref[i]

    @pl.when(filled == tm)
    def _():
        accumulate(slice(None))

    if ragged:
        for s in range(tm // FFN_ROW_GROUP):
            @pl.when(jnp.logical_and(filled < tm, filled > s * FFN_ROW_GROUP))
            def _():
                accumulate(slice(s * FFN_ROW_GROUP, (s + 1) * FFN_ROW_GROUP))

    if ln:
        @pl.when(j == pl.num_programs(1) - 1)
        def _():
            o_ref[...] = _layer_norm(ALPHA * x_ref[...] + o_ref[...], g_ref[...], b_ref[...])


def _ffn_blocks(x, block_e, block_fill, w_gate, w_up, w_down, tm, tf=FF_TILE, residual_ln=None, ragged=False):
    n, xw = x.shape
    packed = x.dtype == jnp.uint32
    d = w_gate.shape[1]
    f = w_gate.shape[2]
    nf = f // tf
    col = lambda i, j, be, bv: (be[i], 0, jnp.where(bv[i] != 0, j, nf - 1))
    rowj = lambda i, j, be, bv: (be[i], jnp.where(bv[i] != 0, j, nf - 1), 0)
    in_specs = [pl.BlockSpec((tm, xw), lambda i, j, be, bv: (i, 0)),
                pl.BlockSpec((1, d, tf), col), pl.BlockSpec((1, d, tf), col), pl.BlockSpec((1, tf, d), rowj)]
    args = [x, w_gate, w_up, w_down]
    if residual_ln is not None:
        gain, bias = residual_ln
        assert not packed
        vec = pl.BlockSpec((1, d), lambda i, j, be, bv: (0, 0))
        in_specs += [vec, vec]
        args += [gain.reshape(1, d), bias.reshape(1, d)]
    grid_spec = pltpu.PrefetchScalarGridSpec(
        num_scalar_prefetch=2,
        grid=(n // tm, nf),
        in_specs=in_specs,
        out_specs=pl.BlockSpec((tm, d), lambda i, j, be, bv: (i, 0)),
    )
    return pl.pallas_call(
        functools.partial(_ffn_kernel, packed=packed, ln=residual_ln is not None, ragged=ragged),
        grid_spec=grid_spec,
        out_shape=jax.ShapeDtypeStruct((n, d), F32),
        compiler_params=_cparams(("parallel", "arbitrary")),
        name="swiglu_blocks",
    )(block_e, block_fill, *args)


def _combine_ln_kernel(i0_ref, i1_ref, src_ref, x_ref, gate_ref, g_ref, b_ref, oa_ref, ob_ref, b0_ref, b1_ref, sem,
                       *, tm, first_tiles):
    _start_row_gather(src_ref, i0_ref, b0_ref, sem.at[0], tm)
    _start_row_gather(src_ref, i1_ref, b1_ref, sem.at[1], tm)
    _wait_row_gather(src_ref, b0_ref, sem.at[0], tm)
    _wait_row_gather(src_ref, b1_ref, sem.at[1], tm)
    f = b0_ref[...] * gate_ref[:, 0:1] + b1_ref[...] * gate_ref[:, 1:2]
    res = _layer_norm(ALPHA * x_ref[...] + f, g_ref[...], b_ref[...])
    i = pl.program_id(0)

    @pl.when(i < first_tiles)
    def _():
        oa_ref[...] = res

    @pl.when(i >= first_tiles)
    def _():
        ob_ref[...] = res


def _combine_ln(x, outs, d0, d1, gates, g, b, n_first, tm=512):
    n, d = x.shape
    first_tiles = n_first // tm
    row = pl.BlockSpec((tm, d), lambda i: (i, 0))
    vec = pl.BlockSpec((1, d), lambda i: (0, 0))
    idx = pl.BlockSpec((1, 1, tm), lambda i: (i, 0, 0), memory_space=pltpu.SMEM)
    return pl.pallas_call(
        functools.partial(_combine_ln_kernel, tm=tm, first_tiles=first_tiles), grid=(n // tm,),
        in_specs=[idx, idx, pl.BlockSpec(memory_space=pl.ANY), row, pl.BlockSpec((tm, LANES), lambda i: (i, 0)), vec, vec],
        out_specs=[pl.BlockSpec((tm, d), lambda i: (jnp.minimum(i, first_tiles - 1), 0)),
                   pl.BlockSpec((tm, d), lambda i: (jnp.maximum(i - first_tiles, 0), 0))],
        out_shape=[jax.ShapeDtypeStruct((n_first, d), F32), jax.ShapeDtypeStruct((n - n_first, d), F32)],
        scratch_shapes=[pltpu.VMEM((tm, d), F32), pltpu.VMEM((tm, d), F32), pltpu.SemaphoreType.DMA((2,))],
        compiler_params=_cparams(("arbitrary",)), name="combine_ln",
    )(d0.reshape(n // tm, 1, tm), d1.reshape(n // tm, 1, tm), outs, x, gates, g.reshape(1, d), b.reshape(1, d))


GATHER_UNROLL = 8


def _start_row_gather(src_ref, idx_ref, buf_ref, sem, tr):
    def issue(r, carry):
        pltpu.make_async_copy(src_ref.at[pl.ds(idx_ref[0, 0, r], 1), :], buf_ref.at[pl.ds(r, 1), :], sem).start()
        return carry

    lax.fori_loop(0, tr, issue, 0, unroll=GATHER_UNROLL)


def _wait_row_gather(src_ref, buf_ref, sem, tr):
    pltpu.make_async_copy(src_ref.at[pl.ds(0, tr), :], buf_ref, sem).wait()


HI16 = 0xFFFF0000


def _pack_bf16_pairs(x):
    c = x.shape[1] // 2
    lo = lax.bitcast_convert_type(x[:, :c].astype(BF16).astype(F32), jnp.uint32)
    hi = lax.bitcast_convert_type(x[:, c:].astype(BF16).astype(F32), jnp.uint32)
    return (hi & jnp.uint32(HI16)) | (lo >> 16)


def _unpack_bf16_pairs(u):
    lo = lax.bitcast_convert_type(u << 16, F32).astype(BF16)
    hi = lax.bitcast_convert_type(u & jnp.uint32(HI16), F32).astype(BF16)
    return lo, hi


def _dispatch_kernel(d0_ref, d1_ref, x_ref, init_ref, xs_ref, pk_ref, sem, *, tm):
    del init_ref
    pk_ref[...] = _pack_bf16_pairs(x_ref[...])

    def issue(r, carry):
        row = pk_ref.at[pl.ds(r, 1), :]
        pltpu.make_async_copy(row, xs_ref.at[pl.ds(d0_ref[0, 0, r], 1), :], sem.at[0]).start()
        pltpu.make_async_copy(row, xs_ref.at[pl.ds(d1_ref[0, 0, r], 1), :], sem.at[1]).start()
        return carry

    lax.fori_loop(0, tm, issue, 0, unroll=GATHER_UNROLL)
    for k in range(TOP_K):
        pltpu.make_async_copy(pk_ref, xs_ref.at[pl.ds(0, tm), :], sem.at[k]).wait()


def _dispatch_rows(x, d0, d1, p_len, tm=512):
    n, d = x.shape
    idx = pl.BlockSpec((1, 1, tm), lambda i: (i, 0, 0), memory_space=pltpu.SMEM)
    return pl.pallas_call(
        functools.partial(_dispatch_kernel, tm=tm),
        grid=(n // tm,),
        in_specs=[idx, idx, pl.BlockSpec((tm, d), lambda i: (i, 0)), pl.BlockSpec(memory_space=pl.ANY)],
        out_specs=pl.BlockSpec(memory_space=pl.ANY),
        out_shape=jax.ShapeDtypeStruct((p_len, d // 2), jnp.uint32),
        scratch_shapes=[pltpu.VMEM((tm, d // 2), jnp.uint32), pltpu.SemaphoreType.DMA((TOP_K,))],
        input_output_aliases={3: 0},
        compiler_params=_cparams(("arbitrary",)),
        name="dispatch_rows",
    )(d0.reshape(n // tm, 1, tm), d1.reshape(n // tm, 1, tm), x, jnp.zeros((p_len, d // 2), jnp.uint32))


def _routing_tables(idx, n, tm):
    flat_e = idx[:, :TOP_K].reshape(-1)
    onehot = (flat_e[:, None] == jnp.arange(N_EXPERTS, dtype=jnp.int32)[None, :]).astype(jnp.int32)
    incl = jnp.cumsum(onehot, axis=0)
    rank = jnp.sum((incl - onehot) * onehot, axis=1)
    counts = incl[-1]
    padded = ((counts + tm - 1) // tm) * tm
    pend = jnp.cumsum(padded)
    pstart = pend - padded
    dest = pstart[flat_e] + rank
    n_blocks = -(-(n * TOP_K) // tm) + N_EXPERTS
    starts = jnp.arange(n_blocks, dtype=jnp.int32) * tm
    block_e = jnp.minimum(jnp.searchsorted(pend, starts, side='right'), N_EXPERTS - 1).astype(jnp.int32)
    block_fill = jnp.clip((pstart + counts)[block_e] - starts, 0, tm).astype(jnp.int32)
    block_fill = jnp.where(starts < pend[-1], block_fill, 0)
    last_e = block_e[jnp.maximum(jnp.sum((block_fill > 0).astype(jnp.int32)) - 1, 0)]
    block_e = jnp.where(block_fill > 0, block_e, last_e)
    dest2 = dest.reshape(n, TOP_K)
    return n_blocks * tm, block_e, block_fill, dest2[:, 0], dest2[:, 1]


def _prep_w_in(w):
    cb, cc, ch = w[:, 0:512], w[:, 512:1024], w[:, 1024:1536]
    z = w[:, 1536:2560]
    xs, bc = w[:, 2560:3584], w[:, 3584:3840]
    dt = w[:, 3840:3872]
    qkv = w[:, 3872:5408]
    pad = jnp.zeros((D_MODEL, A_WIDTH - (A_DT + 2 * SSD_HEADS)), w.dtype)
    return jnp.concatenate([z, xs, cb, cc, ch, bc, dt, pad, qkv], axis=1).astype(BF16)


def _prep_router_w(rw):
    r1 = rw.astype(BF16)
    r2 = (rw - r1.astype(F32)).astype(BF16)
    r3 = (rw - r1.astype(F32) - r2.astype(F32)).astype(BF16)
    pad = jnp.zeros((rw.shape[0], LANES - 3 * N_EXPERTS), BF16)
    return jnp.concatenate([r1, r2, r3, pad], axis=1)


def _trunk(x_parts, seq, p):
    n = sum(a.shape[0] for a in x_parts)
    n_first = x_parts[0].shape[0]
    for l in range(DEPTH):
        proj_a, proj_b = _in_proj(x_parts, _prep_w_in(p['w_in'][l]))
        y_conv = _conv_mix(proj_a, p['conv_w'][l], p['conv_norm_g'][l], seq)
        cf, cbw = _ssd_consts(p['ssd_conv_w'][l], p['ssd_conv_b'][l], p['ssd_dt_bias'][l], p['ssd_a_log'][l],
                              p['ssd_d'][l], p['ssd_norm_g'][l])
        y_ssd = _ssd_pass(proj_a, _ssd_pass(proj_a, None, cf, seq, 0), cbw, seq, 1)
        y_na = _neighbourhood_attention(proj_b, p['na_rel_bias'][l], seq)
        w_out = p['w_out'][l].astype(BF16)
        if l % 2 == 0:
            x = _out_proj_ln(y_conv, y_ssd, y_na, x_parts, w_out, p['na_norm_g'][l], p['ln_mix_g'][l],
                             p['ln_mix_b'][l])[0]
            padf = ((0, 0), (0, D_FF_PAD - D_FF))
            wg = jnp.pad(p['ffn_w_gate'][l // 2], padf).astype(BF16)[None]
            wu = jnp.pad(p['ffn_w_up'][l // 2], padf).astype(BF16)[None]
            wd = jnp.pad(p['ffn_w_down'][l // 2], ((0, D_FF_PAD - D_FF), (0, 0))).astype(BF16)[None]
            nb = n // MOE_TM
            x = _ffn_blocks(x, jnp.zeros((nb,), jnp.int32), jnp.full((nb,), MOE_TM, jnp.int32), wg, wu, wd, MOE_TM,
                            residual_ln=(p['ln_ffn_g'][l], p['ln_ffn_b'][l]))
            outs_split = (x[:n_first], x[n_first:])
        else:
            x, idx, gates = _out_proj_ln(y_conv, y_ssd, y_na, x_parts, w_out, p['na_norm_g'][l], p['ln_mix_g'][l],
                                         p['ln_mix_b'][l], _prep_router_w(p['router_w'][l // 2]))
            p_len, block_e, block_fill, d0, d1 = _routing_tables(idx, n, MOE_TM)
            xs = _dispatch_rows(x, d0, d1, p_len)
            outs = _ffn_blocks(xs, block_e, block_fill, p['moe_w_gate'][l // 2], p['moe_w_up'][l // 2],
                               p['moe_w_down'][l // 2], MOE_TM, ragged=True)
            outs_split = _combine_ln(x, outs, d0, d1, gates, p['ln_ffn_g'][l], p['ln_ffn_b'][l], n_first)
            if l + 1 < DEPTH:
                x = jnp.concatenate(outs_split, axis=0)
        x_parts = [x]
    return outs_split


def kernel(x_prompt, x_sample, w_in, conv_w, conv_norm_g, ssd_conv_w, ssd_conv_b, ssd_dt_bias, ssd_a_log, ssd_d,
           ssd_norm_g, na_rel_bias, na_norm_g, w_out, ln_mix_g, ln_mix_b, ln_ffn_g, ln_ffn_b, ffn_w_gate, ffn_w_up,
           ffn_w_down, router_w, moe_w_gate, moe_w_up, moe_w_down):
    p = dict(w_in=w_in, conv_w=conv_w, conv_norm_g=conv_norm_g, ssd_conv_w=ssd_conv_w, ssd_conv_b=ssd_conv_b,
             ssd_dt_bias=ssd_dt_bias, ssd_a_log=ssd_a_log, ssd_d=ssd_d, ssd_norm_g=ssd_norm_g,
             na_rel_bias=na_rel_bias, na_norm_g=na_norm_g, w_out=w_out, ln_mix_g=ln_mix_g, ln_mix_b=ln_mix_b,
             ln_ffn_g=ln_ffn_g, ln_ffn_b=ln_ffn_b, ffn_w_gate=ffn_w_gate, ffn_w_up=ffn_w_up, ffn_w_down=ffn_w_down,
             router_w=router_w, moe_w_gate=moe_w_gate, moe_w_up=moe_w_up, moe_w_down=moe_w_down)
    bp, seq, d = x_prompt.shape
    bs, seq_s, _ = x_sample.shape
    assert seq == seq_s, "both request groups must share one sequence length"
    yp, ys = _trunk([x_prompt.reshape(bp * seq, d), x_sample.reshape(bs * seq, d)], seq, p)
    return yp.reshape(bp, seq, d), ys.reshape(bs, seq, d)
```

```python
import functools

import numpy as np
import jax
import jax.numpy as jnp
from jax import lax
from jax.experimental import pallas as pl
from jax.experimental.pallas import tpu as pltpu

F32 = jnp.float32
BF16 = jnp.bfloat16

D_MODEL = 2048
DEPTH = 2
GRID_W = 64
CONV_WIDTH = 512
SSD_HEADS = 16
SSD_HEAD_DIM = 64
SSD_WIDTH = 1024
SSD_GROUPS = 2
SSD_STATE = 64
SSD_CHUNK = 128
NA_HEADS = 8
NA_HEAD_DIM = 64
NA_WIDTH = 512
WIN_H = 8
WIN_W = 16
D_FF = 5504
N_EXPERTS = 8
TOP_K = 2
D_FF_EXPERT = 7168
ALPHA = (2 * DEPTH) ** 0.25
LN_EPS = 1e-5
RMS_EPS = 1e-6

LANES = 128
SUBLANES = 8
NEG_BIG = -1e30

A_Z, A_XS, A_CB, A_CC, A_CH, A_BC, A_DT, A_WIDTH = 0, 1024, 2048, 2560, 3072, 3584, 3840, 4096

VMEM_LIMIT = 56 * 1024 * 1024
IN_PROJ_TM, IN_PROJ_TN = 1024, 512
ROW_TILE = 512
CONV_TT = 1024
SSD_CHUNKS_PER_STEP = 4
NA_QROWS = 4
NA_BLOCKS_PER_STEP = 8
OUTPROJ_ROW_SPLITS = 2
MOE_TM = 1024
FF_TILE = 512
D_FF_PAD = -(-D_FF // FF_TILE) * FF_TILE
FFN_ROW_GROUP = 256
GATHER_UNROLL = 8


def _cparams(sem, vmem=VMEM_LIMIT):
    return pltpu.CompilerParams(dimension_semantics=sem, vmem_limit_bytes=vmem)


def _sigmoid(x):
    return 1.0 / (1.0 + jnp.exp(-x))


def _split3(x):
    x1 = x.astype(BF16)
    r1 = x - x1.astype(F32)
    x2 = r1.astype(BF16)
    r2 = r1 - x2.astype(F32)
    return x1, x2, r2.astype(BF16)


def _dot(a, b):
    return jnp.dot(a, b, preferred_element_type=F32)


def _dot_exact_rhs(m, x):
    x1, x2, x3 = _split3(x)
    return (_dot(m, x3) + _dot(m, x2)) + _dot(m, x1)


def _dot_split2_lhs(x, m):
    x1, x2, _ = _split3(x)
    return _dot(x2, m) + _dot(x1, m)


def _log1p(e):
    u = 1.0 + e
    d = u - 1.0
    return jnp.where(d == 0.0, e, jnp.log(u) * (e / jnp.where(d == 0.0, 1.0, d)))


def _layer_norm(v, g, b):
    mu = jnp.mean(v, axis=-1, keepdims=True)
    c = v - mu
    var = jnp.mean(c * c, axis=-1, keepdims=True)
    return c * lax.rsqrt(var + LN_EPS) * g + b


def _row_parts_specs(parts, tm, width, grid_rank):
    if len(parts) == 1:
        return [pl.BlockSpec((tm, width), (lambda i: (i, 0)) if grid_rank == 1 else (lambda i, j: (i, 0)))]
    first_tiles = parts[0].shape[0] // tm
    if grid_rank == 1:
        return [pl.BlockSpec((tm, width), lambda i: (jnp.minimum(i, first_tiles - 1), 0)),
                pl.BlockSpec((tm, width), lambda i: (jnp.maximum(i - first_tiles, 0), 0))]
    return [pl.BlockSpec((tm, width), lambda i, j: (jnp.minimum(i, first_tiles - 1), 0)),
            pl.BlockSpec((tm, width), lambda i, j: (jnp.maximum(i - first_tiles, 0), 0))]


def _read_row_parts(x_refs, first_tiles, rows=slice(None)):
    x = x_refs[0][rows, :]
    if len(x_refs) == 2:
        x = jnp.where(pl.program_id(0) < first_tiles, x, x_refs[1][rows, :])
    return x


def _inproj_kernel(*refs, n_parts, first_tiles, a_blocks):
    x_refs = refs[:n_parts]
    w_ref, oa_ref, ob_ref, xb_ref = refs[n_parts:]
    j = pl.program_id(1)

    @pl.when(j == 0)
    def _():
        xb_ref[...] = _read_row_parts(x_refs, first_tiles).astype(BF16)

    res = _dot(xb_ref[...], w_ref[...])

    @pl.when(j < a_blocks)
    def _():
        oa_ref[...] = res

    @pl.when(j >= a_blocks)
    def _():
        ob_ref[...] = res.astype(ob_ref.dtype)


def _in_proj(x_parts, w, tm=IN_PROJ_TM, tn=IN_PROJ_TN):
    n = sum(a.shape[0] for a in x_parts)
    k = x_parts[0].shape[1]
    a_blocks = A_WIDTH // tn
    b_width = w.shape[1] - A_WIDTH
    first_tiles = x_parts[0].shape[0] // tm
    return pl.pallas_call(
        functools.partial(_inproj_kernel, n_parts=len(x_parts), first_tiles=first_tiles, a_blocks=a_blocks),
        grid=(n // tm, w.shape[1] // tn),
        in_specs=_row_parts_specs(x_parts, tm, k, 2) + [pl.BlockSpec((k, tn), lambda i, j: (0, j))],
        out_specs=[pl.BlockSpec((tm, tn), lambda i, j: (i, jnp.minimum(j, a_blocks - 1))),
                   pl.BlockSpec((tm, tn), lambda i, j: (i, jnp.maximum(j - a_blocks, 0)))],
        out_shape=[jax.ShapeDtypeStruct((n, A_WIDTH), F32), jax.ShapeDtypeStruct((n, b_width), BF16)],
        scratch_shapes=[pltpu.VMEM((tm, k), BF16)],
        compiler_params=_cparams(("parallel", "arbitrary")),
        name="in_proj",
    )(*x_parts, w)


def _shift_rows(x, prev_row, next_row, first, last):
    n = x.shape[0]
    row = lax.broadcasted_iota(jnp.int32, (n, 1), 0)
    prev_row = jnp.where(first, 0.0, prev_row)
    next_row = jnp.where(last, 0.0, next_row)
    x_prev = jnp.where(row == 0, prev_row, pltpu.roll(x, 1, 0))
    x_next = jnp.where(row == n - 1, next_row, pltpu.roll(x, n - 1, 0))
    return x_prev, x_next


def _convmix_kernel(cb_ref, cc_ref, ch_ref, ccp_ref, chp_ref, ccn_ref, chn_ref, w_ref, g_ref, o_ref, *, tiles_per_seq):
    i = pl.program_id(0)
    first = (i % tiles_per_seq) == 0
    last = (i % tiles_per_seq) == tiles_per_seq - 1
    u = cc_ref[...] * ch_ref[...]
    up = ccp_ref[SUBLANES - 1:SUBLANES, :] * chp_ref[SUBLANES - 1:SUBLANES, :]
    un = ccn_ref[0:1, :] * chn_ref[0:1, :]
    u_prev, u_next = _shift_rows(u, up, un, first, last)
    conv = u_prev * w_ref[0:1, :] + u * w_ref[1:2, :] + u_next * w_ref[2:3, :]
    y = cb_ref[...] * conv
    y = y * lax.rsqrt(jnp.mean(y * y, axis=-1, keepdims=True) + RMS_EPS)
    o_ref[...] = (y * g_ref[...]).astype(o_ref.dtype)


def _halo_specs(tt, width, col0, n_rows):
    cb = col0 // width
    r = tt // SUBLANES
    last_blk = n_rows // SUBLANES - 1
    prev = pl.BlockSpec((SUBLANES, width), lambda i: (jnp.maximum(i * r - 1, 0), cb))
    nxt = pl.BlockSpec((SUBLANES, width), lambda i: (jnp.minimum((i + 1) * r, last_blk), cb))
    return prev, nxt


def _conv_mix(proj_a, conv_w, norm_g, seq, tt=CONV_TT):
    n = proj_a.shape[0]
    w = CONV_WIDTH
    main = lambda c0: pl.BlockSpec((tt, w), lambda i: (i, c0 // w))
    ccp, ccn = _halo_specs(tt, w, A_CC, n)
    chp, chn = _halo_specs(tt, w, A_CH, n)
    full = lambda shape: pl.BlockSpec(shape, lambda i: (0, 0))
    return pl.pallas_call(
        functools.partial(_convmix_kernel, tiles_per_seq=seq // tt),
        grid=(n // tt,),
        in_specs=[main(A_CB), main(A_CC), main(A_CH), ccp, chp, ccn, chn, full((3, w)), full((1, w))],
        out_specs=pl.BlockSpec((tt, w), lambda i: (i, 0)),
        out_shape=jax.ShapeDtypeStruct((n, w), BF16),
        compiler_params=_cparams(("parallel",)),
        name="conv_mix",
    )(proj_a, proj_a, proj_a, proj_a, proj_a, proj_a, proj_a, conv_w, norm_g.reshape(1, w))


def _ssd_kernel(*refs, direction, final, tiles):
    if final:
        (xs_s, bc_s, dt_ref, z_ref, yf_ref, dtb_ref, alog_ref, e_ref, hm_ref, dsk_ref, ng_ref,
         o_ref, h_ref, y_ref) = refs
    else:
        (xs_ref, xsp_ref, xsn_ref, bc_ref, bcp_ref, bcn_ref, dt_ref,
         cwx_ref, cbx_ref, cwb_ref, cbb_ref, dtb_ref, alog_ref, e_ref, hm_ref,
         o_ref, xs_s, bc_s, h_ref) = refs
        y_ref = o_ref
    L = SSD_CHUNK
    c = pl.program_id(1)
    tile = c if direction == 0 else tiles - 1 - c
    first = tile == 0
    last = tile == tiles - 1

    @pl.when(c == 0)
    def _():
        h_ref[...] = jnp.zeros_like(h_ref)

    def conv_silu(x_ref, p_ref, n_ref, w_ref, b_ref):
        x = x_ref[...]
        x_prev, x_next = _shift_rows(x, p_ref[SUBLANES - 1:SUBLANES, :], n_ref[0:1, :], first, last)
        v = x_prev * w_ref[0:1, :] + x * w_ref[1:2, :] + x_next * w_ref[2:3, :] + b_ref[...]
        return v * _sigmoid(v)

    if not final:
        xs_s[...] = conv_silu(xs_ref, xsp_ref, xsn_ref, cwx_ref, cbx_ref)
        bc_s[...] = conv_silu(bc_ref, bcp_ref, bcn_ref, cwb_ref, cbb_ref)

    li = lax.broadcasted_iota(jnp.int32, (L, L), 0)
    si = lax.broadcasted_iota(jnp.int32, (L, L), 1)
    valid = (si <= li) if direction == 0 else (si >= li)
    tri = jnp.where(valid, 1.0, 0.0).astype(BF16)
    lane = lax.broadcasted_iota(jnp.int32, (1, LANES), 1)
    nt = (((1,), (1,)), ((), ()))
    e_mat = e_ref[...]
    neg_a = -jnp.exp(alog_ref[...])
    heads_per_group = SSD_HEADS // SSD_GROUPS
    chunks = xs_s.shape[0] // L
    order = range(chunks) if direction == 0 else range(chunks - 1, -1, -1)

    for k in order:
        rows = slice(k * L, (k + 1) * L)
        xs = xs_s[rows, :]
        bm = bc_s[rows, :LANES]
        cm = bc_s[rows, LANES:]

        dtr = dt_ref[rows, :] + dtb_ref[...]
        dt = jnp.maximum(dtr, 0.0) + _log1p(jnp.exp(-jnp.abs(dtr)))
        acum = _dot_exact_rhs(tri, dt * neg_a)
        acum_t = acum.T
        total = acum[L - 1:L, :] if direction == 0 else acum[0:1, :]
        dec_end = jnp.exp(total - acum)
        ea = jnp.exp(acum)

        dtx = _dot_split2_lhs(dt, e_mat)
        eax = _dot_split2_lhs(ea, e_mat)
        dex = _dot_split2_lhs(dec_end * dt, e_mat)

        xdt = xs * dtx
        xw = (xs * dex).astype(BF16)
        cm_b = cm.astype(BF16)
        bm_b = bm.astype(BF16)

        h_prev = h_ref[...]
        y_off = _dot(cm_b, h_prev.astype(BF16)) * eax

        for g in range(SSD_GROUPS):
            cg = jnp.where((lane // SSD_STATE) == g, cm, 0.0).astype(BF16)
            cb = lax.dot_general(cg, bm_b, nt, preferred_element_type=F32)
            for pair in range(heads_per_group // 2):
                acc = None
                for half in range(2):
                    h = g * heads_per_group + pair * 2 + half
                    hl = direction * SSD_HEADS + h
                    seg = acum[:, hl:hl + 1] - acum_t[hl:hl + 1, :]
                    w = (jnp.where(valid, jnp.exp(seg), 0.0) * cb).astype(BF16)
                    c0 = (h // 2) * LANES
                    rhs = jnp.where((lane // SSD_HEAD_DIM) == half, xdt[:, c0:c0 + LANES], 0.0).astype(BF16)
                    part = _dot(w, rhs)
                    acc = part if acc is None else acc + part
                c0 = (g * heads_per_group // 2 + pair) * LANES
                y_pair = acc + y_off[:, c0:c0 + LANES]
                if final:
                    y_pair = yf_ref[rows, c0:c0 + LANES] + y_pair + dsk_ref[:, c0:c0 + LANES] * xs[:, c0:c0 + LANES]
                    zz = z_ref[rows, c0:c0 + LANES]
                    y_pair = y_pair * (zz * _sigmoid(zz))
                y_ref[rows, c0:c0 + LANES] = y_pair

        cd = eax[L - 1:L, :] if direction == 0 else eax[0:1, :]
        s_new = _dot(bm.T.astype(BF16), xw)
        h_ref[...] = (cd * h_prev + s_new) * hm_ref[...]

    if final:
        gw = SSD_WIDTH // SSD_GROUPS
        for g in range(SSD_GROUPS):
            y = y_ref[:, g * gw:(g + 1) * gw]
            y = y * lax.rsqrt(jnp.mean(y * y, axis=-1, keepdims=True) + RMS_EPS)
            o_ref[:, g * gw:(g + 1) * gw] = (y * ng_ref[:, g * gw:(g + 1) * gw]).astype(o_ref.dtype)


def _ssd_pass(proj_a, fwd_outs, params, seq, direction):
    n = proj_a.shape[0]
    T = SSD_CHUNK * SSD_CHUNKS_PER_STEP
    tiles = seq // T
    nseq = n // seq
    final = direction == 1
    cw, cb_, dtb, alog, e_mat, hmask, dsk, ng = params
    bcw = 2 * SSD_GROUPS * SSD_STATE

    def row_blk(b, c):
        t = c if direction == 0 else tiles - 1 - c
        return b * tiles + t

    def main(width, col0):
        return pl.BlockSpec((T, width), lambda b, c: (row_blk(b, c), col0 // width))

    r = T // SUBLANES
    last_blk = n // SUBLANES - 1

    def halo(width, col0):
        prev = pl.BlockSpec((SUBLANES, width), lambda b, c: (jnp.maximum(row_blk(b, c) * r - 1, 0), col0 // width))
        nxt = pl.BlockSpec((SUBLANES, width), lambda b, c: (jnp.minimum((row_blk(b, c) + 1) * r, last_blk), col0 // width))
        return prev, nxt

    xsp, xsn = halo(SSD_WIDTH, A_XS)
    bcp, bcn = halo(bcw, A_BC)
    full = lambda a: pl.BlockSpec(a.shape, lambda b, c: (0,) * a.ndim)

    rows = lambda width: pl.BlockSpec((T, width), lambda b, c: (row_blk(b, c), 0))
    if final:
        y_fwd, xs_act, bc_act = fwd_outs
        in_specs = [rows(SSD_WIDTH), rows(bcw), main(LANES, A_DT), main(SSD_WIDTH, A_Z), rows(SSD_WIDTH)]
        args = [xs_act, bc_act, proj_a, proj_a, y_fwd]
        consts = [dtb, alog, e_mat, hmask, dsk, ng]
        out_specs = rows(SSD_WIDTH)
        out_shape = jax.ShapeDtypeStruct((n, SSD_WIDTH), BF16)
    else:
        in_specs = [main(SSD_WIDTH, A_XS), xsp, xsn, main(bcw, A_BC), bcp, bcn, main(LANES, A_DT)]
        args = [proj_a] * 7
        consts = [cw[:, :SSD_WIDTH], cb_[:, :SSD_WIDTH], cw[:, SSD_WIDTH:], cb_[:, SSD_WIDTH:], dtb, alog, e_mat, hmask]
        out_specs = [rows(SSD_WIDTH), rows(SSD_WIDTH), rows(bcw)]
        out_shape = [jax.ShapeDtypeStruct((n, SSD_WIDTH), F32), jax.ShapeDtypeStruct((n, SSD_WIDTH), F32),
                     jax.ShapeDtypeStruct((n, bcw), F32)]
    in_specs += [full(a) for a in consts]
    args += consts
    scratch = [pltpu.VMEM((LANES, SSD_WIDTH), F32)]
    if final:
        scratch.append(pltpu.VMEM((T, SSD_WIDTH), F32))
    return pl.pallas_call(
        functools.partial(_ssd_kernel, direction=direction, final=final, tiles=tiles),
        grid=(nseq, tiles),
        in_specs=in_specs,
        out_specs=out_specs,
        out_shape=out_shape,
        scratch_shapes=scratch,
        compiler_params=_cparams(("parallel", "arbitrary")),
        name="ssd_bwd" if final else "ssd_fwd",
    )(*args)


def _ssd_consts(ssd_conv_w, ssd_conv_b, ssd_dt_bias, ssd_a_log, ssd_d, ssd_norm_g):
    pad = LANES - 2 * SSD_HEADS
    dtb = jnp.pad(ssd_dt_bias.reshape(1, -1), ((0, 0), (0, pad)))
    alog = jnp.pad(ssd_a_log.reshape(1, -1), ((0, 0), (0, pad)))
    col_head = np.arange(SSD_WIDTH) // SSD_HEAD_DIM
    e_mats = [jnp.asarray((np.arange(LANES)[:, None] == d * SSD_HEADS + col_head[None, :]), BF16) for d in range(2)]
    row_group = np.arange(LANES) // SSD_STATE
    col_group = col_head // (SSD_HEADS // SSD_GROUPS)
    hmask = jnp.asarray(row_group[:, None] == col_group[None, :], F32)
    dsk = jnp.repeat(ssd_d, SSD_HEAD_DIM).reshape(1, SSD_WIDTH)
    ng = ssd_norm_g.reshape(1, SSD_WIDTH)
    cb_ = ssd_conv_b.reshape(1, -1)
    return [(ssd_conv_w, cb_, dtb, alog, e_mats[d], hmask, dsk, ng) for d in range(2)]


NA_KROWS = NA_QROWS + WIN_H


def _na_kernel(kb_ref, cls_ref, q_ref, k_ref, v_ref, bias_ref, o_ref, *, blocks_per_step):
    step = pl.program_id(2)
    lane = lax.broadcasted_iota(jnp.int32, (1, LANES), 1)
    nt = (((1,), (1,)), ((), ()))
    nq = NA_QROWS * GRID_W
    nk = NA_KROWS * GRID_W
    for j in range(blocks_per_step):
        blk = step * blocks_per_step + j
        k0 = pl.multiple_of(kb_ref[blk] * GRID_W, GRID_W)
        cls = cls_ref[blk]
        q = q_ref[j * nq:(j + 1) * nq, :] * (NA_HEAD_DIM ** -0.5)
        kk = k_ref[pl.ds(k0, nk), :]
        vv = v_ref[pl.ds(k0, nk), :]
        outs = []
        for hh in range(2):
            qm = jnp.where((lane // NA_HEAD_DIM) == hh, q, jnp.zeros_like(q))
            s = lax.dot_general(qm, kk, nt, preferred_element_type=F32) + bias_ref[cls, hh]
            m = jnp.max(s, axis=-1, keepdims=True)
            p = jnp.exp(s - m)
            l = jnp.sum(p, axis=-1, keepdims=True)
            outs.append(_dot(p.astype(BF16), vv) / l)
        o = jnp.where((lane // NA_HEAD_DIM) == 0, outs[0], outs[1])
        o_ref[j * nq:(j + 1) * nq, :] = o.astype(o_ref.dtype)


def _na_tables(rel_bias, rows):
    qc = np.arange(GRID_W)
    cs = np.clip(qc - WIN_W // 2, 0, GRID_W - WIN_W)
    kc = np.arange(GRID_W)
    col_valid = (kc[None, :] >= cs[:, None]) & (kc[None, :] < cs[:, None] + WIN_W)
    dj = np.clip(kc[None, :] - qc[:, None] + WIN_W - 1, 0, 2 * WIN_W - 2)
    nblk = rows // NA_QROWS
    kb = np.clip(np.arange(nblk) * NA_QROWS - WIN_H // 2, 0, rows - NA_KROWS)
    patterns, cls = [], []
    for b in range(nblk):
        r = b * NA_QROWS + np.arange(NA_QROWS)
        rs = np.clip(r - WIN_H // 2, 0, rows - WIN_H)
        key = (tuple(rs - kb[b]), int(kb[b] - b * NA_QROWS))
        if key not in patterns:
            patterns.append(key)
        cls.append(patterns.index(key))
    nh, nw = 2 * WIN_H - 1, 2 * WIN_W - 1
    bias_ext = jnp.pad(rel_bias.astype(F32), ((0, 0), (0, 1), (0, 1)), constant_values=NEG_BIG)
    di_sel = []
    for off, shift in patterns:
        kr = np.arange(NA_KROWS)[None, :]
        i = np.arange(NA_QROWS)[:, None]
        rel = kr - np.asarray(off)[:, None]
        row_valid = (rel >= 0) & (rel < WIN_H)
        di = np.where(row_valid, np.clip(shift + kr - i + WIN_H - 1, 0, nh - 1), nh)
        di_sel.append(np.eye(nh + 1, dtype=np.float32)[di])
    dj_sel = np.eye(nw + 1, dtype=np.float32)[np.where(col_valid, dj, nw)]
    tab = jnp.einsum('pikd,hde,qwe->phiqkw', np.stack(di_sel), bias_ext, dj_sel, precision=lax.Precision.HIGHEST)
    tab = tab.reshape(len(patterns), NA_HEADS, NA_QROWS * GRID_W, NA_KROWS * GRID_W)
    return jnp.asarray(kb, jnp.int32), jnp.asarray(cls, jnp.int32), tab


def _neighbourhood_attention(proj_b, rel_bias, seq, blocks_per_step=NA_BLOCKS_PER_STEP):
    n = proj_b.shape[0]
    nseq = n // seq
    rows = seq // GRID_W
    kb, cls, bias_tab = _na_tables(rel_bias, rows)
    tq = blocks_per_step * NA_QROWS * GRID_W
    pairs = NA_HEADS // 2
    steps = seq // tq
    ncls = bias_tab.shape[0]
    grid_spec = pltpu.PrefetchScalarGridSpec(
        num_scalar_prefetch=2,
        grid=(nseq, pairs, steps),
        in_specs=[pl.BlockSpec((tq, LANES), lambda b, p, r, kb_, cls_: (b * steps + r, p)),
                  pl.BlockSpec((seq, LANES), lambda b, p, r, kb_, cls_: (b, pairs + p)),
                  pl.BlockSpec((seq, LANES), lambda b, p, r, kb_, cls_: (b, 2 * pairs + p)),
                  pl.BlockSpec((ncls, 2) + bias_tab.shape[2:], lambda b, p, r, kb_, cls_: (0, p, 0, 0))],
        out_specs=pl.BlockSpec((tq, LANES), lambda b, p, r, kb_, cls_: (b * steps + r, p)),
    )
    return pl.pallas_call(
        functools.partial(_na_kernel, blocks_per_step=blocks_per_step),
        grid_spec=grid_spec,
        out_shape=jax.ShapeDtypeStruct((n, NA_WIDTH), F32),
        compiler_params=_cparams(("parallel", "arbitrary", "arbitrary")),
        name="nbr_attn",
    )(kb, cls, proj_b, proj_b, proj_b, bias_tab)


def _rms_rows(y, g):
    return y * lax.rsqrt(jnp.mean(y * y, axis=-1, keepdims=True) + RMS_EPS) * g


def _outproj_kernel(*refs, router, n_parts, first_tiles):
    yc_ref, ys_ref, yn_ref = refs[:3]
    x_refs = refs[3:3 + n_parts]
    rest = refs[3 + n_parts:]
    if router:
        w_ref, nag_ref, g_ref, b_ref, rw_ref, o_ref, idx_ref, gate_ref = rest
    else:
        w_ref, nag_ref, g_ref, b_ref, o_ref = rest
    c1 = CONV_WIDTH
    c2 = CONV_WIDTH + SSD_WIDTH
    rows = o_ref.shape[0] // OUTPROJ_ROW_SPLITS
    for part in range(OUTPROJ_ROW_SPLITS):
        sl = slice(part * rows, (part + 1) * rows)
        yn = _rms_rows(yn_ref[sl, :], nag_ref[...]).astype(BF16)
        mix = _dot(yc_ref[sl, :], w_ref[0:c1, :]) + _dot(ys_ref[sl, :], w_ref[c1:c2, :]) + _dot(yn, w_ref[c2:, :])
        out = _layer_norm(ALPHA * _read_row_parts(x_refs, first_tiles, sl) + mix, g_ref[...], b_ref[...])
        o_ref[sl, :] = out
        if not router:
            continue
        o1, o2, _ = _split3(out)
        t1 = _dot(o1, rw_ref[...])
        t2 = _dot(o2, rw_ref[...])
        sh1 = LANES - N_EXPERTS
        sh2 = LANES - 2 * N_EXPERTS
        logits = ((pltpu.roll(t2, sh1, 1) + pltpu.roll(t1, sh2, 1)) + (t2 + pltpu.roll(t1, sh1, 1))) + t1
        lane = lax.broadcasted_iota(jnp.int32, (rows, LANES), 1)
        lg = jnp.where(lane < N_EXPERTS, logits, -jnp.inf)
        m1 = jnp.max(lg, axis=-1, keepdims=True)
        i1 = jnp.min(jnp.where(lg == m1, lane, LANES), axis=-1, keepdims=True)
        lg2 = jnp.where(lane == i1, -jnp.inf, lg)
        m2 = jnp.max(lg2, axis=-1, keepdims=True)
        i2 = jnp.min(jnp.where(lg2 == m2, lane, LANES), axis=-1, keepdims=True)
        e2 = jnp.exp(m2 - m1)
        den = 1.0 + e2
        idx_ref[sl, :] = jnp.where(lane == 0, i1, jnp.where(lane == 1, i2, 0))
        gate_ref[sl, :] = jnp.where(lane == 0, 1.0 / den, jnp.where(lane == 1, e2 / den, 0.0))


def _out_proj_ln(yc, ys, yn, x_parts, w_out, na_g, ln_g, ln_b, router_w=None, tm=ROW_TILE):
    n = yc.shape[0]
    router = router_w is not None
    row = lambda w: pl.BlockSpec((tm, w), lambda i: (i, 0))
    full = lambda a: pl.BlockSpec(a.shape, lambda i: (0,) * a.ndim)
    consts = [w_out, na_g.reshape(1, -1), ln_g.reshape(1, -1), ln_b.reshape(1, -1)]
    if router:
        consts.append(router_w)
    out_shape = [jax.ShapeDtypeStruct((n, D_MODEL), F32)]
    out_specs = [row(D_MODEL)]
    if router:
        out_shape += [jax.ShapeDtypeStruct((n, LANES), jnp.int32), jax.ShapeDtypeStruct((n, LANES), F32)]
        out_specs += [row(LANES), row(LANES)]
    res = pl.pallas_call(
        functools.partial(_outproj_kernel, router=router, n_parts=len(x_parts),
                          first_tiles=x_parts[0].shape[0] // tm),
        grid=(n // tm,),
        in_specs=[row(CONV_WIDTH), row(SSD_WIDTH), row(NA_WIDTH)] + _row_parts_specs(x_parts, tm, D_MODEL, 1)
        + [full(a) for a in consts],
        out_specs=out_specs,
        out_shape=out_shape,
        compiler_params=_cparams(("parallel",)),
        name="out_proj_router" if router else "out_proj",
    )(yc, ys, yn, *x_parts, *consts)
    return res


def _ffn_kernel(be_ref, bv_ref, x_ref, wg_ref, wu_ref, wd_ref, *rest, packed, ln, ragged):
    if ln:
        g_ref, b_ref, o_ref = rest
    else:
        (o_ref,) = rest
    i = pl.program_id(0)
    j = pl.program_id(1)

    @pl.when(j == 0)
    def _():
        o_ref[...] = jnp.zeros_like(o_ref)

    def accumulate(rows):
        if packed:
            lo, hi = _unpack_bf16_pairs(x_ref[rows, :])
            c = lo.shape[1]
            g = _dot(lo, wg_ref[0, :c, :].astype(BF16)) + _dot(hi, wg_ref[0, c:, :].astype(BF16))
            u = _dot(lo, wu_ref[0, :c, :].astype(BF16)) + _dot(hi, wu_ref[0, c:, :].astype(BF16))
        else:
            x = x_ref[rows, :].astype(BF16)
            g = _dot(x, wg_ref[0].astype(BF16))
            u = _dot(x, wu_ref[0].astype(BF16))
        h = ((g * _sigmoid(g)) * u).astype(BF16)
        o_ref[rows, :] += _dot(h, wd_ref[0].astype(BF16))

    tm = o_ref.shape[0]
    filled = bv_ref[i]

    @pl.when(filled == tm)
    def _():
        accumulate(slice(None))

    if ragged:
        for s in range(tm // FFN_ROW_GROUP):
            @pl.when(jnp.logical_and(filled < tm, filled > s * FFN_ROW_GROUP))
            def _():
                accumulate(slice(s * FFN_ROW_GROUP, (s + 1) * FFN_ROW_GROUP))

    if ln:
        @pl.when(j == pl.num_programs(1) - 1)
        def _():
            o_ref[...] = _layer_norm(ALPHA * x_ref[...] + o_ref[...], g_ref[...], b_ref[...])


def _ffn_blocks(x, block_e, block_fill, w_gate, w_up, w_down, tm, tf=FF_TILE, residual_ln=None, ragged=False):
    n, xw = x.shape
    packed = x.dtype == jnp.uint32
    d = w_gate.shape[1]
    f = w_gate.shape[2]
    nf = f // tf
    col = lambda i, j, be, bv: (be[i], 0, jnp.where(bv[i] != 0, j, nf - 1))
    rowj = lambda i, j, be, bv: (be[i], jnp.where(bv[i] != 0, j, nf - 1), 0)
    in_specs = [pl.BlockSpec((tm, xw), lambda i, j, be, bv: (i, 0)),
                pl.BlockSpec((1, d, tf), col), pl.BlockSpec((1, d, tf), col), pl.BlockSpec((1, tf, d), rowj)]
    args = [x, w_gate, w_up, w_down]
    if residual_ln is not None:
        gain, bias = residual_ln
        assert not packed
        vec = pl.BlockSpec((1, d), lambda i, j, be, bv: (0, 0))
        in_specs += [vec, vec]
        args += [gain.reshape(1, d), bias.reshape(1, d)]
    grid_spec = pltpu.PrefetchScalarGridSpec(
        num_scalar_prefetch=2,
        grid=(n // tm, nf),
        in_specs=in_specs,
        out_specs=pl.BlockSpec((tm, d), lambda i, j, be, bv: (i, 0)),
    )
    return pl.pallas_call(
        functools.partial(_ffn_kernel, packed=packed, ln=residual_ln is not None, ragged=ragged),
        grid_spec=grid_spec,
        out_shape=jax.ShapeDtypeStruct((n, d), F32),
        compiler_params=_cparams(("parallel", "arbitrary")),
        name="swiglu_blocks",
    )(block_e, block_fill, *args)


def _combine_ln_kernel(i0_ref, i1_ref, n0_ref, n1_ref, src_ref, x_ref, gate_ref, g_ref, b_ref, oa_ref, ob_ref,
                       b0_ref, b1_ref, sem, *, tm, first_tiles):
    i = pl.program_id(0)
    slot = i % 2

    def start(idx0_ref, idx1_ref, s):
        _start_row_gather(src_ref, idx0_ref, b0_ref.at[s], sem.at[s, 0], tm)
        _start_row_gather(src_ref, idx1_ref, b1_ref.at[s], sem.at[s, 1], tm)

    @pl.when(i == 0)
    def _():
        start(i0_ref, i1_ref, 0)

    @pl.when(i + 1 < pl.num_programs(0))
    def _():
        start(n0_ref, n1_ref, 1 - slot)

    _wait_row_gather(src_ref, b0_ref.at[slot], sem.at[slot, 0], tm)
    _wait_row_gather(src_ref, b1_ref.at[slot], sem.at[slot, 1], tm)
    f = b0_ref[slot] * gate_ref[:, 0:1] + b1_ref[slot] * gate_ref[:, 1:2]
    res = _layer_norm(ALPHA * x_ref[...] + f, g_ref[...], b_ref[...])

    @pl.when(i < first_tiles)
    def _():
        oa_ref[...] = res

    @pl.when(i >= first_tiles)
    def _():
        ob_ref[...] = res


def _combine_ln(x, outs, d0, d1, gates, g, b, n_first, tm=ROW_TILE):
    n, d = x.shape
    nt = n // tm
    first_tiles = n_first // tm
    row = pl.BlockSpec((tm, d), lambda i: (i, 0))
    vec = pl.BlockSpec((1, d), lambda i: (0, 0))
    idx = pl.BlockSpec((1, 1, tm), lambda i: (i, 0, 0), memory_space=pltpu.SMEM)
    idx_next = pl.BlockSpec((1, 1, tm), lambda i: (jnp.minimum(i + 1, nt - 1), 0, 0), memory_space=pltpu.SMEM)
    d0, d1 = d0.reshape(nt, 1, tm), d1.reshape(nt, 1, tm)
    return pl.pallas_call(
        functools.partial(_combine_ln_kernel, tm=tm, first_tiles=first_tiles), grid=(nt,),
        in_specs=[idx, idx, idx_next, idx_next, pl.BlockSpec(memory_space=pl.ANY), row,
                  pl.BlockSpec((tm, LANES), lambda i: (i, 0)), vec, vec],
        out_specs=[pl.BlockSpec((tm, d), lambda i: (jnp.minimum(i, first_tiles - 1), 0)),
                   pl.BlockSpec((tm, d), lambda i: (jnp.maximum(i - first_tiles, 0), 0))],
        out_shape=[jax.ShapeDtypeStruct((n_first, d), F32), jax.ShapeDtypeStruct((n - n_first, d), F32)],
        scratch_shapes=[pltpu.VMEM((2, tm, d), F32), pltpu.VMEM((2, tm, d), F32), pltpu.SemaphoreType.DMA((2, 2))],
        compiler_params=_cparams(("arbitrary",)), name="combine_ln",
    )(d0, d1, d0, d1, outs, x, gates, g.reshape(1, d), b.reshape(1, d))


def _start_row_gather(src_ref, idx_ref, buf_ref, sem, tr):
    def issue(r, carry):
        pltpu.make_async_copy(src_ref.at[pl.ds(idx_ref[0, 0, r], 1), :], buf_ref.at[pl.ds(r, 1), :], sem).start()
        return carry

    lax.fori_loop(0, tr, issue, 0, unroll=GATHER_UNROLL)


def _wait_row_gather(src_ref, buf_ref, sem, tr):
    pltpu.make_async_copy(src_ref.at[pl.ds(0, tr), :], buf_ref, sem).wait()


HI16 = 0xFFFF0000


def _pack_bf16_pairs(x):
    c = x.shape[1] // 2
    lo = lax.bitcast_convert_type(x[:, :c].astype(BF16).astype(F32), jnp.uint32)
    hi = lax.bitcast_convert_type(x[:, c:].astype(BF16).astype(F32), jnp.uint32)
    return (hi & jnp.uint32(HI16)) | (lo >> 16)


def _unpack_bf16_pairs(u):
    lo = lax.bitcast_convert_type(u << 16, F32).astype(BF16)
    hi = lax.bitcast_convert_type(u & jnp.uint32(HI16), F32).astype(BF16)
    return lo, hi


def _dispatch_kernel(d0_ref, d1_ref, x_ref, init_ref, xs_ref, pk_ref, sem, *, tm):
    del init_ref
    pk_ref[...] = _pack_bf16_pairs(x_ref[...])

    def issue(r, carry):
        row = pk_ref.at[pl.ds(r, 1), :]
        pltpu.make_async_copy(row, xs_ref.at[pl.ds(d0_ref[0, 0, r], 1), :], sem.at[0]).start()
        pltpu.make_async_copy(row, xs_ref.at[pl.ds(d1_ref[0, 0, r], 1), :], sem.at[1]).start()
        return carry

    lax.fori_loop(0, tm, issue, 0, unroll=GATHER_UNROLL)
    for k in range(TOP_K):
        pltpu.make_async_copy(pk_ref, xs_ref.at[pl.ds(0, tm), :], sem.at[k]).wait()


def _dispatch_rows(x, d0, d1, p_len, tm=ROW_TILE):
    n, d = x.shape
    idx = pl.BlockSpec((1, 1, tm), lambda i: (i, 0, 0), memory_space=pltpu.SMEM)
    return pl.pallas_call(
        functools.partial(_dispatch_kernel, tm=tm),
        grid=(n // tm,),
        in_specs=[idx, idx, pl.BlockSpec((tm, d), lambda i: (i, 0)), pl.BlockSpec(memory_space=pl.ANY)],
        out_specs=pl.BlockSpec(memory_space=pl.ANY),
        out_shape=jax.ShapeDtypeStruct((p_len, d // 2), jnp.uint32),
        scratch_shapes=[pltpu.VMEM((tm, d // 2), jnp.uint32), pltpu.SemaphoreType.DMA((TOP_K,))],
        input_output_aliases={3: 0},
        compiler_params=_cparams(("arbitrary",)),
        name="dispatch_rows",
    )(d0.reshape(n // tm, 1, tm), d1.reshape(n // tm, 1, tm), x, jnp.zeros((p_len, d // 2), jnp.uint32))


def _routing_tables(idx, n, tm):
    flat_e = idx[:, :TOP_K].reshape(-1)
    onehot = (flat_e[:, None] == jnp.arange(N_EXPERTS, dtype=jnp.int32)[None, :]).astype(jnp.int32)
    incl = jnp.cumsum(onehot, axis=0)
    rank = jnp.sum((incl - onehot) * onehot, axis=1)
    counts = incl[-1]
    padded = ((counts + tm - 1) // tm) * tm
    pend = jnp.cumsum(padded)
    pstart = pend - padded
    dest = pstart[flat_e] + rank
    n_blocks = -(-(n * TOP_K) // tm) + N_EXPERTS
    starts = jnp.arange(n_blocks, dtype=jnp.int32) * tm
    block_e = jnp.minimum(jnp.searchsorted(pend, starts, side='right'), N_EXPERTS - 1).astype(jnp.int32)
    block_fill = jnp.clip((pstart + counts)[block_e] - starts, 0, tm).astype(jnp.int32)
    block_fill = jnp.where(starts < pend[-1], block_fill, 0)
    last_e = block_e[jnp.maximum(jnp.sum((block_fill > 0).astype(jnp.int32)) - 1, 0)]
    block_e = jnp.where(block_fill > 0, block_e, last_e)
    dest2 = dest.reshape(n, TOP_K)
    return n_blocks * tm, block_e, block_fill, dest2[:, 0], dest2[:, 1]


def _prep_w_in(w):
    bc_w = 2 * SSD_GROUPS * SSD_STATE
    widths = (CONV_WIDTH, CONV_WIDTH, CONV_WIDTH, SSD_WIDTH, SSD_WIDTH, bc_w, 2 * SSD_HEADS, 3 * NA_WIDTH)
    cb, cc, ch, z, xs, bc, dt, qkv = jnp.split(w, [int(o) for o in np.cumsum(widths)[:-1]], axis=1)
    pad = jnp.zeros((D_MODEL, A_WIDTH - (A_DT + 2 * SSD_HEADS)), w.dtype)
    return jnp.concatenate([z, xs, cb, cc, ch, bc, dt, pad, qkv], axis=1).astype(BF16)


def _prep_router_w(rw):
    r1 = rw.astype(BF16)
    r2 = (rw - r1.astype(F32)).astype(BF16)
    r3 = (rw - r1.astype(F32) - r2.astype(F32)).astype(BF16)
    pad = jnp.zeros((rw.shape[0], LANES - 3 * N_EXPERTS), BF16)
    return jnp.concatenate([r1, r2, r3, pad], axis=1)


def _trunk(x_parts, seq, p):
    n = sum(a.shape[0] for a in x_parts)
    n_first = x_parts[0].shape[0]
    for l in range(DEPTH):
        proj_a, proj_b = _in_proj(x_parts, _prep_w_in(p['w_in'][l]))
        y_conv = _conv_mix(proj_a, p['conv_w'][l], p['conv_norm_g'][l], seq)
        cf, cbw = _ssd_consts(p['ssd_conv_w'][l], p['ssd_conv_b'][l], p['ssd_dt_bias'][l], p['ssd_a_log'][l],
                              p['ssd_d'][l], p['ssd_norm_g'][l])
        y_ssd = _ssd_pass(proj_a, _ssd_pass(proj_a, None, cf, seq, 0), cbw, seq, 1)
        y_na = _neighbourhood_attention(proj_b, p['na_rel_bias'][l], seq)
        w_out = p['w_out'][l].astype(BF16)
        if l % 2 == 0:
            x = _out_proj_ln(y_conv, y_ssd, y_na, x_parts, w_out, p['na_norm_g'][l], p['ln_mix_g'][l],
                             p['ln_mix_b'][l])[0]
            padf = ((0, 0), (0, D_FF_PAD - D_FF))
            wg = jnp.pad(p['ffn_w_gate'][l // 2], padf).astype(BF16)[None]
            wu = jnp.pad(p['ffn_w_up'][l // 2], padf).astype(BF16)[None]
            wd = jnp.pad(p['ffn_w_down'][l // 2], ((0, D_FF_PAD - D_FF), (0, 0))).astype(BF16)[None]
            nb = n // MOE_TM
            x = _ffn_blocks(x, jnp.zeros((nb,), jnp.int32), jnp.full((nb,), MOE_TM, jnp.int32), wg, wu, wd, MOE_TM,
                            residual_ln=(p['ln_ffn_g'][l], p['ln_ffn_b'][l]))
            outs_split = (x[:n_first], x[n_first:])
        else:
            x, idx, gates = _out_proj_ln(y_conv, y_ssd, y_na, x_parts, w_out, p['na_norm_g'][l], p['ln_mix_g'][l],
                                         p['ln_mix_b'][l], _prep_router_w(p['router_w'][l // 2]))
            p_len, block_e, block_fill, d0, d1 = _routing_tables(idx, n, MOE_TM)
            xs = _dispatch_rows(x, d0, d1, p_len)
            outs = _ffn_blocks(xs, block_e, block_fill, p['moe_w_gate'][l // 2], p['moe_w_up'][l // 2],
                               p['moe_w_down'][l // 2], MOE_TM, ragged=True)
            outs_split = _combine_ln(x, outs, d0, d1, gates, p['ln_ffn_g'][l], p['ln_ffn_b'][l], n_first)
            if l + 1 < DEPTH:
                x = jnp.concatenate(outs_split, axis=0)
        x_parts = [x]
    return outs_split


def kernel(x_prompt, x_sample, w_in, conv_w, conv_norm_g, ssd_conv_w, ssd_conv_b, ssd_dt_bias, ssd_a_log, ssd_d,
           ssd_norm_g, na_rel_bias, na_norm_g, w_out, ln_mix_g, ln_mix_b, ln_ffn_g, ln_ffn_b, ffn_w_gate, ffn_w_up,
           ffn_w_down, router_w, moe_w_gate, moe_w_up, moe_w_down):
    p = dict(w_in=w_in, conv_w=conv_w, conv_norm_g=conv_norm_g, ssd_conv_w=ssd_conv_w, ssd_conv_b=ssd_conv_b,
             ssd_dt_bias=ssd_dt_bias, ssd_a_log=ssd_a_log, ssd_d=ssd_d, ssd_norm_g=ssd_norm_g,
             na_rel_bias=na_rel_bias, na_norm_g=na_norm_g, w_out=w_out, ln_mix_g=ln_mix_g, ln_mix_b=ln_mix_b,
             ln_ffn_g=ln_ffn_g, ln_ffn_b=ln_ffn_b, ffn_w_gate=ffn_w_gate, ffn_w_up=ffn_w_up, ffn_w_down=ffn_w_down,
             router_w=router_w, moe_w_gate=moe_w_gate, moe_w_up=moe_w_up, moe_w_down=moe_w_down)
    bp, seq, d = x_prompt.shape
    bs, seq_s, _ = x_sample.shape
    assert seq == seq_s, "both request groups must share one sequence length"
    yp, ys = _trunk([x_prompt.reshape(bp * seq, d), x_sample.reshape(bs * seq, d)], seq, p)
    return yp.reshape(bp, seq, d), ys.reshape(bs, seq, d)
```

```python
import functools

import numpy as np
import jax
import jax.numpy as jnp
from jax import lax
from jax.experimental import pallas as pl
from jax.experimental.pallas import tpu as pltpu

F32 = jnp.float32
BF16 = jnp.bfloat16

D_MODEL = 2048
DEPTH = 2
GRID_W = 64
CONV_WIDTH = 512
SSD_HEADS = 16
SSD_HEAD_DIM = 64
SSD_WIDTH = 1024
SSD_GROUPS = 2
SSD_STATE = 64
SSD_CHUNK = 128
NA_HEADS = 8
NA_HEAD_DIM = 64
NA_WIDTH = 512
WIN_H = 8
WIN_W = 16
D_FF = 5504
N_EXPERTS = 8
TOP_K = 2
D_FF_EXPERT = 7168
ALPHA = (2 * DEPTH) ** 0.25
LN_EPS = 1e-5
RMS_EPS = 1e-6

LANES = 128
SUBLANES = 8
NEG_BIG = -1e30

A_Z, A_XS, A_CB, A_CC, A_CH, A_BC, A_DT, A_WIDTH = 0, 1024, 2048, 2560, 3072, 3584, 3840, 4096

VMEM_LIMIT = 56 * 1024 * 1024
IN_PROJ_TM, IN_PROJ_TN = 1024, 512
ROW_TILE = 512
CONV_TT = 1024
SSD_CHUNKS_PER_STEP = 8
NA_QROWS = 4
NA_BLOCKS_PER_STEP = 16
OUTPROJ_ROW_SPLITS = 2
MOE_TM = 1024
FF_TILE = 512
D_FF_PAD = -(-D_FF // FF_TILE) * FF_TILE
FFN_ROW_GROUP = 256
GATHER_UNROLL = 8


def _cparams(sem, vmem=VMEM_LIMIT):
    return pltpu.CompilerParams(dimension_semantics=sem, vmem_limit_bytes=vmem)


def _sigmoid(x):
    return 1.0 / (1.0 + jnp.exp(-x))


def _split3(x):
    x1 = x.astype(BF16)
    r1 = x - x1.astype(F32)
    x2 = r1.astype(BF16)
    r2 = r1 - x2.astype(F32)
    return x1, x2, r2.astype(BF16)


def _dot(a, b):
    return jnp.dot(a, b, preferred_element_type=F32)


def _dot_exact_rhs(m, x):
    x1, x2, x3 = _split3(x)
    return (_dot(m, x3) + _dot(m, x2)) + _dot(m, x1)


def _dot_split2_lhs(x, m):
    x1, x2, _ = _split3(x)
    return _dot(x2, m) + _dot(x1, m)


def _log1p(e):
    u = 1.0 + e
    d = u - 1.0
    return jnp.where(d == 0.0, e, jnp.log(u) * (e / jnp.where(d == 0.0, 1.0, d)))


def _layer_norm(v, g, b):
    mu = jnp.mean(v, axis=-1, keepdims=True)
    c = v - mu
    var = jnp.mean(c * c, axis=-1, keepdims=True)
    return c * lax.rsqrt(var + LN_EPS) * g + b


def _row_parts_specs(parts, tm, width, grid_rank):
    if len(parts) == 1:
        return [pl.BlockSpec((tm, width), (lambda i: (i, 0)) if grid_rank == 1 else (lambda i, j: (i, 0)))]
    first_tiles = parts[0].shape[0] // tm
    if grid_rank == 1:
        return [pl.BlockSpec((tm, width), lambda i: (jnp.minimum(i, first_tiles - 1), 0)),
                pl.BlockSpec((tm, width), lambda i: (jnp.maximum(i - first_tiles, 0), 0))]
    return [pl.BlockSpec((tm, width), lambda i, j: (jnp.minimum(i, first_tiles - 1), 0)),
            pl.BlockSpec((tm, width), lambda i, j: (jnp.maximum(i - first_tiles, 0), 0))]


def _read_row_parts(x_refs, first_tiles, rows=slice(None)):
    x = x_refs[0][rows, :]
    if len(x_refs) == 2:
        x = jnp.where(pl.program_id(0) < first_tiles, x, x_refs[1][rows, :])
    return x


def _inproj_kernel(*refs, n_parts, first_tiles, a_blocks):
    x_refs = refs[:n_parts]
    w_ref, oa_ref, ob_ref, xb_ref = refs[n_parts:]
    j = pl.program_id(1)

    @pl.when(j == 0)
    def _():
        xb_ref[...] = _read_row_parts(x_refs, first_tiles).astype(BF16)

    res = _dot(xb_ref[...], w_ref[...])

    @pl.when(j < a_blocks)
    def _():
        oa_ref[...] = res

    @pl.when(j >= a_blocks)
    def _():
        ob_ref[...] = res.astype(ob_ref.dtype)


def _in_proj(x_parts, w, tm=IN_PROJ_TM, tn=IN_PROJ_TN):
    n = sum(a.shape[0] for a in x_parts)
    k = x_parts[0].shape[1]
    a_blocks = A_WIDTH // tn
    b_width = w.shape[1] - A_WIDTH
    first_tiles = x_parts[0].shape[0] // tm
    return pl.pallas_call(
        functools.partial(_inproj_kernel, n_parts=len(x_parts), first_tiles=first_tiles, a_blocks=a_blocks),
        grid=(n // tm, w.shape[1] // tn),
        in_specs=_row_parts_specs(x_parts, tm, k, 2) + [pl.BlockSpec((k, tn), lambda i, j: (0, j))],
        out_specs=[pl.BlockSpec((tm, tn), lambda i, j: (i, jnp.minimum(j, a_blocks - 1))),
                   pl.BlockSpec((tm, tn), lambda i, j: (i, jnp.maximum(j - a_blocks, 0)))],
        out_shape=[jax.ShapeDtypeStruct((n, A_WIDTH), F32), jax.ShapeDtypeStruct((n, b_width), BF16)],
        scratch_shapes=[pltpu.VMEM((tm, k), BF16)],
        compiler_params=_cparams(("parallel", "arbitrary")),
        name="in_proj",
    )(*x_parts, w)


def _shift_rows(x, prev_row, next_row, first, last):
    n = x.shape[0]
    row = lax.broadcasted_iota(jnp.int32, (n, 1), 0)
    prev_row = jnp.where(first, 0.0, prev_row)
    next_row = jnp.where(last, 0.0, next_row)
    x_prev = jnp.where(row == 0, prev_row, pltpu.roll(x, 1, 0))
    x_next = jnp.where(row == n - 1, next_row, pltpu.roll(x, n - 1, 0))
    return x_prev, x_next


def _convmix_kernel(cb_ref, cc_ref, ch_ref, ccp_ref, chp_ref, ccn_ref, chn_ref, w_ref, g_ref, o_ref, *, tiles_per_seq):
    i = pl.program_id(0)
    first = (i % tiles_per_seq) == 0
    last = (i % tiles_per_seq) == tiles_per_seq - 1
    u = cc_ref[...] * ch_ref[...]
    up = ccp_ref[SUBLANES - 1:SUBLANES, :] * chp_ref[SUBLANES - 1:SUBLANES, :]
    un = ccn_ref[0:1, :] * chn_ref[0:1, :]
    u_prev, u_next = _shift_rows(u, up, un, first, last)
    conv = u_prev * w_ref[0:1, :] + u * w_ref[1:2, :] + u_next * w_ref[2:3, :]
    y = cb_ref[...] * conv
    y = y * lax.rsqrt(jnp.mean(y * y, axis=-1, keepdims=True) + RMS_EPS)
    o_ref[...] = (y * g_ref[...]).astype(o_ref.dtype)


def _halo_specs(tt, width, col0, n_rows):
    cb = col0 // width
    r = tt // SUBLANES
    last_blk = n_rows // SUBLANES - 1
    prev = pl.BlockSpec((SUBLANES, width), lambda i: (jnp.maximum(i * r - 1, 0), cb))
    nxt = pl.BlockSpec((SUBLANES, width), lambda i: (jnp.minimum((i + 1) * r, last_blk), cb))
    return prev, nxt


def _conv_mix(proj_a, conv_w, norm_g, seq, tt=CONV_TT):
    n = proj_a.shape[0]
    w = CONV_WIDTH
    main = lambda c0: pl.BlockSpec((tt, w), lambda i: (i, c0 // w))
    ccp, ccn = _halo_specs(tt, w, A_CC, n)
    chp, chn = _halo_specs(tt, w, A_CH, n)
    full = lambda shape: pl.BlockSpec(shape, lambda i: (0, 0))
    return pl.pallas_call(
        functools.partial(_convmix_kernel, tiles_per_seq=seq // tt),
        grid=(n // tt,),
        in_specs=[main(A_CB), main(A_CC), main(A_CH), ccp, chp, ccn, chn, full((3, w)), full((1, w))],
        out_specs=pl.BlockSpec((tt, w), lambda i: (i, 0)),
        out_shape=jax.ShapeDtypeStruct((n, w), BF16),
        compiler_params=_cparams(("parallel",)),
        name="conv_mix",
    )(proj_a, proj_a, proj_a, proj_a, proj_a, proj_a, proj_a, conv_w, norm_g.reshape(1, w))


def _ssd_kernel(*refs, direction, final, tiles):
    if final:
        (xs_s, bc_s, dt_ref, z_ref, yf_ref, dtb_ref, alog_ref, e_ref, hm_ref, dsk_ref, ng_ref,
         o_ref, h_ref, y_ref) = refs
    else:
        (xs_ref, xsp_ref, xsn_ref, bc_ref, bcp_ref, bcn_ref, dt_ref,
         cwx_ref, cbx_ref, cwb_ref, cbb_ref, dtb_ref, alog_ref, e_ref, hm_ref,
         o_ref, xs_s, bc_s, h_ref) = refs
        y_ref = o_ref
    L = SSD_CHUNK
    c = pl.program_id(1)
    tile = c if direction == 0 else tiles - 1 - c
    first = tile == 0
    last = tile == tiles - 1

    @pl.when(c == 0)
    def _():
        h_ref[...] = jnp.zeros_like(h_ref)

    def conv_silu(x_ref, p_ref, n_ref, w_ref, b_ref):
        x = x_ref[...]
        x_prev, x_next = _shift_rows(x, p_ref[SUBLANES - 1:SUBLANES, :], n_ref[0:1, :], first, last)
        v = x_prev * w_ref[0:1, :] + x * w_ref[1:2, :] + x_next * w_ref[2:3, :] + b_ref[...]
        return v * _sigmoid(v)

    if not final:
        xs_s[...] = conv_silu(xs_ref, xsp_ref, xsn_ref, cwx_ref, cbx_ref)
        bc_s[...] = conv_silu(bc_ref, bcp_ref, bcn_ref, cwb_ref, cbb_ref)

    li = lax.broadcasted_iota(jnp.int32, (L, L), 0)
    si = lax.broadcasted_iota(jnp.int32, (L, L), 1)
    valid = (si <= li) if direction == 0 else (si >= li)
    tri = jnp.where(valid, 1.0, 0.0).astype(BF16)
    lane = lax.broadcasted_iota(jnp.int32, (1, LANES), 1)
    nt = (((1,), (1,)), ((), ()))
    e_mat = e_ref[...]
    neg_a = -jnp.exp(alog_ref[...])
    heads_per_group = SSD_HEADS // SSD_GROUPS
    chunks = xs_s.shape[0] // L
    order = range(chunks) if direction == 0 else range(chunks - 1, -1, -1)

    for k in order:
        rows = slice(k * L, (k + 1) * L)
        xs = xs_s[rows, :]
        bm = bc_s[rows, :LANES]
        cm = bc_s[rows, LANES:]

        dtr = dt_ref[rows, :] + dtb_ref[...]
        dt = jnp.maximum(dtr, 0.0) + _log1p(jnp.exp(-jnp.abs(dtr)))
        acum = _dot_exact_rhs(tri, dt * neg_a)
        acum_t = acum.T
        total = acum[L - 1:L, :] if direction == 0 else acum[0:1, :]
        dec_end = jnp.exp(total - acum)
        ea = jnp.exp(acum)

        dtx = _dot_split2_lhs(dt, e_mat)
        eax = _dot_split2_lhs(ea, e_mat)
        dex = _dot_split2_lhs(dec_end * dt, e_mat)

        xdt = xs * dtx
        xw = (xs * dex).astype(BF16)
        cm_b = cm.astype(BF16)
        bm_b = bm.astype(BF16)

        h_prev = h_ref[...]
        y_off = _dot(cm_b, h_prev.astype(BF16)) * eax

        for g in range(SSD_GROUPS):
            cg = jnp.where((lane // SSD_STATE) == g, cm, 0.0).astype(BF16)
            cb = lax.dot_general(cg, bm_b, nt, preferred_element_type=F32)
            for pair in range(heads_per_group // 2):
                acc = None
                for half in range(2):
                    h = g * heads_per_group + pair * 2 + half
                    hl = direction * SSD_HEADS + h
                    seg = acum[:, hl:hl + 1] - acum_t[hl:hl + 1, :]
                    w = (jnp.where(valid, jnp.exp(seg), 0.0) * cb).astype(BF16)
                    c0 = (h // 2) * LANES
                    rhs = jnp.where((lane // SSD_HEAD_DIM) == half, xdt[:, c0:c0 + LANES], 0.0).astype(BF16)
                    part = _dot(w, rhs)
                    acc = part if acc is None else acc + part
                c0 = (g * heads_per_group // 2 + pair) * LANES
                y_pair = acc + y_off[:, c0:c0 + LANES]
                if final:
                    y_pair = yf_ref[rows, c0:c0 + LANES] + y_pair + dsk_ref[:, c0:c0 + LANES] * xs[:, c0:c0 + LANES]
                    zz = z_ref[rows, c0:c0 + LANES]
                    y_pair = y_pair * (zz * _sigmoid(zz))
                y_ref[rows, c0:c0 + LANES] = y_pair

        cd = eax[L - 1:L, :] if direction == 0 else eax[0:1, :]
        s_new = _dot(bm.T.astype(BF16), xw)
        h_ref[...] = (cd * h_prev + s_new) * hm_ref[...]

    if final:
        gw = SSD_WIDTH // SSD_GROUPS
        for g in range(SSD_GROUPS):
            y = y_ref[:, g * gw:(g + 1) * gw]
            y = y * lax.rsqrt(jnp.mean(y * y, axis=-1, keepdims=True) + RMS_EPS)
            o_ref[:, g * gw:(g + 1) * gw] = (y * ng_ref[:, g * gw:(g + 1) * gw]).astype(o_ref.dtype)


def _ssd_pass(proj_a, fwd_outs, params, seq, direction):
    n = proj_a.shape[0]
    T = SSD_CHUNK * SSD_CHUNKS_PER_STEP
    tiles = seq // T
    nseq = n // seq
    final = direction == 1
    cw, cb_, dtb, alog, e_mat, hmask, dsk, ng = params
    bcw = 2 * SSD_GROUPS * SSD_STATE

    def row_blk(b, c):
        t = c if direction == 0 else tiles - 1 - c
        return b * tiles + t

    def main(width, col0):
        return pl.BlockSpec((T, width), lambda b, c: (row_blk(b, c), col0 // width))

    r = T // SUBLANES
    last_blk = n // SUBLANES - 1

    def halo(width, col0):
        prev = pl.BlockSpec((SUBLANES, width), lambda b, c: (jnp.maximum(row_blk(b, c) * r - 1, 0), col0 // width))
        nxt = pl.BlockSpec((SUBLANES, width), lambda b, c: (jnp.minimum((row_blk(b, c) + 1) * r, last_blk), col0 // width))
        return prev, nxt

    xsp, xsn = halo(SSD_WIDTH, A_XS)
    bcp, bcn = halo(bcw, A_BC)
    full = lambda a: pl.BlockSpec(a.shape, lambda b, c: (0,) * a.ndim)

    rows = lambda width: pl.BlockSpec((T, width), lambda b, c: (row_blk(b, c), 0))
    if final:
        y_fwd, xs_act, bc_act = fwd_outs
        in_specs = [rows(SSD_WIDTH), rows(bcw), main(LANES, A_DT), main(SSD_WIDTH, A_Z), rows(SSD_WIDTH)]
        args = [xs_act, bc_act, proj_a, proj_a, y_fwd]
        consts = [dtb, alog, e_mat, hmask, dsk, ng]
        out_specs = rows(SSD_WIDTH)
        out_shape = jax.ShapeDtypeStruct((n, SSD_WIDTH), BF16)
    else:
        in_specs = [main(SSD_WIDTH, A_XS), xsp, xsn, main(bcw, A_BC), bcp, bcn, main(LANES, A_DT)]
        args = [proj_a] * 7
        consts = [cw[:, :SSD_WIDTH], cb_[:, :SSD_WIDTH], cw[:, SSD_WIDTH:], cb_[:, SSD_WIDTH:], dtb, alog, e_mat, hmask]
        out_specs = [rows(SSD_WIDTH), rows(SSD_WIDTH), rows(bcw)]
        out_shape = [jax.ShapeDtypeStruct((n, SSD_WIDTH), F32), jax.ShapeDtypeStruct((n, SSD_WIDTH), F32),
                     jax.ShapeDtypeStruct((n, bcw), F32)]
    in_specs += [full(a) for a in consts]
    args += consts
    scratch = [pltpu.VMEM((LANES, SSD_WIDTH), F32)]
    if final:
        scratch.append(pltpu.VMEM((T, SSD_WIDTH), F32))
    return pl.pallas_call(
        functools.partial(_ssd_kernel, direction=direction, final=final, tiles=tiles),
        grid=(nseq, tiles),
        in_specs=in_specs,
        out_specs=out_specs,
        out_shape=out_shape,
        scratch_shapes=scratch,
        compiler_params=_cparams(("parallel", "arbitrary")),
        name="ssd_bwd" if final else "ssd_fwd",
    )(*args)


def _ssd_consts(ssd_conv_w, ssd_conv_b, ssd_dt_bias, ssd_a_log, ssd_d, ssd_norm_g):
    pad = LANES - 2 * SSD_HEADS
    dtb = jnp.pad(ssd_dt_bias.reshape(1, -1), ((0, 0), (0, pad)))
    alog = jnp.pad(ssd_a_log.reshape(1, -1), ((0, 0), (0, pad)))
    col_head = np.arange(SSD_WIDTH) // SSD_HEAD_DIM
    e_mats = [jnp.asarray((np.arange(LANES)[:, None] == d * SSD_HEADS + col_head[None, :]), BF16) for d in range(2)]
    row_group = np.arange(LANES) // SSD_STATE
    col_group = col_head // (SSD_HEADS // SSD_GROUPS)
    hmask = jnp.asarray(row_group[:, None] == col_group[None, :], F32)
    dsk = jnp.repeat(ssd_d, SSD_HEAD_DIM).reshape(1, SSD_WIDTH)
    ng = ssd_norm_g.reshape(1, SSD_WIDTH)
    cb_ = ssd_conv_b.reshape(1, -1)
    return [(ssd_conv_w, cb_, dtb, alog, e_mats[d], hmask, dsk, ng) for d in range(2)]


NA_KROWS = NA_QROWS + WIN_H


def _na_kernel(kb_ref, cls_ref, q_ref, k_ref, v_ref, bias_ref, o_ref, *, blocks_per_step):
    step = pl.program_id(2)
    lane = lax.broadcasted_iota(jnp.int32, (1, LANES), 1)
    nt = (((1,), (1,)), ((), ()))
    nq = NA_QROWS * GRID_W
    nk = NA_KROWS * GRID_W
    for j in range(blocks_per_step):
        blk = step * blocks_per_step + j
        k0 = pl.multiple_of(kb_ref[blk] * GRID_W, GRID_W)
        cls = cls_ref[blk]
        q = q_ref[j * nq:(j + 1) * nq, :] * (NA_HEAD_DIM ** -0.5)
        kk = k_ref[pl.ds(k0, nk), :]
        vv = v_ref[pl.ds(k0, nk), :]
        outs = []
        for hh in range(2):
            qm = jnp.where((lane // NA_HEAD_DIM) == hh, q, jnp.zeros_like(q))
            s = lax.dot_general(qm, kk, nt, preferred_element_type=F32) + bias_ref[cls, hh]
            m = jnp.max(s, axis=-1, keepdims=True)
            p = jnp.exp(s - m)
            l = jnp.sum(p, axis=-1, keepdims=True)
            outs.append(_dot(p.astype(BF16), vv) / l)
        o = jnp.where((lane // NA_HEAD_DIM) == 0, outs[0], outs[1])
        o_ref[j * nq:(j + 1) * nq, :] = o.astype(o_ref.dtype)


def _na_tables(rel_bias, rows):
    qc = np.arange(GRID_W)
    cs = np.clip(qc - WIN_W // 2, 0, GRID_W - WIN_W)
    kc = np.arange(GRID_W)
    col_valid = (kc[None, :] >= cs[:, None]) & (kc[None, :] < cs[:, None] + WIN_W)
    dj = np.clip(kc[None, :] - qc[:, None] + WIN_W - 1, 0, 2 * WIN_W - 2)
    nblk = rows // NA_QROWS
    kb = np.clip(np.arange(nblk) * NA_QROWS - WIN_H // 2, 0, rows - NA_KROWS)
    patterns, cls = [], []
    for b in range(nblk):
        r = b * NA_QROWS + np.arange(NA_QROWS)
        rs = np.clip(r - WIN_H // 2, 0, rows - WIN_H)
        key = (tuple(rs - kb[b]), int(kb[b] - b * NA_QROWS))
        if key not in patterns:
            patterns.append(key)
        cls.append(patterns.index(key))
    nh, nw = 2 * WIN_H - 1, 2 * WIN_W - 1
    bias_ext = jnp.pad(rel_bias.astype(F32), ((0, 0), (0, 1), (0, 1)), constant_values=NEG_BIG)
    di_sel = []
    for off, shift in patterns:
        kr = np.arange(NA_KROWS)[None, :]
        i = np.arange(NA_QROWS)[:, None]
        rel = kr - np.asarray(off)[:, None]
        row_valid = (rel >= 0) & (rel < WIN_H)
        di = np.where(row_valid, np.clip(shift + kr - i + WIN_H - 1, 0, nh - 1), nh)
        di_sel.append(np.eye(nh + 1, dtype=np.float32)[di])
    dj_sel = np.eye(nw + 1, dtype=np.float32)[np.where(col_valid, dj, nw)]
    tab = jnp.einsum('pikd,hde,qwe->phiqkw', np.stack(di_sel), bias_ext, dj_sel, precision=lax.Precision.HIGHEST)
    tab = tab.reshape(len(patterns), NA_HEADS, NA_QROWS * GRID_W, NA_KROWS * GRID_W)
    return jnp.asarray(kb, jnp.int32), jnp.asarray(cls, jnp.int32), tab


def _neighbourhood_attention(proj_b, rel_bias, seq, blocks_per_step=NA_BLOCKS_PER_STEP):
    n = proj_b.shape[0]
    nseq = n // seq
    rows = seq // GRID_W
    kb, cls, bias_tab = _na_tables(rel_bias, rows)
    tq = blocks_per_step * NA_QROWS * GRID_W
    pairs = NA_HEADS // 2
    steps = seq // tq
    ncls = bias_tab.shape[0]
    grid_spec = pltpu.PrefetchScalarGridSpec(
        num_scalar_prefetch=2,
        grid=(nseq, pairs, steps),
        in_specs=[pl.BlockSpec((tq, LANES), lambda b, p, r, kb_, cls_: (b * steps + r, p)),
                  pl.BlockSpec((seq, LANES), lambda b, p, r, kb_, cls_: (b, pairs + p)),
                  pl.BlockSpec((seq, LANES), lambda b, p, r, kb_, cls_: (b, 2 * pairs + p)),
                  pl.BlockSpec((ncls, 2) + bias_tab.shape[2:], lambda b, p, r, kb_, cls_: (0, p, 0, 0))],
        out_specs=pl.BlockSpec((tq, LANES), lambda b, p, r, kb_, cls_: (b * steps + r, p)),
    )
    return pl.pallas_call(
        functools.partial(_na_kernel, blocks_per_step=blocks_per_step),
        grid_spec=grid_spec,
        out_shape=jax.ShapeDtypeStruct((n, NA_WIDTH), F32),
        compiler_params=_cparams(("parallel", "arbitrary", "arbitrary")),
        name="nbr_attn",
    )(kb, cls, proj_b, proj_b, proj_b, bias_tab)


def _rms_rows(y, g):
    return y * lax.rsqrt(jnp.mean(y * y, axis=-1, keepdims=True) + RMS_EPS) * g


def _outproj_kernel(*refs, router, n_parts, first_tiles):
    yc_ref, ys_ref, yn_ref = refs[:3]
    x_refs = refs[3:3 + n_parts]
    rest = refs[3 + n_parts:]
    if router:
        w_ref, nag_ref, g_ref, b_ref, rw_ref, o_ref, idx_ref, gate_ref = rest
    else:
        w_ref, nag_ref, g_ref, b_ref, o_ref = rest
    c1 = CONV_WIDTH
    c2 = CONV_WIDTH + SSD_WIDTH
    rows = o_ref.shape[0] // OUTPROJ_ROW_SPLITS
    for part in range(OUTPROJ_ROW_SPLITS):
        sl = slice(part * rows, (part + 1) * rows)
        yn = _rms_rows(yn_ref[sl, :], nag_ref[...]).astype(BF16)
        mix = _dot(yc_ref[sl, :], w_ref[0:c1, :]) + _dot(ys_ref[sl, :], w_ref[c1:c2, :]) + _dot(yn, w_ref[c2:, :])
        out = _layer_norm(ALPHA * _read_row_parts(x_refs, first_tiles, sl) + mix, g_ref[...], b_ref[...])
        o_ref[sl, :] = out
        if not router:
            continue
        o1, o2, _ = _split3(out)
        t1 = _dot(o1, rw_ref[...])
        t2 = _dot(o2, rw_ref[...])
        sh1 = LANES - N_EXPERTS
        sh2 = LANES - 2 * N_EXPERTS
        logits = ((pltpu.roll(t2, sh1, 1) + pltpu.roll(t1, sh2, 1)) + (t2 + pltpu.roll(t1, sh1, 1))) + t1
        lane = lax.broadcasted_iota(jnp.int32, (rows, LANES), 1)
        lg = jnp.where(lane < N_EXPERTS, logits, -jnp.inf)
        m1 = jnp.max(lg, axis=-1, keepdims=True)
        i1 = jnp.min(jnp.where(lg == m1, lane, LANES), axis=-1, keepdims=True)
        lg2 = jnp.where(lane == i1, -jnp.inf, lg)
        m2 = jnp.max(lg2, axis=-1, keepdims=True)
        i2 = jnp.min(jnp.where(lg2 == m2, lane, LANES), axis=-1, keepdims=True)
        e2 = jnp.exp(m2 - m1)
        den = 1.0 + e2
        idx_ref[sl, :] = jnp.where(lane == 0, i1, jnp.where(lane == 1, i2, 0))
        gate_ref[sl, :] = jnp.where(lane == 0, 1.0 / den, jnp.where(lane == 1, e2 / den, 0.0))


def _out_proj_ln(yc, ys, yn, x_parts, w_out, na_g, ln_g, ln_b, router_w=None, tm=ROW_TILE):
    n = yc.shape[0]
    router = router_w is not None
    row = lambda w: pl.BlockSpec((tm, w), lambda i: (i, 0))
    full = lambda a: pl.BlockSpec(a.shape, lambda i: (0,) * a.ndim)
    consts = [w_out, na_g.reshape(1, -1), ln_g.reshape(1, -1), ln_b.reshape(1, -1)]
    if router:
        consts.append(router_w)
    out_shape = [jax.ShapeDtypeStruct((n, D_MODEL), F32)]
    out_specs = [row(D_MODEL)]
    if router:
        out_shape += [jax.ShapeDtypeStruct((n, LANES), jnp.int32), jax.ShapeDtypeStruct((n, LANES), F32)]
        out_specs += [row(LANES), row(LANES)]
    res = pl.pallas_call(
        functools.partial(_outproj_kernel, router=router, n_parts=len(x_parts),
                          first_tiles=x_parts[0].shape[0] // tm),
        grid=(n // tm,),
        in_specs=[row(CONV_WIDTH), row(SSD_WIDTH), row(NA_WIDTH)] + _row_parts_specs(x_parts, tm, D_MODEL, 1)
        + [full(a) for a in consts],
        out_specs=out_specs,
        out_shape=out_shape,
        compiler_params=_cparams(("parallel",)),
        name="out_proj_router" if router else "out_proj",
    )(yc, ys, yn, *x_parts, *consts)
    return res


def _ffn_kernel(be_ref, bv_ref, x_ref, wg_ref, wu_ref, wd_ref, *rest, packed, ln, ragged):
    if ln:
        g_ref, b_ref, o_ref = rest
    else:
        (o_ref,) = rest
    i = pl.program_id(0)
    j = pl.program_id(1)

    @pl.when(j == 0)
    def _():
        o_ref[...] = jnp.zeros_like(o_ref)

    def accumulate(rows):
        if packed:
            lo, hi = _unpack_bf16_pairs(x_ref[rows, :])
            c = lo.shape[1]
            g = _dot(lo, wg_ref[0, :c, :].astype(BF16)) + _dot(hi, wg_ref[0, c:, :].astype(BF16))
            u = _dot(lo, wu_ref[0, :c, :].astype(BF16)) + _dot(hi, wu_ref[0, c:, :].astype(BF16))
        else:
            x = x_ref[rows, :].astype(BF16)
            g = _dot(x, wg_ref[0].astype(BF16))
            u = _dot(x, wu_ref[0].astype(BF16))
        h = ((g * _sigmoid(g)) * u).astype(BF16)
        o_ref[rows, :] += _dot(h, wd_ref[0].astype(BF16))

    tm = o_ref.shape[0]
    filled = bv_ref[i]

    @pl.when(filled == tm)
    def _():
        accumulate(slice(None))

    if ragged:
        for s in range(tm // FFN_ROW_GROUP):
            @pl.when(jnp.logical_and(filled < tm, filled > s * FFN_ROW_GROUP))
            def _():
                accumulate(slice(s * FFN_ROW_GROUP, (s + 1) * FFN_ROW_GROUP))

    if ln:
        @pl.when(j == pl.num_programs(1) - 1)
        def _():
            o_ref[...] = _layer_norm(ALPHA * x_ref[...] + o_ref[...], g_ref[...], b_ref[...])


def _ffn_blocks(x, block_e, block_fill, w_gate, w_up, w_down, tm, tf=FF_TILE, residual_ln=None, ragged=False):
    n, xw = x.shape
    packed = x.dtype == jnp.uint32
    d = w_gate.shape[1]
    f = w_gate.shape[2]
    nf = f // tf
    col = lambda i, j, be, bv: (be[i], 0, jnp.where(bv[i] != 0, j, nf - 1))
    rowj = lambda i, j, be, bv: (be[i], jnp.where(bv[i] != 0, j, nf - 1), 0)
    in_specs = [pl.BlockSpec((tm, xw), lambda i, j, be, bv: (i, 0)),
                pl.BlockSpec((1, d, tf), col), pl.BlockSpec((1, d, tf), col), pl.BlockSpec((1, tf, d), rowj)]
    args = [x, w_gate, w_up, w_down]
    if residual_ln is not None:
        gain, bias = residual_ln
        assert not packed
        vec = pl.BlockSpec((1, d), lambda i, j, be, bv: (0, 0))
        in_specs += [vec, vec]
        args += [gain.reshape(1, d), bias.reshape(1, d)]
    grid_spec = pltpu.PrefetchScalarGridSpec(
        num_scalar_prefetch=2,
        grid=(n // tm, nf),
        in_specs=in_specs,
        out_specs=pl.BlockSpec((tm, d), lambda i, j, be, bv: (i, 0)),
    )
    return pl.pallas_call(
        functools.partial(_ffn_kernel, packed=packed, ln=residual_ln is not None, ragged=ragged),
        grid_spec=grid_spec,
        out_shape=jax.ShapeDtypeStruct((n, d), F32),
        compiler_params=_cparams(("parallel", "arbitrary")),
        name="swiglu_blocks",
    )(block_e, block_fill, *args)


def _combine_ln_kernel(i0_ref, i1_ref, n0_ref, n1_ref, src_ref, x_ref, gate_ref, g_ref, b_ref, oa_ref, ob_ref,
                       b0_ref, b1_ref, sem, *, tm, first_tiles):
    i = pl.program_id(0)
    slot = i % 2

    def start(idx0_ref, idx1_ref, s):
        _start_row_gather(src_ref, idx0_ref, b0_ref.at[s], sem.at[s, 0], tm)
        _start_row_gather(src_ref, idx1_ref, b1_ref.at[s], sem.at[s, 1], tm)

    @pl.when(i == 0)
    def _():
        start(i0_ref, i1_ref, 0)

    @pl.when(i + 1 < pl.num_programs(0))
    def _():
        start(n0_ref, n1_ref, 1 - slot)

    _wait_row_gather(src_ref, b0_ref.at[slot], sem.at[slot, 0], tm)
    _wait_row_gather(src_ref, b1_ref.at[slot], sem.at[slot, 1], tm)
    f = b0_ref[slot] * gate_ref[:, 0:1] + b1_ref[slot] * gate_ref[:, 1:2]
    res = _layer_norm(ALPHA * x_ref[...] + f, g_ref[...], b_ref[...])

    @pl.when(i < first_tiles)
    def _():
        oa_ref[...] = res

    @pl.when(i >= first_tiles)
    def _():
        ob_ref[...] = res


def _combine_ln(x, outs, d0, d1, gates, g, b, n_first, tm=ROW_TILE):
    n, d = x.shape
    nt = n // tm
    first_tiles = n_first // tm
    row = pl.BlockSpec((tm, d), lambda i: (i, 0))
    vec = pl.BlockSpec((1, d), lambda i: (0, 0))
    idx = pl.BlockSpec((1, 1, tm), lambda i: (i, 0, 0), memory_space=pltpu.SMEM)
    idx_next = pl.BlockSpec((1, 1, tm), lambda i: (jnp.minimum(i + 1, nt - 1), 0, 0), memory_space=pltpu.SMEM)
    d0, d1 = d0.reshape(nt, 1, tm), d1.reshape(nt, 1, tm)
    return pl.pallas_call(
        functools.partial(_combine_ln_kernel, tm=tm, first_tiles=first_tiles), grid=(nt,),
        in_specs=[idx, idx, idx_next, idx_next, pl.BlockSpec(memory_space=pl.ANY), row,
                  pl.BlockSpec((tm, LANES), lambda i: (i, 0)), vec, vec],
        out_specs=[pl.BlockSpec((tm, d), lambda i: (jnp.minimum(i, first_tiles - 1), 0)),
                   pl.BlockSpec((tm, d), lambda i: (jnp.maximum(i - first_tiles, 0), 0))],
        out_shape=[jax.ShapeDtypeStruct((n_first, d), F32), jax.ShapeDtypeStruct((n - n_first, d), F32)],
        scratch_shapes=[pltpu.VMEM((2, tm, d), F32), pltpu.VMEM((2, tm, d), F32), pltpu.SemaphoreType.DMA((2, 2))],
        compiler_params=_cparams(("arbitrary",)), name="combine_ln",
    )(d0, d1, d0, d1, outs, x, gates, g.reshape(1, d), b.reshape(1, d))


def _start_row_gather(src_ref, idx_ref, buf_ref, sem, tr):
    def issue(r, carry):
        pltpu.make_async_copy(src_ref.at[pl.ds(idx_ref[0, 0, r], 1), :], buf_ref.at[pl.ds(r, 1), :], sem).start()
        return carry

    lax.fori_loop(0, tr, issue, 0, unroll=GATHER_UNROLL)


def _wait_row_gather(src_ref, buf_ref, sem, tr):
    pltpu.make_async_copy(src_ref.at[pl.ds(0, tr), :], buf_ref, sem).wait()


HI16 = 0xFFFF0000


def _pack_bf16_pairs(x):
    c = x.shape[1] // 2
    lo = lax.bitcast_convert_type(x[:, :c].astype(BF16).astype(F32), jnp.uint32)
    hi = lax.bitcast_convert_type(x[:, c:].astype(BF16).astype(F32), jnp.uint32)
    return (hi & jnp.uint32(HI16)) | (lo >> 16)


def _unpack_bf16_pairs(u):
    lo = lax.bitcast_convert_type(u << 16, F32).astype(BF16)
    hi = lax.bitcast_convert_type(u & jnp.uint32(HI16), F32).astype(BF16)
    return lo, hi


def _zero_row_runs(first, length, max_len, z_ref, o_ref, sem):
    zrows = z_ref.shape[0]

    def copy(row, rows):
        return pltpu.make_async_copy(z_ref.at[pl.ds(0, rows), :], o_ref.at[pl.ds(row, rows), :], sem)

    def for_each_run(action):
        head = jnp.minimum((-first) % SUBLANES, length)
        tail = (length - head) % SUBLANES
        body = length - head - tail
        for k in range(SUBLANES - 1):
            @pl.when(k < head)
            def _(k=k):
                action(copy(first + k, 1))
        row = first + head
        for bit in range(SUBLANES.bit_length() - 1, max_len.bit_length()):
            size = 1 << bit
            take = (body >> bit) & 1
            rows = min(size, zrows)
            for piece in range(size // rows):
                @pl.when(take == 1)
                def _(row=row, piece=piece, rows=rows):
                    action(copy(pl.multiple_of(row + piece * rows, SUBLANES), rows))
            row = row + take * size
        for k in range(SUBLANES - 1):
            @pl.when(k < tail)
            def _(k=k, row=row):
                action(copy(row + k, 1))

    for_each_run(lambda c: c.start())
    for_each_run(lambda c: c.wait())


def _dispatch_kernel(pad_start_ref, pad_len_ref, d0_ref, d1_ref, x_ref, xs_ref, pk_ref, z_ref, sem, zsem,
                     *, tm, max_pad):
    @pl.when(pl.program_id(0) == 0)
    def _():
        z_ref[...] = jnp.zeros_like(z_ref)

        def zero_region(r, carry):
            _zero_row_runs(pad_start_ref[r], pad_len_ref[r], max_pad, z_ref, xs_ref, zsem)
            return carry

        lax.fori_loop(0, pad_start_ref.shape[0], zero_region, 0)

    pk_ref[...] = _pack_bf16_pairs(x_ref[...])

    def issue(r, carry):
        row = pk_ref.at[pl.ds(r, 1), :]
        pltpu.make_async_copy(row, xs_ref.at[pl.ds(d0_ref[0, 0, r], 1), :], sem.at[0]).start()
        pltpu.make_async_copy(row, xs_ref.at[pl.ds(d1_ref[0, 0, r], 1), :], sem.at[1]).start()
        return carry

    lax.fori_loop(0, tm, issue, 0, unroll=GATHER_UNROLL)
    for k in range(TOP_K):
        pltpu.make_async_copy(pk_ref, xs_ref.at[pl.ds(0, tm), :], sem.at[k]).wait()


def _dispatch_rows(x, d0, d1, pad_start, pad_len, p_len, max_pad, tm=ROW_TILE):
    n, d = x.shape
    idx = pl.BlockSpec((1, 1, tm), lambda i, ps, pn: (i, 0, 0), memory_space=pltpu.SMEM)
    grid_spec = pltpu.PrefetchScalarGridSpec(
        num_scalar_prefetch=2,
        grid=(n // tm,),
        in_specs=[idx, idx, pl.BlockSpec((tm, d), lambda i, ps, pn: (i, 0))],
        out_specs=pl.BlockSpec(memory_space=pl.ANY),
        scratch_shapes=[pltpu.VMEM((tm, d // 2), jnp.uint32), pltpu.VMEM((tm, d // 2), jnp.uint32),
                        pltpu.SemaphoreType.DMA((TOP_K,)), pltpu.SemaphoreType.DMA(())],
    )
    return pl.pallas_call(
        functools.partial(_dispatch_kernel, tm=tm, max_pad=max_pad),
        grid_spec=grid_spec,
        out_shape=jax.ShapeDtypeStruct((p_len, d // 2), jnp.uint32),
        compiler_params=_cparams(("arbitrary",)),
        name="dispatch_rows",
    )(pad_start, pad_len, d0.reshape(n // tm, 1, tm), d1.reshape(n // tm, 1, tm), x)


def _routing_tables(idx, n, tm):
    flat_e = idx[:, :TOP_K].reshape(-1)
    onehot = (flat_e[:, None] == jnp.arange(N_EXPERTS, dtype=jnp.int32)[None, :]).astype(jnp.int32)
    incl = jnp.cumsum(onehot, axis=0)
    rank = jnp.sum((incl - onehot) * onehot, axis=1)
    counts = incl[-1]
    padded = ((counts + tm - 1) // tm) * tm
    pend = jnp.cumsum(padded)
    pstart = pend - padded
    dest = pstart[flat_e] + rank
    n_blocks = -(-(n * TOP_K) // tm) + N_EXPERTS
    starts = jnp.arange(n_blocks, dtype=jnp.int32) * tm
    block_e = jnp.minimum(jnp.searchsorted(pend, starts, side='right'), N_EXPERTS - 1).astype(jnp.int32)
    block_fill = jnp.clip((pstart + counts)[block_e] - starts, 0, tm).astype(jnp.int32)
    block_fill = jnp.where(starts < pend[-1], block_fill, 0)
    last_e = block_e[jnp.maximum(jnp.sum((block_fill > 0).astype(jnp.int32)) - 1, 0)]
    block_e = jnp.where(block_fill > 0, block_e, last_e)
    dest2 = dest.reshape(n, TOP_K)
    pad_start = jnp.concatenate([pstart + counts, pend[-1:]]).astype(jnp.int32)
    pad_len = jnp.concatenate([padded - counts, n_blocks * tm - pend[-1:]]).astype(jnp.int32)
    return (pad_start, pad_len, n_blocks * tm), block_e, block_fill, dest2[:, 0], dest2[:, 1]


def _prep_w_in(w):
    bc_w = 2 * SSD_GROUPS * SSD_STATE
    widths = (CONV_WIDTH, CONV_WIDTH, CONV_WIDTH, SSD_WIDTH, SSD_WIDTH, bc_w, 2 * SSD_HEADS, 3 * NA_WIDTH)
    cb, cc, ch, z, xs, bc, dt, qkv = jnp.split(w, [int(o) for o in np.cumsum(widths)[:-1]], axis=1)
    pad = jnp.zeros((D_MODEL, A_WIDTH - (A_DT + 2 * SSD_HEADS)), w.dtype)
    return jnp.concatenate([z, xs, cb, cc, ch, bc, dt, pad, qkv], axis=1).astype(BF16)


def _prep_router_w(rw):
    r1 = rw.astype(BF16)
    r2 = (rw - r1.astype(F32)).astype(BF16)
    r3 = (rw - r1.astype(F32) - r2.astype(F32)).astype(BF16)
    pad = jnp.zeros((rw.shape[0], LANES - 3 * N_EXPERTS), BF16)
    return jnp.concatenate([r1, r2, r3, pad], axis=1)


def _trunk(x_parts, seq, p):
    n = sum(a.shape[0] for a in x_parts)
    n_first = x_parts[0].shape[0]
    for l in range(DEPTH):
        proj_a, proj_b = _in_proj(x_parts, _prep_w_in(p['w_in'][l]))
        y_conv = _conv_mix(proj_a, p['conv_w'][l], p['conv_norm_g'][l], seq)
        cf, cbw = _ssd_consts(p['ssd_conv_w'][l], p['ssd_conv_b'][l], p['ssd_dt_bias'][l], p['ssd_a_log'][l],
                              p['ssd_d'][l], p['ssd_norm_g'][l])
        y_ssd = _ssd_pass(proj_a, _ssd_pass(proj_a, None, cf, seq, 0), cbw, seq, 1)
        y_na = _neighbourhood_attention(proj_b, p['na_rel_bias'][l], seq)
        w_out = p['w_out'][l].astype(BF16)
        if l % 2 == 0:
            x = _out_proj_ln(y_conv, y_ssd, y_na, x_parts, w_out, p['na_norm_g'][l], p['ln_mix_g'][l],
                             p['ln_mix_b'][l])[0]
            padf = ((0, 0), (0, D_FF_PAD - D_FF))
            wg = jnp.pad(p['ffn_w_gate'][l // 2], padf).astype(BF16)[None]
            wu = jnp.pad(p['ffn_w_up'][l // 2], padf).astype(BF16)[None]
            wd = jnp.pad(p['ffn_w_down'][l // 2], ((0, D_FF_PAD - D_FF), (0, 0))).astype(BF16)[None]
            nb = n // MOE_TM
            x = _ffn_blocks(x, jnp.zeros((nb,), jnp.int32), jnp.full((nb,), MOE_TM, jnp.int32), wg, wu, wd, MOE_TM,
                            residual_ln=(p['ln_ffn_g'][l], p['ln_ffn_b'][l]))
            outs_split = (x[:n_first], x[n_first:])
        else:
            x, idx, gates = _out_proj_ln(y_conv, y_ssd, y_na, x_parts, w_out, p['na_norm_g'][l], p['ln_mix_g'][l],
                                         p['ln_mix_b'][l], _prep_router_w(p['router_w'][l // 2]))
            (pad_start, pad_len, p_len), block_e, block_fill, d0, d1 = _routing_tables(idx, n, MOE_TM)
            xs = _dispatch_rows(x, d0, d1, pad_start, pad_len, p_len, N_EXPERTS * MOE_TM)
            outs = _ffn_blocks(xs, block_e, block_fill, p['moe_w_gate'][l // 2], p['moe_w_up'][l // 2],
                               p['moe_w_down'][l // 2], MOE_TM, ragged=True)
            outs_split = _combine_ln(x, outs, d0, d1, gates, p['ln_ffn_g'][l], p['ln_ffn_b'][l], n_first)
            if l + 1 < DEPTH:
                x = jnp.concatenate(outs_split, axis=0)
        x_parts = [x]
    return outs_split


def kernel(x_prompt, x_sample, w_in, conv_w, conv_norm_g, ssd_conv_w, ssd_conv_b, ssd_dt_bias, ssd_a_log, ssd_d,
           ssd_norm_g, na_rel_bias, na_norm_g, w_out, ln_mix_g, ln_mix_b, ln_ffn_g, ln_ffn_b, ffn_w_gate, ffn_w_up,
           ffn_w_down, router_w, moe_w_gate, moe_w_up, moe_w_down):
    p = dict(w_in=w_in, conv_w=conv_w, conv_norm_g=conv_norm_g, ssd_conv_w=ssd_conv_w, ssd_conv_b=ssd_conv_b,
             ssd_dt_bias=ssd_dt_bias, ssd_a_log=ssd_a_log, ssd_d=ssd_d, ssd_norm_g=ssd_norm_g,
             na_rel_bias=na_rel_bias, na_norm_g=na_norm_g, w_out=w_out, ln_mix_g=ln_mix_g, ln_mix_b=ln_mix_b,
             ln_ffn_g=ln_ffn_g, ln_ffn_b=ln_ffn_b, ffn_w_gate=ffn_w_gate, ffn_w_up=ffn_w_up, ffn_w_down=ffn_w_down,
             router_w=router_w, moe_w_gate=moe_w_gate, moe_w_up=moe_w_up, moe_w_down=moe_w_down)
    bp, seq, d = x_prompt.shape
    bs, seq_s, _ = x_sample.shape
    assert seq == seq_s, "both request groups must share one sequence length"
    yp, ys = _trunk([x_prompt.reshape(bp * seq, d), x_sample.reshape(bs * seq, d)], seq, p)
    return yp.reshape(bp, seq, d), ys.reshape(bs, seq, d)
```

```python
import functools

import numpy as np
import jax
import jax.numpy as jnp
from jax import lax
from jax.experimental import pallas as pl
from jax.experimental.pallas import tpu as pltpu

F32 = jnp.float32
BF16 = jnp.bfloat16

D_MODEL = 2048
DEPTH = 2
GRID_W = 64
CONV_WIDTH = 512
SSD_HEADS = 16
SSD_HEAD_DIM = 64
SSD_WIDTH = 1024
SSD_GROUPS = 2
SSD_STATE = 64
SSD_CHUNK = 128
NA_HEADS = 8
NA_HEAD_DIM = 64
NA_WIDTH = 512
WIN_H = 8
WIN_W = 16
D_FF = 5504
N_EXPERTS = 8
TOP_K = 2
D_FF_EXPERT = 7168
ALPHA = (2 * DEPTH) ** 0.25
LN_EPS = 1e-5
RMS_EPS = 1e-6

LANES = 128
SUBLANES = 8
NEG_BIG = -1e30

A_Z, A_XS, A_CB, A_CC, A_CH, A_BC, A_DT, A_WIDTH = 0, 1024, 2048, 2560, 3072, 3584, 3840, 4096

VMEM_LIMIT = 56 * 1024 * 1024
IN_PROJ_TM, IN_PROJ_TN = 1024, 512
ROW_TILE = 512
CONV_TT = 1024
SSD_CHUNKS_PER_STEP = 8
NA_QROWS = 4
NA_BLOCKS_PER_STEP = 16
OUTPROJ_ROW_SPLITS = 2
MOE_TM = 1024
FF_TILE = 512
D_FF_PAD = -(-D_FF // FF_TILE) * FF_TILE
FFN_ROW_GROUP = 256
GATHER_UNROLL = 8


def _cparams(sem, vmem=VMEM_LIMIT):
    return pltpu.CompilerParams(dimension_semantics=sem, vmem_limit_bytes=vmem)


def _sigmoid(x):
    return 1.0 / (1.0 + jnp.exp(-x))


def _split3(x):
    x1 = x.astype(BF16)
    r1 = x - x1.astype(F32)
    x2 = r1.astype(BF16)
    r2 = r1 - x2.astype(F32)
    return x1, x2, r2.astype(BF16)


def _dot(a, b):
    return jnp.dot(a, b, preferred_element_type=F32)


def _dot_exact_rhs(m, x):
    x1, x2, x3 = _split3(x)
    return (_dot(m, x3) + _dot(m, x2)) + _dot(m, x1)


def _dot_split2_lhs(x, m):
    x1, x2, _ = _split3(x)
    return _dot(x2, m) + _dot(x1, m)


def _log1p(e):
    u = 1.0 + e
    d = u - 1.0
    return jnp.where(d == 0.0, e, jnp.log(u) * (e / jnp.where(d == 0.0, 1.0, d)))


def _layer_norm(v, g, b):
    mu = jnp.mean(v, axis=-1, keepdims=True)
    c = v - mu
    var = jnp.mean(c * c, axis=-1, keepdims=True)
    return c * lax.rsqrt(var + LN_EPS) * g + b


def _row_parts_specs(parts, tm, width, grid_rank):
    if len(parts) == 1:
        return [pl.BlockSpec((tm, width), (lambda i: (i, 0)) if grid_rank == 1 else (lambda i, j: (i, 0)))]
    first_tiles = parts[0].shape[0] // tm
    if grid_rank == 1:
        return [pl.BlockSpec((tm, width), lambda i: (jnp.minimum(i, first_tiles - 1), 0)),
                pl.BlockSpec((tm, width), lambda i: (jnp.maximum(i - first_tiles, 0), 0))]
    return [pl.BlockSpec((tm, width), lambda i, j: (jnp.minimum(i, first_tiles - 1), 0)),
            pl.BlockSpec((tm, width), lambda i, j: (jnp.maximum(i - first_tiles, 0), 0))]


def _read_row_parts(x_refs, first_tiles, rows=slice(None)):
    x = x_refs[0][rows, :]
    if len(x_refs) == 2:
        x = jnp.where(pl.program_id(0) < first_tiles, x, x_refs[1][rows, :])
    return x


def _inproj_kernel(*refs, n_parts, first_tiles, a_blocks):
    x_refs = refs[:n_parts]
    w_ref, oa_ref, ob_ref, xb_ref = refs[n_parts:]
    j = pl.program_id(1)

    @pl.when(j == 0)
    def _():
        xb_ref[...] = _read_row_parts(x_refs, first_tiles).astype(BF16)

    res = _dot(xb_ref[...], w_ref[...])

    @pl.when(j < a_blocks)
    def _():
        oa_ref[...] = res

    @pl.when(j >= a_blocks)
    def _():
        ob_ref[...] = res.astype(ob_ref.dtype)


def _inproj_pair_kernel(x_ref, w0_ref, w1_ref, oa_ref, oqk_ref, ov_ref, xb_ref, *, a_steps):
    j = pl.program_id(1)
    tn = w0_ref.shape[1]

    @pl.when(j == 0)
    def _():
        xb_ref[...] = x_ref[...].astype(BF16)

    @pl.when(j < a_steps)
    def _():
        xb = xb_ref[...]
        oa_ref[:, :tn] = _dot(xb, w0_ref[...])
        oa_ref[:, tn:] = _dot(xb, w1_ref[...])

    @pl.when(j == a_steps)
    def _():
        xb = xb_ref[...]
        oqk_ref[:, :tn] = _dot(xb, w0_ref[...]).astype(oqk_ref.dtype)
        oqk_ref[:, tn:] = _dot(xb, w1_ref[...]).astype(oqk_ref.dtype)

    @pl.when(j == a_steps + 1)
    def _():
        ov_ref[...] = _dot(xb_ref[...], w0_ref[...]).astype(ov_ref.dtype)


def _in_proj_paired(x, w, tm, tn):
    n, k = x.shape
    tiles = w.shape[1] // tn
    a_steps = A_WIDTH // (2 * tn)
    assert tiles == 2 * a_steps + 3 and NA_WIDTH == tn
    wspec = lambda off: pl.BlockSpec((k, tn), lambda i, j: (0, jnp.minimum(2 * j + off, tiles - 1)))
    proj_a, qk, v = pl.pallas_call(
        functools.partial(_inproj_pair_kernel, a_steps=a_steps),
        grid=(n // tm, a_steps + 2),
        in_specs=[pl.BlockSpec((tm, k), lambda i, j: (i, 0)), wspec(0), wspec(1)],
        out_specs=[pl.BlockSpec((tm, 2 * tn), lambda i, j: (i, jnp.minimum(j, a_steps - 1))),
                   pl.BlockSpec((tm, 2 * tn), lambda i, j: (i, 0)),
                   pl.BlockSpec((tm, tn), lambda i, j: (i, 0))],
        out_shape=[jax.ShapeDtypeStruct((n, A_WIDTH), F32), jax.ShapeDtypeStruct((n, 2 * tn), BF16),
                   jax.ShapeDtypeStruct((n, tn), BF16)],
        scratch_shapes=[pltpu.VMEM((tm, k), BF16)],
        compiler_params=_cparams(("parallel", "arbitrary")),
        name="in_proj_paired",
    )(x, w, w)
    blocks = NA_WIDTH // LANES
    return proj_a, ((qk, 0), (qk, blocks), (v, 0))


def _in_proj(x_parts, w, tm=IN_PROJ_TM, tn=IN_PROJ_TN):
    if len(x_parts) == 1:
        return _in_proj_paired(x_parts[0], w, tm, tn)
    n = sum(a.shape[0] for a in x_parts)
    k = x_parts[0].shape[1]
    a_blocks = A_WIDTH // tn
    b_width = w.shape[1] - A_WIDTH
    first_tiles = x_parts[0].shape[0] // tm
    proj_a, qkv = pl.pallas_call(
        functools.partial(_inproj_kernel, n_parts=len(x_parts), first_tiles=first_tiles, a_blocks=a_blocks),
        grid=(n // tm, w.shape[1] // tn),
        in_specs=_row_parts_specs(x_parts, tm, k, 2) + [pl.BlockSpec((k, tn), lambda i, j: (0, j))],
        out_specs=[pl.BlockSpec((tm, tn), lambda i, j: (i, jnp.minimum(j, a_blocks - 1))),
                   pl.BlockSpec((tm, tn), lambda i, j: (i, jnp.maximum(j - a_blocks, 0)))],
        out_shape=[jax.ShapeDtypeStruct((n, A_WIDTH), F32), jax.ShapeDtypeStruct((n, b_width), BF16)],
        scratch_shapes=[pltpu.VMEM((tm, k), BF16)],
        compiler_params=_cparams(("parallel", "arbitrary")),
        name="in_proj",
    )(*x_parts, w)
    blocks = NA_WIDTH // LANES
    return proj_a, ((qkv, 0), (qkv, blocks), (qkv, 2 * blocks))


def _shift_rows(x, prev_row, next_row, first, last):
    n = x.shape[0]
    row = lax.broadcasted_iota(jnp.int32, (n, 1), 0)
    prev_row = jnp.where(first, 0.0, prev_row)
    next_row = jnp.where(last, 0.0, next_row)
    x_prev = jnp.where(row == 0, prev_row, pltpu.roll(x, 1, 0))
    x_next = jnp.where(row == n - 1, next_row, pltpu.roll(x, n - 1, 0))
    return x_prev, x_next


def _convmix_kernel(cb_ref, cc_ref, ch_ref, ccp_ref, chp_ref, ccn_ref, chn_ref, w_ref, g_ref, o_ref, *, tiles_per_seq):
    i = pl.program_id(0)
    first = (i % tiles_per_seq) == 0
    last = (i % tiles_per_seq) == tiles_per_seq - 1
    u = cc_ref[...] * ch_ref[...]
    up = ccp_ref[SUBLANES - 1:SUBLANES, :] * chp_ref[SUBLANES - 1:SUBLANES, :]
    un = ccn_ref[0:1, :] * chn_ref[0:1, :]
    u_prev, u_next = _shift_rows(u, up, un, first, last)
    conv = u_prev * w_ref[0:1, :] + u * w_ref[1:2, :] + u_next * w_ref[2:3, :]
    y = cb_ref[...] * conv
    y = y * lax.rsqrt(jnp.mean(y * y, axis=-1, keepdims=True) + RMS_EPS)
    o_ref[...] = (y * g_ref[...]).astype(o_ref.dtype)


def _halo_specs(tt, width, col0, n_rows):
    cb = col0 // width
    r = tt // SUBLANES
    last_blk = n_rows // SUBLANES - 1
    prev = pl.BlockSpec((SUBLANES, width), lambda i: (jnp.maximum(i * r - 1, 0), cb))
    nxt = pl.BlockSpec((SUBLANES, width), lambda i: (jnp.minimum((i + 1) * r, last_blk), cb))
    return prev, nxt


def _conv_mix(proj_a, conv_w, norm_g, seq, tt=CONV_TT):
    n = proj_a.shape[0]
    w = CONV_WIDTH
    main = lambda c0: pl.BlockSpec((tt, w), lambda i: (i, c0 // w))
    ccp, ccn = _halo_specs(tt, w, A_CC, n)
    chp, chn = _halo_specs(tt, w, A_CH, n)
    full = lambda shape: pl.BlockSpec(shape, lambda i: (0, 0))
    return pl.pallas_call(
        functools.partial(_convmix_kernel, tiles_per_seq=seq // tt),
        grid=(n // tt,),
        in_specs=[main(A_CB), main(A_CC), main(A_CH), ccp, chp, ccn, chn, full((3, w)), full((1, w))],
        out_specs=pl.BlockSpec((tt, w), lambda i: (i, 0)),
        out_shape=jax.ShapeDtypeStruct((n, w), BF16),
        compiler_params=_cparams(("parallel",)),
        name="conv_mix",
    )(proj_a, proj_a, proj_a, proj_a, proj_a, proj_a, proj_a, conv_w, norm_g.reshape(1, w))


def _ssd_kernel(*refs, direction, final, tiles):
    if final:
        (xs_s, bc_s, dt_ref, z_ref, yf_ref, dtb_ref, alog_ref, e_ref, hm_ref, dsk_ref, ng_ref,
         o_ref, h_ref, y_ref) = refs
    else:
        (xs_ref, xsp_ref, xsn_ref, bc_ref, bcp_ref, bcn_ref, dt_ref,
         cwx_ref, cbx_ref, cwb_ref, cbb_ref, dtb_ref, alog_ref, e_ref, hm_ref,
         o_ref, xs_s, bc_s, h_ref) = refs
        y_ref = o_ref
    L = SSD_CHUNK
    c = pl.program_id(1)
    tile = c if direction == 0 else tiles - 1 - c
    first = tile == 0
    last = tile == tiles - 1

    @pl.when(c == 0)
    def _():
        h_ref[...] = jnp.zeros_like(h_ref)

    def conv_silu(x_ref, p_ref, n_ref, w_ref, b_ref):
        x = x_ref[...]
        x_prev, x_next = _shift_rows(x, p_ref[SUBLANES - 1:SUBLANES, :], n_ref[0:1, :], first, last)
        v = x_prev * w_ref[0:1, :] + x * w_ref[1:2, :] + x_next * w_ref[2:3, :] + b_ref[...]
        return v * _sigmoid(v)

    if not final:
        xs_s[...] = conv_silu(xs_ref, xsp_ref, xsn_ref, cwx_ref, cbx_ref)
        bc_s[...] = conv_silu(bc_ref, bcp_ref, bcn_ref, cwb_ref, cbb_ref)

    li = lax.broadcasted_iota(jnp.int32, (L, L), 0)
    si = lax.broadcasted_iota(jnp.int32, (L, L), 1)
    valid = (si <= li) if direction == 0 else (si >= li)
    tri = jnp.where(valid, 1.0, 0.0).astype(BF16)
    lane = lax.broadcasted_iota(jnp.int32, (1, LANES), 1)
    nt = (((1,), (1,)), ((), ()))
    e_mat = e_ref[...]
    neg_a = -jnp.exp(alog_ref[...])
    heads_per_group = SSD_HEADS // SSD_GROUPS
    chunks = xs_s.shape[0] // L
    order = range(chunks) if direction == 0 else range(chunks - 1, -1, -1)

    for k in order:
        rows = slice(k * L, (k + 1) * L)
        xs = xs_s[rows, :]
        bm = bc_s[rows, :LANES]
        cm = bc_s[rows, LANES:]

        dtr = dt_ref[rows, :] + dtb_ref[...]
        dt = jnp.maximum(dtr, 0.0) + _log1p(jnp.exp(-jnp.abs(dtr)))
        acum = _dot_exact_rhs(tri, dt * neg_a)
        acum_t = acum.T
        total = acum[L - 1:L, :] if direction == 0 else acum[0:1, :]
        dec_end = jnp.exp(total - acum)
        ea = jnp.exp(acum)

        dtx = _dot_split2_lhs(dt, e_mat)
        eax = _dot_split2_lhs(ea, e_mat)
        dex = _dot_split2_lhs(dec_end * dt, e_mat)

        xdt = xs * dtx
        xw = (xs * dex).astype(BF16)
        cm_b = cm.astype(BF16)
        bm_b = bm.astype(BF16)

        h_prev = h_ref[...]
        y_off = _dot(cm_b, h_prev.astype(BF16)) * eax

        for g in range(SSD_GROUPS):
            cg = jnp.where((lane // SSD_STATE) == g, cm, 0.0).astype(BF16)
            cb = lax.dot_general(cg, bm_b, nt, preferred_element_type=F32)
            for pair in range(heads_per_group // 2):
                acc = None
                for half in range(2):
                    h = g * heads_per_group + pair * 2 + half
                    hl = direction * SSD_HEADS + h
                    seg = acum[:, hl:hl + 1] - acum_t[hl:hl + 1, :]
                    w = (jnp.where(valid, jnp.exp(seg), 0.0) * cb).astype(BF16)
                    c0 = (h // 2) * LANES
                    rhs = jnp.where((lane // SSD_HEAD_DIM) == half, xdt[:, c0:c0 + LANES], 0.0).astype(BF16)
                    part = _dot(w, rhs)
                    acc = part if acc is None else acc + part
                c0 = (g * heads_per_group // 2 + pair) * LANES
                y_pair = acc + y_off[:, c0:c0 + LANES]
                if final:
                    y_pair = yf_ref[rows, c0:c0 + LANES] + y_pair + dsk_ref[:, c0:c0 + LANES] * xs[:, c0:c0 + LANES]
                    zz = z_ref[rows, c0:c0 + LANES]
                    y_pair = y_pair * (zz * _sigmoid(zz))
                y_ref[rows, c0:c0 + LANES] = y_pair

        cd = eax[L - 1:L, :] if direction == 0 else eax[0:1, :]
        s_new = _dot(bm.T.astype(BF16), xw)
        h_ref[...] = (cd * h_prev + s_new) * hm_ref[...]

    if final:
        gw = SSD_WIDTH // SSD_GROUPS
        for g in range(SSD_GROUPS):
            y = y_ref[:, g * gw:(g + 1) * gw]
            y = y * lax.rsqrt(jnp.mean(y * y, axis=-1, keepdims=True) + RMS_EPS)
            o_ref[:, g * gw:(g + 1) * gw] = (y * ng_ref[:, g * gw:(g + 1) * gw]).astype(o_ref.dtype)


def _ssd_pass(proj_a, fwd_outs, params, seq, direction):
    n = proj_a.shape[0]
    T = SSD_CHUNK * SSD_CHUNKS_PER_STEP
    tiles = seq // T
    nseq = n // seq
    final = direction == 1
    cw, cb_, dtb, alog, e_mat, hmask, dsk, ng = params
    bcw = 2 * SSD_GROUPS * SSD_STATE

    def row_blk(b, c):
        t = c if direction == 0 else tiles - 1 - c
        return b * tiles + t

    def main(width, col0):
        return pl.BlockSpec((T, width), lambda b, c: (row_blk(b, c), col0 // width))

    r = T // SUBLANES
    last_blk = n // SUBLANES - 1

    def halo(width, col0):
        prev = pl.BlockSpec((SUBLANES, width), lambda b, c: (jnp.maximum(row_blk(b, c) * r - 1, 0), col0 // width))
        nxt = pl.BlockSpec((SUBLANES, width), lambda b, c: (jnp.minimum((row_blk(b, c) + 1) * r, last_blk), col0 // width))
        return prev, nxt

    xsp, xsn = halo(SSD_WIDTH, A_XS)
    bcp, bcn = halo(bcw, A_BC)
    full = lambda a: pl.BlockSpec(a.shape, lambda b, c: (0,) * a.ndim)

    rows = lambda width: pl.BlockSpec((T, width), lambda b, c: (row_blk(b, c), 0))
    if final:
        y_fwd, xs_act, bc_act = fwd_outs
        in_specs = [rows(SSD_WIDTH), rows(bcw), main(LANES, A_DT), main(SSD_WIDTH, A_Z), rows(SSD_WIDTH)]
        args = [xs_act, bc_act, proj_a, proj_a, y_fwd]
        consts = [dtb, alog, e_mat, hmask, dsk, ng]
        out_specs = rows(SSD_WIDTH)
        out_shape = jax.ShapeDtypeStruct((n, SSD_WIDTH), BF16)
    else:
        in_specs = [main(SSD_WIDTH, A_XS), xsp, xsn, main(bcw, A_BC), bcp, bcn, main(LANES, A_DT)]
        args = [proj_a] * 7
        consts = [cw[:, :SSD_WIDTH], cb_[:, :SSD_WIDTH], cw[:, SSD_WIDTH:], cb_[:, SSD_WIDTH:], dtb, alog, e_mat, hmask]
        out_specs = [rows(SSD_WIDTH), rows(SSD_WIDTH), rows(bcw)]
        out_shape = [jax.ShapeDtypeStruct((n, SSD_WIDTH), F32), jax.ShapeDtypeStruct((n, SSD_WIDTH), F32),
                     jax.ShapeDtypeStruct((n, bcw), F32)]
    in_specs += [full(a) for a in consts]
    args += consts
    scratch = [pltpu.VMEM((LANES, SSD_WIDTH), F32)]
    if final:
        scratch.append(pltpu.VMEM((T, SSD_WIDTH), F32))
    return pl.pallas_call(
        functools.partial(_ssd_kernel, direction=direction, final=final, tiles=tiles),
        grid=(nseq, tiles),
        in_specs=in_specs,
        out_specs=out_specs,
        out_shape=out_shape,
        scratch_shapes=scratch,
        compiler_params=_cparams(("parallel", "arbitrary")),
        name="ssd_bwd" if final else "ssd_fwd",
    )(*args)


def _ssd_consts(ssd_conv_w, ssd_conv_b, ssd_dt_bias, ssd_a_log, ssd_d, ssd_norm_g):
    pad = LANES - 2 * SSD_HEADS
    dtb = jnp.pad(ssd_dt_bias.reshape(1, -1), ((0, 0), (0, pad)))
    alog = jnp.pad(ssd_a_log.reshape(1, -1), ((0, 0), (0, pad)))
    col_head = np.arange(SSD_WIDTH) // SSD_HEAD_DIM
    e_mats = [jnp.asarray((np.arange(LANES)[:, None] == d * SSD_HEADS + col_head[None, :]), BF16) for d in range(2)]
    row_group = np.arange(LANES) // SSD_STATE
    col_group = col_head // (SSD_HEADS // SSD_GROUPS)
    hmask = jnp.asarray(row_group[:, None] == col_group[None, :], F32)
    dsk = jnp.repeat(ssd_d, SSD_HEAD_DIM).reshape(1, SSD_WIDTH)
    ng = ssd_norm_g.reshape(1, SSD_WIDTH)
    cb_ = ssd_conv_b.reshape(1, -1)
    return [(ssd_conv_w, cb_, dtb, alog, e_mats[d], hmask, dsk, ng) for d in range(2)]


NA_KROWS = NA_QROWS + WIN_H


def _na_kernel(kb_ref, cls_ref, q_ref, k_ref, v_ref, bias_ref, o_ref, *, blocks_per_step):
    step = pl.program_id(2)
    lane = lax.broadcasted_iota(jnp.int32, (1, LANES), 1)
    nt = (((1,), (1,)), ((), ()))
    nq = NA_QROWS * GRID_W
    nk = NA_KROWS * GRID_W
    for j in range(blocks_per_step):
        blk = step * blocks_per_step + j
        k0 = pl.multiple_of(kb_ref[blk] * GRID_W, GRID_W)
        cls = cls_ref[blk]
        q = q_ref[j * nq:(j + 1) * nq, :] * (NA_HEAD_DIM ** -0.5)
        kk = k_ref[pl.ds(k0, nk), :]
        vv = v_ref[pl.ds(k0, nk), :]
        outs = []
        for hh in range(2):
            qm = jnp.where((lane // NA_HEAD_DIM) == hh, q, jnp.zeros_like(q))
            s = lax.dot_general(qm, kk, nt, preferred_element_type=F32) + bias_ref[cls, hh]
            m = jnp.max(s, axis=-1, keepdims=True)
            p = jnp.exp(s - m)
            l = jnp.sum(p, axis=-1, keepdims=True)
            outs.append(_dot(p.astype(BF16), vv) / l)
        o = jnp.where((lane // NA_HEAD_DIM) == 0, outs[0], outs[1])
        o_ref[j * nq:(j + 1) * nq, :] = o.astype(o_ref.dtype)


def _na_tables(rel_bias, rows):
    qc = np.arange(GRID_W)
    cs = np.clip(qc - WIN_W // 2, 0, GRID_W - WIN_W)
    kc = np.arange(GRID_W)
    col_valid = (kc[None, :] >= cs[:, None]) & (kc[None, :] < cs[:, None] + WIN_W)
    dj = np.clip(kc[None, :] - qc[:, None] + WIN_W - 1, 0, 2 * WIN_W - 2)
    nblk = rows // NA_QROWS
    kb = np.clip(np.arange(nblk) * NA_QROWS - WIN_H // 2, 0, rows - NA_KROWS)
    patterns, cls = [], []
    for b in range(nblk):
        r = b * NA_QROWS + np.arange(NA_QROWS)
        rs = np.clip(r - WIN_H // 2, 0, rows - WIN_H)
        key = (tuple(rs - kb[b]), int(kb[b] - b * NA_QROWS))
        if key not in patterns:
            patterns.append(key)
        cls.append(patterns.index(key))
    nh, nw = 2 * WIN_H - 1, 2 * WIN_W - 1
    bias_ext = jnp.pad(rel_bias.astype(F32), ((0, 0), (0, 1), (0, 1)), constant_values=NEG_BIG)
    di_sel = []
    for off, shift in patterns:
        kr = np.arange(NA_KROWS)[None, :]
        i = np.arange(NA_QROWS)[:, None]
        rel = kr - np.asarray(off)[:, None]
        row_valid = (rel >= 0) & (rel < WIN_H)
        di = np.where(row_valid, np.clip(shift + kr - i + WIN_H - 1, 0, nh - 1), nh)
        di_sel.append(np.eye(nh + 1, dtype=np.float32)[di])
    dj_sel = np.eye(nw + 1, dtype=np.float32)[np.where(col_valid, dj, nw)]
    tab = jnp.einsum('pikd,hde,qwe->phiqkw', np.stack(di_sel), bias_ext, dj_sel, precision=lax.Precision.HIGHEST)
    tab = tab.reshape(len(patterns), NA_HEADS, NA_QROWS * GRID_W, NA_KROWS * GRID_W)
    return jnp.asarray(kb, jnp.int32), jnp.asarray(cls, jnp.int32), tab


def _neighbourhood_attention(qkv, rel_bias, seq, blocks_per_step=NA_BLOCKS_PER_STEP):
    (q_arr, q0), (k_arr, k0), (v_arr, v0) = qkv
    n = q_arr.shape[0]
    nseq = n // seq
    rows = seq // GRID_W
    kb, cls, bias_tab = _na_tables(rel_bias, rows)
    tq = blocks_per_step * NA_QROWS * GRID_W
    pairs = NA_HEADS // 2
    steps = seq // tq
    ncls = bias_tab.shape[0]
    grid_spec = pltpu.PrefetchScalarGridSpec(
        num_scalar_prefetch=2,
        grid=(nseq, pairs, steps),
        in_specs=[pl.BlockSpec((tq, LANES), lambda b, p, r, kb_, cls_: (b * steps + r, q0 + p)),
                  pl.BlockSpec((seq, LANES), lambda b, p, r, kb_, cls_: (b, k0 + p)),
                  pl.BlockSpec((seq, LANES), lambda b, p, r, kb_, cls_: (b, v0 + p)),
                  pl.BlockSpec((ncls, 2) + bias_tab.shape[2:], lambda b, p, r, kb_, cls_: (0, p, 0, 0))],
        out_specs=pl.BlockSpec((tq, LANES), lambda b, p, r, kb_, cls_: (b * steps + r, p)),
    )
    return pl.pallas_call(
        functools.partial(_na_kernel, blocks_per_step=blocks_per_step),
        grid_spec=grid_spec,
        out_shape=jax.ShapeDtypeStruct((n, NA_WIDTH), F32),
        compiler_params=_cparams(("parallel", "arbitrary", "arbitrary")),
        name="nbr_attn",
    )(kb, cls, q_arr, k_arr, v_arr, bias_tab)


def _rms_rows(y, g):
    return y * lax.rsqrt(jnp.mean(y * y, axis=-1, keepdims=True) + RMS_EPS) * g


def _outproj_kernel(*refs, router, n_parts, first_tiles):
    yc_ref, ys_ref, yn_ref = refs[:3]
    x_refs = refs[3:3 + n_parts]
    rest = refs[3 + n_parts:]
    if router:
        w_ref, nag_ref, g_ref, b_ref, rw_ref, o_ref, idx_ref, gate_ref = rest
    else:
        w_ref, nag_ref, g_ref, b_ref, o_ref = rest
    c1 = CONV_WIDTH
    c2 = CONV_WIDTH + SSD_WIDTH
    rows = o_ref.shape[0] // OUTPROJ_ROW_SPLITS
    for part in range(OUTPROJ_ROW_SPLITS):
        sl = slice(part * rows, (part + 1) * rows)
        yn = _rms_rows(yn_ref[sl, :], nag_ref[...]).astype(BF16)
        mix = _dot(yc_ref[sl, :], w_ref[0:c1, :]) + _dot(ys_ref[sl, :], w_ref[c1:c2, :]) + _dot(yn, w_ref[c2:, :])
        out = _layer_norm(ALPHA * _read_row_parts(x_refs, first_tiles, sl) + mix, g_ref[...], b_ref[...])
        o_ref[sl, :] = out
        if not router:
            continue
        o1, o2, _ = _split3(out)
        t1 = _dot(o1, rw_ref[...])
        t2 = _dot(o2, rw_ref[...])
        sh1 = LANES - N_EXPERTS
        sh2 = LANES - 2 * N_EXPERTS
        logits = ((pltpu.roll(t2, sh1, 1) + pltpu.roll(t1, sh2, 1)) + (t2 + pltpu.roll(t1, sh1, 1))) + t1
        lane = lax.broadcasted_iota(jnp.int32, (rows, LANES), 1)
        lg = jnp.where(lane < N_EXPERTS, logits, -jnp.inf)
        m1 = jnp.max(lg, axis=-1, keepdims=True)
        i1 = jnp.min(jnp.where(lg == m1, lane, LANES), axis=-1, keepdims=True)
        lg2 = jnp.where(lane == i1, -jnp.inf, lg)
        m2 = jnp.max(lg2, axis=-1, keepdims=True)
        i2 = jnp.min(jnp.where(lg2 == m2, lane, LANES), axis=-1, keepdims=True)
        e2 = jnp.exp(m2 - m1)
        den = 1.0 + e2
        idx_ref[sl, :] = jnp.where(lane == 0, i1, jnp.where(lane == 1, i2, 0))
        gate_ref[sl, :] = jnp.where(lane == 0, 1.0 / den, jnp.where(lane == 1, e2 / den, 0.0))


def _out_proj_ln(yc, ys, yn, x_parts, w_out, na_g, ln_g, ln_b, router_w=None, tm=ROW_TILE):
    n = yc.shape[0]
    router = router_w is not None
    row = lambda w: pl.BlockSpec((tm, w), lambda i: (i, 0))
    full = lambda a: pl.BlockSpec(a.shape, lambda i: (0,) * a.ndim)
    consts = [w_out, na_g.reshape(1, -1), ln_g.reshape(1, -1), ln_b.reshape(1, -1)]
    if router:
        consts.append(router_w)
    out_shape = [jax.ShapeDtypeStruct((n, D_MODEL), F32)]
    out_specs = [row(D_MODEL)]
    if router:
        out_shape += [jax.ShapeDtypeStruct((n, LANES), jnp.int32), jax.ShapeDtypeStruct((n, LANES), F32)]
        out_specs += [row(LANES), row(LANES)]
    res = pl.pallas_call(
        functools.partial(_outproj_kernel, router=router, n_parts=len(x_parts),
                          first_tiles=x_parts[0].shape[0] // tm),
        grid=(n // tm,),
        in_specs=[row(CONV_WIDTH), row(SSD_WIDTH), row(NA_WIDTH)] + _row_parts_specs(x_parts, tm, D_MODEL, 1)
        + [full(a) for a in consts],
        out_specs=out_specs,
        out_shape=out_shape,
        compiler_params=_cparams(("parallel",)),
        name="out_proj_router" if router else "out_proj",
    )(yc, ys, yn, *x_parts, *consts)
    return res


def _ffn_kernel(be_ref, bv_ref, x_ref, wg_ref, wu_ref, wd_ref, *rest, packed, ln, ragged):
    if ln:
        g_ref, b_ref, o_ref = rest
    else:
        (o_ref,) = rest
    i = pl.program_id(0)
    j = pl.program_id(1)

    @pl.when(j == 0)
    def _():
        o_ref[...] = jnp.zeros_like(o_ref)

    def accumulate(rows):
        if packed:
            lo, hi = _unpack_bf16_pairs(x_ref[rows, :])
            c = lo.shape[1]
            g = _dot(lo, wg_ref[0, :c, :].astype(BF16)) + _dot(hi, wg_ref[0, c:, :].astype(BF16))
            u = _dot(lo, wu_ref[0, :c, :].astype(BF16)) + _dot(hi, wu_ref[0, c:, :].astype(BF16))
        else:
            x = x_ref[rows, :].astype(BF16)
            g = _dot(x, wg_ref[0].astype(BF16))
            u = _dot(x, wu_ref[0].astype(BF16))
        h = ((g * _sigmoid(g)) * u).astype(BF16)
        o_ref[rows, :] += _dot(h, wd_ref[0].astype(BF16))

    tm = o_ref.shape[0]
    filled = bv_ref[i]

    @pl.when(filled == tm)
    def _():
        accumulate(slice(None))

    if ragged:
        for s in range(tm // FFN_ROW_GROUP):
            @pl.when(jnp.logical_and(filled < tm, filled > s * FFN_ROW_GROUP))
            def _():
                accumulate(slice(s * FFN_ROW_GROUP, (s + 1) * FFN_ROW_GROUP))

    if ln:
        @pl.when(j == pl.num_programs(1) - 1)
        def _():
            o_ref[...] = _layer_norm(ALPHA * x_ref[...] + o_ref[...], g_ref[...], b_ref[...])


def _ffn_blocks(x, block_e, block_fill, w_gate, w_up, w_down, tm, tf=FF_TILE, residual_ln=None, ragged=False):
    n, xw = x.shape
    packed = x.dtype == jnp.uint32
    d = w_gate.shape[1]
    f = w_gate.shape[2]
    nf = f // tf
    col = lambda i, j, be, bv: (be[i], 0, jnp.where(bv[i] != 0, j, nf - 1))
    rowj = lambda i, j, be, bv: (be[i], jnp.where(bv[i] != 0, j, nf - 1), 0)
    in_specs = [pl.BlockSpec((tm, xw), lambda i, j, be, bv: (i, 0)),
                pl.BlockSpec((1, d, tf), col), pl.BlockSpec((1, d, tf), col), pl.BlockSpec((1, tf, d), rowj)]
    args = [x, w_gate, w_up, w_down]
    if residual_ln is not None:
        gain, bias = residual_ln
        assert not packed
        vec = pl.BlockSpec((1, d), lambda i, j, be, bv: (0, 0))
        in_specs += [vec, vec]
        args += [gain.reshape(1, d), bias.reshape(1, d)]
    grid_spec = pltpu.PrefetchScalarGridSpec(
        num_scalar_prefetch=2,
        grid=(n // tm, nf),
        in_specs=in_specs,
        out_specs=pl.BlockSpec((tm, d), lambda i, j, be, bv: (i, 0)),
    )
    return pl.pallas_call(
        functools.partial(_ffn_kernel, packed=packed, ln=residual_ln is not None, ragged=ragged),
        grid_spec=grid_spec,
        out_shape=jax.ShapeDtypeStruct((n, d), F32),
        compiler_params=_cparams(("parallel", "arbitrary")),
        name="swiglu_blocks",
    )(block_e, block_fill, *args)


def _combine_ln_kernel(i0_ref, i1_ref, n0_ref, n1_ref, src_ref, x_ref, gate_ref, g_ref, b_ref, oa_ref, ob_ref,
                       b0_ref, b1_ref, sem, *, tm, first_tiles):
    i = pl.program_id(0)
    slot = i % 2

    def start(idx0_ref, idx1_ref, s):
        _start_row_gather(src_ref, idx0_ref, b0_ref.at[s], sem.at[s, 0], tm)
        _start_row_gather(src_ref, idx1_ref, b1_ref.at[s], sem.at[s, 1], tm)

    @pl.when(i == 0)
    def _():
        start(i0_ref, i1_ref, 0)

    @pl.when(i + 1 < pl.num_programs(0))
    def _():
        start(n0_ref, n1_ref, 1 - slot)

    _wait_row_gather(src_ref, b0_ref.at[slot], sem.at[slot, 0], tm)
    _wait_row_gather(src_ref, b1_ref.at[slot], sem.at[slot, 1], tm)
    f = b0_ref[slot] * gate_ref[:, 0:1] + b1_ref[slot] * gate_ref[:, 1:2]
    res = _layer_norm(ALPHA * x_ref[...] + f, g_ref[...], b_ref[...])

    @pl.when(i < first_tiles)
    def _():
        oa_ref[...] = res

    @pl.when(i >= first_tiles)
    def _():
        ob_ref[...] = res


def _combine_ln(x, outs, d0, d1, gates, g, b, n_first, tm=ROW_TILE):
    n, d = x.shape
    nt = n // tm
    first_tiles = n_first // tm
    row = pl.BlockSpec((tm, d), lambda i: (i, 0))
    vec = pl.BlockSpec((1, d), lambda i: (0, 0))
    idx = pl.BlockSpec((1, 1, tm), lambda i: (i, 0, 0), memory_space=pltpu.SMEM)
    idx_next = pl.BlockSpec((1, 1, tm), lambda i: (jnp.minimum(i + 1, nt - 1), 0, 0), memory_space=pltpu.SMEM)
    d0, d1 = d0.reshape(nt, 1, tm), d1.reshape(nt, 1, tm)
    return pl.pallas_call(
        functools.partial(_combine_ln_kernel, tm=tm, first_tiles=first_tiles), grid=(nt,),
        in_specs=[idx, idx, idx_next, idx_next, pl.BlockSpec(memory_space=pl.ANY), row,
                  pl.BlockSpec((tm, LANES), lambda i: (i, 0)), vec, vec],
        out_specs=[pl.BlockSpec((tm, d), lambda i: (jnp.minimum(i, first_tiles - 1), 0)),
                   pl.BlockSpec((tm, d), lambda i: (jnp.maximum(i - first_tiles, 0), 0))],
        out_shape=[jax.ShapeDtypeStruct((n_first, d), F32), jax.ShapeDtypeStruct((n - n_first, d), F32)],
        scratch_shapes=[pltpu.VMEM((2, tm, d), F32), pltpu.VMEM((2, tm, d), F32), pltpu.SemaphoreType.DMA((2, 2))],
        compiler_params=_cparams(("arbitrary",)), name="combine_ln",
    )(d0, d1, d0, d1, outs, x, gates, g.reshape(1, d), b.reshape(1, d))


def _start_row_gather(src_ref, idx_ref, buf_ref, sem, tr):
    def issue(r, carry):
        pltpu.make_async_copy(src_ref.at[pl.ds(idx_ref[0, 0, r], 1), :], buf_ref.at[pl.ds(r, 1), :], sem).start()
        return carry

    lax.fori_loop(0, tr, issue, 0, unroll=GATHER_UNROLL)


def _wait_row_gather(src_ref, buf_ref, sem, tr):
    pltpu.make_async_copy(src_ref.at[pl.ds(0, tr), :], buf_ref, sem).wait()


HI16 = 0xFFFF0000


def _pack_bf16_pairs(x):
    c = x.shape[1] // 2
    lo = lax.bitcast_convert_type(x[:, :c].astype(BF16).astype(F32), jnp.uint32)
    hi = lax.bitcast_convert_type(x[:, c:].astype(BF16).astype(F32), jnp.uint32)
    return (hi & jnp.uint32(HI16)) | (lo >> 16)


def _unpack_bf16_pairs(u):
    lo = lax.bitcast_convert_type(u << 16, F32).astype(BF16)
    hi = lax.bitcast_convert_type(u & jnp.uint32(HI16), F32).astype(BF16)
    return lo, hi


def _zero_row_runs(first, length, max_len, z_ref, o_ref, sem):
    zrows = z_ref.shape[0]

    def copy(row, rows):
        return pltpu.make_async_copy(z_ref.at[pl.ds(0, rows), :], o_ref.at[pl.ds(row, rows), :], sem)

    def for_each_run(action):
        head = jnp.minimum((-first) % SUBLANES, length)
        tail = (length - head) % SUBLANES
        body = length - head - tail
        for k in range(SUBLANES - 1):
            @pl.when(k < head)
            def _(k=k):
                action(copy(first + k, 1))
        row = first + head
        for bit in range(SUBLANES.bit_length() - 1, max_len.bit_length()):
            size = 1 << bit
            take = (body >> bit) & 1
            rows = min(size, zrows)
            for piece in range(size // rows):
                @pl.when(take == 1)
                def _(row=row, piece=piece, rows=rows):
                    action(copy(pl.multiple_of(row + piece * rows, SUBLANES), rows))
            row = row + take * size
        for k in range(SUBLANES - 1):
            @pl.when(k < tail)
            def _(k=k, row=row):
                action(copy(row + k, 1))

    for_each_run(lambda c: c.start())
    for_each_run(lambda c: c.wait())


def _dispatch_kernel(pad_start_ref, pad_len_ref, d0_ref, d1_ref, x_ref, xs_ref, pk_ref, z_ref, sem, zsem,
                     *, tm, max_pad):
    @pl.when(pl.program_id(0) == 0)
    def _():
        z_ref[...] = jnp.zeros_like(z_ref)

        def zero_region(r, carry):
            _zero_row_runs(pad_start_ref[r], pad_len_ref[r], max_pad, z_ref, xs_ref, zsem)
            return carry

        lax.fori_loop(0, pad_start_ref.shape[0], zero_region, 0)

    pk_ref[...] = _pack_bf16_pairs(x_ref[...])

    def issue(r, carry):
        row = pk_ref.at[pl.ds(r, 1), :]
        pltpu.make_async_copy(row, xs_ref.at[pl.ds(d0_ref[0, 0, r], 1), :], sem.at[0]).start()
        pltpu.make_async_copy(row, xs_ref.at[pl.ds(d1_ref[0, 0, r], 1), :], sem.at[1]).start()
        return carry

    lax.fori_loop(0, tm, issue, 0, unroll=GATHER_UNROLL)
    for k in range(TOP_K):
        pltpu.make_async_copy(pk_ref, xs_ref.at[pl.ds(0, tm), :], sem.at[k]).wait()


def _dispatch_rows(x, d0, d1, pad_start, pad_len, p_len, max_pad, tm=ROW_TILE):
    n, d = x.shape
    idx = pl.BlockSpec((1, 1, tm), lambda i, ps, pn: (i, 0, 0), memory_space=pltpu.SMEM)
    grid_spec = pltpu.PrefetchScalarGridSpec(
        num_scalar_prefetch=2,
        grid=(n // tm,),
        in_specs=[idx, idx, pl.BlockSpec((tm, d), lambda i, ps, pn: (i, 0))],
        out_specs=pl.BlockSpec(memory_space=pl.ANY),
        scratch_shapes=[pltpu.VMEM((tm, d // 2), jnp.uint32), pltpu.VMEM((tm, d // 2), jnp.uint32),
                        pltpu.SemaphoreType.DMA((TOP_K,)), pltpu.SemaphoreType.DMA(())],
    )
    return pl.pallas_call(
        functools.partial(_dispatch_kernel, tm=tm, max_pad=max_pad),
        grid_spec=grid_spec,
        out_shape=jax.ShapeDtypeStruct((p_len, d // 2), jnp.uint32),
        compiler_params=_cparams(("arbitrary",)),
        name="dispatch_rows",
    )(pad_start, pad_len, d0.reshape(n // tm, 1, tm), d1.reshape(n // tm, 1, tm), x)


def _routing_tables(idx, n, tm):
    flat_e = idx[:, :TOP_K].reshape(-1)
    onehot = (flat_e[:, None] == jnp.arange(N_EXPERTS, dtype=jnp.int32)[None, :]).astype(jnp.int32)
    incl = jnp.cumsum(onehot, axis=0)
    rank = jnp.sum((incl - onehot) * onehot, axis=1)
    counts = incl[-1]
    padded = ((counts + tm - 1) // tm) * tm
    pend = jnp.cumsum(padded)
    pstart = pend - padded
    dest = pstart[flat_e] + rank
    n_blocks = -(-(n * TOP_K) // tm) + N_EXPERTS
    starts = jnp.arange(n_blocks, dtype=jnp.int32) * tm
    block_e = jnp.minimum(jnp.searchsorted(pend, starts, side='right'), N_EXPERTS - 1).astype(jnp.int32)
    block_fill = jnp.clip((pstart + counts)[block_e] - starts, 0, tm).astype(jnp.int32)
    block_fill = jnp.where(starts < pend[-1], block_fill, 0)
    last_e = block_e[jnp.maximum(jnp.sum((block_fill > 0).astype(jnp.int32)) - 1, 0)]
    block_e = jnp.where(block_fill > 0, block_e, last_e)
    dest2 = dest.reshape(n, TOP_K)
    pad_start = jnp.concatenate([pstart + counts, pend[-1:]]).astype(jnp.int32)
    pad_len = jnp.concatenate([padded - counts, n_blocks * tm - pend[-1:]]).astype(jnp.int32)
    return (pad_start, pad_len, n_blocks * tm), block_e, block_fill, dest2[:, 0], dest2[:, 1]


def _prep_w_in(w):
    bc_w = 2 * SSD_GROUPS * SSD_STATE
    widths = (CONV_WIDTH, CONV_WIDTH, CONV_WIDTH, SSD_WIDTH, SSD_WIDTH, bc_w, 2 * SSD_HEADS, 3 * NA_WIDTH)
    cb, cc, ch, z, xs, bc, dt, qkv = jnp.split(w, [int(o) for o in np.cumsum(widths)[:-1]], axis=1)
    pad = jnp.zeros((D_MODEL, A_WIDTH - (A_DT + 2 * SSD_HEADS)), w.dtype)
    return jnp.concatenate([z, xs, cb, cc, ch, bc, dt, pad, qkv], axis=1).astype(BF16)


def _prep_router_w(rw):
    r1 = rw.astype(BF16)
    r2 = (rw - r1.astype(F32)).astype(BF16)
    r3 = (rw - r1.astype(F32) - r2.astype(F32)).astype(BF16)
    pad = jnp.zeros((rw.shape[0], LANES - 3 * N_EXPERTS), BF16)
    return jnp.concatenate([r1, r2, r3, pad], axis=1)


def _trunk(x_parts, seq, p):
    n = sum(a.shape[0] for a in x_parts)
    n_first = x_parts[0].shape[0]
    for l in range(DEPTH):
        proj_a, qkv = _in_proj(x_parts, _prep_w_in(p['w_in'][l]))
        y_conv = _conv_mix(proj_a, p['conv_w'][l], p['conv_norm_g'][l], seq)
        cf, cbw = _ssd_consts(p['ssd_conv_w'][l], p['ssd_conv_b'][l], p['ssd_dt_bias'][l], p['ssd_a_log'][l],
                              p['ssd_d'][l], p['ssd_norm_g'][l])
        y_ssd = _ssd_pass(proj_a, _ssd_pass(proj_a, None, cf, seq, 0), cbw, seq, 1)
        y_na = _neighbourhood_attention(qkv, p['na_rel_bias'][l], seq)
        w_out = p['w_out'][l].astype(BF16)
        if l % 2 == 0:
            x = _out_proj_ln(y_conv, y_ssd, y_na, x_parts, w_out, p['na_norm_g'][l], p['ln_mix_g'][l],
                             p['ln_mix_b'][l])[0]
            padf = ((0, 0), (0, D_FF_PAD - D_FF))
            wg = jnp.pad(p['ffn_w_gate'][l // 2], padf).astype(BF16)[None]
            wu = jnp.pad(p['ffn_w_up'][l // 2], padf).astype(BF16)[None]
            wd = jnp.pad(p['ffn_w_down'][l // 2], ((0, D_FF_PAD - D_FF), (0, 0))).astype(BF16)[None]
            nb = n // MOE_TM
            x = _ffn_blocks(x, jnp.zeros((nb,), jnp.int32), jnp.full((nb,), MOE_TM, jnp.int32), wg, wu, wd, MOE_TM,
                            residual_ln=(p['ln_ffn_g'][l], p['ln_ffn_b'][l]))
            outs_split = (x[:n_first], x[n_first:])
        else:
            x, idx, gates = _out_proj_ln(y_conv, y_ssd, y_na, x_parts, w_out, p['na_norm_g'][l], p['ln_mix_g'][l],
                                         p['ln_mix_b'][l], _prep_router_w(p['router_w'][l // 2]))
            (pad_start, pad_len, p_len), block_e, block_fill, d0, d1 = _routing_tables(idx, n, MOE_TM)
            xs = _dispatch_rows(x, d0, d1, pad_start, pad_len, p_len, N_EXPERTS * MOE_TM)
            outs = _ffn_blocks(xs, block_e, block_fill, p['moe_w_gate'][l // 2], p['moe_w_up'][l // 2],
                               p['moe_w_down'][l // 2], MOE_TM, ragged=True)
            outs_split = _combine_ln(x, outs, d0, d1, gates, p['ln_ffn_g'][l], p['ln_ffn_b'][l], n_first)
            if l + 1 < DEPTH:
                x = jnp.concatenate(outs_split, axis=0)
        x_parts = [x]
    return outs_split


def kernel(x_prompt, x_sample, w_in, conv_w, conv_norm_g, ssd_conv_w, ssd_conv_b, ssd_dt_bias, ssd_a_log, ssd_d,
           ssd_norm_g, na_rel_bias, na_norm_g, w_out, ln_mix_g, ln_mix_b, ln_ffn_g, ln_ffn_b, ffn_w_gate, ffn_w_up,
           ffn_w_down, router_w, moe_w_gate, moe_w_up, moe_w_down):
    p = dict(w_in=w_in, conv_w=conv_w, conv_norm_g=conv_norm_g, ssd_conv_w=ssd_conv_w, ssd_conv_b=ssd_conv_b,
             ssd_dt_bias=ssd_dt_bias, ssd_a_log=ssd_a_log, ssd_d=ssd_d, ssd_norm_g=ssd_norm_g,
             na_rel_bias=na_rel_bias, na_norm_g=na_norm_g, w_out=w_out, ln_mix_g=ln_mix_g, ln_mix_b=ln_mix_b,
             ln_ffn_g=ln_ffn_g, ln_ffn_b=ln_ffn_b, ffn_w_gate=ffn_w_gate, ffn_w_up=ffn_w_up, ffn_w_down=ffn_w_down,
             router_w=router_w, moe_w_gate=moe_w_gate, moe_w_up=moe_w_up, moe_w_down=moe_w_down)
    bp, seq, d = x_prompt.shape
    bs, seq_s, _ = x_sample.shape
    assert seq == seq_s, "both request groups must share one sequence length"
    yp, ys = _trunk([x_prompt.reshape(bp * seq, d), x_sample.reshape(bs * seq, d)], seq, p)
    return yp.reshape(bp, seq, d), ys.reshape(bs, seq, d)
```

```python
import functools

import numpy as np
import jax
import jax.numpy as jnp
from jax import lax
from jax.experimental import pallas as pl
from jax.experimental.pallas import tpu as pltpu

F32 = jnp.float32
BF16 = jnp.bfloat16

D_MODEL = 2048
DEPTH = 2
GRID_W = 64
CONV_WIDTH = 512
SSD_HEADS = 16
SSD_HEAD_DIM = 64
SSD_WIDTH = 1024
SSD_GROUPS = 2
SSD_STATE = 64
SSD_CHUNK = 128
NA_HEADS = 8
NA_HEAD_DIM = 64
NA_WIDTH = 512
WIN_H = 8
WIN_W = 16
D_FF = 5504
N_EXPERTS = 8
TOP_K = 2
D_FF_EXPERT = 7168
ALPHA = (2 * DEPTH) ** 0.25
LN_EPS = 1e-5
RMS_EPS = 1e-6

LANES = 128
SUBLANES = 8
NEG_BIG = -1e30

A_Z, A_XS, A_CB, A_CC, A_CH, A_BC, A_DT, A_WIDTH = 0, 1024, 2048, 2560, 3072, 3584, 3840, 4096

VMEM_LIMIT = 56 * 1024 * 1024
IN_PROJ_TM, IN_PROJ_TN = 1024, 512
ROW_TILE = 512
CONV_TT = 1024
SSD_CHUNKS_PER_STEP = 8
NA_QROWS = 4
NA_BLOCKS_PER_STEP = 16
OUTPROJ_ROW_SPLITS = 2
MOE_TM = 1024
FF_TILE = 512
D_FF_PAD = -(-D_FF // FF_TILE) * FF_TILE
FFN_ROW_GROUP = 256
GATHER_UNROLL = 8


def _cparams(sem, vmem=VMEM_LIMIT):
    return pltpu.CompilerParams(dimension_semantics=sem, vmem_limit_bytes=vmem)


def _sigmoid(x):
    return 1.0 / (1.0 + jnp.exp(-x))


def _split3(x):
    x1 = x.astype(BF16)
    r1 = x - x1.astype(F32)
    x2 = r1.astype(BF16)
    r2 = r1 - x2.astype(F32)
    return x1, x2, r2.astype(BF16)


def _dot(a, b):
    return jnp.dot(a, b, preferred_element_type=F32)


def _dot_exact_rhs(m, x):
    x1, x2, x3 = _split3(x)
    return (_dot(m, x3) + _dot(m, x2)) + _dot(m, x1)


def _dot_split2_lhs(x, m):
    x1, x2, _ = _split3(x)
    return _dot(x2, m) + _dot(x1, m)


def _log1p(e):
    u = 1.0 + e
    d = u - 1.0
    return jnp.where(d == 0.0, e, jnp.log(u) * (e / jnp.where(d == 0.0, 1.0, d)))


def _layer_norm(v, g, b):
    mu = jnp.mean(v, axis=-1, keepdims=True)
    c = v - mu
    var = jnp.mean(c * c, axis=-1, keepdims=True)
    return c * lax.rsqrt(var + LN_EPS) * g + b


def _row_parts_specs(parts, tm, width, grid_rank):
    if len(parts) == 1:
        return [pl.BlockSpec((tm, width), (lambda i: (i, 0)) if grid_rank == 1 else (lambda i, j: (i, 0)))]
    first_tiles = parts[0].shape[0] // tm
    if grid_rank == 1:
        return [pl.BlockSpec((tm, width), lambda i: (jnp.minimum(i, first_tiles - 1), 0)),
                pl.BlockSpec((tm, width), lambda i: (jnp.maximum(i - first_tiles, 0), 0))]
    return [pl.BlockSpec((tm, width), lambda i, j: (jnp.minimum(i, first_tiles - 1), 0)),
            pl.BlockSpec((tm, width), lambda i, j: (jnp.maximum(i - first_tiles, 0), 0))]


def _read_row_parts(x_refs, first_tiles, rows=slice(None)):
    x = x_refs[0][rows, :]
    if len(x_refs) == 2:
        x = jnp.where(pl.program_id(0) < first_tiles, x, x_refs[1][rows, :])
    return x


def _inproj_kernel(*refs, n_parts, first_tiles, a_blocks):
    x_refs = refs[:n_parts]
    w_ref, oa_ref, ob_ref, xb_ref = refs[n_parts:]
    j = pl.program_id(1)

    @pl.when(j == 0)
    def _():
        xb_ref[...] = _read_row_parts(x_refs, first_tiles).astype(BF16)

    res = _dot(xb_ref[...], w_ref[...])

    @pl.when(j < a_blocks)
    def _():
        oa_ref[...] = res

    @pl.when(j >= a_blocks)
    def _():
        ob_ref[...] = res.astype(ob_ref.dtype)


def _inproj_pair_kernel(x_ref, w0_ref, w1_ref, oa_ref, oqk_ref, ov_ref, xb_ref, *, a_steps):
    j = pl.program_id(1)
    tn = w0_ref.shape[1]

    @pl.when(j == 0)
    def _():
        xb_ref[...] = x_ref[...].astype(BF16)

    @pl.when(j < a_steps)
    def _():
        xb = xb_ref[...]
        oa_ref[:, :tn] = _dot(xb, w0_ref[...])
        oa_ref[:, tn:] = _dot(xb, w1_ref[...])

    @pl.when(j == a_steps)
    def _():
        xb = xb_ref[...]
        oqk_ref[:, :tn] = _dot(xb, w0_ref[...]).astype(oqk_ref.dtype)
        oqk_ref[:, tn:] = _dot(xb, w1_ref[...]).astype(oqk_ref.dtype)

    @pl.when(j == a_steps + 1)
    def _():
        ov_ref[...] = _dot(xb_ref[...], w0_ref[...]).astype(ov_ref.dtype)


def _in_proj_paired(x, w, tm, tn):
    n, k = x.shape
    tiles = w.shape[1] // tn
    a_steps = A_WIDTH // (2 * tn)
    assert tiles == 2 * a_steps + 3 and NA_WIDTH == tn
    wspec = lambda off: pl.BlockSpec((k, tn), lambda i, j: (0, jnp.minimum(2 * j + off, tiles - 1)))
    proj_a, qk, v = pl.pallas_call(
        functools.partial(_inproj_pair_kernel, a_steps=a_steps),
        grid=(n // tm, a_steps + 2),
        in_specs=[pl.BlockSpec((tm, k), lambda i, j: (i, 0)), wspec(0), wspec(1)],
        out_specs=[pl.BlockSpec((tm, 2 * tn), lambda i, j: (i, jnp.minimum(j, a_steps - 1))),
                   pl.BlockSpec((tm, 2 * tn), lambda i, j: (i, 0)),
                   pl.BlockSpec((tm, tn), lambda i, j: (i, 0))],
        out_shape=[jax.ShapeDtypeStruct((n, A_WIDTH), F32), jax.ShapeDtypeStruct((n, 2 * tn), BF16),
                   jax.ShapeDtypeStruct((n, tn), BF16)],
        scratch_shapes=[pltpu.VMEM((tm, k), BF16)],
        compiler_params=_cparams(("parallel", "arbitrary")),
        name="in_proj_paired",
    )(x, w, w)
    blocks = NA_WIDTH // LANES
    return proj_a, ((qk, 0), (qk, blocks), (v, 0))


def _in_proj(x_parts, w, tm=IN_PROJ_TM, tn=IN_PROJ_TN):
    if len(x_parts) == 1:
        return _in_proj_paired(x_parts[0], w, tm, tn)
    n = sum(a.shape[0] for a in x_parts)
    k = x_parts[0].shape[1]
    a_blocks = A_WIDTH // tn
    b_width = w.shape[1] - A_WIDTH
    first_tiles = x_parts[0].shape[0] // tm
    proj_a, qkv = pl.pallas_call(
        functools.partial(_inproj_kernel, n_parts=len(x_parts), first_tiles=first_tiles, a_blocks=a_blocks),
        grid=(n // tm, w.shape[1] // tn),
        in_specs=_row_parts_specs(x_parts, tm, k, 2) + [pl.BlockSpec((k, tn), lambda i, j: (0, j))],
        out_specs=[pl.BlockSpec((tm, tn), lambda i, j: (i, jnp.minimum(j, a_blocks - 1))),
                   pl.BlockSpec((tm, tn), lambda i, j: (i, jnp.maximum(j - a_blocks, 0)))],
        out_shape=[jax.ShapeDtypeStruct((n, A_WIDTH), F32), jax.ShapeDtypeStruct((n, b_width), BF16)],
        scratch_shapes=[pltpu.VMEM((tm, k), BF16)],
        compiler_params=_cparams(("parallel", "arbitrary")),
        name="in_proj",
    )(*x_parts, w)
    blocks = NA_WIDTH // LANES
    return proj_a, ((qkv, 0), (qkv, blocks), (qkv, 2 * blocks))


def _shift_rows(x, prev_row, next_row, first, last):
    n = x.shape[0]
    row = lax.broadcasted_iota(jnp.int32, (n, 1), 0)
    prev_row = jnp.where(first, 0.0, prev_row)
    next_row = jnp.where(last, 0.0, next_row)
    x_prev = jnp.where(row == 0, prev_row, pltpu.roll(x, 1, 0))
    x_next = jnp.where(row == n - 1, next_row, pltpu.roll(x, n - 1, 0))
    return x_prev, x_next


def _convmix_kernel(cb_ref, cc_ref, ch_ref, ccp_ref, chp_ref, ccn_ref, chn_ref, w_ref, g_ref, o_ref, *, tiles_per_seq):
    i = pl.program_id(0)
    first = (i % tiles_per_seq) == 0
    last = (i % tiles_per_seq) == tiles_per_seq - 1
    u = cc_ref[...] * ch_ref[...]
    up = ccp_ref[SUBLANES - 1:SUBLANES, :] * chp_ref[SUBLANES - 1:SUBLANES, :]
    un = ccn_ref[0:1, :] * chn_ref[0:1, :]
    u_prev, u_next = _shift_rows(u, up, un, first, last)
    conv = u_prev * w_ref[0:1, :] + u * w_ref[1:2, :] + u_next * w_ref[2:3, :]
    y = cb_ref[...] * conv
    y = y * lax.rsqrt(jnp.mean(y * y, axis=-1, keepdims=True) + RMS_EPS)
    o_ref[...] = (y * g_ref[...]).astype(o_ref.dtype)


def _halo_specs(tt, width, col0, n_rows):
    cb = col0 // width
    r = tt // SUBLANES
    last_blk = n_rows // SUBLANES - 1
    prev = pl.BlockSpec((SUBLANES, width), lambda i: (jnp.maximum(i * r - 1, 0), cb))
    nxt = pl.BlockSpec((SUBLANES, width), lambda i: (jnp.minimum((i + 1) * r, last_blk), cb))
    return prev, nxt


def _conv_mix(proj_a, conv_w, norm_g, seq, tt=CONV_TT):
    n = proj_a.shape[0]
    w = CONV_WIDTH
    main = lambda c0: pl.BlockSpec((tt, w), lambda i: (i, c0 // w))
    ccp, ccn = _halo_specs(tt, w, A_CC, n)
    chp, chn = _halo_specs(tt, w, A_CH, n)
    full = lambda shape: pl.BlockSpec(shape, lambda i: (0, 0))
    return pl.pallas_call(
        functools.partial(_convmix_kernel, tiles_per_seq=seq // tt),
        grid=(n // tt,),
        in_specs=[main(A_CB), main(A_CC), main(A_CH), ccp, chp, ccn, chn, full((3, w)), full((1, w))],
        out_specs=pl.BlockSpec((tt, w), lambda i: (i, 0)),
        out_shape=jax.ShapeDtypeStruct((n, w), BF16),
        compiler_params=_cparams(("parallel",)),
        name="conv_mix",
    )(proj_a, proj_a, proj_a, proj_a, proj_a, proj_a, proj_a, conv_w, norm_g.reshape(1, w))


def _ssd_kernel(*refs, direction, final, tiles):
    if final:
        (xs_s, bc_s, dt_ref, z_ref, yf_ref, dtb_ref, alog_ref, e_ref, hm_ref, dsk_ref, ng_ref,
         o_ref, h_ref, y_ref) = refs
    else:
        (xs_ref, xsp_ref, xsn_ref, bc_ref, bcp_ref, bcn_ref, dt_ref,
         cwx_ref, cbx_ref, cwb_ref, cbb_ref, dtb_ref, alog_ref, e_ref, hm_ref,
         o_ref, xs_s, bc_s, h_ref) = refs
        y_ref = o_ref
    L = SSD_CHUNK
    c = pl.program_id(1)
    tile = c if direction == 0 else tiles - 1 - c
    first = tile == 0
    last = tile == tiles - 1

    @pl.when(c == 0)
    def _():
        h_ref[...] = jnp.zeros_like(h_ref)

    def conv_silu(x_ref, p_ref, n_ref, w_ref, b_ref):
        x = x_ref[...]
        x_prev, x_next = _shift_rows(x, p_ref[SUBLANES - 1:SUBLANES, :], n_ref[0:1, :], first, last)
        v = x_prev * w_ref[0:1, :] + x * w_ref[1:2, :] + x_next * w_ref[2:3, :] + b_ref[...]
        return v * _sigmoid(v)

    if not final:
        xs_s[...] = conv_silu(xs_ref, xsp_ref, xsn_ref, cwx_ref, cbx_ref)
        bc_s[...] = conv_silu(bc_ref, bcp_ref, bcn_ref, cwb_ref, cbb_ref)

    li = lax.broadcasted_iota(jnp.int32, (L, L), 0)
    si = lax.broadcasted_iota(jnp.int32, (L, L), 1)
    valid = (si <= li) if direction == 0 else (si >= li)
    tri = jnp.where(valid, 1.0, 0.0).astype(BF16)
    lane = lax.broadcasted_iota(jnp.int32, (1, LANES), 1)
    nt = (((1,), (1,)), ((), ()))
    e_mat = e_ref[...]
    neg_a = -jnp.exp(alog_ref[...])
    heads_per_group = SSD_HEADS // SSD_GROUPS
    chunks = xs_s.shape[0] // L
    order = range(chunks) if direction == 0 else range(chunks - 1, -1, -1)

    for k in order:
        rows = slice(k * L, (k + 1) * L)
        xs = xs_s[rows, :]
        bm = bc_s[rows, :LANES]
        cm = bc_s[rows, LANES:]

        dtr = dt_ref[rows, :] + dtb_ref[...]
        dt = jnp.maximum(dtr, 0.0) + _log1p(jnp.exp(-jnp.abs(dtr)))
        acum = _dot_exact_rhs(tri, dt * neg_a)
        acum_t = acum.T
        total = acum[L - 1:L, :] if direction == 0 else acum[0:1, :]
        dec_end = jnp.exp(total - acum)
        ea = jnp.exp(acum)

        dtx = _dot_split2_lhs(dt, e_mat)
        eax = _dot_split2_lhs(ea, e_mat)
        dex = _dot_split2_lhs(dec_end * dt, e_mat)

        xdt = xs * dtx
        xw = (xs * dex).astype(BF16)
        cm_b = cm.astype(BF16)
        bm_b = bm.astype(BF16)

        h_prev = h_ref[...]
        y_off = _dot(cm_b, h_prev.astype(BF16)) * eax

        for g in range(SSD_GROUPS):
            cg = jnp.where((lane // SSD_STATE) == g, cm, 0.0).astype(BF16)
            cb = lax.dot_general(cg, bm_b, nt, preferred_element_type=F32)
            for pair in range(heads_per_group // 2):
                acc = None
                for half in range(2):
                    h = g * heads_per_group + pair * 2 + half
                    hl = direction * SSD_HEADS + h
                    seg = acum[:, hl:hl + 1] - acum_t[hl:hl + 1, :]
                    w = (jnp.where(valid, jnp.exp(seg), 0.0) * cb).astype(BF16)
                    c0 = (h // 2) * LANES
                    rhs = jnp.where((lane // SSD_HEAD_DIM) == half, xdt[:, c0:c0 + LANES], 0.0).astype(BF16)
                    part = _dot(w, rhs)
                    acc = part if acc is None else acc + part
                c0 = (g * heads_per_group // 2 + pair) * LANES
                y_pair = acc + y_off[:, c0:c0 + LANES]
                if final:
                    y_pair = yf_ref[rows, c0:c0 + LANES] + y_pair + dsk_ref[:, c0:c0 + LANES] * xs[:, c0:c0 + LANES]
                    zz = z_ref[rows, c0:c0 + LANES]
                    y_pair = y_pair * (zz * _sigmoid(zz))
                y_ref[rows, c0:c0 + LANES] = y_pair

        cd = eax[L - 1:L, :] if direction == 0 else eax[0:1, :]
        s_new = _dot(bm.T.astype(BF16), xw)
        h_ref[...] = (cd * h_prev + s_new) * hm_ref[...]

    if final:
        gw = SSD_WIDTH // SSD_GROUPS
        for g in range(SSD_GROUPS):
            y = y_ref[:, g * gw:(g + 1) * gw]
            y = y * lax.rsqrt(jnp.mean(y * y, axis=-1, keepdims=True) + RMS_EPS)
            o_ref[:, g * gw:(g + 1) * gw] = (y * ng_ref[:, g * gw:(g + 1) * gw]).astype(o_ref.dtype)


def _ssd_pass(proj_a, fwd_outs, params, seq, direction):
    n = proj_a.shape[0]
    T = SSD_CHUNK * SSD_CHUNKS_PER_STEP
    tiles = seq // T
    nseq = n // seq
    final = direction == 1
    cw, cb_, dtb, alog, e_mat, hmask, dsk, ng = params
    bcw = 2 * SSD_GROUPS * SSD_STATE

    def row_blk(b, c):
        t = c if direction == 0 else tiles - 1 - c
        return b * tiles + t

    def main(width, col0):
        return pl.BlockSpec((T, width), lambda b, c: (row_blk(b, c), col0 // width))

    r = T // SUBLANES
    last_blk = n // SUBLANES - 1

    def halo(width, col0):
        prev = pl.BlockSpec((SUBLANES, width), lambda b, c: (jnp.maximum(row_blk(b, c) * r - 1, 0), col0 // width))
        nxt = pl.BlockSpec((SUBLANES, width), lambda b, c: (jnp.minimum((row_blk(b, c) + 1) * r, last_blk), col0 // width))
        return prev, nxt

    xsp, xsn = halo(SSD_WIDTH, A_XS)
    bcp, bcn = halo(bcw, A_BC)
    full = lambda a: pl.BlockSpec(a.shape, lambda b, c: (0,) * a.ndim)

    rows = lambda width: pl.BlockSpec((T, width), lambda b, c: (row_blk(b, c), 0))
    if final:
        y_fwd, xs_act, bc_act = fwd_outs
        in_specs = [rows(SSD_WIDTH), rows(bcw), main(LANES, A_DT), main(SSD_WIDTH, A_Z), rows(SSD_WIDTH)]
        args = [xs_act, bc_act, proj_a, proj_a, y_fwd]
        consts = [dtb, alog, e_mat, hmask, dsk, ng]
        out_specs = rows(SSD_WIDTH)
        out_shape = jax.ShapeDtypeStruct((n, SSD_WIDTH), BF16)
    else:
        in_specs = [main(SSD_WIDTH, A_XS), xsp, xsn, main(bcw, A_BC), bcp, bcn, main(LANES, A_DT)]
        args = [proj_a] * 7
        consts = [cw[:, :SSD_WIDTH], cb_[:, :SSD_WIDTH], cw[:, SSD_WIDTH:], cb_[:, SSD_WIDTH:], dtb, alog, e_mat, hmask]
        out_specs = [rows(SSD_WIDTH), rows(SSD_WIDTH), rows(bcw)]
        out_shape = [jax.ShapeDtypeStruct((n, SSD_WIDTH), F32), jax.ShapeDtypeStruct((n, SSD_WIDTH), F32),
                     jax.ShapeDtypeStruct((n, bcw), F32)]
    in_specs += [full(a) for a in consts]
    args += consts
    scratch = [pltpu.VMEM((LANES, SSD_WIDTH), F32)]
    if final:
        scratch.append(pltpu.VMEM((T, SSD_WIDTH), F32))
    return pl.pallas_call(
        functools.partial(_ssd_kernel, direction=direction, final=final, tiles=tiles),
        grid=(nseq, tiles),
        in_specs=in_specs,
        out_specs=out_specs,
        out_shape=out_shape,
        scratch_shapes=scratch,
        compiler_params=_cparams(("parallel", "arbitrary")),
        name="ssd_bwd" if final else "ssd_fwd",
    )(*args)


def _ssd_consts(ssd_conv_w, ssd_conv_b, ssd_dt_bias, ssd_a_log, ssd_d, ssd_norm_g):
    pad = LANES - 2 * SSD_HEADS
    dtb = jnp.pad(ssd_dt_bias.reshape(1, -1), ((0, 0), (0, pad)))
    alog = jnp.pad(ssd_a_log.reshape(1, -1), ((0, 0), (0, pad)))
    col_head = np.arange(SSD_WIDTH) // SSD_HEAD_DIM
    e_mats = [jnp.asarray((np.arange(LANES)[:, None] == d * SSD_HEADS + col_head[None, :]), BF16) for d in range(2)]
    row_group = np.arange(LANES) // SSD_STATE
    col_group = col_head // (SSD_HEADS // SSD_GROUPS)
    hmask = jnp.asarray(row_group[:, None] == col_group[None, :], F32)
    dsk = jnp.repeat(ssd_d, SSD_HEAD_DIM).reshape(1, SSD_WIDTH)
    ng = ssd_norm_g.reshape(1, SSD_WIDTH)
    cb_ = ssd_conv_b.reshape(1, -1)
    return [(ssd_conv_w, cb_, dtb, alog, e_mats[d], hmask, dsk, ng) for d in range(2)]


NA_KROWS = NA_QROWS + WIN_H


def _na_kernel(kb_ref, cls_ref, q_ref, k_ref, v_ref, bias_ref, o_ref, *, blocks_per_step):
    step = pl.program_id(2)
    lane = lax.broadcasted_iota(jnp.int32, (1, LANES), 1)
    nt = (((1,), (1,)), ((), ()))
    nq = NA_QROWS * GRID_W
    nk = NA_KROWS * GRID_W
    for j in range(blocks_per_step):
        blk = step * blocks_per_step + j
        k0 = pl.multiple_of(kb_ref[blk] * GRID_W, GRID_W)
        cls = cls_ref[blk]
        q = q_ref[j * nq:(j + 1) * nq, :] * (NA_HEAD_DIM ** -0.5)
        kk = k_ref[pl.ds(k0, nk), :]
        vv = v_ref[pl.ds(k0, nk), :]
        outs = []
        for hh in range(2):
            qm = jnp.where((lane // NA_HEAD_DIM) == hh, q, jnp.zeros_like(q))
            s = lax.dot_general(qm, kk, nt, preferred_element_type=F32) + bias_ref[cls, hh]
            m = jnp.max(s, axis=-1, keepdims=True)
            p = jnp.exp(s - m)
            l = jnp.sum(p, axis=-1, keepdims=True)
            outs.append(_dot(p.astype(BF16), vv) / l)
        o = jnp.where((lane // NA_HEAD_DIM) == 0, outs[0], outs[1])
        o_ref[j * nq:(j + 1) * nq, :] = o.astype(o_ref.dtype)


def _na_tables(rel_bias, rows):
    qc = np.arange(GRID_W)
    cs = np.clip(qc - WIN_W // 2, 0, GRID_W - WIN_W)
    kc = np.arange(GRID_W)
    col_valid = (kc[None, :] >= cs[:, None]) & (kc[None, :] < cs[:, None] + WIN_W)
    dj = np.clip(kc[None, :] - qc[:, None] + WIN_W - 1, 0, 2 * WIN_W - 2)
    nblk = rows // NA_QROWS
    kb = np.clip(np.arange(nblk) * NA_QROWS - WIN_H // 2, 0, rows - NA_KROWS)
    patterns, cls = [], []
    for b in range(nblk):
        r = b * NA_QROWS + np.arange(NA_QROWS)
        rs = np.clip(r - WIN_H // 2, 0, rows - WIN_H)
        key = (tuple(rs - kb[b]), int(kb[b] - b * NA_QROWS))
        if key not in patterns:
            patterns.append(key)
        cls.append(patterns.index(key))
    nh, nw = 2 * WIN_H - 1, 2 * WIN_W - 1
    bias_ext = jnp.pad(rel_bias.astype(F32), ((0, 0), (0, 1), (0, 1)), constant_values=NEG_BIG)
    di_sel = []
    for off, shift in patterns:
        kr = np.arange(NA_KROWS)[None, :]
        i = np.arange(NA_QROWS)[:, None]
        rel = kr - np.asarray(off)[:, None]
        row_valid = (rel >= 0) & (rel < WIN_H)
        di = np.where(row_valid, np.clip(shift + kr - i + WIN_H - 1, 0, nh - 1), nh)
        di_sel.append(np.eye(nh + 1, dtype=np.float32)[di])
    dj_sel = np.eye(nw + 1, dtype=np.float32)[np.where(col_valid, dj, nw)]
    tab = jnp.einsum('pikd,hde,qwe->phiqkw', np.stack(di_sel), bias_ext, dj_sel, precision=lax.Precision.HIGHEST)
    tab = tab.reshape(len(patterns), NA_HEADS, NA_QROWS * GRID_W, NA_KROWS * GRID_W)
    return jnp.asarray(kb, jnp.int32), jnp.asarray(cls, jnp.int32), tab


def _neighbourhood_attention(qkv, rel_bias, seq, blocks_per_step=NA_BLOCKS_PER_STEP):
    (q_arr, q0), (k_arr, k0), (v_arr, v0) = qkv
    n = q_arr.shape[0]
    nseq = n // seq
    rows = seq // GRID_W
    kb, cls, bias_tab = _na_tables(rel_bias, rows)
    tq = blocks_per_step * NA_QROWS * GRID_W
    pairs = NA_HEADS // 2
    steps = seq // tq
    ncls = bias_tab.shape[0]
    grid_spec = pltpu.PrefetchScalarGridSpec(
        num_scalar_prefetch=2,
        grid=(nseq, pairs, steps),
        in_specs=[pl.BlockSpec((tq, LANES), lambda b, p, r, kb_, cls_: (b * steps + r, q0 + p)),
                  pl.BlockSpec((seq, LANES), lambda b, p, r, kb_, cls_: (b, k0 + p)),
                  pl.BlockSpec((seq, LANES), lambda b, p, r, kb_, cls_: (b, v0 + p)),
                  pl.BlockSpec((ncls, 2) + bias_tab.shape[2:], lambda b, p, r, kb_, cls_: (0, p, 0, 0))],
        out_specs=pl.BlockSpec((tq, LANES), lambda b, p, r, kb_, cls_: (b * steps + r, p)),
    )
    return pl.pallas_call(
        functools.partial(_na_kernel, blocks_per_step=blocks_per_step),
        grid_spec=grid_spec,
        out_shape=jax.ShapeDtypeStruct((n, NA_WIDTH), F32),
        compiler_params=_cparams(("parallel", "arbitrary", "arbitrary")),
        name="nbr_attn",
    )(kb, cls, q_arr, k_arr, v_arr, bias_tab)


def _rms_rows(y, g):
    return y * lax.rsqrt(jnp.mean(y * y, axis=-1, keepdims=True) + RMS_EPS) * g


def _outproj_kernel(*refs, router, n_parts, first_tiles):
    yc_ref, ys_ref, yn_ref = refs[:3]
    x_refs = refs[3:3 + n_parts]
    rest = refs[3 + n_parts:]
    if router:
        w_ref, nag_ref, g_ref, b_ref, rw_ref, o_ref, idx_ref, gate_ref = rest
    else:
        w_ref, nag_ref, g_ref, b_ref, o_ref = rest
    c1 = CONV_WIDTH
    c2 = CONV_WIDTH + SSD_WIDTH
    rows = o_ref.shape[0] // OUTPROJ_ROW_SPLITS
    for part in range(OUTPROJ_ROW_SPLITS):
        sl = slice(part * rows, (part + 1) * rows)
        yn = _rms_rows(yn_ref[sl, :], nag_ref[...]).astype(BF16)
        mix = _dot(yc_ref[sl, :], w_ref[0:c1, :]) + _dot(ys_ref[sl, :], w_ref[c1:c2, :]) + _dot(yn, w_ref[c2:, :])
        out = _layer_norm(ALPHA * _read_row_parts(x_refs, first_tiles, sl) + mix, g_ref[...], b_ref[...])
        o_ref[sl, :] = out
        if not router:
            continue
        o1, o2, _ = _split3(out)
        t1 = _dot(o1, rw_ref[...])
        t2 = _dot(o2, rw_ref[...])
        sh1 = LANES - N_EXPERTS
        sh2 = LANES - 2 * N_EXPERTS
        logits = ((pltpu.roll(t2, sh1, 1) + pltpu.roll(t1, sh2, 1)) + (t2 + pltpu.roll(t1, sh1, 1))) + t1
        lane = lax.broadcasted_iota(jnp.int32, (rows, LANES), 1)
        lg = jnp.where(lane < N_EXPERTS, logits, -jnp.inf)
        m1 = jnp.max(lg, axis=-1, keepdims=True)
        i1 = jnp.min(jnp.where(lg == m1, lane, LANES), axis=-1, keepdims=True)
        lg2 = jnp.where(lane == i1, -jnp.inf, lg)
        m2 = jnp.max(lg2, axis=-1, keepdims=True)
        i2 = jnp.min(jnp.where(lg2 == m2, lane, LANES), axis=-1, keepdims=True)
        e2 = jnp.exp(m2 - m1)
        den = 1.0 + e2
        idx_ref[sl, :] = jnp.where(lane == 0, i1, jnp.where(lane == 1, i2, 0))
        gate_ref[sl, :] = jnp.where(lane == 0, 1.0 / den, jnp.where(lane == 1, e2 / den, 0.0))


def _out_proj_ln(yc, ys, yn, x_parts, w_out, na_g, ln_g, ln_b, router_w=None, tm=ROW_TILE):
    n = yc.shape[0]
    router = router_w is not None
    row = lambda w: pl.BlockSpec((tm, w), lambda i: (i, 0))
    full = lambda a: pl.BlockSpec(a.shape, lambda i: (0,) * a.ndim)
    consts = [w_out, na_g.reshape(1, -1), ln_g.reshape(1, -1), ln_b.reshape(1, -1)]
    if router:
        consts.append(router_w)
    out_shape = [jax.ShapeDtypeStruct((n, D_MODEL), F32)]
    out_specs = [row(D_MODEL)]
    if router:
        out_shape += [jax.ShapeDtypeStruct((n, LANES), jnp.int32), jax.ShapeDtypeStruct((n, LANES), F32)]
        out_specs += [row(LANES), row(LANES)]
    res = pl.pallas_call(
        functools.partial(_outproj_kernel, router=router, n_parts=len(x_parts),
                          first_tiles=x_parts[0].shape[0] // tm),
        grid=(n // tm,),
        in_specs=[row(CONV_WIDTH), row(SSD_WIDTH), row(NA_WIDTH)] + _row_parts_specs(x_parts, tm, D_MODEL, 1)
        + [full(a) for a in consts],
        out_specs=out_specs,
        out_shape=out_shape,
        compiler_params=_cparams(("parallel",)),
        name="out_proj_router" if router else "out_proj",
    )(yc, ys, yn, *x_parts, *consts)
    return res


def _ffn_kernel(be_ref, bv_ref, x_ref, wg_ref, wu_ref, wd_ref, *rest, packed, ln, ragged):
    if ln:
        g_ref, b_ref, o_ref = rest
    else:
        (o_ref,) = rest
    i = pl.program_id(0)
    j = pl.program_id(1)

    @pl.when(j == 0)
    def _():
        o_ref[...] = jnp.zeros_like(o_ref)

    def accumulate(rows):
        if packed:
            lo, hi = _unpack_bf16_pairs(x_ref[rows, :])
            c = lo.shape[1]
            g = _dot(lo, wg_ref[0, :c, :].astype(BF16)) + _dot(hi, wg_ref[0, c:, :].astype(BF16))
            u = _dot(lo, wu_ref[0, :c, :].astype(BF16)) + _dot(hi, wu_ref[0, c:, :].astype(BF16))
        else:
            x = x_ref[rows, :].astype(BF16)
            g = _dot(x, wg_ref[0].astype(BF16))
            u = _dot(x, wu_ref[0].astype(BF16))
        h = ((g * _sigmoid(g)) * u).astype(BF16)
        o_ref[rows, :] += _dot(h, wd_ref[0].astype(BF16))

    tm = o_ref.shape[0]
    filled = bv_ref[i]

    @pl.when(filled == tm)
    def _():
        accumulate(slice(None))

    if ragged:
        for s in range(tm // FFN_ROW_GROUP):
            @pl.when(jnp.logical_and(filled < tm, filled > s * FFN_ROW_GROUP))
            def _():
                accumulate(slice(s * FFN_ROW_GROUP, (s + 1) * FFN_ROW_GROUP))

    if ln:
        @pl.when(j == pl.num_programs(1) - 1)
        def _():
            o_ref[...] = _layer_norm(ALPHA * x_ref[...] + o_ref[...], g_ref[...], b_ref[...])


def _ffn_blocks(x, block_e, block_fill, w_gate, w_up, w_down, tm, tf=FF_TILE, residual_ln=None, ragged=False):
    n, xw = x.shape
    packed = x.dtype == jnp.uint32
    d = w_gate.shape[1]
    f = w_gate.shape[2]
    nf = f // tf
    col = lambda i, j, be, bv: (be[i], 0, jnp.where(bv[i] != 0, j, nf - 1))
    rowj = lambda i, j, be, bv: (be[i], jnp.where(bv[i] != 0, j, nf - 1), 0)
    in_specs = [pl.BlockSpec((tm, xw), lambda i, j, be, bv: (i, 0)),
                pl.BlockSpec((1, d, tf), col), pl.BlockSpec((1, d, tf), col), pl.BlockSpec((1, tf, d), rowj)]
    args = [x, w_gate, w_up, w_down]
    if residual_ln is not None:
        gain, bias = residual_ln
        assert not packed
        vec = pl.BlockSpec((1, d), lambda i, j, be, bv: (0, 0))
        in_specs += [vec, vec]
        args += [gain.reshape(1, d), bias.reshape(1, d)]
    grid_spec = pltpu.PrefetchScalarGridSpec(
        num_scalar_prefetch=2,
        grid=(n // tm, nf),
        in_specs=in_specs,
        out_specs=pl.BlockSpec((tm, d), lambda i, j, be, bv: (i, 0)),
    )
    return pl.pallas_call(
        functools.partial(_ffn_kernel, packed=packed, ln=residual_ln is not None, ragged=ragged),
        grid_spec=grid_spec,
        out_shape=jax.ShapeDtypeStruct((n, d), F32),
        compiler_params=_cparams(("parallel", "arbitrary")),
        name="swiglu_blocks",
    )(block_e, block_fill, *args)


def _combine_ln_kernel(i0_ref, i1_ref, n0_ref, n1_ref, src_ref, x_ref, gate_ref, g_ref, b_ref, oa_ref, ob_ref,
                       b0_ref, b1_ref, sem, *, tm, first_tiles):
    i = pl.program_id(0)
    slot = i % 2

    def start(idx0_ref, idx1_ref, s):
        _start_row_gather(src_ref, idx0_ref, b0_ref.at[s], sem.at[s, 0], tm)
        _start_row_gather(src_ref, idx1_ref, b1_ref.at[s], sem.at[s, 1], tm)

    @pl.when(i == 0)
    def _():
        start(i0_ref, i1_ref, 0)

    @pl.when(i + 1 < pl.num_programs(0))
    def _():
        start(n0_ref, n1_ref, 1 - slot)

    _wait_row_gather(src_ref, b0_ref.at[slot], sem.at[slot, 0], tm)
    _wait_row_gather(src_ref, b1_ref.at[slot], sem.at[slot, 1], tm)
    f = b0_ref[slot] * gate_ref[:, 0:1] + b1_ref[slot] * gate_ref[:, 1:2]
    res = _layer_norm(ALPHA * x_ref[...] + f, g_ref[...], b_ref[...])

    @pl.when(i < first_tiles)
    def _():
        oa_ref[...] = res

    @pl.when(i >= first_tiles)
    def _():
        ob_ref[...] = res


def _combine_ln(x, outs, d0, d1, gates, g, b, n_first, tm=ROW_TILE):
    n, d = x.shape
    nt = n // tm
    first_tiles = n_first // tm
    row = pl.BlockSpec((tm, d), lambda i: (i, 0))
    vec = pl.BlockSpec((1, d), lambda i: (0, 0))
    idx = pl.BlockSpec((1, 1, tm), lambda i: (i, 0, 0), memory_space=pltpu.SMEM)
    idx_next = pl.BlockSpec((1, 1, tm), lambda i: (jnp.minimum(i + 1, nt - 1), 0, 0), memory_space=pltpu.SMEM)
    d0, d1 = d0.reshape(nt, 1, tm), d1.reshape(nt, 1, tm)
    return pl.pallas_call(
        functools.partial(_combine_ln_kernel, tm=tm, first_tiles=first_tiles), grid=(nt,),
        in_specs=[idx, idx, idx_next, idx_next, pl.BlockSpec(memory_space=pl.ANY), row,
                  pl.BlockSpec((tm, LANES), lambda i: (i, 0)), vec, vec],
        out_specs=[pl.BlockSpec((tm, d), lambda i: (jnp.minimum(i, first_tiles - 1), 0)),
                   pl.BlockSpec((tm, d), lambda i: (jnp.maximum(i - first_tiles, 0), 0))],
        out_shape=[jax.ShapeDtypeStruct((n_first, d), F32), jax.ShapeDtypeStruct((n - n_first, d), F32)],
        scratch_shapes=[pltpu.VMEM((2, tm, d), F32), pltpu.VMEM((2, tm, d), F32), pltpu.SemaphoreType.DMA((2, 2))],
        compiler_params=_cparams(("arbitrary",)), name="combine_ln",
    )(d0, d1, d0, d1, outs, x, gates, g.reshape(1, d), b.reshape(1, d))


def _start_row_gather(src_ref, idx_ref, buf_ref, sem, tr):
    def issue(r, carry):
        pltpu.make_async_copy(src_ref.at[pl.ds(idx_ref[0, 0, r], 1), :], buf_ref.at[pl.ds(r, 1), :], sem).start()
        return carry

    lax.fori_loop(0, tr, issue, 0, unroll=GATHER_UNROLL)


def _wait_row_gather(src_ref, buf_ref, sem, tr):
    pltpu.make_async_copy(src_ref.at[pl.ds(0, tr), :], buf_ref, sem).wait()


HI16 = 0xFFFF0000


def _pack_bf16_pairs(x):
    c = x.shape[1] // 2
    lo = lax.bitcast_convert_type(x[:, :c].astype(BF16).astype(F32), jnp.uint32)
    hi = lax.bitcast_convert_type(x[:, c:].astype(BF16).astype(F32), jnp.uint32)
    return (hi & jnp.uint32(HI16)) | (lo >> 16)


def _unpack_bf16_pairs(u):
    lo = lax.bitcast_convert_type(u << 16, F32).astype(BF16)
    hi = lax.bitcast_convert_type(u & jnp.uint32(HI16), F32).astype(BF16)
    return lo, hi


def _zero_row_runs(first, length, max_len, z_ref, o_ref, sem):
    zrows = z_ref.shape[0]

    def copy(row, rows):
        return pltpu.make_async_copy(z_ref.at[pl.ds(0, rows), :], o_ref.at[pl.ds(row, rows), :], sem)

    def for_each_run(action):
        head = jnp.minimum((-first) % SUBLANES, length)
        tail = (length - head) % SUBLANES
        body = length - head - tail
        for k in range(SUBLANES - 1):
            @pl.when(k < head)
            def _(k=k):
                action(copy(first + k, 1))
        row = first + head
        for bit in range(SUBLANES.bit_length() - 1, max_len.bit_length()):
            size = 1 << bit
            take = (body >> bit) & 1
            rows = min(size, zrows)
            for piece in range(size // rows):
                @pl.when(take == 1)
                def _(row=row, piece=piece, rows=rows):
                    action(copy(pl.multiple_of(row + piece * rows, SUBLANES), rows))
            row = row + take * size
        for k in range(SUBLANES - 1):
            @pl.when(k < tail)
            def _(k=k, row=row):
                action(copy(row + k, 1))

    for_each_run(lambda c: c.start())
    for_each_run(lambda c: c.wait())


def _dispatch_kernel(pad_start_ref, pad_len_ref, d0_ref, d1_ref, x_ref, xs_ref, pk_ref, z_ref, sem, zsem,
                     *, tm, max_pad):
    @pl.when(pl.program_id(0) == 0)
    def _():
        z_ref[...] = jnp.zeros_like(z_ref)

        def zero_region(r, carry):
            _zero_row_runs(pad_start_ref[r], pad_len_ref[r], max_pad, z_ref, xs_ref, zsem)
            return carry

        lax.fori_loop(0, pad_start_ref.shape[0], zero_region, 0)

    pk_ref[...] = _pack_bf16_pairs(x_ref[...])

    def issue(r, carry):
        row = pk_ref.at[pl.ds(r, 1), :]
        pltpu.make_async_copy(row, xs_ref.at[pl.ds(d0_ref[0, 0, r], 1), :], sem.at[0]).start()
        pltpu.make_async_copy(row, xs_ref.at[pl.ds(d1_ref[0, 0, r], 1), :], sem.at[1]).start()
        return carry

    lax.fori_loop(0, tm, issue, 0, unroll=GATHER_UNROLL)
    for k in range(TOP_K):
        pltpu.make_async_copy(pk_ref, xs_ref.at[pl.ds(0, tm), :], sem.at[k]).wait()


def _dispatch_rows(x, d0, d1, pad_start, pad_len, p_len, max_pad, tm=ROW_TILE):
    n, d = x.shape
    idx = pl.BlockSpec((1, 1, tm), lambda i, ps, pn: (i, 0, 0), memory_space=pltpu.SMEM)
    grid_spec = pltpu.PrefetchScalarGridSpec(
        num_scalar_prefetch=2,
        grid=(n // tm,),
        in_specs=[idx, idx, pl.BlockSpec((tm, d), lambda i, ps, pn: (i, 0))],
        out_specs=pl.BlockSpec(memory_space=pl.ANY),
        scratch_shapes=[pltpu.VMEM((tm, d // 2), jnp.uint32), pltpu.VMEM((tm, d // 2), jnp.uint32),
                        pltpu.SemaphoreType.DMA((TOP_K,)), pltpu.SemaphoreType.DMA(())],
    )
    return pl.pallas_call(
        functools.partial(_dispatch_kernel, tm=tm, max_pad=max_pad),
        grid_spec=grid_spec,
        out_shape=jax.ShapeDtypeStruct((p_len, d // 2), jnp.uint32),
        compiler_params=_cparams(("arbitrary",)),
        name="dispatch_rows",
    )(pad_start, pad_len, d0.reshape(n // tm, 1, tm), d1.reshape(n // tm, 1, tm), x)


def _routing_tables(idx, n, tm):
    flat_e = idx[:, :TOP_K].reshape(-1)
    onehot = (flat_e[:, None] == jnp.arange(N_EXPERTS, dtype=jnp.int32)[None, :]).astype(jnp.int32)
    incl = jnp.cumsum(onehot, axis=0)
    rank = jnp.sum((incl - onehot) * onehot, axis=1)
    counts = incl[-1]
    padded = ((counts + tm - 1) // tm) * tm
    pend = jnp.cumsum(padded)
    pstart = pend - padded
    dest = pstart[flat_e] + rank
    n_blocks = -(-(n * TOP_K) // tm) + N_EXPERTS
    starts = jnp.arange(n_blocks, dtype=jnp.int32) * tm
    block_e = jnp.sum((pend[None, :] <= starts[:, None]).astype(jnp.int32), axis=1)
    block_e = jnp.minimum(block_e, N_EXPERTS - 1)
    block_fill = jnp.clip((pstart + counts)[block_e] - starts, 0, tm).astype(jnp.int32)
    block_fill = jnp.where(starts < pend[-1], block_fill, 0)
    last_e = block_e[jnp.maximum(jnp.sum((block_fill > 0).astype(jnp.int32)) - 1, 0)]
    block_e = jnp.where(block_fill > 0, block_e, last_e)
    dest2 = dest.reshape(n, TOP_K)
    pad_start = jnp.concatenate([pstart + counts, pend[-1:]]).astype(jnp.int32)
    pad_len = jnp.concatenate([padded - counts, n_blocks * tm - pend[-1:]]).astype(jnp.int32)
    return (pad_start, pad_len, n_blocks * tm), block_e, block_fill, dest2[:, 0], dest2[:, 1]


def _prep_w_in(w):
    bc_w = 2 * SSD_GROUPS * SSD_STATE
    widths = (CONV_WIDTH, CONV_WIDTH, CONV_WIDTH, SSD_WIDTH, SSD_WIDTH, bc_w, 2 * SSD_HEADS, 3 * NA_WIDTH)
    cb, cc, ch, z, xs, bc, dt, qkv = jnp.split(w, [int(o) for o in np.cumsum(widths)[:-1]], axis=1)
    pad = jnp.zeros((D_MODEL, A_WIDTH - (A_DT + 2 * SSD_HEADS)), w.dtype)
    return jnp.concatenate([z, xs, cb, cc, ch, bc, dt, pad, qkv], axis=1).astype(BF16)


def _prep_router_w(rw):
    r1 = rw.astype(BF16)
    r2 = (rw - r1.astype(F32)).astype(BF16)
    r3 = (rw - r1.astype(F32) - r2.astype(F32)).astype(BF16)
    pad = jnp.zeros((rw.shape[0], LANES - 3 * N_EXPERTS), BF16)
    return jnp.concatenate([r1, r2, r3, pad], axis=1)


def _trunk(x_parts, seq, p):
    n = sum(a.shape[0] for a in x_parts)
    n_first = x_parts[0].shape[0]
    for l in range(DEPTH):
        proj_a, qkv = _in_proj(x_parts, _prep_w_in(p['w_in'][l]))
        y_conv = _conv_mix(proj_a, p['conv_w'][l], p['conv_norm_g'][l], seq)
        cf, cbw = _ssd_consts(p['ssd_conv_w'][l], p['ssd_conv_b'][l], p['ssd_dt_bias'][l], p['ssd_a_log'][l],
                              p['ssd_d'][l], p['ssd_norm_g'][l])
        y_ssd = _ssd_pass(proj_a, _ssd_pass(proj_a, None, cf, seq, 0), cbw, seq, 1)
        y_na = _neighbourhood_attention(qkv, p['na_rel_bias'][l], seq)
        w_out = p['w_out'][l].astype(BF16)
        if l % 2 == 0:
            x = _out_proj_ln(y_conv, y_ssd, y_na, x_parts, w_out, p['na_norm_g'][l], p['ln_mix_g'][l],
                             p['ln_mix_b'][l])[0]
            padf = ((0, 0), (0, D_FF_PAD - D_FF))
            wg = jnp.pad(p['ffn_w_gate'][l // 2].astype(BF16), padf)[None]
            wu = jnp.pad(p['ffn_w_up'][l // 2].astype(BF16), padf)[None]
            wd = jnp.pad(p['ffn_w_down'][l // 2].astype(BF16), ((0, D_FF_PAD - D_FF), (0, 0)))[None]
            nb = n // MOE_TM
            x = _ffn_blocks(x, jnp.zeros((nb,), jnp.int32), jnp.full((nb,), MOE_TM, jnp.int32), wg, wu, wd, MOE_TM,
                            residual_ln=(p['ln_ffn_g'][l], p['ln_ffn_b'][l]))
            outs_split = (x[:n_first], x[n_first:])
        else:
            x, idx, gates = _out_proj_ln(y_conv, y_ssd, y_na, x_parts, w_out, p['na_norm_g'][l], p['ln_mix_g'][l],
                                         p['ln_mix_b'][l], _prep_router_w(p['router_w'][l // 2]))
            (pad_start, pad_len, p_len), block_e, block_fill, d0, d1 = _routing_tables(idx, n, MOE_TM)
            xs = _dispatch_rows(x, d0, d1, pad_start, pad_len, p_len, N_EXPERTS * MOE_TM)
            outs = _ffn_blocks(xs, block_e, block_fill, p['moe_w_gate'][l // 2], p['moe_w_up'][l // 2],
                               p['moe_w_down'][l // 2], MOE_TM, ragged=True)
            outs_split = _combine_ln(x, outs, d0, d1, gates, p['ln_ffn_g'][l], p['ln_ffn_b'][l], n_first)
            if l + 1 < DEPTH:
                x = jnp.concatenate(outs_split, axis=0)
        x_parts = [x]
    return outs_split


def kernel(x_prompt, x_sample, w_in, conv_w, conv_norm_g, ssd_conv_w, ssd_conv_b, ssd_dt_bias, ssd_a_log, ssd_d,
           ssd_norm_g, na_rel_bias, na_norm_g, w_out, ln_mix_g, ln_mix_b, ln_ffn_g, ln_ffn_b, ffn_w_gate, ffn_w_up,
           ffn_w_down, router_w, moe_w_gate, moe_w_up, moe_w_down):
    p = dict(w_in=w_in, conv_w=conv_w, conv_norm_g=conv_norm_g, ssd_conv_w=ssd_conv_w, ssd_conv_b=ssd_conv_b,
             ssd_dt_bias=ssd_dt_bias, ssd_a_log=ssd_a_log, ssd_d=ssd_d, ssd_norm_g=ssd_norm_g,
             na_rel_bias=na_rel_bias, na_norm_g=na_norm_g, w_out=w_out, ln_mix_g=ln_mix_g, ln_mix_b=ln_mix_b,
             ln_ffn_g=ln_ffn_g, ln_ffn_b=ln_ffn_b, ffn_w_gate=ffn_w_gate, ffn_w_up=ffn_w_up, ffn_w_down=ffn_w_down,
             router_w=router_w, moe_w_gate=moe_w_gate, moe_w_up=moe_w_up, moe_w_down=moe_w_down)
    bp, seq, d = x_prompt.shape
    bs, seq_s, _ = x_sample.shape
    assert seq == seq_s, "both request groups must share one sequence length"
    yp, ys = _trunk([x_prompt.reshape(bp * seq, d), x_sample.reshape(bs * seq, d)], seq, p)
    return yp.reshape(bp, seq, d), ys.reshape(bs, seq, d)
```

```python
import functools

import numpy as np
import jax
import jax.numpy as jnp
from jax import lax
from jax.experimental import pallas as pl
from jax.experimental.pallas import tpu as pltpu

F32 = jnp.float32
BF16 = jnp.bfloat16

D_MODEL = 2048
DEPTH = 2
GRID_W = 64
CONV_WIDTH = 512
SSD_HEADS = 16
SSD_HEAD_DIM = 64
SSD_WIDTH = 1024
SSD_GROUPS = 2
SSD_STATE = 64
SSD_CHUNK = 128
NA_HEADS = 8
NA_HEAD_DIM = 64
NA_WIDTH = 512
WIN_H = 8
WIN_W = 16
D_FF = 5504
N_EXPERTS = 8
TOP_K = 2
D_FF_EXPERT = 7168
ALPHA = (2 * DEPTH) ** 0.25
LN_EPS = 1e-5
RMS_EPS = 1e-6

LANES = 128
SUBLANES = 8
NEG_BIG = -1e30

A_Z, A_XS, A_CB, A_CC, A_CH, A_BC, A_DT, A_WIDTH = 0, 1024, 2048, 2560, 3072, 3584, 3840, 4096

VMEM_LIMIT = 56 * 1024 * 1024
IN_PROJ_TM, IN_PROJ_TN = 1024, 512
ROW_TILE = 512
CONV_TT = 1024
SSD_CHUNKS_PER_STEP = 8
NA_QROWS = 4
NA_BLOCKS_PER_STEP = 16
OUTPROJ_ROW_SPLITS = 2
MOE_TM = 1024
FF_TILE = 512
D_FF_PAD = -(-D_FF // FF_TILE) * FF_TILE
FFN_ROW_GROUP = 256
GATHER_UNROLL = 8


def _cparams(sem, vmem=VMEM_LIMIT):
    return pltpu.CompilerParams(dimension_semantics=sem, vmem_limit_bytes=vmem)


def _sigmoid(x):
    return 1.0 / (1.0 + jnp.exp(-x))


def _split3(x):
    x1 = x.astype(BF16)
    r1 = x - x1.astype(F32)
    x2 = r1.astype(BF16)
    r2 = r1 - x2.astype(F32)
    return x1, x2, r2.astype(BF16)


def _dot(a, b):
    return jnp.dot(a, b, preferred_element_type=F32)


def _dot_exact_rhs(m, x):
    x1, x2, x3 = _split3(x)
    return (_dot(m, x3) + _dot(m, x2)) + _dot(m, x1)


def _dot_split2_lhs(x, m):
    x1, x2, _ = _split3(x)
    return _dot(x2, m) + _dot(x1, m)


def _log1p(e):
    u = 1.0 + e
    d = u - 1.0
    return jnp.where(d == 0.0, e, jnp.log(u) * (e / jnp.where(d == 0.0, 1.0, d)))


def _layer_norm(v, g, b):
    mu = jnp.mean(v, axis=-1, keepdims=True)
    c = v - mu
    var = jnp.mean(c * c, axis=-1, keepdims=True)
    return c * lax.rsqrt(var + LN_EPS) * g + b


def _row_parts_specs(parts, tm, width, grid_rank):
    if len(parts) == 1:
        return [pl.BlockSpec((tm, width), (lambda i: (i, 0)) if grid_rank == 1 else (lambda i, j: (i, 0)))]
    first_tiles = parts[0].shape[0] // tm
    if grid_rank == 1:
        return [pl.BlockSpec((tm, width), lambda i: (jnp.minimum(i, first_tiles - 1), 0)),
                pl.BlockSpec((tm, width), lambda i: (jnp.maximum(i - first_tiles, 0), 0))]
    return [pl.BlockSpec((tm, width), lambda i, j: (jnp.minimum(i, first_tiles - 1), 0)),
            pl.BlockSpec((tm, width), lambda i, j: (jnp.maximum(i - first_tiles, 0), 0))]


def _read_row_parts(x_refs, first_tiles, rows=slice(None)):
    x = x_refs[0][rows, :]
    if len(x_refs) == 2:
        x = jnp.where(pl.program_id(0) < first_tiles, x, x_refs[1][rows, :])
    return x


def _inproj_kernel(*refs, n_parts, first_tiles, a_blocks):
    x_refs = refs[:n_parts]
    w_ref, oa_ref, ob_ref, xb_ref = refs[n_parts:]
    j = pl.program_id(1)

    @pl.when(j == 0)
    def _():
        xb_ref[...] = _read_row_parts(x_refs, first_tiles).astype(BF16)

    res = _dot(xb_ref[...], w_ref[...])

    @pl.when(j < a_blocks)
    def _():
        oa_ref[...] = res

    @pl.when(j >= a_blocks)
    def _():
        ob_ref[...] = res.astype(ob_ref.dtype)


def _inproj_pair_kernel(x_ref, w0_ref, w1_ref, oa_ref, oqk_ref, ov_ref, xb_ref, *, a_steps):
    j = pl.program_id(1)
    tn = w0_ref.shape[1]

    @pl.when(j == 0)
    def _():
        xb_ref[...] = x_ref[...].astype(BF16)

    @pl.when(j < a_steps)
    def _():
        xb = xb_ref[...]
        oa_ref[:, :tn] = _dot(xb, w0_ref[...])
        oa_ref[:, tn:] = _dot(xb, w1_ref[...])

    @pl.when(j == a_steps)
    def _():
        xb = xb_ref[...]
        oqk_ref[:, :tn] = _dot(xb, w0_ref[...]).astype(oqk_ref.dtype)
        oqk_ref[:, tn:] = _dot(xb, w1_ref[...]).astype(oqk_ref.dtype)

    @pl.when(j == a_steps + 1)
    def _():
        ov_ref[...] = _dot(xb_ref[...], w0_ref[...]).astype(ov_ref.dtype)


def _in_proj_paired(x, w, tm, tn):
    n, k = x.shape
    tiles = w.shape[1] // tn
    a_steps = A_WIDTH // (2 * tn)
    assert tiles == 2 * a_steps + 3 and NA_WIDTH == tn
    wspec = lambda off: pl.BlockSpec((k, tn), lambda i, j: (0, jnp.minimum(2 * j + off, tiles - 1)))
    proj_a, qk, v = pl.pallas_call(
        functools.partial(_inproj_pair_kernel, a_steps=a_steps),
        grid=(n // tm, a_steps + 2),
        in_specs=[pl.BlockSpec((tm, k), lambda i, j: (i, 0)), wspec(0), wspec(1)],
        out_specs=[pl.BlockSpec((tm, 2 * tn), lambda i, j: (i, jnp.minimum(j, a_steps - 1))),
                   pl.BlockSpec((tm, 2 * tn), lambda i, j: (i, 0)),
                   pl.BlockSpec((tm, tn), lambda i, j: (i, 0))],
        out_shape=[jax.ShapeDtypeStruct((n, A_WIDTH), F32), jax.ShapeDtypeStruct((n, 2 * tn), BF16),
                   jax.ShapeDtypeStruct((n, tn), BF16)],
        scratch_shapes=[pltpu.VMEM((tm, k), BF16)],
        compiler_params=_cparams(("parallel", "arbitrary")),
        name="in_proj_paired",
    )(x, w, w)
    blocks = NA_WIDTH // LANES
    return proj_a, ((qk, 0), (qk, blocks), (v, 0))


def _in_proj(x_parts, w, tm=IN_PROJ_TM, tn=IN_PROJ_TN):
    if len(x_parts) == 1:
        return _in_proj_paired(x_parts[0], w, tm, tn)
    n = sum(a.shape[0] for a in x_parts)
    k = x_parts[0].shape[1]
    a_blocks = A_WIDTH // tn
    b_width = w.shape[1] - A_WIDTH
    first_tiles = x_parts[0].shape[0] // tm
    proj_a, qkv = pl.pallas_call(
        functools.partial(_inproj_kernel, n_parts=len(x_parts), first_tiles=first_tiles, a_blocks=a_blocks),
        grid=(n // tm, w.shape[1] // tn),
        in_specs=_row_parts_specs(x_parts, tm, k, 2) + [pl.BlockSpec((k, tn), lambda i, j: (0, j))],
        out_specs=[pl.BlockSpec((tm, tn), lambda i, j: (i, jnp.minimum(j, a_blocks - 1))),
                   pl.BlockSpec((tm, tn), lambda i, j: (i, jnp.maximum(j - a_blocks, 0)))],
        out_shape=[jax.ShapeDtypeStruct((n, A_WIDTH), F32), jax.ShapeDtypeStruct((n, b_width), BF16)],
        scratch_shapes=[pltpu.VMEM((tm, k), BF16)],
        compiler_params=_cparams(("parallel", "arbitrary")),
        name="in_proj",
    )(*x_parts, w)
    blocks = NA_WIDTH // LANES
    return proj_a, ((qkv, 0), (qkv, blocks), (qkv, 2 * blocks))


def _shift_rows(x, prev_row, next_row, first, last):
    n = x.shape[0]
    row = lax.broadcasted_iota(jnp.int32, (n, 1), 0)
    prev_row = jnp.where(first, 0.0, prev_row)
    next_row = jnp.where(last, 0.0, next_row)
    x_prev = jnp.where(row == 0, prev_row, pltpu.roll(x, 1, 0))
    x_next = jnp.where(row == n - 1, next_row, pltpu.roll(x, n - 1, 0))
    return x_prev, x_next


def _convmix_kernel(cb_ref, cc_ref, ch_ref, ccp_ref, chp_ref, ccn_ref, chn_ref, w_ref, g_ref, o_ref, *, tiles_per_seq):
    i = pl.program_id(0)
    first = (i % tiles_per_seq) == 0
    last = (i % tiles_per_seq) == tiles_per_seq - 1
    u = cc_ref[...] * ch_ref[...]
    up = ccp_ref[SUBLANES - 1:SUBLANES, :] * chp_ref[SUBLANES - 1:SUBLANES, :]
    un = ccn_ref[0:1, :] * chn_ref[0:1, :]
    u_prev, u_next = _shift_rows(u, up, un, first, last)
    conv = u_prev * w_ref[0:1, :] + u * w_ref[1:2, :] + u_next * w_ref[2:3, :]
    y = cb_ref[...] * conv
    y = y * lax.rsqrt(jnp.mean(y * y, axis=-1, keepdims=True) + RMS_EPS)
    o_ref[...] = (y * g_ref[...]).astype(o_ref.dtype)


def _halo_specs(tt, width, col0, n_rows):
    cb = col0 // width
    r = tt // SUBLANES
    last_blk = n_rows // SUBLANES - 1
    prev = pl.BlockSpec((SUBLANES, width), lambda i: (jnp.maximum(i * r - 1, 0), cb))
    nxt = pl.BlockSpec((SUBLANES, width), lambda i: (jnp.minimum((i + 1) * r, last_blk), cb))
    return prev, nxt


def _conv_mix(proj_a, conv_w, norm_g, seq, tt=CONV_TT):
    n = proj_a.shape[0]
    w = CONV_WIDTH
    main = lambda c0: pl.BlockSpec((tt, w), lambda i: (i, c0 // w))
    ccp, ccn = _halo_specs(tt, w, A_CC, n)
    chp, chn = _halo_specs(tt, w, A_CH, n)
    full = lambda shape: pl.BlockSpec(shape, lambda i: (0, 0))
    return pl.pallas_call(
        functools.partial(_convmix_kernel, tiles_per_seq=seq // tt),
        grid=(n // tt,),
        in_specs=[main(A_CB), main(A_CC), main(A_CH), ccp, chp, ccn, chn, full((3, w)), full((1, w))],
        out_specs=pl.BlockSpec((tt, w), lambda i: (i, 0)),
        out_shape=jax.ShapeDtypeStruct((n, w), BF16),
        compiler_params=_cparams(("parallel",)),
        name="conv_mix",
    )(proj_a, proj_a, proj_a, proj_a, proj_a, proj_a, proj_a, conv_w, norm_g.reshape(1, w))


def _ssd_kernel(*refs, direction, final, tiles):
    if final:
        (xs_s, bc_s, dt_ref, z_ref, yf_ref, dtb_ref, alog_ref, e_ref, hm_ref, dsk_ref, ng_ref,
         o_ref, h_ref, y_ref) = refs
    else:
        (xs_ref, xsp_ref, xsn_ref, bc_ref, bcp_ref, bcn_ref, dt_ref,
         cwx_ref, cbx_ref, cwb_ref, cbb_ref, dtb_ref, alog_ref, e_ref, hm_ref,
         o_ref, xs_s, bc_s, h_ref) = refs
        y_ref = o_ref
    L = SSD_CHUNK
    c = pl.program_id(1)
    tile = c if direction == 0 else tiles - 1 - c
    first = tile == 0
    last = tile == tiles - 1

    @pl.when(c == 0)
    def _():
        h_ref[...] = jnp.zeros_like(h_ref)

    def conv_silu(x_ref, p_ref, n_ref, w_ref, b_ref):
        x = x_ref[...]
        x_prev, x_next = _shift_rows(x, p_ref[SUBLANES - 1:SUBLANES, :], n_ref[0:1, :], first, last)
        v = x_prev * w_ref[0:1, :] + x * w_ref[1:2, :] + x_next * w_ref[2:3, :] + b_ref[...]
        return v * _sigmoid(v)

    if not final:
        xs_s[...] = conv_silu(xs_ref, xsp_ref, xsn_ref, cwx_ref, cbx_ref)
        bc_s[...] = conv_silu(bc_ref, bcp_ref, bcn_ref, cwb_ref, cbb_ref)

    li = lax.broadcasted_iota(jnp.int32, (L, L), 0)
    si = lax.broadcasted_iota(jnp.int32, (L, L), 1)
    valid = (si <= li) if direction == 0 else (si >= li)
    tri = jnp.where(valid, 1.0, 0.0).astype(BF16)
    lane = lax.broadcasted_iota(jnp.int32, (1, LANES), 1)
    nt = (((1,), (1,)), ((), ()))
    e_mat = e_ref[...]
    neg_a = -jnp.exp(alog_ref[...])
    heads_per_group = SSD_HEADS // SSD_GROUPS
    chunks = xs_s.shape[0] // L
    order = range(chunks) if direction == 0 else range(chunks - 1, -1, -1)

    for k in order:
        rows = slice(k * L, (k + 1) * L)
        xs = xs_s[rows, :]
        bm = bc_s[rows, :LANES]
        cm = bc_s[rows, LANES:]

        dtr = dt_ref[rows, :] + dtb_ref[...]
        dt = jnp.maximum(dtr, 0.0) + _log1p(jnp.exp(-jnp.abs(dtr)))
        acum = _dot_exact_rhs(tri, dt * neg_a)
        acum_t = acum.T
        total = acum[L - 1:L, :] if direction == 0 else acum[0:1, :]
        dec_end = jnp.exp(total - acum)
        ea = jnp.exp(acum)

        dtx = _dot_split2_lhs(dt, e_mat)
        eax = _dot_split2_lhs(ea, e_mat)
        dex = _dot_split2_lhs(dec_end * dt, e_mat)

        xdt = xs * dtx
        xw = (xs * dex).astype(BF16)
        cm_b = cm.astype(BF16)
        bm_b = bm.astype(BF16)

        h_prev = h_ref[...]
        y_off = _dot(cm_b, h_prev.astype(BF16)) * eax

        for g in range(SSD_GROUPS):
            cg = jnp.where((lane // SSD_STATE) == g, cm, 0.0).astype(BF16)
            cb = lax.dot_general(cg, bm_b, nt, preferred_element_type=F32)
            for pair in range(heads_per_group // 2):
                acc = None
                for half in range(2):
                    h = g * heads_per_group + pair * 2 + half
                    hl = direction * SSD_HEADS + h
                    seg = acum[:, hl:hl + 1] - acum_t[hl:hl + 1, :]
                    w = (jnp.where(valid, jnp.exp(seg), 0.0) * cb).astype(BF16)
                    c0 = (h // 2) * LANES
                    rhs = jnp.where((lane // SSD_HEAD_DIM) == half, xdt[:, c0:c0 + LANES], 0.0).astype(BF16)
                    part = _dot(w, rhs)
                    acc = part if acc is None else acc + part
                c0 = (g * heads_per_group // 2 + pair) * LANES
                y_pair = acc + y_off[:, c0:c0 + LANES]
                if final:
                    y_pair = yf_ref[rows, c0:c0 + LANES] + y_pair + dsk_ref[:, c0:c0 + LANES] * xs[:, c0:c0 + LANES]
                    zz = z_ref[rows, c0:c0 + LANES]
                    y_pair = y_pair * (zz * _sigmoid(zz))
                y_ref[rows, c0:c0 + LANES] = y_pair

        cd = eax[L - 1:L, :] if direction == 0 else eax[0:1, :]
        s_new = _dot(bm.T.astype(BF16), xw)
        h_ref[...] = (cd * h_prev + s_new) * hm_ref[...]

    if final:
        gw = SSD_WIDTH // SSD_GROUPS
        for g in range(SSD_GROUPS):
            y = y_ref[:, g * gw:(g + 1) * gw]
            y = y * lax.rsqrt(jnp.mean(y * y, axis=-1, keepdims=True) + RMS_EPS)
            o_ref[:, g * gw:(g + 1) * gw] = (y * ng_ref[:, g * gw:(g + 1) * gw]).astype(o_ref.dtype)


def _ssd_pass(proj_a, fwd_outs, params, seq, direction):
    n = proj_a.shape[0]
    T = SSD_CHUNK * SSD_CHUNKS_PER_STEP
    tiles = seq // T
    nseq = n // seq
    final = direction == 1
    cw, cb_, dtb, alog, e_mat, hmask, dsk, ng = params
    bcw = 2 * SSD_GROUPS * SSD_STATE

    def row_blk(b, c):
        t = c if direction == 0 else tiles - 1 - c
        return b * tiles + t

    def main(width, col0):
        return pl.BlockSpec((T, width), lambda b, c: (row_blk(b, c), col0 // width))

    r = T // SUBLANES
    last_blk = n // SUBLANES - 1

    def halo(width, col0):
        prev = pl.BlockSpec((SUBLANES, width), lambda b, c: (jnp.maximum(row_blk(b, c) * r - 1, 0), col0 // width))
        nxt = pl.BlockSpec((SUBLANES, width), lambda b, c: (jnp.minimum((row_blk(b, c) + 1) * r, last_blk), col0 // width))
        return prev, nxt

    xsp, xsn = halo(SSD_WIDTH, A_XS)
    bcp, bcn = halo(bcw, A_BC)
    full = lambda a: pl.BlockSpec(a.shape, lambda b, c: (0,) * a.ndim)

    rows = lambda width: pl.BlockSpec((T, width), lambda b, c: (row_blk(b, c), 0))
    if final:
        y_fwd, xs_act, bc_act = fwd_outs
        in_specs = [rows(SSD_WIDTH), rows(bcw), main(LANES, A_DT), main(SSD_WIDTH, A_Z), rows(SSD_WIDTH)]
        args = [xs_act, bc_act, proj_a, proj_a, y_fwd]
        consts = [dtb, alog, e_mat, hmask, dsk, ng]
        out_specs = rows(SSD_WIDTH)
        out_shape = jax.ShapeDtypeStruct((n, SSD_WIDTH), BF16)
    else:
        in_specs = [main(SSD_WIDTH, A_XS), xsp, xsn, main(bcw, A_BC), bcp, bcn, main(LANES, A_DT)]
        args = [proj_a] * 7
        consts = [cw[:, :SSD_WIDTH], cb_[:, :SSD_WIDTH], cw[:, SSD_WIDTH:], cb_[:, SSD_WIDTH:], dtb, alog, e_mat, hmask]
        out_specs = [rows(SSD_WIDTH), rows(SSD_WIDTH), rows(bcw)]
        out_shape = [jax.ShapeDtypeStruct((n, SSD_WIDTH), F32), jax.ShapeDtypeStruct((n, SSD_WIDTH), F32),
                     jax.ShapeDtypeStruct((n, bcw), F32)]
    in_specs += [full(a) for a in consts]
    args += consts
    scratch = [pltpu.VMEM((LANES, SSD_WIDTH), F32)]
    if final:
        scratch.append(pltpu.VMEM((T, SSD_WIDTH), F32))
    return pl.pallas_call(
        functools.partial(_ssd_kernel, direction=direction, final=final, tiles=tiles),
        grid=(nseq, tiles),
        in_specs=in_specs,
        out_specs=out_specs,
        out_shape=out_shape,
        scratch_shapes=scratch,
        compiler_params=_cparams(("parallel", "arbitrary")),
        name="ssd_bwd" if final else "ssd_fwd",
    )(*args)


def _ssd_consts(ssd_conv_w, ssd_conv_b, ssd_dt_bias, ssd_a_log, ssd_d, ssd_norm_g):
    pad = LANES - 2 * SSD_HEADS
    dtb = jnp.pad(ssd_dt_bias.reshape(1, -1), ((0, 0), (0, pad)))
    alog = jnp.pad(ssd_a_log.reshape(1, -1), ((0, 0), (0, pad)))
    col_head = np.arange(SSD_WIDTH) // SSD_HEAD_DIM
    e_mats = [jnp.asarray((np.arange(LANES)[:, None] == d * SSD_HEADS + col_head[None, :]), BF16) for d in range(2)]
    row_group = np.arange(LANES) // SSD_STATE
    col_group = col_head // (SSD_HEADS // SSD_GROUPS)
    hmask = jnp.asarray(row_group[:, None] == col_group[None, :], F32)
    dsk = jnp.repeat(ssd_d, SSD_HEAD_DIM).reshape(1, SSD_WIDTH)
    ng = ssd_norm_g.reshape(1, SSD_WIDTH)
    cb_ = ssd_conv_b.reshape(1, -1)
    return [(ssd_conv_w, cb_, dtb, alog, e_mats[d], hmask, dsk, ng) for d in range(2)]


NA_KROWS = NA_QROWS + WIN_H


def _na_kernel(kb_ref, cls_ref, q_ref, k_ref, v_ref, bias_ref, o_ref, *, blocks_per_step):
    step = pl.program_id(2)
    lane = lax.broadcasted_iota(jnp.int32, (1, LANES), 1)
    nt = (((1,), (1,)), ((), ()))
    nq = NA_QROWS * GRID_W
    nk = NA_KROWS * GRID_W
    for j in range(blocks_per_step):
        blk = step * blocks_per_step + j
        k0 = pl.multiple_of(kb_ref[blk] * GRID_W, GRID_W)
        cls = cls_ref[blk]
        q = q_ref[j * nq:(j + 1) * nq, :] * (NA_HEAD_DIM ** -0.5)
        kk = k_ref[pl.ds(k0, nk), :]
        vv = v_ref[pl.ds(k0, nk), :]
        outs = []
        for hh in range(2):
            qm = jnp.where((lane // NA_HEAD_DIM) == hh, q, jnp.zeros_like(q))
            s = lax.dot_general(qm, kk, nt, preferred_element_type=F32) + bias_ref[cls, hh]
            m = jnp.max(s, axis=-1, keepdims=True)
            p = jnp.exp(s - m)
            l = jnp.sum(p, axis=-1, keepdims=True)
            outs.append(_dot(p.astype(BF16), vv) / l)
        o = jnp.where((lane // NA_HEAD_DIM) == 0, outs[0], outs[1])
        o_ref[j * nq:(j + 1) * nq, :] = o.astype(o_ref.dtype)


def _na_tables(rel_bias, rows):
    qc = np.arange(GRID_W)
    cs = np.clip(qc - WIN_W // 2, 0, GRID_W - WIN_W)
    kc = np.arange(GRID_W)
    col_valid = (kc[None, :] >= cs[:, None]) & (kc[None, :] < cs[:, None] + WIN_W)
    dj = np.clip(kc[None, :] - qc[:, None] + WIN_W - 1, 0, 2 * WIN_W - 2)
    nblk = rows // NA_QROWS
    kb = np.clip(np.arange(nblk) * NA_QROWS - WIN_H // 2, 0, rows - NA_KROWS)
    patterns, cls = [], []
    for b in range(nblk):
        r = b * NA_QROWS + np.arange(NA_QROWS)
        rs = np.clip(r - WIN_H // 2, 0, rows - WIN_H)
        key = (tuple(rs - kb[b]), int(kb[b] - b * NA_QROWS))
        if key not in patterns:
            patterns.append(key)
        cls.append(patterns.index(key))
    nh, nw = 2 * WIN_H - 1, 2 * WIN_W - 1
    bias_ext = jnp.pad(rel_bias.astype(F32), ((0, 0), (0, 1), (0, 1)), constant_values=NEG_BIG)
    di_sel = []
    for off, shift in patterns:
        kr = np.arange(NA_KROWS)[None, :]
        i = np.arange(NA_QROWS)[:, None]
        rel = kr - np.asarray(off)[:, None]
        row_valid = (rel >= 0) & (rel < WIN_H)
        di = np.where(row_valid, np.clip(shift + kr - i + WIN_H - 1, 0, nh - 1), nh)
        di_sel.append(np.eye(nh + 1, dtype=np.float32)[di])
    dj_sel = np.eye(nw + 1, dtype=np.float32)[np.where(col_valid, dj, nw)]
    tab = jnp.einsum('pikd,hde,qwe->phiqkw', np.stack(di_sel), bias_ext, dj_sel, precision=lax.Precision.HIGHEST)
    tab = tab.reshape(len(patterns), NA_HEADS, NA_QROWS * GRID_W, NA_KROWS * GRID_W)
    return jnp.asarray(kb, jnp.int32), jnp.asarray(cls, jnp.int32), tab


def _neighbourhood_attention(qkv, rel_bias, seq, blocks_per_step=NA_BLOCKS_PER_STEP):
    (q_arr, q0), (k_arr, k0), (v_arr, v0) = qkv
    n = q_arr.shape[0]
    nseq = n // seq
    rows = seq // GRID_W
    kb, cls, bias_tab = _na_tables(rel_bias, rows)
    tq = blocks_per_step * NA_QROWS * GRID_W
    pairs = NA_HEADS // 2
    steps = seq // tq
    ncls = bias_tab.shape[0]
    grid_spec = pltpu.PrefetchScalarGridSpec(
        num_scalar_prefetch=2,
        grid=(nseq, pairs, steps),
        in_specs=[pl.BlockSpec((tq, LANES), lambda b, p, r, kb_, cls_: (b * steps + r, q0 + p)),
                  pl.BlockSpec((seq, LANES), lambda b, p, r, kb_, cls_: (b, k0 + p)),
                  pl.BlockSpec((seq, LANES), lambda b, p, r, kb_, cls_: (b, v0 + p)),
                  pl.BlockSpec((ncls, 2) + bias_tab.shape[2:], lambda b, p, r, kb_, cls_: (0, p, 0, 0))],
        out_specs=pl.BlockSpec((tq, LANES), lambda b, p, r, kb_, cls_: (b * steps + r, p)),
    )
    return pl.pallas_call(
        functools.partial(_na_kernel, blocks_per_step=blocks_per_step),
        grid_spec=grid_spec,
        out_shape=jax.ShapeDtypeStruct((n, NA_WIDTH), F32),
        compiler_params=_cparams(("parallel", "arbitrary", "arbitrary")),
        name="nbr_attn",
    )(kb, cls, q_arr, k_arr, v_arr, bias_tab)


def _rms_rows(y, g):
    return y * lax.rsqrt(jnp.mean(y * y, axis=-1, keepdims=True) + RMS_EPS) * g


def _outproj_kernel(*refs, router, n_parts, first_tiles):
    yc_ref, ys_ref, yn_ref = refs[:3]
    x_refs = refs[3:3 + n_parts]
    rest = refs[3 + n_parts:]
    if router:
        w_ref, nag_ref, g_ref, b_ref, rw_ref, o_ref, idx_ref, gate_ref = rest
    else:
        w_ref, nag_ref, g_ref, b_ref, o_ref = rest
    c1 = CONV_WIDTH
    c2 = CONV_WIDTH + SSD_WIDTH
    rows = o_ref.shape[0] // OUTPROJ_ROW_SPLITS
    for part in range(OUTPROJ_ROW_SPLITS):
        sl = slice(part * rows, (part + 1) * rows)
        yn = _rms_rows(yn_ref[sl, :], nag_ref[...]).astype(BF16)
        mix = _dot(yc_ref[sl, :], w_ref[0:c1, :]) + _dot(ys_ref[sl, :], w_ref[c1:c2, :]) + _dot(yn, w_ref[c2:, :])
        out = _layer_norm(ALPHA * _read_row_parts(x_refs, first_tiles, sl) + mix, g_ref[...], b_ref[...])
        o_ref[sl, :] = out
        if not router:
            continue
        o1, o2, _ = _split3(out)
        t1 = _dot(o1, rw_ref[...])
        t2 = _dot(o2, rw_ref[...])
        sh1 = LANES - N_EXPERTS
        sh2 = LANES - 2 * N_EXPERTS
        logits = ((pltpu.roll(t2, sh1, 1) + pltpu.roll(t1, sh2, 1)) + (t2 + pltpu.roll(t1, sh1, 1))) + t1
        lane = lax.broadcasted_iota(jnp.int32, (rows, LANES), 1)
        lg = jnp.where(lane < N_EXPERTS, logits, -jnp.inf)
        m1 = jnp.max(lg, axis=-1, keepdims=True)
        i1 = jnp.min(jnp.where(lg == m1, lane, LANES), axis=-1, keepdims=True)
        lg2 = jnp.where(lane == i1, -jnp.inf, lg)
        m2 = jnp.max(lg2, axis=-1, keepdims=True)
        i2 = jnp.min(jnp.where(lg2 == m2, lane, LANES), axis=-1, keepdims=True)
        e2 = jnp.exp(m2 - m1)
        den = 1.0 + e2
        idx_ref[sl, :] = jnp.where(lane == 0, i1, jnp.where(lane == 1, i2, 0))
        gate_ref[sl, :] = jnp.where(lane == 0, 1.0 / den, jnp.where(lane == 1, e2 / den, 0.0))


def _out_proj_ln(yc, ys, yn, x_parts, w_out, na_g, ln_g, ln_b, router_w=None, tm=ROW_TILE):
    n = yc.shape[0]
    router = router_w is not None
    row = lambda w: pl.BlockSpec((tm, w), lambda i: (i, 0))
    full = lambda a: pl.BlockSpec(a.shape, lambda i: (0,) * a.ndim)
    consts = [w_out, na_g.reshape(1, -1), ln_g.reshape(1, -1), ln_b.reshape(1, -1)]
    if router:
        consts.append(router_w)
    out_shape = [jax.ShapeDtypeStruct((n, D_MODEL), F32)]
    out_specs = [row(D_MODEL)]
    if router:
        out_shape += [jax.ShapeDtypeStruct((n, LANES), jnp.int32), jax.ShapeDtypeStruct((n, LANES), F32)]
        out_specs += [row(LANES), row(LANES)]
    res = pl.pallas_call(
        functools.partial(_outproj_kernel, router=router, n_parts=len(x_parts),
                          first_tiles=x_parts[0].shape[0] // tm),
        grid=(n // tm,),
        in_specs=[row(CONV_WIDTH), row(SSD_WIDTH), row(NA_WIDTH)] + _row_parts_specs(x_parts, tm, D_MODEL, 1)
        + [full(a) for a in consts],
        out_specs=out_specs,
        out_shape=out_shape,
        compiler_params=_cparams(("parallel",)),
        name="out_proj_router" if router else "out_proj",
    )(yc, ys, yn, *x_parts, *consts)
    return res


def _ffn_kernel(be_ref, bv_ref, x_ref, wg_ref, wu_ref, wd_ref, *rest, packed, ln, ragged):
    if ln:
        g_ref, b_ref, o_ref = rest
    else:
        (o_ref,) = rest
    i = pl.program_id(0)
    j = pl.program_id(1)

    @pl.when(j == 0)
    def _():
        o_ref[...] = jnp.zeros_like(o_ref)

    def accumulate(rows):
        if packed:
            lo, hi = _unpack_bf16_pairs(x_ref[rows, :])
            c = lo.shape[1]
            g = _dot(lo, wg_ref[0, :c, :].astype(BF16)) + _dot(hi, wg_ref[0, c:, :].astype(BF16))
            u = _dot(lo, wu_ref[0, :c, :].astype(BF16)) + _dot(hi, wu_ref[0, c:, :].astype(BF16))
        else:
            x = x_ref[rows, :].astype(BF16)
            g = _dot(x, wg_ref[0].astype(BF16))
            u = _dot(x, wu_ref[0].astype(BF16))
        h = ((g * _sigmoid(g)) * u).astype(BF16)
        o_ref[rows, :] += _dot(h, wd_ref[0].astype(BF16))

    tm = o_ref.shape[0]
    filled = bv_ref[i]

    @pl.when(filled == tm)
    def _():
        accumulate(slice(None))

    if ragged:
        for s in range(tm // FFN_ROW_GROUP):
            @pl.when(jnp.logical_and(filled < tm, filled > s * FFN_ROW_GROUP))
            def _():
                accumulate(slice(s * FFN_ROW_GROUP, (s + 1) * FFN_ROW_GROUP))

    if ln:
        @pl.when(j == pl.num_programs(1) - 1)
        def _():
            o_ref[...] = _layer_norm(ALPHA * x_ref[...] + o_ref[...], g_ref[...], b_ref[...])


def _ffn_blocks(x, block_e, block_fill, w_gate, w_up, w_down, tm, tf=FF_TILE, residual_ln=None, ragged=False):
    n, xw = x.shape
    packed = x.dtype == jnp.uint32
    d = w_gate.shape[1]
    f = w_gate.shape[2]
    nf = f // tf
    col = lambda i, j, be, bv: (be[i], 0, jnp.where(bv[i] != 0, j, nf - 1))
    rowj = lambda i, j, be, bv: (be[i], jnp.where(bv[i] != 0, j, nf - 1), 0)
    in_specs = [pl.BlockSpec((tm, xw), lambda i, j, be, bv: (i, 0)),
                pl.BlockSpec((1, d, tf), col), pl.BlockSpec((1, d, tf), col), pl.BlockSpec((1, tf, d), rowj)]
    args = [x, w_gate, w_up, w_down]
    if residual_ln is not None:
        gain, bias = residual_ln
        assert not packed
        vec = pl.BlockSpec((1, d), lambda i, j, be, bv: (0, 0))
        in_specs += [vec, vec]
        args += [gain.reshape(1, d), bias.reshape(1, d)]
    grid_spec = pltpu.PrefetchScalarGridSpec(
        num_scalar_prefetch=2,
        grid=(n // tm, nf),
        in_specs=in_specs,
        out_specs=pl.BlockSpec((tm, d), lambda i, j, be, bv: (i, 0)),
    )
    return pl.pallas_call(
        functools.partial(_ffn_kernel, packed=packed, ln=residual_ln is not None, ragged=ragged),
        grid_spec=grid_spec,
        out_shape=jax.ShapeDtypeStruct((n, d), F32),
        compiler_params=_cparams(("parallel", "arbitrary")),
        name="swiglu_blocks",
    )(block_e, block_fill, *args)


def _combine_ln_kernel(i0_ref, i1_ref, n0_ref, n1_ref, src_ref, x_ref, gate_ref, g_ref, b_ref, oa_ref, ob_ref,
                       b0_ref, b1_ref, sem, *, tm, first_tiles):
    i = pl.program_id(0)
    slot = i % 2

    def start(idx0_ref, idx1_ref, s):
        _start_row_gather(src_ref, idx0_ref, b0_ref.at[s], sem.at[s, 0], tm)
        _start_row_gather(src_ref, idx1_ref, b1_ref.at[s], sem.at[s, 1], tm)

    @pl.when(i == 0)
    def _():
        start(i0_ref, i1_ref, 0)

    @pl.when(i + 1 < pl.num_programs(0))
    def _():
        start(n0_ref, n1_ref, 1 - slot)

    _wait_row_gather(src_ref, b0_ref.at[slot], sem.at[slot, 0], tm)
    _wait_row_gather(src_ref, b1_ref.at[slot], sem.at[slot, 1], tm)
    f = b0_ref[slot] * gate_ref[:, 0:1] + b1_ref[slot] * gate_ref[:, 1:2]
    res = _layer_norm(ALPHA * x_ref[...] + f, g_ref[...], b_ref[...])

    @pl.when(i < first_tiles)
    def _():
        oa_ref[...] = res

    @pl.when(i >= first_tiles)
    def _():
        ob_ref[...] = res


def _combine_ln(x, outs, d0, d1, gates, g, b, n_first, tm=ROW_TILE):
    n, d = x.shape
    nt = n // tm
    first_tiles = n_first // tm
    row = pl.BlockSpec((tm, d), lambda i: (i, 0))
    vec = pl.BlockSpec((1, d), lambda i: (0, 0))
    idx = pl.BlockSpec((1, 1, tm), lambda i: (i, 0, 0), memory_space=pltpu.SMEM)
    idx_next = pl.BlockSpec((1, 1, tm), lambda i: (jnp.minimum(i + 1, nt - 1), 0, 0), memory_space=pltpu.SMEM)
    d0, d1 = d0.reshape(nt, 1, tm), d1.reshape(nt, 1, tm)
    return pl.pallas_call(
        functools.partial(_combine_ln_kernel, tm=tm, first_tiles=first_tiles), grid=(nt,),
        in_specs=[idx, idx, idx_next, idx_next, pl.BlockSpec(memory_space=pl.ANY), row,
                  pl.BlockSpec((tm, LANES), lambda i: (i, 0)), vec, vec],
        out_specs=[pl.BlockSpec((tm, d), lambda i: (jnp.minimum(i, first_tiles - 1), 0)),
                   pl.BlockSpec((tm, d), lambda i: (jnp.maximum(i - first_tiles, 0), 0))],
        out_shape=[jax.ShapeDtypeStruct((n_first, d), F32), jax.ShapeDtypeStruct((n - n_first, d), F32)],
        scratch_shapes=[pltpu.VMEM((2, tm, d), F32), pltpu.VMEM((2, tm, d), F32), pltpu.SemaphoreType.DMA((2, 2))],
        compiler_params=_cparams(("arbitrary",)), name="combine_ln",
    )(d0, d1, d0, d1, outs, x, gates, g.reshape(1, d), b.reshape(1, d))


def _start_row_gather(src_ref, idx_ref, buf_ref, sem, tr):
    def issue(r, carry):
        pltpu.make_async_copy(src_ref.at[pl.ds(idx_ref[0, 0, r], 1), :], buf_ref.at[pl.ds(r, 1), :], sem).start()
        return carry

    lax.fori_loop(0, tr, issue, 0, unroll=GATHER_UNROLL)


def _wait_row_gather(src_ref, buf_ref, sem, tr):
    pltpu.make_async_copy(src_ref.at[pl.ds(0, tr), :], buf_ref, sem).wait()


HI16 = 0xFFFF0000


def _pack_bf16_pairs(x):
    c = x.shape[1] // 2
    lo = lax.bitcast_convert_type(x[:, :c].astype(BF16).astype(F32), jnp.uint32)
    hi = lax.bitcast_convert_type(x[:, c:].astype(BF16).astype(F32), jnp.uint32)
    return (hi & jnp.uint32(HI16)) | (lo >> 16)


def _unpack_bf16_pairs(u):
    lo = lax.bitcast_convert_type(u << 16, F32).astype(BF16)
    hi = lax.bitcast_convert_type(u & jnp.uint32(HI16), F32).astype(BF16)
    return lo, hi


def _zero_row_runs(first, length, max_len, z_ref, o_ref, sem):
    zrows = z_ref.shape[0]

    def copy(row, rows):
        return pltpu.make_async_copy(z_ref.at[pl.ds(0, rows), :], o_ref.at[pl.ds(row, rows), :], sem)

    def for_each_run(action):
        head = jnp.minimum((-first) % SUBLANES, length)
        tail = (length - head) % SUBLANES
        body = length - head - tail
        for k in range(SUBLANES - 1):
            @pl.when(k < head)
            def _(k=k):
                action(copy(first + k, 1))
        row = first + head
        for bit in range(SUBLANES.bit_length() - 1, max_len.bit_length()):
            size = 1 << bit
            take = (body >> bit) & 1
            rows = min(size, zrows)
            for piece in range(size // rows):
                @pl.when(take == 1)
                def _(row=row, piece=piece, rows=rows):
                    action(copy(pl.multiple_of(row + piece * rows, SUBLANES), rows))
            row = row + take * size
        for k in range(SUBLANES - 1):
            @pl.when(k < tail)
            def _(k=k, row=row):
                action(copy(row + k, 1))

    for_each_run(lambda c: c.start())
    for_each_run(lambda c: c.wait())


def _dispatch_kernel(pad_start_ref, pad_len_ref, d0_ref, d1_ref, x_ref, xs_ref, pk_ref, z_ref, sem, zsem,
                     *, tm, max_pad):
    @pl.when(pl.program_id(0) == 0)
    def _():
        z_ref[...] = jnp.zeros_like(z_ref)

        def zero_region(r, carry):
            _zero_row_runs(pad_start_ref[r], pad_len_ref[r], max_pad, z_ref, xs_ref, zsem)
            return carry

        lax.fori_loop(0, pad_start_ref.shape[0], zero_region, 0)

    pk_ref[...] = _pack_bf16_pairs(x_ref[...])

    def issue(r, carry):
        row = pk_ref.at[pl.ds(r, 1), :]
        pltpu.make_async_copy(row, xs_ref.at[pl.ds(d0_ref[0, 0, r], 1), :], sem.at[0]).start()
        pltpu.make_async_copy(row, xs_ref.at[pl.ds(d1_ref[0, 0, r], 1), :], sem.at[1]).start()
        return carry

    lax.fori_loop(0, tm, issue, 0, unroll=GATHER_UNROLL)
    for k in range(TOP_K):
        pltpu.make_async_copy(pk_ref, xs_ref.at[pl.ds(0, tm), :], sem.at[k]).wait()


def _dispatch_rows(x, d0, d1, pad_start, pad_len, p_len, max_pad, tm=ROW_TILE):
    n, d = x.shape
    idx = pl.BlockSpec((1, 1, tm), lambda i, ps, pn: (i, 0, 0), memory_space=pltpu.SMEM)
    grid_spec = pltpu.PrefetchScalarGridSpec(
        num_scalar_prefetch=2,
        grid=(n // tm,),
        in_specs=[idx, idx, pl.BlockSpec((tm, d), lambda i, ps, pn: (i, 0))],
        out_specs=pl.BlockSpec(memory_space=pl.ANY),
        scratch_shapes=[pltpu.VMEM((tm, d // 2), jnp.uint32), pltpu.VMEM((tm, d // 2), jnp.uint32),
                        pltpu.SemaphoreType.DMA((TOP_K,)), pltpu.SemaphoreType.DMA(())],
    )
    return pl.pallas_call(
        functools.partial(_dispatch_kernel, tm=tm, max_pad=max_pad),
        grid_spec=grid_spec,
        out_shape=jax.ShapeDtypeStruct((p_len, d // 2), jnp.uint32),
        compiler_params=_cparams(("arbitrary",)),
        name="dispatch_rows",
    )(pad_start, pad_len, d0.reshape(n // tm, 1, tm), d1.reshape(n // tm, 1, tm), x)


def _routing_tables(idx, n, tm):
    flat_e = idx[:, :TOP_K].reshape(-1)
    onehot = (flat_e[:, None] == jnp.arange(N_EXPERTS, dtype=jnp.int32)[None, :]).astype(jnp.int32)
    incl = jnp.cumsum(onehot, axis=0)
    rank = jnp.sum((incl - onehot) * onehot, axis=1)
    counts = incl[-1]
    padded = ((counts + tm - 1) // tm) * tm
    pend = jnp.cumsum(padded)
    pstart = pend - padded
    dest = pstart[flat_e] + rank
    n_blocks = -(-(n * TOP_K) // tm) + N_EXPERTS
    starts = jnp.arange(n_blocks, dtype=jnp.int32) * tm
    block_e = jnp.sum((pend[None, :] <= starts[:, None]).astype(jnp.int32), axis=1)
    block_e = jnp.minimum(block_e, N_EXPERTS - 1)
    block_fill = jnp.clip((pstart + counts)[block_e] - starts, 0, tm).astype(jnp.int32)
    block_fill = jnp.where(starts < pend[-1], block_fill, 0)
    last_e = block_e[jnp.maximum(jnp.sum((block_fill > 0).astype(jnp.int32)) - 1, 0)]
    block_e = jnp.where(block_fill > 0, block_e, last_e)
    dest2 = dest.reshape(n, TOP_K)
    pad_start = jnp.concatenate([pstart + counts, pend[-1:]]).astype(jnp.int32)
    pad_len = jnp.concatenate([padded - counts, n_blocks * tm - pend[-1:]]).astype(jnp.int32)
    return (pad_start, pad_len, n_blocks * tm), block_e, block_fill, dest2[:, 0], dest2[:, 1]


def _prep_w_in(w):
    bc_w = 2 * SSD_GROUPS * SSD_STATE
    widths = (CONV_WIDTH, CONV_WIDTH, CONV_WIDTH, SSD_WIDTH, SSD_WIDTH, bc_w, 2 * SSD_HEADS, 3 * NA_WIDTH)
    cb, cc, ch, z, xs, bc, dt, qkv = jnp.split(w.astype(BF16), [int(o) for o in np.cumsum(widths)[:-1]], axis=1)
    pad = jnp.zeros((D_MODEL, A_WIDTH - (A_DT + 2 * SSD_HEADS)), BF16)
    return jnp.concatenate([z, xs, cb, cc, ch, bc, dt, pad, qkv], axis=1)


def _prep_router_w(rw):
    r1 = rw.astype(BF16)
    r2 = (rw - r1.astype(F32)).astype(BF16)
    r3 = (rw - r1.astype(F32) - r2.astype(F32)).astype(BF16)
    pad = jnp.zeros((rw.shape[0], LANES - 3 * N_EXPERTS), BF16)
    return jnp.concatenate([r1, r2, r3, pad], axis=1)


def _trunk(x_parts, seq, p):
    n = sum(a.shape[0] for a in x_parts)
    n_first = x_parts[0].shape[0]
    for l in range(DEPTH):
        proj_a, qkv = _in_proj(x_parts, _prep_w_in(p['w_in'][l]))
        y_conv = _conv_mix(proj_a, p['conv_w'][l], p['conv_norm_g'][l], seq)
        cf, cbw = _ssd_consts(p['ssd_conv_w'][l], p['ssd_conv_b'][l], p['ssd_dt_bias'][l], p['ssd_a_log'][l],
                              p['ssd_d'][l], p['ssd_norm_g'][l])
        y_ssd = _ssd_pass(proj_a, _ssd_pass(proj_a, None, cf, seq, 0), cbw, seq, 1)
        y_na = _neighbourhood_attention(qkv, p['na_rel_bias'][l], seq)
        w_out = p['w_out'][l].astype(BF16)
        if l % 2 == 0:
            x = _out_proj_ln(y_conv, y_ssd, y_na, x_parts, w_out, p['na_norm_g'][l], p['ln_mix_g'][l],
                             p['ln_mix_b'][l])[0]
            padf = ((0, 0), (0, D_FF_PAD - D_FF))
            wg = jnp.pad(p['ffn_w_gate'][l // 2].astype(BF16), padf)[None]
            wu = jnp.pad(p['ffn_w_up'][l // 2].astype(BF16), padf)[None]
            wd = jnp.pad(p['ffn_w_down'][l // 2].astype(BF16), ((0, D_FF_PAD - D_FF), (0, 0)))[None]
            nb = n // MOE_TM
            x = _ffn_blocks(x, jnp.zeros((nb,), jnp.int32), jnp.full((nb,), MOE_TM, jnp.int32), wg, wu, wd, MOE_TM,
                            residual_ln=(p['ln_ffn_g'][l], p['ln_ffn_b'][l]))
            outs_split = (x[:n_first], x[n_first:])
        else:
            x, idx, gates = _out_proj_ln(y_conv, y_ssd, y_na, x_parts, w_out, p['na_norm_g'][l], p['ln_mix_g'][l],
                                         p['ln_mix_b'][l], _prep_router_w(p['router_w'][l // 2]))
            (pad_start, pad_len, p_len), block_e, block_fill, d0, d1 = _routing_tables(idx, n, MOE_TM)
            xs = _dispatch_rows(x, d0, d1, pad_start, pad_len, p_len, N_EXPERTS * MOE_TM)
            outs = _ffn_blocks(xs, block_e, block_fill, p['moe_w_gate'][l // 2], p['moe_w_up'][l // 2],
                               p['moe_w_down'][l // 2], MOE_TM, ragged=True)
            outs_split = _combine_ln(x, outs, d0, d1, gates, p['ln_ffn_g'][l], p['ln_ffn_b'][l], n_first)
            if l + 1 < DEPTH:
                x = jnp.concatenate(outs_split, axis=0)
        x_parts = [x]
    return outs_split


def kernel(x_prompt, x_sample, w_in, conv_w, conv_norm_g, ssd_conv_w, ssd_conv_b, ssd_dt_bias, ssd_a_log, ssd_d,
           ssd_norm_g, na_rel_bias, na_norm_g, w_out, ln_mix_g, ln_mix_b, ln_ffn_g, ln_ffn_b, ffn_w_gate, ffn_w_up,
           ffn_w_down, router_w, moe_w_gate, moe_w_up, moe_w_down):
    p = dict(w_in=w_in, conv_w=conv_w, conv_norm_g=conv_norm_g, ssd_conv_w=ssd_conv_w, ssd_conv_b=ssd_conv_b,
             ssd_dt_bias=ssd_dt_bias, ssd_a_log=ssd_a_log, ssd_d=ssd_d, ssd_norm_g=ssd_norm_g,
             na_rel_bias=na_rel_bias, na_norm_g=na_norm_g, w_out=w_out, ln_mix_g=ln_mix_g, ln_mix_b=ln_mix_b,
             ln_ffn_g=ln_ffn_g, ln_ffn_b=ln_ffn_b, ffn_w_gate=ffn_w_gate, ffn_w_up=ffn_w_up, ffn_w_down=ffn_w_down,
             router_w=router_w, moe_w_gate=moe_w_gate, moe_w_up=moe_w_up, moe_w_down=moe_w_down)
    bp, seq, d = x_prompt.shape
    bs, seq_s, _ = x_sample.shape
    assert seq == seq_s, "both request groups must share one sequence length"
    yp, ys = _trunk([x_prompt.reshape(bp * seq, d), x_sample.reshape(bs * seq, d)], seq, p)
    return yp.reshape(bp, seq, d), ys.reshape(bs, seq, d)
```

```python
import functools

import numpy as np
import jax
import jax.numpy as jnp
from jax import lax
from jax.experimental import pallas as pl
from jax.experimental.pallas import tpu as pltpu

F32 = jnp.float32
BF16 = jnp.bfloat16

D_MODEL = 2048
DEPTH = 2
GRID_W = 64
CONV_WIDTH = 512
SSD_HEADS = 16
SSD_HEAD_DIM = 64
SSD_WIDTH = 1024
SSD_GROUPS = 2
SSD_STATE = 64
SSD_CHUNK = 128
NA_HEADS = 8
NA_HEAD_DIM = 64
NA_WIDTH = 512
WIN_H = 8
WIN_W = 16
D_FF = 5504
N_EXPERTS = 8
TOP_K = 2
D_FF_EXPERT = 7168
ALPHA = (2 * DEPTH) ** 0.25
LN_EPS = 1e-5
RMS_EPS = 1e-6

LANES = 128
SUBLANES = 8
NEG_BIG = -1e30

A_Z, A_XS, A_CB, A_CC, A_CH, A_BC, A_DT, A_WIDTH = 0, 1024, 2048, 2560, 3072, 3584, 3840, 4096

VMEM_LIMIT = 56 * 1024 * 1024
IN_PROJ_TM, IN_PROJ_TN = 1024, 512
ROW_TILE = 512
CONV_TT = 1024
SSD_CHUNKS_PER_STEP = 8
NA_QROWS = 4
NA_BLOCKS_PER_STEP = 16
OUTPROJ_ROW_SPLITS = 2
MOE_TM = 1024
FF_TILE = 512
D_FF_PAD = -(-D_FF // FF_TILE) * FF_TILE
FFN_ROW_GROUP = 256
GATHER_UNROLL = 8


def _cparams(sem, vmem=VMEM_LIMIT):
    return pltpu.CompilerParams(dimension_semantics=sem, vmem_limit_bytes=vmem)


def _sigmoid(x):
    return 1.0 / (1.0 + jnp.exp(-x))


def _split3(x):
    x1 = x.astype(BF16)
    r1 = x - x1.astype(F32)
    x2 = r1.astype(BF16)
    r2 = r1 - x2.astype(F32)
    return x1, x2, r2.astype(BF16)


def _dot(a, b):
    return jnp.dot(a, b, preferred_element_type=F32)


def _dot_exact_rhs(m, x):
    x1, x2, x3 = _split3(x)
    return (_dot(m, x3) + _dot(m, x2)) + _dot(m, x1)


def _dot_split2_lhs(x, m):
    x1, x2, _ = _split3(x)
    return _dot(x2, m) + _dot(x1, m)


def _log1p(e):
    u = 1.0 + e
    d = u - 1.0
    return jnp.where(d == 0.0, e, jnp.log(u) * (e / jnp.where(d == 0.0, 1.0, d)))


def _layer_norm(v, g, b):
    mu = jnp.mean(v, axis=-1, keepdims=True)
    c = v - mu
    var = jnp.mean(c * c, axis=-1, keepdims=True)
    return c * lax.rsqrt(var + LN_EPS) * g + b


def _row_parts_specs(parts, tm, width, grid_rank):
    if len(parts) == 1:
        return [pl.BlockSpec((tm, width), (lambda i: (i, 0)) if grid_rank == 1 else (lambda i, j: (i, 0)))]
    first_tiles = parts[0].shape[0] // tm
    if grid_rank == 1:
        return [pl.BlockSpec((tm, width), lambda i: (jnp.minimum(i, first_tiles - 1), 0)),
                pl.BlockSpec((tm, width), lambda i: (jnp.maximum(i - first_tiles, 0), 0))]
    return [pl.BlockSpec((tm, width), lambda i, j: (jnp.minimum(i, first_tiles - 1), 0)),
            pl.BlockSpec((tm, width), lambda i, j: (jnp.maximum(i - first_tiles, 0), 0))]


def _read_row_parts(x_refs, first_tiles, rows=slice(None)):
    x = x_refs[0][rows, :]
    if len(x_refs) == 2:
        x = jnp.where(pl.program_id(0) < first_tiles, x, x_refs[1][rows, :])
    return x


def _inproj_kernel(*refs, n_parts, first_tiles, a_blocks):
    x_refs = refs[:n_parts]
    w_ref, oa_ref, ob_ref, xb_ref = refs[n_parts:]
    j = pl.program_id(1)

    @pl.when(j == 0)
    def _():
        xb_ref[...] = _read_row_parts(x_refs, first_tiles).astype(BF16)

    res = _dot(xb_ref[...], w_ref[...])

    @pl.when(j < a_blocks)
    def _():
        oa_ref[...] = res

    @pl.when(j >= a_blocks)
    def _():
        ob_ref[...] = res.astype(ob_ref.dtype)


def _inproj_pair_kernel(x_ref, w0_ref, w1_ref, oa_ref, oqk_ref, ov_ref, xb_ref, *, a_steps):
    j = pl.program_id(1)
    tn = w0_ref.shape[1]

    @pl.when(j == 0)
    def _():
        xb_ref[...] = x_ref[...].astype(BF16)

    @pl.when(j < a_steps)
    def _():
        xb = xb_ref[...]
        oa_ref[:, :tn] = _dot(xb, w0_ref[...])
        oa_ref[:, tn:] = _dot(xb, w1_ref[...])

    @pl.when(j == a_steps)
    def _():
        xb = xb_ref[...]
        oqk_ref[:, :tn] = _dot(xb, w0_ref[...]).astype(oqk_ref.dtype)
        oqk_ref[:, tn:] = _dot(xb, w1_ref[...]).astype(oqk_ref.dtype)

    @pl.when(j == a_steps + 1)
    def _():
        ov_ref[...] = _dot(xb_ref[...], w0_ref[...]).astype(ov_ref.dtype)


def _in_proj_paired(x, w, tm, tn):
    n, k = x.shape
    tiles = w.shape[1] // tn
    a_steps = A_WIDTH // (2 * tn)
    assert tiles == 2 * a_steps + 3 and NA_WIDTH == tn
    wspec = lambda off: pl.BlockSpec((k, tn), lambda i, j: (0, jnp.minimum(2 * j + off, tiles - 1)))
    proj_a, qk, v = pl.pallas_call(
        functools.partial(_inproj_pair_kernel, a_steps=a_steps),
        grid=(n // tm, a_steps + 2),
        in_specs=[pl.BlockSpec((tm, k), lambda i, j: (i, 0)), wspec(0), wspec(1)],
        out_specs=[pl.BlockSpec((tm, 2 * tn), lambda i, j: (i, jnp.minimum(j, a_steps - 1))),
                   pl.BlockSpec((tm, 2 * tn), lambda i, j: (i, 0)),
                   pl.BlockSpec((tm, tn), lambda i, j: (i, 0))],
        out_shape=[jax.ShapeDtypeStruct((n, A_WIDTH), F32), jax.ShapeDtypeStruct((n, 2 * tn), BF16),
                   jax.ShapeDtypeStruct((n, tn), BF16)],
        scratch_shapes=[pltpu.VMEM((tm, k), BF16)],
        compiler_params=_cparams(("parallel", "arbitrary")),
        name="in_proj_paired",
    )(x, w, w)
    blocks = NA_WIDTH // LANES
    return proj_a, ((qk, 0), (qk, blocks), (v, 0))


def _in_proj(x_parts, w, tm=IN_PROJ_TM, tn=IN_PROJ_TN):
    if len(x_parts) == 1:
        return _in_proj_paired(x_parts[0], w, tm, tn)
    n = sum(a.shape[0] for a in x_parts)
    k = x_parts[0].shape[1]
    a_blocks = A_WIDTH // tn
    b_width = w.shape[1] - A_WIDTH
    first_tiles = x_parts[0].shape[0] // tm
    proj_a, qkv = pl.pallas_call(
        functools.partial(_inproj_kernel, n_parts=len(x_parts), first_tiles=first_tiles, a_blocks=a_blocks),
        grid=(n // tm, w.shape[1] // tn),
        in_specs=_row_parts_specs(x_parts, tm, k, 2) + [pl.BlockSpec((k, tn), lambda i, j: (0, j))],
        out_specs=[pl.BlockSpec((tm, tn), lambda i, j: (i, jnp.minimum(j, a_blocks - 1))),
                   pl.BlockSpec((tm, tn), lambda i, j: (i, jnp.maximum(j - a_blocks, 0)))],
        out_shape=[jax.ShapeDtypeStruct((n, A_WIDTH), F32), jax.ShapeDtypeStruct((n, b_width), BF16)],
        scratch_shapes=[pltpu.VMEM((tm, k), BF16)],
        compiler_params=_cparams(("parallel", "arbitrary")),
        name="in_proj",
    )(*x_parts, w)
    blocks = NA_WIDTH // LANES
    return proj_a, ((qkv, 0), (qkv, blocks), (qkv, 2 * blocks))


def _shift_rows(x, prev_row, next_row, first, last):
    n = x.shape[0]
    row = lax.broadcasted_iota(jnp.int32, (n, 1), 0)
    prev_row = jnp.where(first, 0.0, prev_row)
    next_row = jnp.where(last, 0.0, next_row)
    x_prev = jnp.where(row == 0, prev_row, pltpu.roll(x, 1, 0))
    x_next = jnp.where(row == n - 1, next_row, pltpu.roll(x, n - 1, 0))
    return x_prev, x_next


def _convmix_kernel(cb_ref, cc_ref, ch_ref, ccp_ref, chp_ref, ccn_ref, chn_ref, w_ref, g_ref, o_ref, *, tiles_per_seq):
    i = pl.program_id(0)
    first = (i % tiles_per_seq) == 0
    last = (i % tiles_per_seq) == tiles_per_seq - 1
    u = cc_ref[...] * ch_ref[...]
    up = ccp_ref[SUBLANES - 1:SUBLANES, :] * chp_ref[SUBLANES - 1:SUBLANES, :]
    un = ccn_ref[0:1, :] * chn_ref[0:1, :]
    u_prev, u_next = _shift_rows(u, up, un, first, last)
    conv = u_prev * w_ref[0:1, :] + u * w_ref[1:2, :] + u_next * w_ref[2:3, :]
    y = cb_ref[...] * conv
    y = y * lax.rsqrt(jnp.mean(y * y, axis=-1, keepdims=True) + RMS_EPS)
    o_ref[...] = (y * g_ref[...]).astype(o_ref.dtype)


def _halo_specs(tt, width, col0, n_rows):
    cb = col0 // width
    r = tt // SUBLANES
    last_blk = n_rows // SUBLANES - 1
    prev = pl.BlockSpec((SUBLANES, width), lambda i: (jnp.maximum(i * r - 1, 0), cb))
    nxt = pl.BlockSpec((SUBLANES, width), lambda i: (jnp.minimum((i + 1) * r, last_blk), cb))
    return prev, nxt


def _conv_mix(proj_a, conv_w, norm_g, seq, tt=CONV_TT):
    n = proj_a.shape[0]
    w = CONV_WIDTH
    main = lambda c0: pl.BlockSpec((tt, w), lambda i: (i, c0 // w))
    ccp, ccn = _halo_specs(tt, w, A_CC, n)
    chp, chn = _halo_specs(tt, w, A_CH, n)
    full = lambda shape: pl.BlockSpec(shape, lambda i: (0, 0))
    return pl.pallas_call(
        functools.partial(_convmix_kernel, tiles_per_seq=seq // tt),
        grid=(n // tt,),
        in_specs=[main(A_CB), main(A_CC), main(A_CH), ccp, chp, ccn, chn, full((3, w)), full((1, w))],
        out_specs=pl.BlockSpec((tt, w), lambda i: (i, 0)),
        out_shape=jax.ShapeDtypeStruct((n, w), BF16),
        compiler_params=_cparams(("parallel",)),
        name="conv_mix",
    )(proj_a, proj_a, proj_a, proj_a, proj_a, proj_a, proj_a, conv_w, norm_g.reshape(1, w))


def _ssd_kernel(*refs, direction, final, tiles):
    if final:
        (xs_s, bc_s, dt_ref, z_ref, yf_ref, dtb_ref, alog_ref, e_ref, hm_ref, dsk_ref, ng_ref,
         o_ref, h_ref, y_ref) = refs
    else:
        (xs_ref, xsp_ref, xsn_ref, bc_ref, bcp_ref, bcn_ref, dt_ref,
         cwx_ref, cbx_ref, cwb_ref, cbb_ref, dtb_ref, alog_ref, e_ref, hm_ref,
         o_ref, xs_s, bc_s, h_ref) = refs
        y_ref = o_ref
    L = SSD_CHUNK
    c = pl.program_id(1)
    tile = c if direction == 0 else tiles - 1 - c
    first = tile == 0
    last = tile == tiles - 1

    @pl.when(c == 0)
    def _():
        h_ref[...] = jnp.zeros_like(h_ref)

    def conv_silu(x_ref, p_ref, n_ref, w_ref, b_ref):
        x = x_ref[...]
        x_prev, x_next = _shift_rows(x, p_ref[SUBLANES - 1:SUBLANES, :], n_ref[0:1, :], first, last)
        v = x_prev * w_ref[0:1, :] + x * w_ref[1:2, :] + x_next * w_ref[2:3, :] + b_ref[...]
        return v * _sigmoid(v)

    if not final:
        xs_s[...] = conv_silu(xs_ref, xsp_ref, xsn_ref, cwx_ref, cbx_ref)
        bc_s[...] = conv_silu(bc_ref, bcp_ref, bcn_ref, cwb_ref, cbb_ref)

    li = lax.broadcasted_iota(jnp.int32, (L, L), 0)
    si = lax.broadcasted_iota(jnp.int32, (L, L), 1)
    valid = (si <= li) if direction == 0 else (si >= li)
    tri = jnp.where(valid, 1.0, 0.0).astype(BF16)
    lane = lax.broadcasted_iota(jnp.int32, (1, LANES), 1)
    nt = (((1,), (1,)), ((), ()))
    e_mat = e_ref[...]
    neg_a = -jnp.exp(alog_ref[...])
    heads_per_group = SSD_HEADS // SSD_GROUPS
    chunks = xs_s.shape[0] // L
    order = range(chunks) if direction == 0 else range(chunks - 1, -1, -1)

    for k in order:
        rows = slice(k * L, (k + 1) * L)
        xs = xs_s[rows, :]
        bm = bc_s[rows, :LANES]
        cm = bc_s[rows, LANES:]

        dtr = dt_ref[rows, :] + dtb_ref[...]
        dt = jnp.maximum(dtr, 0.0) + _log1p(jnp.exp(-jnp.abs(dtr)))
        acum = _dot_exact_rhs(tri, dt * neg_a)
        acum_t = acum.T
        total = acum[L - 1:L, :] if direction == 0 else acum[0:1, :]
        dec_end = jnp.exp(total - acum)
        ea = jnp.exp(acum)

        dtx = _dot_split2_lhs(dt, e_mat)
        eax = _dot_split2_lhs(ea, e_mat)
        dex = _dot_split2_lhs(dec_end * dt, e_mat)

        xdt = xs * dtx
        xw = (xs * dex).astype(BF16)
        cm_b = cm.astype(BF16)
        bm_b = bm.astype(BF16)

        h_prev = h_ref[...]
        y_off = _dot(cm_b, h_prev.astype(BF16)) * eax

        for g in range(SSD_GROUPS):
            cg = jnp.where((lane // SSD_STATE) == g, cm, 0.0).astype(BF16)
            cb = lax.dot_general(cg, bm_b, nt, preferred_element_type=F32)
            for pair in range(heads_per_group // 2):
                acc = None
                for half in range(2):
                    h = g * heads_per_group + pair * 2 + half
                    hl = direction * SSD_HEADS + h
                    seg = acum[:, hl:hl + 1] - acum_t[hl:hl + 1, :]
                    w = (jnp.where(valid, jnp.exp(seg), 0.0) * cb).astype(BF16)
                    c0 = (h // 2) * LANES
                    rhs = jnp.where((lane // SSD_HEAD_DIM) == half, xdt[:, c0:c0 + LANES], 0.0).astype(BF16)
                    part = _dot(w, rhs)
                    acc = part if acc is None else acc + part
                c0 = (g * heads_per_group // 2 + pair) * LANES
                y_pair = acc + y_off[:, c0:c0 + LANES]
                if final:
                    y_pair = yf_ref[rows, c0:c0 + LANES] + y_pair + dsk_ref[:, c0:c0 + LANES] * xs[:, c0:c0 + LANES]
                    zz = z_ref[rows, c0:c0 + LANES]
                    y_pair = y_pair * (zz * _sigmoid(zz))
                y_ref[rows, c0:c0 + LANES] = y_pair

        cd = eax[L - 1:L, :] if direction == 0 else eax[0:1, :]
        s_new = _dot(bm.T.astype(BF16), xw)
        h_ref[...] = (cd * h_prev + s_new) * hm_ref[...]

    if final:
        gw = SSD_WIDTH // SSD_GROUPS
        for g in range(SSD_GROUPS):
            y = y_ref[:, g * gw:(g + 1) * gw]
            y = y * lax.rsqrt(jnp.mean(y * y, axis=-1, keepdims=True) + RMS_EPS)
            o_ref[:, g * gw:(g + 1) * gw] = (y * ng_ref[:, g * gw:(g + 1) * gw]).astype(o_ref.dtype)


def _ssd_pass(proj_a, fwd_outs, params, seq, direction):
    n = proj_a.shape[0]
    T = SSD_CHUNK * SSD_CHUNKS_PER_STEP
    tiles = seq // T
    nseq = n // seq
    final = direction == 1
    cw, cb_, dtb, alog, e_mat, hmask, dsk, ng = params
    bcw = 2 * SSD_GROUPS * SSD_STATE

    def row_blk(b, c):
        t = c if direction == 0 else tiles - 1 - c
        return b * tiles + t

    def main(width, col0):
        return pl.BlockSpec((T, width), lambda b, c: (row_blk(b, c), col0 // width))

    r = T // SUBLANES
    last_blk = n // SUBLANES - 1

    def halo(width, col0):
        prev = pl.BlockSpec((SUBLANES, width), lambda b, c: (jnp.maximum(row_blk(b, c) * r - 1, 0), col0 // width))
        nxt = pl.BlockSpec((SUBLANES, width), lambda b, c: (jnp.minimum((row_blk(b, c) + 1) * r, last_blk), col0 // width))
        return prev, nxt

    xsp, xsn = halo(SSD_WIDTH, A_XS)
    bcp, bcn = halo(bcw, A_BC)
    full = lambda a: pl.BlockSpec(a.shape, lambda b, c: (0,) * a.ndim)

    rows = lambda width: pl.BlockSpec((T, width), lambda b, c: (row_blk(b, c), 0))
    if final:
        y_fwd, xs_act, bc_act = fwd_outs
        in_specs = [rows(SSD_WIDTH), rows(bcw), main(LANES, A_DT), main(SSD_WIDTH, A_Z), rows(SSD_WIDTH)]
        args = [xs_act, bc_act, proj_a, proj_a, y_fwd]
        consts = [dtb, alog, e_mat, hmask, dsk, ng]
        out_specs = rows(SSD_WIDTH)
        out_shape = jax.ShapeDtypeStruct((n, SSD_WIDTH), BF16)
    else:
        in_specs = [main(SSD_WIDTH, A_XS), xsp, xsn, main(bcw, A_BC), bcp, bcn, main(LANES, A_DT)]
        args = [proj_a] * 7
        consts = [cw[:, :SSD_WIDTH], cb_[:, :SSD_WIDTH], cw[:, SSD_WIDTH:], cb_[:, SSD_WIDTH:], dtb, alog, e_mat, hmask]
        out_specs = [rows(SSD_WIDTH), rows(SSD_WIDTH), rows(bcw)]
        out_shape = [jax.ShapeDtypeStruct((n, SSD_WIDTH), F32), jax.ShapeDtypeStruct((n, SSD_WIDTH), F32),
                     jax.ShapeDtypeStruct((n, bcw), F32)]
    in_specs += [full(a) for a in consts]
    args += consts
    scratch = [pltpu.VMEM((LANES, SSD_WIDTH), F32)]
    if final:
        scratch.append(pltpu.VMEM((T, SSD_WIDTH), F32))
    return pl.pallas_call(
        functools.partial(_ssd_kernel, direction=direction, final=final, tiles=tiles),
        grid=(nseq, tiles),
        in_specs=in_specs,
        out_specs=out_specs,
        out_shape=out_shape,
        scratch_shapes=scratch,
        compiler_params=_cparams(("parallel", "arbitrary")),
        name="ssd_bwd" if final else "ssd_fwd",
    )(*args)


def _ssd_consts(ssd_conv_w, ssd_conv_b, ssd_dt_bias, ssd_a_log, ssd_d, ssd_norm_g):
    pad = LANES - 2 * SSD_HEADS
    dtb = jnp.pad(ssd_dt_bias.reshape(1, -1), ((0, 0), (0, pad)))
    alog = jnp.pad(ssd_a_log.reshape(1, -1), ((0, 0), (0, pad)))
    col_head = np.arange(SSD_WIDTH) // SSD_HEAD_DIM
    e_mats = [jnp.asarray((np.arange(LANES)[:, None] == d * SSD_HEADS + col_head[None, :]), BF16) for d in range(2)]
    row_group = np.arange(LANES) // SSD_STATE
    col_group = col_head // (SSD_HEADS // SSD_GROUPS)
    hmask = jnp.asarray(row_group[:, None] == col_group[None, :], F32)
    dsk = jnp.repeat(ssd_d, SSD_HEAD_DIM).reshape(1, SSD_WIDTH)
    ng = ssd_norm_g.reshape(1, SSD_WIDTH)
    cb_ = ssd_conv_b.reshape(1, -1)
    return [(ssd_conv_w, cb_, dtb, alog, e_mats[d], hmask, dsk, ng) for d in range(2)]


NA_KROWS = NA_QROWS + WIN_H


def _na_kernel(kb_ref, cls_ref, q_ref, k_ref, v_ref, bias_ref, o_ref, *, blocks_per_step):
    step = pl.program_id(2)
    lane = lax.broadcasted_iota(jnp.int32, (1, LANES), 1)
    nt = (((1,), (1,)), ((), ()))
    nq = NA_QROWS * GRID_W
    nk = NA_KROWS * GRID_W
    for j in range(blocks_per_step):
        blk = step * blocks_per_step + j
        k0 = pl.multiple_of(kb_ref[blk] * GRID_W, GRID_W)
        cls = cls_ref[blk]
        q = q_ref[j * nq:(j + 1) * nq, :] * (NA_HEAD_DIM ** -0.5)
        kk = k_ref[pl.ds(k0, nk), :]
        vv = v_ref[pl.ds(k0, nk), :]
        outs = []
        for hh in range(2):
            qm = jnp.where((lane // NA_HEAD_DIM) == hh, q, jnp.zeros_like(q))
            s = lax.dot_general(qm, kk, nt, preferred_element_type=F32) + bias_ref[cls, hh]
            m = jnp.max(s, axis=-1, keepdims=True)
            p = jnp.exp(s - m)
            l = jnp.sum(p, axis=-1, keepdims=True)
            outs.append(_dot(p.astype(BF16), vv) / l)
        o = jnp.where((lane // NA_HEAD_DIM) == 0, outs[0], outs[1])
        o_ref[j * nq:(j + 1) * nq, :] = o.astype(o_ref.dtype)


def _na_tables(rel_bias, rows):
    qc = np.arange(GRID_W)
    cs = np.clip(qc - WIN_W // 2, 0, GRID_W - WIN_W)
    kc = np.arange(GRID_W)
    col_valid = (kc[None, :] >= cs[:, None]) & (kc[None, :] < cs[:, None] + WIN_W)
    dj = np.clip(kc[None, :] - qc[:, None] + WIN_W - 1, 0, 2 * WIN_W - 2)
    nblk = rows // NA_QROWS
    kb = np.clip(np.arange(nblk) * NA_QROWS - WIN_H // 2, 0, rows - NA_KROWS)
    patterns, cls = [], []
    for b in range(nblk):
        r = b * NA_QROWS + np.arange(NA_QROWS)
        rs = np.clip(r - WIN_H // 2, 0, rows - WIN_H)
        key = (tuple(rs - kb[b]), int(kb[b] - b * NA_QROWS))
        if key not in patterns:
            patterns.append(key)
        cls.append(patterns.index(key))
    nh, nw = 2 * WIN_H - 1, 2 * WIN_W - 1
    bias_ext = jnp.pad(rel_bias.astype(F32), ((0, 0), (0, 1), (0, 1)), constant_values=NEG_BIG)
    di_sel = []
    for off, shift in patterns:
        kr = np.arange(NA_KROWS)[None, :]
        i = np.arange(NA_QROWS)[:, None]
        rel = kr - np.asarray(off)[:, None]
        row_valid = (rel >= 0) & (rel < WIN_H)
        di = np.where(row_valid, np.clip(shift + kr - i + WIN_H - 1, 0, nh - 1), nh)
        di_sel.append(np.eye(nh + 1, dtype=np.float32)[di])
    dj_sel = np.eye(nw + 1, dtype=np.float32)[np.where(col_valid, dj, nw)]
    tab = jnp.einsum('pikd,hde,qwe->phiqkw', np.stack(di_sel), bias_ext, dj_sel, precision=lax.Precision.HIGHEST)
    tab = tab.reshape(len(patterns), NA_HEADS, NA_QROWS * GRID_W, NA_KROWS * GRID_W)
    return jnp.asarray(kb, jnp.int32), jnp.asarray(cls, jnp.int32), tab


def _neighbourhood_attention(qkv, rel_bias, seq, blocks_per_step=NA_BLOCKS_PER_STEP):
    (q_arr, q0), (k_arr, k0), (v_arr, v0) = qkv
    n = q_arr.shape[0]
    nseq = n // seq
    rows = seq // GRID_W
    kb, cls, bias_tab = _na_tables(rel_bias, rows)
    tq = blocks_per_step * NA_QROWS * GRID_W
    pairs = NA_HEADS // 2
    steps = seq // tq
    ncls = bias_tab.shape[0]
    grid_spec = pltpu.PrefetchScalarGridSpec(
        num_scalar_prefetch=2,
        grid=(nseq, pairs, steps),
        in_specs=[pl.BlockSpec((tq, LANES), lambda b, p, r, kb_, cls_: (b * steps + r, q0 + p)),
                  pl.BlockSpec((seq, LANES), lambda b, p, r, kb_, cls_: (b, k0 + p)),
                  pl.BlockSpec((seq, LANES), lambda b, p, r, kb_, cls_: (b, v0 + p)),
                  pl.BlockSpec((ncls, 2) + bias_tab.shape[2:], lambda b, p, r, kb_, cls_: (0, p, 0, 0))],
        out_specs=pl.BlockSpec((tq, LANES), lambda b, p, r, kb_, cls_: (b * steps + r, p)),
    )
    return pl.pallas_call(
        functools.partial(_na_kernel, blocks_per_step=blocks_per_step),
        grid_spec=grid_spec,
        out_shape=jax.ShapeDtypeStruct((n, NA_WIDTH), F32),
        compiler_params=_cparams(("parallel", "arbitrary", "arbitrary")),
        name="nbr_attn",
    )(kb, cls, q_arr, k_arr, v_arr, bias_tab)


def _rms_rows(y, g):
    return y * lax.rsqrt(jnp.mean(y * y, axis=-1, keepdims=True) + RMS_EPS) * g


def _outproj_kernel(*refs, router, n_parts, first_tiles):
    yc_ref, ys_ref, yn_ref = refs[:3]
    x_refs = refs[3:3 + n_parts]
    rest = refs[3 + n_parts:]
    if router:
        w_ref, nag_ref, g_ref, b_ref, rw_ref, o_ref, idx_ref, gate_ref = rest
    else:
        w_ref, nag_ref, g_ref, b_ref, o_ref = rest
    c1 = CONV_WIDTH
    c2 = CONV_WIDTH + SSD_WIDTH
    rows = o_ref.shape[0] // OUTPROJ_ROW_SPLITS
    for part in range(OUTPROJ_ROW_SPLITS):
        sl = slice(part * rows, (part + 1) * rows)
        yn = _rms_rows(yn_ref[sl, :], nag_ref[...]).astype(BF16)
        mix = _dot(yc_ref[sl, :], w_ref[0:c1, :]) + _dot(ys_ref[sl, :], w_ref[c1:c2, :]) + _dot(yn, w_ref[c2:, :])
        out = _layer_norm(ALPHA * _read_row_parts(x_refs, first_tiles, sl) + mix, g_ref[...], b_ref[...])
        o_ref[sl, :] = out
        if not router:
            continue
        o1, o2, _ = _split3(out)
        t1 = _dot(o1, rw_ref[...])
        t2 = _dot(o2, rw_ref[...])
        sh1 = LANES - N_EXPERTS
        sh2 = LANES - 2 * N_EXPERTS
        logits = ((pltpu.roll(t2, sh1, 1) + pltpu.roll(t1, sh2, 1)) + (t2 + pltpu.roll(t1, sh1, 1))) + t1
        lane = lax.broadcasted_iota(jnp.int32, (rows, LANES), 1)
        lg = jnp.where(lane < N_EXPERTS, logits, -jnp.inf)
        m1 = jnp.max(lg, axis=-1, keepdims=True)
        i1 = jnp.min(jnp.where(lg == m1, lane, LANES), axis=-1, keepdims=True)
        lg2 = jnp.where(lane == i1, -jnp.inf, lg)
        m2 = jnp.max(lg2, axis=-1, keepdims=True)
        i2 = jnp.min(jnp.where(lg2 == m2, lane, LANES), axis=-1, keepdims=True)
        e2 = jnp.exp(m2 - m1)
        den = 1.0 + e2
        idx_ref[sl, :] = jnp.where(lane == 0, i1, jnp.where(lane == 1, i2, 0))
        gate_ref[sl, :] = jnp.where(lane == 0, 1.0 / den, jnp.where(lane == 1, e2 / den, 0.0))


def _out_proj_ln(yc, ys, yn, x_parts, w_out, na_g, ln_g, ln_b, router_w=None, tm=ROW_TILE):
    n = yc.shape[0]
    router = router_w is not None
    row = lambda w: pl.BlockSpec((tm, w), lambda i: (i, 0))
    full = lambda a: pl.BlockSpec(a.shape, lambda i: (0,) * a.ndim)
    consts = [w_out, na_g.reshape(1, -1), ln_g.reshape(1, -1), ln_b.reshape(1, -1)]
    if router:
        consts.append(router_w)
    out_shape = [jax.ShapeDtypeStruct((n, D_MODEL), F32)]
    out_specs = [row(D_MODEL)]
    if router:
        out_shape += [jax.ShapeDtypeStruct((n, LANES), jnp.int32), jax.ShapeDtypeStruct((n, LANES), F32)]
        out_specs += [row(LANES), row(LANES)]
    res = pl.pallas_call(
        functools.partial(_outproj_kernel, router=router, n_parts=len(x_parts),
                          first_tiles=x_parts[0].shape[0] // tm),
        grid=(n // tm,),
        in_specs=[row(CONV_WIDTH), row(SSD_WIDTH), row(NA_WIDTH)] + _row_parts_specs(x_parts, tm, D_MODEL, 1)
        + [full(a) for a in consts],
        out_specs=out_specs,
        out_shape=out_shape,
        compiler_params=_cparams(("parallel",)),
        name="out_proj_router" if router else "out_proj",
    )(yc, ys, yn, *x_parts, *consts)
    return res


def _ffn_kernel(be_ref, bv_ref, x_ref, wg_ref, wu_ref, wd_ref, *rest, packed, ln, ragged):
    if ln:
        g_ref, b_ref, o_ref = rest
    else:
        (o_ref,) = rest
    i = pl.program_id(0)
    j = pl.program_id(1)

    @pl.when(j == 0)
    def _():
        o_ref[...] = jnp.zeros_like(o_ref)

    def accumulate(rows):
        if packed:
            lo, hi = _unpack_bf16_pairs(x_ref[rows, :])
            c = lo.shape[1]
            g = _dot(lo, wg_ref[0, :c, :].astype(BF16)) + _dot(hi, wg_ref[0, c:, :].astype(BF16))
            u = _dot(lo, wu_ref[0, :c, :].astype(BF16)) + _dot(hi, wu_ref[0, c:, :].astype(BF16))
        else:
            x = x_ref[rows, :].astype(BF16)
            g = _dot(x, wg_ref[0].astype(BF16))
            u = _dot(x, wu_ref[0].astype(BF16))
        h = ((g * _sigmoid(g)) * u).astype(BF16)
        o_ref[rows, :] += _dot(h, wd_ref[0].astype(BF16))

    tm = o_ref.shape[0]
    filled = bv_ref[i]

    @pl.when(filled == tm)
    def _():
        accumulate(slice(None))

    if ragged:
        for s in range(tm // FFN_ROW_GROUP):
            @pl.when(jnp.logical_and(filled < tm, filled > s * FFN_ROW_GROUP))
            def _():
                accumulate(slice(s * FFN_ROW_GROUP, (s + 1) * FFN_ROW_GROUP))

    if ln:
        @pl.when(j == pl.num_programs(1) - 1)
        def _():
            o_ref[...] = _layer_norm(ALPHA * x_ref[...] + o_ref[...], g_ref[...], b_ref[...])


def _ffn_blocks(x, block_e, block_fill, w_gate, w_up, w_down, tm, tf=FF_TILE, residual_ln=None, ragged=False):
    n, xw = x.shape
    packed = x.dtype == jnp.uint32
    d = w_gate.shape[1]
    f = w_gate.shape[2]
    nf = f // tf
    col = lambda i, j, be, bv: (be[i], 0, jnp.where(bv[i] != 0, j, nf - 1))
    rowj = lambda i, j, be, bv: (be[i], jnp.where(bv[i] != 0, j, nf - 1), 0)
    in_specs = [pl.BlockSpec((tm, xw), lambda i, j, be, bv: (i, 0)),
                pl.BlockSpec((1, d, tf), col), pl.BlockSpec((1, d, tf), col), pl.BlockSpec((1, tf, d), rowj)]
    args = [x, w_gate, w_up, w_down]
    if residual_ln is not None:
        gain, bias = residual_ln
        assert not packed
        vec = pl.BlockSpec((1, d), lambda i, j, be, bv: (0, 0))
        in_specs += [vec, vec]
        args += [gain.reshape(1, d), bias.reshape(1, d)]
    grid_spec = pltpu.PrefetchScalarGridSpec(
        num_scalar_prefetch=2,
        grid=(n // tm, nf),
        in_specs=in_specs,
        out_specs=pl.BlockSpec((tm, d), lambda i, j, be, bv: (i, 0)),
    )
    return pl.pallas_call(
        functools.partial(_ffn_kernel, packed=packed, ln=residual_ln is not None, ragged=ragged),
        grid_spec=grid_spec,
        out_shape=jax.ShapeDtypeStruct((n, d), F32),
        compiler_params=_cparams(("parallel", "arbitrary")),
        name="swiglu_blocks",
    )(block_e, block_fill, *args)


def _combine_ln_kernel(i0_ref, i1_ref, n0_ref, n1_ref, src_ref, x_ref, gate_ref, g_ref, b_ref, oa_ref, ob_ref,
                       b0_ref, b1_ref, sem, *, tm, first_tiles):
    i = pl.program_id(0)
    slot = i % 2

    def start(idx0_ref, idx1_ref, s):
        _start_row_gather(src_ref, (idx0_ref, idx1_ref), (b0_ref.at[s], b1_ref.at[s]), (sem.at[s, 0], sem.at[s, 1]), tm)

    @pl.when(i == 0)
    def _():
        start(i0_ref, i1_ref, 0)

    @pl.when(i + 1 < pl.num_programs(0))
    def _():
        start(n0_ref, n1_ref, 1 - slot)

    _wait_row_gather(src_ref, b0_ref.at[slot], sem.at[slot, 0], tm)
    _wait_row_gather(src_ref, b1_ref.at[slot], sem.at[slot, 1], tm)
    f = b0_ref[slot] * gate_ref[:, 0:1] + b1_ref[slot] * gate_ref[:, 1:2]
    res = _layer_norm(ALPHA * x_ref[...] + f, g_ref[...], b_ref[...])

    @pl.when(i < first_tiles)
    def _():
        oa_ref[...] = res

    @pl.when(i >= first_tiles)
    def _():
        ob_ref[...] = res


def _combine_ln(x, outs, d0, d1, gates, g, b, n_first, tm=ROW_TILE):
    n, d = x.shape
    nt = n // tm
    first_tiles = n_first // tm
    row = pl.BlockSpec((tm, d), lambda i: (i, 0))
    vec = pl.BlockSpec((1, d), lambda i: (0, 0))
    idx = pl.BlockSpec((1, 1, tm), lambda i: (i, 0, 0), memory_space=pltpu.SMEM)
    idx_next = pl.BlockSpec((1, 1, tm), lambda i: (jnp.minimum(i + 1, nt - 1), 0, 0), memory_space=pltpu.SMEM)
    d0, d1 = d0.reshape(nt, 1, tm), d1.reshape(nt, 1, tm)
    return pl.pallas_call(
        functools.partial(_combine_ln_kernel, tm=tm, first_tiles=first_tiles), grid=(nt,),
        in_specs=[idx, idx, idx_next, idx_next, pl.BlockSpec(memory_space=pl.ANY), row,
                  pl.BlockSpec((tm, LANES), lambda i: (i, 0)), vec, vec],
        out_specs=[pl.BlockSpec((tm, d), lambda i: (jnp.minimum(i, first_tiles - 1), 0)),
                   pl.BlockSpec((tm, d), lambda i: (jnp.maximum(i - first_tiles, 0), 0))],
        out_shape=[jax.ShapeDtypeStruct((n_first, d), F32), jax.ShapeDtypeStruct((n - n_first, d), F32)],
        scratch_shapes=[pltpu.VMEM((2, tm, d), F32), pltpu.VMEM((2, tm, d), F32), pltpu.SemaphoreType.DMA((2, 2))],
        compiler_params=_cparams(("arbitrary",)), name="combine_ln",
    )(d0, d1, d0, d1, outs, x, gates, g.reshape(1, d), b.reshape(1, d))


def _start_row_gather(src_ref, idx_refs, buf_refs, sems, tr):
    def issue(r, carry):
        for k, (idx_ref, buf_ref, sem) in enumerate(zip(idx_refs, buf_refs, sems)):
            pltpu.make_async_copy(src_ref.at[pl.ds(idx_ref[0, 0, r], 1), :], buf_ref.at[pl.ds(r, 1), :],
                                  sem).start(priority=k % 2)
        return carry

    lax.fori_loop(0, tr, issue, 0, unroll=GATHER_UNROLL)


def _wait_row_gather(src_ref, buf_ref, sem, tr):
    pltpu.make_async_copy(src_ref.at[pl.ds(0, tr), :], buf_ref, sem).wait()


HI16 = 0xFFFF0000


def _pack_bf16_pairs(x):
    c = x.shape[1] // 2
    lo = lax.bitcast_convert_type(x[:, :c].astype(BF16).astype(F32), jnp.uint32)
    hi = lax.bitcast_convert_type(x[:, c:].astype(BF16).astype(F32), jnp.uint32)
    return (hi & jnp.uint32(HI16)) | (lo >> 16)


def _unpack_bf16_pairs(u):
    lo = lax.bitcast_convert_type(u << 16, F32).astype(BF16)
    hi = lax.bitcast_convert_type(u & jnp.uint32(HI16), F32).astype(BF16)
    return lo, hi


def _zero_row_runs(first, length, max_len, z_ref, o_ref, sem):
    zrows = z_ref.shape[0]

    def copy(row, rows):
        return pltpu.make_async_copy(z_ref.at[pl.ds(0, rows), :], o_ref.at[pl.ds(row, rows), :], sem)

    def for_each_run(action):
        head = jnp.minimum((-first) % SUBLANES, length)
        tail = (length - head) % SUBLANES
        body = length - head - tail
        for k in range(SUBLANES - 1):
            @pl.when(k < head)
            def _(k=k):
                action(copy(first + k, 1))
        row = first + head
        for bit in range(SUBLANES.bit_length() - 1, max_len.bit_length()):
            size = 1 << bit
            take = (body >> bit) & 1
            rows = min(size, zrows)
            for piece in range(size // rows):
                @pl.when(take == 1)
                def _(row=row, piece=piece, rows=rows):
                    action(copy(pl.multiple_of(row + piece * rows, SUBLANES), rows))
            row = row + take * size
        for k in range(SUBLANES - 1):
            @pl.when(k < tail)
            def _(k=k, row=row):
                action(copy(row + k, 1))

    for_each_run(lambda c: c.start())
    for_each_run(lambda c: c.wait())


def _dispatch_kernel(pad_start_ref, pad_len_ref, d0_ref, d1_ref, x_ref, xs_ref, pk_ref, z_ref, sem, zsem,
                     *, tm, max_pad):
    @pl.when(pl.program_id(0) == 0)
    def _():
        z_ref[...] = jnp.zeros_like(z_ref)

        def zero_region(r, carry):
            _zero_row_runs(pad_start_ref[r], pad_len_ref[r], max_pad, z_ref, xs_ref, zsem)
            return carry

        lax.fori_loop(0, pad_start_ref.shape[0], zero_region, 0)

    pk_ref[...] = _pack_bf16_pairs(x_ref[...])

    def issue(r, carry):
        row = pk_ref.at[pl.ds(r, 1), :]
        pltpu.make_async_copy(row, xs_ref.at[pl.ds(d0_ref[0, 0, r], 1), :], sem.at[0]).start(priority=0)
        pltpu.make_async_copy(row, xs_ref.at[pl.ds(d1_ref[0, 0, r], 1), :], sem.at[1]).start(priority=1)
        return carry

    lax.fori_loop(0, tm, issue, 0, unroll=GATHER_UNROLL)
    for k in range(TOP_K):
        pltpu.make_async_copy(pk_ref, xs_ref.at[pl.ds(0, tm), :], sem.at[k]).wait()


def _dispatch_rows(x, d0, d1, pad_start, pad_len, p_len, max_pad, tm=ROW_TILE):
    n, d = x.shape
    idx = pl.BlockSpec((1, 1, tm), lambda i, ps, pn: (i, 0, 0), memory_space=pltpu.SMEM)
    grid_spec = pltpu.PrefetchScalarGridSpec(
        num_scalar_prefetch=2,
        grid=(n // tm,),
        in_specs=[idx, idx, pl.BlockSpec((tm, d), lambda i, ps, pn: (i, 0))],
        out_specs=pl.BlockSpec(memory_space=pl.ANY),
        scratch_shapes=[pltpu.VMEM((tm, d // 2), jnp.uint32), pltpu.VMEM((tm, d // 2), jnp.uint32),
                        pltpu.SemaphoreType.DMA((TOP_K,)), pltpu.SemaphoreType.DMA(())],
    )
    return pl.pallas_call(
        functools.partial(_dispatch_kernel, tm=tm, max_pad=max_pad),
        grid_spec=grid_spec,
        out_shape=jax.ShapeDtypeStruct((p_len, d // 2), jnp.uint32),
        compiler_params=_cparams(("arbitrary",)),
        name="dispatch_rows",
    )(pad_start, pad_len, d0.reshape(n // tm, 1, tm), d1.reshape(n // tm, 1, tm), x)


def _routing_tables(idx, n, tm):
    flat_e = idx[:, :TOP_K].reshape(-1)
    onehot = (flat_e[:, None] == jnp.arange(N_EXPERTS, dtype=jnp.int32)[None, :]).astype(jnp.int32)
    incl = jnp.cumsum(onehot, axis=0)
    rank = jnp.sum((incl - onehot) * onehot, axis=1)
    counts = incl[-1]
    padded = ((counts + tm - 1) // tm) * tm
    pend = jnp.cumsum(padded)
    pstart = pend - padded
    dest = pstart[flat_e] + rank
    n_blocks = -(-(n * TOP_K) // tm) + N_EXPERTS
    starts = jnp.arange(n_blocks, dtype=jnp.int32) * tm
    block_e = jnp.sum((pend[None, :] <= starts[:, None]).astype(jnp.int32), axis=1)
    block_e = jnp.minimum(block_e, N_EXPERTS - 1)
    block_fill = jnp.clip((pstart + counts)[block_e] - starts, 0, tm).astype(jnp.int32)
    block_fill = jnp.where(starts < pend[-1], block_fill, 0)
    last_e = block_e[jnp.maximum(jnp.sum((block_fill > 0).astype(jnp.int32)) - 1, 0)]
    block_e = jnp.where(block_fill > 0, block_e, last_e)
    dest2 = dest.reshape(n, TOP_K)
    pad_start = jnp.concatenate([pstart + counts, pend[-1:]]).astype(jnp.int32)
    pad_len = jnp.concatenate([padded - counts, n_blocks * tm - pend[-1:]]).astype(jnp.int32)
    return (pad_start, pad_len, n_blocks * tm), block_e, block_fill, dest2[:, 0], dest2[:, 1]


def _prep_w_in(w):
    bc_w = 2 * SSD_GROUPS * SSD_STATE
    widths = (CONV_WIDTH, CONV_WIDTH, CONV_WIDTH, SSD_WIDTH, SSD_WIDTH, bc_w, 2 * SSD_HEADS, 3 * NA_WIDTH)
    cb, cc, ch, z, xs, bc, dt, qkv = jnp.split(w, [int(o) for o in np.cumsum(widths)[:-1]], axis=1)
    pad = jnp.zeros((D_MODEL, A_WIDTH - (A_DT + 2 * SSD_HEADS)), w.dtype)
    return jnp.concatenate([z, xs, cb, cc, ch, bc, dt, pad, qkv], axis=1).astype(BF16)


def _prep_router_w(rw):
    r1 = rw.astype(BF16)
    r2 = (rw - r1.astype(F32)).astype(BF16)
    r3 = (rw - r1.astype(F32) - r2.astype(F32)).astype(BF16)
    pad = jnp.zeros((rw.shape[0], LANES - 3 * N_EXPERTS), BF16)
    return jnp.concatenate([r1, r2, r3, pad], axis=1)


def _trunk(x_parts, seq, p):
    n = sum(a.shape[0] for a in x_parts)
    n_first = x_parts[0].shape[0]
    for l in range(DEPTH):
        proj_a, qkv = _in_proj(x_parts, _prep_w_in(p['w_in'][l]))
        y_conv = _conv_mix(proj_a, p['conv_w'][l], p['conv_norm_g'][l], seq)
        cf, cbw = _ssd_consts(p['ssd_conv_w'][l], p['ssd_conv_b'][l], p['ssd_dt_bias'][l], p['ssd_a_log'][l],
                              p['ssd_d'][l], p['ssd_norm_g'][l])
        y_ssd = _ssd_pass(proj_a, _ssd_pass(proj_a, None, cf, seq, 0), cbw, seq, 1)
        y_na = _neighbourhood_attention(qkv, p['na_rel_bias'][l], seq)
        w_out = p['w_out'][l].astype(BF16)
        if l % 2 == 0:
            x = _out_proj_ln(y_conv, y_ssd, y_na, x_parts, w_out, p['na_norm_g'][l], p['ln_mix_g'][l],
                             p['ln_mix_b'][l])[0]
            padf = ((0, 0), (0, D_FF_PAD - D_FF))
            wg = jnp.pad(p['ffn_w_gate'][l // 2].astype(BF16), padf)[None]
            wu = jnp.pad(p['ffn_w_up'][l // 2].astype(BF16), padf)[None]
            wd = jnp.pad(p['ffn_w_down'][l // 2].astype(BF16), ((0, D_FF_PAD - D_FF), (0, 0)))[None]
            nb = n // MOE_TM
            x = _ffn_blocks(x, jnp.zeros((nb,), jnp.int32), jnp.full((nb,), MOE_TM, jnp.int32), wg, wu, wd, MOE_TM,
                            residual_ln=(p['ln_ffn_g'][l], p['ln_ffn_b'][l]))
            outs_split = (x[:n_first], x[n_first:])
        else:
            x, idx, gates = _out_proj_ln(y_conv, y_ssd, y_na, x_parts, w_out, p['na_norm_g'][l], p['ln_mix_g'][l],
                                         p['ln_mix_b'][l], _prep_router_w(p['router_w'][l // 2]))
            (pad_start, pad_len, p_len), block_e, block_fill, d0, d1 = _routing_tables(idx, n, MOE_TM)
            xs = _dispatch_rows(x, d0, d1, pad_start, pad_len, p_len, N_EXPERTS * MOE_TM)
            outs = _ffn_blocks(xs, block_e, block_fill, p['moe_w_gate'][l // 2], p['moe_w_up'][l // 2],
                               p['moe_w_down'][l // 2], MOE_TM, ragged=True)
            outs_split = _combine_ln(x, outs, d0, d1, gates, p['ln_ffn_g'][l], p['ln_ffn_b'][l], n_first)
            if l + 1 < DEPTH:
                x = jnp.concatenate(outs_split, axis=0)
        x_parts = [x]
    return outs_split


def kernel(x_prompt, x_sample, w_in, conv_w, conv_norm_g, ssd_conv_w, ssd_conv_b, ssd_dt_bias, ssd_a_log, ssd_d,
           ssd_norm_g, na_rel_bias, na_norm_g, w_out, ln_mix_g, ln_mix_b, ln_ffn_g, ln_ffn_b, ffn_w_gate, ffn_w_up,
           ffn_w_down, router_w, moe_w_gate, moe_w_up, moe_w_down):
    p = dict(w_in=w_in, conv_w=conv_w, conv_norm_g=conv_norm_g, ssd_conv_w=ssd_conv_w, ssd_conv_b=ssd_conv_b,
             ssd_dt_bias=ssd_dt_bias, ssd_a_log=ssd_a_log, ssd_d=ssd_d, ssd_norm_g=ssd_norm_g,
             na_rel_bias=na_rel_bias, na_norm_g=na_norm_g, w_out=w_out, ln_mix_g=ln_mix_g, ln_mix_b=ln_mix_b,
             ln_ffn_g=ln_ffn_g, ln_ffn_b=ln_ffn_b, ffn_w_gate=ffn_w_gate, ffn_w_up=ffn_w_up, ffn_w_down=ffn_w_down,
             router_w=router_w, moe_w_gate=moe_w_gate, moe_w_up=moe_w_up, moe_w_down=moe_w_down)
    bp, seq, d = x_prompt.shape
    bs, seq_s, _ = x_sample.shape
    assert seq == seq_s, "both request groups must share one sequence length"
    yp, ys = _trunk([x_prompt.reshape(bp * seq, d), x_sample.reshape(bs * seq, d)], seq, p)
    return yp.reshape(bp, seq, d), ys.reshape(bs, seq, d)
```
